```python
import jax, jax.numpy as jnp
from jax import lax
import numpy as np

D_MODEL = 1024
BATCH = 8
SEQ = 2048
DEPTH = 2

GRID_W = 64
CTX_LEN = 256
MIX_W = D_MODEL
N_MIXERS = 4
GROUP_W = MIX_W // N_MIXERS
HEAD_DIM = 64
A_HEADS = GROUP_W // HEAD_DIM
A_KV_HEADS = A_HEADS // 2
A_WINDOW = 128
A_BLOCK = 128
B_CHUNK = 128
B_GROUPS = 4
B_GROUP_W = GROUP_W // B_GROUPS
C_POOLS = (2, 4, 8, 16)
C_GROUP_W = GROUP_W // len(C_POOLS)
D_HEADS = GROUP_W // HEAD_DIM
D_WIN_ROWS = 8
D_WIN_COLS = 16
FF_DIM = ((8 * D_MODEL + 3 * 256 - 1) // (3 * 256)) * 256
ROPE_BASE = 10000.0
NORM_EPS = 1e-6
NEG_INF = -1e30
IN_WIDTHS = (A_HEADS * HEAD_DIM, A_KV_HEADS * HEAD_DIM, A_KV_HEADS * HEAD_DIM,
             GROUP_W, GROUP_W,
             GROUP_W,
             D_HEADS * HEAD_DIM, D_HEADS * HEAD_DIM, D_HEADS * HEAD_DIM)
IN_TOTAL = sum(IN_WIDTHS)

kernel_name = "hymba_style_hybrid_dit_block"


def rms_norm(x, g):
    xf = x.astype(jnp.float32)
    y = xf * lax.rsqrt(jnp.mean(xf * xf, axis=-1, keepdims=True) + NORM_EPS)
    return (y * g.astype(jnp.float32)).astype(x.dtype)


def layer_norm(x, g):
    xf = x.astype(jnp.float32)
    mu = jnp.mean(xf, axis=-1, keepdims=True)
    var = jnp.mean(jnp.square(xf - mu), axis=-1, keepdims=True)
    return ((xf - mu) * lax.rsqrt(var + NORM_EPS) * g.astype(jnp.float32)).astype(x.dtype)


def modulate(h, shift, scale):
    return h * (1 + scale) + shift


def heads(t, n):
    return t.reshape(t.shape[:-1] + (n, HEAD_DIM))


def split_cols(p):
    offs = np.cumsum(IN_WIDTHS)[:-1].tolist()
    return jnp.split(p, offs, axis=-1)


def col_block(w, i):
    start = sum(IN_WIDTHS[:i])
    return w[:, start:start + IN_WIDTHS[i]]


def rope_1d(x, pos):
    half = x.shape[-1] // 2
    inv = jnp.power(ROPE_BASE, -jnp.arange(half, dtype=jnp.float32) / half)
    ang = pos.astype(jnp.float32)[:, None] * inv[None, :]
    cos = jnp.cos(ang)[None, :, None, :]
    sin = jnp.sin(ang)[None, :, None, :]
    xf = x.astype(jnp.float32)
    x1, x2 = xf[..., :half], xf[..., half:]
    return jnp.concatenate([x1 * cos - x2 * sin, x1 * sin + x2 * cos], axis=-1).astype(x.dtype)


def axial_rope(x, rows, cols):
    h = x.shape[-1] // 2
    return jnp.concatenate([rope_1d(x[..., :h], rows), rope_1d(x[..., h:], cols)], axis=-1)


def window_attention(q, k, v, kc, vc, sink):
    b, s, nh, dh = q.shape
    nkv = k.shape[2]
    g = nh // nkv
    nb = s // A_BLOCK
    scale = dh ** -0.5
    qb = q.reshape(b, nb, A_BLOCK, nkv, g, dh)
    pad = ((0, 0), (A_BLOCK, A_BLOCK), (0, 0), (0, 0))
    kp = jnp.pad(k, pad).reshape(b, nb + 2, A_BLOCK, nkv, dh)
    vp = jnp.pad(v, pad).reshape(b, nb + 2, A_BLOCK, nkv, dh)
    kb = jnp.concatenate([kp[:, :-2], kp[:, 1:-1], kp[:, 2:]], axis=2)
    vb = jnp.concatenate([vp[:, :-2], vp[:, 1:-1], vp[:, 2:]], axis=2)
    s_loc = jnp.einsum('bnqhgd,bnkhd->bnhgqk', qb, kb).astype(jnp.float32) * scale
    blk = jnp.arange(nb)[:, None, None]
    qpos = blk * A_BLOCK + jnp.arange(A_BLOCK)[None, :, None]
    kpos = (blk - 1) * A_BLOCK + jnp.arange(3 * A_BLOCK)[None, None, :]
    valid = (jnp.abs(kpos - qpos) <= A_WINDOW) & (kpos >= 0) & (kpos < s)
    s_loc = jnp.where(valid[None, :, None, None], s_loc, NEG_INF)
    s_ctx = jnp.einsum('bnqhgd,blhd->bnhgql', qb, kc).astype(jnp.float32) * scale
    s_sink = jnp.broadcast_to(sink.astype(jnp.float32).reshape(nkv, g)[None, None, :, :, None, None],
                              s_loc.shape[:-1] + (1,))
    probs = jax.nn.softmax(jnp.concatenate([s_loc, s_ctx, s_sink], axis=-1), axis=-1)
    nloc = 3 * A_BLOCK
    nctx = kc.shape[1]
    p_loc = probs[..., :nloc].astype(v.dtype)
    p_ctx = probs[..., nloc:nloc + nctx].astype(v.dtype)
    out = (jnp.einsum('bnhgqk,bnkhd->bnqhgd', p_loc, vb)
           + jnp.einsum('bnhgql,blhd->bnqhgd', p_ctx, vc))
    return out.reshape(b, s, nh * dh)


def context_attention(qc, kc, vc, sink):
    b, l, nh, dh = qc.shape
    nkv = kc.shape[2]
    g = nh // nkv
    qg = qc.reshape(b, l, nkv, g, dh)
    s = jnp.einsum('blhgd,bmhd->bhglm', qg, kc).astype(jnp.float32) * dh ** -0.5
    if sink is not None:
        s_sink = jnp.broadcast_to(sink.astype(jnp.float32).reshape(nkv, g)[None, :, :, None, None],
                                  s.shape[:-1] + (1,))
        s = jnp.concatenate([s, s_sink], axis=-1)
    probs = jax.nn.softmax(s, axis=-1)[..., :l].astype(vc.dtype)
    out = jnp.einsum('bhglm,bmhd->blhgd', probs, vc)
    return out.reshape(b, l, nh * dh)


def neighbourhood_attention(q, k, v, kc, vc, rpb):
    b, s, nh, dh = q.shape
    rows = s // GRID_W
    kh = min(D_WIN_ROWS, rows)
    kw = D_WIN_COLS
    scale = dh ** -0.5
    qg = q.reshape(b, rows, GRID_W, nh, dh)
    kg = k.reshape(b, rows, GRID_W, nh, dh)
    vg = v.reshape(b, rows, GRID_W, nh, dh)
    r = jnp.arange(rows)
    cidx = jnp.arange(GRID_W)
    row_start = jnp.clip(r - kh // 2, 0, rows - kh)
    row_idx = row_start[:, None] + jnp.arange(kh)[None, :]
    k_rows = kg[:, row_idx]
    v_rows = vg[:, row_idx]
    s_loc = jnp.einsum('brchd,brkwhd->bhrckw', qg, k_rows).astype(jnp.float32) * scale
    col_start = jnp.clip(cidx - kw // 2, 0, GRID_W - kw)
    col_ok = (cidx[None, :] >= col_start[:, None]) & (cidx[None, :] < col_start[:, None] + kw)
    roff = row_idx - r[:, None] + (D_WIN_ROWS - 1)
    coff = jnp.clip(cidx[None, :] - cidx[:, None], -(kw - 1), kw - 1) + (D_WIN_COLS - 1)
    bias = rpb[:, roff[:, None, :, None], coff[None, :, None, :]]
    s_loc = jnp.where(col_ok[None, None, None, :, None, :], s_loc + bias.astype(jnp.float32)[None], NEG_INF)
    nloc = kh * GRID_W
    s_loc = s_loc.reshape(b, nh, rows, GRID_W, nloc)
    s_ctx = jnp.einsum('brchd,blhd->bhrcl', qg, kc).astype(jnp.float32) * scale
    probs = jax.nn.softmax(jnp.concatenate([s_loc, s_ctx], axis=-1), axis=-1)
    p_loc = probs[..., :nloc].reshape(b, nh, rows, GRID_W, kh, GRID_W).astype(v.dtype)
    p_ctx = probs[..., nloc:].astype(v.dtype)
    out = (jnp.einsum('bhrckw,brkwhd->brchd', p_loc, v_rows)
           + jnp.einsum('bhrcl,blhd->brchd', p_ctx, vc))
    return out.reshape(b, s, nh * dh)


def spatial_gating(u, v, v_gain, w_s, b_s):
    b, n, _ = u.shape
    nc = n // B_CHUNK
    u = jax.nn.gelu(u)
    v = layer_norm(jax.nn.gelu(v), v_gain)
    vc = v.reshape(b, nc, B_CHUNK, B_GROUPS, B_GROUP_W)
    z = jnp.einsum('gpq,bnqgc->bnpgc', w_s, vc) + b_s.T[None, None, :, :, None]
    return u * z.reshape(b, n, GROUP_W)


def multiscale_pool(y, w_pool, scale):
    b, n, ch = y.shape
    yf = y.astype(jnp.float32)
    cs = jnp.concatenate([jnp.zeros((b, 1, ch), jnp.float32), jnp.cumsum(yf, axis=1)], axis=1)
    t = jnp.arange(n)
    outs = []
    for gi, w in enumerate(C_POOLS):
        sl = slice(gi * C_GROUP_W, (gi + 1) * C_GROUP_W)
        lo = jnp.clip(t - w // 2, 0, n)
        hi = jnp.clip(t - w // 2 + w, 0, n)
        csg = cs[..., sl]
        cnt = (hi - lo).astype(jnp.float32)[None, :, None]
        pooled = (csg[:, hi] - csg[:, lo]) / cnt - yf[..., sl]
        outs.append(jnp.einsum('bnc,cd->bnd', pooled.astype(y.dtype), w_pool[gi]))
    return jnp.concatenate(outs, axis=-1) * scale


def swiglu(h, wg, wu, wd):
    return (jax.nn.silu(h @ wg) * (h @ wu)) @ wd


def setup_inputs(seed: int = 0) -> dict:
    key = jax.random.key(seed)
    ks = jax.random.split(key, 24)
    f32 = jnp.float32

    def nrm(k, shape, s):
        return jax.random.normal(k, shape, f32) * s

    D, L = D_MODEL, DEPTH
    return {
        "x": nrm(ks[0], (BATCH, SEQ, D), 1.0),
        "c": nrm(ks[1], (BATCH, D), 1.0),
        "ctx": nrm(ks[2], (BATCH, CTX_LEN, D), 1.0),
        "c_ctx": nrm(ks[3], (D,), 1.0),
        "w_mod": nrm(ks[4], (L, D, 6 * D), 0.5 * D ** -0.5),
        "b_mod": nrm(ks[5], (L, 6 * D), 0.02),
        "g_mix": 1.0 + nrm(ks[6], (L, D), 0.05),
        "g_ffn": 1.0 + nrm(ks[7], (L, D), 0.05),
        "w_in": nrm(ks[8], (L, D, IN_TOTAL), D ** -0.5),
        "w_out": nrm(ks[9], (L, MIX_W, D), MIX_W ** -0.5),
        "a_q_gain": 1.0 + nrm(ks[10], (L, HEAD_DIM), 0.05),
        "a_k_gain": 1.0 + nrm(ks[11], (L, HEAD_DIM), 0.05),
        "a_sink": nrm(ks[12], (L, A_HEADS), 0.5),
        "b_v_gain": 1.0 + nrm(ks[13], (L, GROUP_W), 0.05),
        "b_w_s": nrm(ks[14], (L, B_GROUPS, B_CHUNK, B_CHUNK), B_CHUNK ** -0.5),
        "b_b_s": 1.0 + nrm(ks[15], (L, B_GROUPS, B_CHUNK), 0.02),
        "c_w_pool": nrm(ks[16], (L, len(C_POOLS), C_GROUP_W, C_GROUP_W), C_GROUP_W ** -0.5),
        "c_scale": 1.0 + nrm(ks[17], (L, GROUP_W), 0.1),
        "d_q_gain": 1.0 + nrm(ks[18], (L, HEAD_DIM), 0.05),
        "d_k_gain": 1.0 + nrm(ks[19], (L, HEAD_DIM), 0.05),
        "d_rpb": nrm(ks[20], (L, D_HEADS, 2 * D_WIN_ROWS - 1, 2 * D_WIN_COLS - 1), 0.1),
        "w_gate": nrm(ks[21], (L, D, FF_DIM), D ** -0.5),
        "w_up": nrm(ks[22], (L, D, FF_DIM), D ** -0.5),
        "w_down": nrm(ks[23], (L, FF_DIM, D), FF_DIM ** -0.5),
    }


def reference(x, c, ctx, c_ctx, w_mod, b_mod, g_mix, g_ffn, w_in, w_out, a_q_gain, a_k_gain, a_sink,
              b_v_gain, b_w_s, b_b_s, c_w_pool, c_scale, d_q_gain, d_k_gain, d_rpb, w_gate, w_up, w_down):
    s = x.shape[1]
    t = jnp.arange(s)
    row_pos = t // GRID_W
    col_pos = t % GRID_W
    xc = ctx
    c_act = jax.nn.silu(c)
    cc_act = jax.nn.silu(c_ctx)
    for l in range(DEPTH):
        last = l == DEPTH - 1
        mod = (c_act @ w_mod[l] + b_mod[l])[:, None, :]
        sh1, sc1, g1, sh2, sc2, g2 = jnp.split(mod, 6, axis=-1)
        modc = cc_act @ w_mod[l] + b_mod[l]
        csh1, csc1, cg1, csh2, csc2, cg2 = jnp.split(modc, 6, axis=-1)

        h = modulate(rms_norm(x, g_mix[l]), sh1, sc1)
        hc = modulate(rms_norm(xc, g_mix[l]), csh1, csc1)
        aq, ak, av, bu, bv, cin, dq, dk, dv = split_cols(h @ w_in[l])
        if last:
            cak, cav, cdk, cdv = [hc @ col_block(w_in[l], i) for i in (1, 2, 7, 8)]
        else:
            caq, cak, cav, cbu, cbv, ccin, cdq, cdk, cdv = split_cols(hc @ w_in[l])

        kc_a = rms_norm(heads(cak, A_KV_HEADS), a_k_gain[l])
        vc_a = heads(cav, A_KV_HEADS)
        kc_d = rms_norm(heads(cdk, D_HEADS), d_k_gain[l])
        vc_d = heads(cdv, D_HEADS)

        q_a = axial_rope(rms_norm(heads(aq, A_HEADS), a_q_gain[l]), row_pos, col_pos)
        k_a = axial_rope(rms_norm(heads(ak, A_KV_HEADS), a_k_gain[l]), row_pos, col_pos)
        out_a = window_attention(q_a, k_a, heads(av, A_KV_HEADS), kc_a, vc_a, a_sink[l])
        out_b = spatial_gating(bu, bv, b_v_gain[l], b_w_s[l], b_b_s[l])
        out_c = multiscale_pool(cin, c_w_pool[l], c_scale[l])
        out_d = neighbourhood_attention(rms_norm(heads(dq, D_HEADS), d_q_gain[l]),
                                        rms_norm(heads(dk, D_HEADS), d_k_gain[l]),
                                        heads(dv, D_HEADS), kc_d, vc_d, d_rpb[l])
        mix = jnp.concatenate([out_a, out_b, out_c, out_d], axis=-1) @ w_out[l]
        x = x + g1 * mix
        x = x + g2 * swiglu(modulate(rms_norm(x, g_ffn[l]), sh2, sc2), w_gate[l], w_up[l], w_down[l])

        if not last:
            oc_a = context_attention(rms_norm(heads(caq, A_HEADS), a_q_gain[l]), kc_a, vc_a, a_sink[l])
            oc_b = spatial_gating(cbu, cbv, b_v_gain[l], b_w_s[l], b_b_s[l])
            oc_c = multiscale_pool(ccin, c_w_pool[l], c_scale[l])
            oc_d = context_attention(rms_norm(heads(cdq, D_HEADS), d_q_gain[l]), kc_d, vc_d, None)
            mixc = jnp.concatenate([oc_a, oc_b, oc_c, oc_d], axis=-1) @ w_out[l]
            xc = xc + cg1 * mixc
            xc = xc + cg2 * swiglu(modulate(rms_norm(xc, g_ffn[l]), csh2, csc2), w_gate[l], w_up[l], w_down[l])
    return x
```

```python
import functools

import jax
import jax.numpy as jnp
import numpy as np
from jax import lax
from jax.experimental import pallas as pl
from jax.experimental.pallas import tpu as pltpu

F32 = jnp.float32
BF16 = jnp.bfloat16

D_MODEL = 1024
DEPTH = 2
GRID_W = 64
HEAD_DIM = 64
GROUP_W = 256
A_HEADS = 4
A_KV_HEADS = 2
A_WINDOW = 128
A_BLOCK = 128
B_CHUNK = 128
B_GROUPS = 4
B_GROUP_W = 64
C_POOLS = (2, 4, 8, 16)
C_GROUP_W = 64
D_HEADS = 4
D_WIN_ROWS = 8
D_WIN_COLS = 16
FF_DIM = 2816
ROPE_BASE = 10000.0
NORM_EPS = 1e-6
NEG_INF = -1e30
IN_TOTAL = 2048
OFF_AQ, OFF_AK, OFF_AV, OFF_BU, OFF_BV, OFF_C, OFF_DQ, OFF_DK, OFF_DV = (
    0, 256, 384, 512, 768, 1024, 1280, 1536, 1792)

MOD_ROWS = 16
MOD_TN = 1536
IN_TM = 256
MIX_TQ = 256
FFN_TM = 512
FFN_TF = 1408
POOL_HALO = 8
VMEM_LIMIT = 56 * 1024 * 1024


def _dot(a, b):
    return jnp.dot(a, b, preferred_element_type=F32)


def _dot_nt(a, b):
    return lax.dot_general(a, b, (((1,), (1,)), ((), ())), preferred_element_type=F32)


def _mod_kernel(c_ref, w_ref, b_ref, o_ref):
    cv = c_ref[...]
    act = cv * jax.nn.sigmoid(cv)
    o_ref[0] = _dot(act.astype(BF16), w_ref[0].astype(BF16)) + b_ref[0]


def _modulation(cvec, w_mod, b_mod):
    depth, d, n = w_mod.shape
    return pl.pallas_call(
        _mod_kernel,
        out_shape=jax.ShapeDtypeStruct((depth, MOD_ROWS, n), F32),
        grid=(depth, n // MOD_TN),
        in_specs=[
            pl.BlockSpec((MOD_ROWS, d), lambda l, j: (0, 0)),
            pl.BlockSpec((1, d, MOD_TN), lambda l, j: (l, 0, j)),
            pl.BlockSpec((1, 1, MOD_TN), lambda l, j: (l, 0, j)),
        ],
        out_specs=pl.BlockSpec((1, MOD_ROWS, MOD_TN), lambda l, j: (l, 0, j)),
        compiler_params=pltpu.CompilerParams(
            dimension_semantics=("arbitrary", "arbitrary"), vmem_limit_bytes=VMEM_LIMIT),
        name="modulation",
    )(cvec, w_mod, b_mod.reshape(depth, 1, n))


def _head_norm(t, seg, gain):
    sq = t * t
    hi = sq.astype(BF16)
    lo = (sq - hi.astype(F32)).astype(BF16)
    ms = _dot(hi, seg) + _dot(lo, seg)
    return t * lax.rsqrt(ms + NORM_EPS) * gain


def _rope(y, cos, sin_signed):
    w = y.shape[-1]
    up = pltpu.roll(y, w - 16, axis=1)
    dn = pltpu.roll(y, 16, axis=1)
    lane = lax.broadcasted_iota(jnp.int32, y.shape, 1)
    swapped = jnp.where((lane & 31) < 16, up, dn)
    return y * cos + swapped * sin_signed


def _in_kernel(x_ref, sh_ref, sc_ref, g_ref, w_ref, seg_ref, aqg_ref, akg_ref, dqg_ref, dkg_ref,
               cosq_ref, sinq_ref, cosk_ref, sink_ref, bvg_ref, ws_ref, bs_ref,
               qa_ref, ka_ref, va_ref, ob_ref, c_ref, qd_ref, kd_ref, vd_ref, *, use_rope):
    xt = x_ref[0]
    ms = jnp.mean(xt * xt, axis=-1, keepdims=True)
    h = xt * lax.rsqrt(ms + NORM_EPS) * g_ref[...]
    h = (h * (1.0 + sc_ref[0]) + sh_ref[0]).astype(BF16)

    def proj(off, width):
        return _dot(h, w_ref[:, off:off + width])

    seg = seg_ref[...]
    qa = _head_norm(proj(OFF_AQ, 256), seg, aqg_ref[...])
    ka = _head_norm(proj(OFF_AK, 128), seg[:128, :128], akg_ref[...])
    if use_rope:
        qa = _rope(qa, cosq_ref[...], sinq_ref[...])
        ka = _rope(ka, cosk_ref[...], sink_ref[...])
    scale = HEAD_DIM ** -0.5
    qa_ref[0] = (qa * scale).astype(BF16)
    ka_ref[0] = ka.astype(BF16)
    va_ref[0] = proj(OFF_AV, 128).astype(BF16)

    qd = _head_norm(proj(OFF_DQ, 256), seg, dqg_ref[...])
    qd_ref[0] = (qd * scale).astype(BF16)
    kd_ref[0] = _head_norm(proj(OFF_DK, 256), seg, dkg_ref[...]).astype(BF16)
    vd_ref[0] = proj(OFF_DV, 256).astype(BF16)

    c_ref[0] = proj(OFF_C, 256)

    u = jax.nn.gelu(proj(OFF_BU, 256))
    v = jax.nn.gelu(proj(OFF_BV, 256))
    mu = jnp.mean(v, axis=-1, keepdims=True)
    vc = v - mu
    var = jnp.mean(vc * vc, axis=-1, keepdims=True)
    vn = (vc * lax.rsqrt(var + NORM_EPS) * bvg_ref[...]).astype(BF16)
    lane = lax.broadcasted_iota(jnp.int32, (B_CHUNK, GROUP_W), 1)
    zero = jnp.zeros((B_CHUNK, GROUP_W), BF16)
    tm = xt.shape[0]
    for ch in range(tm // B_CHUNK):
        rows = slice(ch * B_CHUNK, (ch + 1) * B_CHUNK)
        vch = vn[rows]
        stacked = jnp.concatenate(
            [jnp.where((lane >= g * B_GROUP_W) & (lane < (g + 1) * B_GROUP_W), vch, zero)
             for g in range(B_GROUPS)], axis=0)
        z = _dot(ws_ref[...], stacked) + bs_ref[...]
        ob_ref[0, rows, :] = (u[rows] * z).astype(BF16)


def _in_projection(x, shift, scale, g_mix, w_in, seg, aqg, akg, dqg, dkg, tables, bvg, ws_cat, bs_t,
                   *, use_rope):
    b, t, d = x.shape
    cosq, sinq, cosk, sink = tables
    tm = IN_TM
    tok = lambda bi, i: (bi, i, 0)
    per_b = lambda bi, i: (bi, 0, 0)
    const2 = lambda bi, i: (0, 0)
    pos2 = lambda bi, i: (i, 0)
    out_shapes = (
        jax.ShapeDtypeStruct((b, t, 256), BF16),
        jax.ShapeDtypeStruct((b, t, 128), BF16),
        jax.ShapeDtypeStruct((b, t, 128), BF16),
        jax.ShapeDtypeStruct((b, t, 256), BF16),
        jax.ShapeDtypeStruct((b, t, 256), F32),
        jax.ShapeDtypeStruct((b, t, 256), BF16),
        jax.ShapeDtypeStruct((b, t, 256), BF16),
        jax.ShapeDtypeStruct((b, t, 256), BF16),
    )
    out_specs = tuple(pl.BlockSpec((1, tm, s.shape[-1]), tok) for s in out_shapes)
    return pl.pallas_call(
        functools.partial(_in_kernel, use_rope=use_rope),
        out_shape=out_shapes,
        grid=(b, t // tm),
        in_specs=[
            pl.BlockSpec((1, tm, d), tok),
            pl.BlockSpec((1, 1, d), per_b),
            pl.BlockSpec((1, 1, d), per_b),
            pl.BlockSpec((1, d), const2),
            pl.BlockSpec((d, IN_TOTAL), const2),
            pl.BlockSpec((256, 256), const2),
            pl.BlockSpec((1, 256), const2),
            pl.BlockSpec((1, 128), const2),
            pl.BlockSpec((1, 256), const2),
            pl.BlockSpec((1, 256), const2),
            pl.BlockSpec((tm, 256), pos2),
            pl.BlockSpec((tm, 256), pos2),
            pl.BlockSpec((tm, 128), pos2),
            pl.BlockSpec((tm, 128), pos2),
            pl.BlockSpec((1, 256), const2),
            pl.BlockSpec((B_CHUNK, B_GROUPS * B_CHUNK), const2),
            pl.BlockSpec((B_CHUNK, GROUP_W), const2),
        ],
        out_specs=out_specs,
        compiler_params=pltpu.CompilerParams(
            dimension_semantics=("arbitrary", "arbitrary"), vmem_limit_bytes=VMEM_LIMIT),
        name="in_projection_rope" if use_rope else "in_projection_ctx",
    )(x, shift, scale, g_mix, w_in, seg, aqg, akg, dqg, dkg, cosq, sinq, cosk, sink, bvg, ws_cat, bs_t)


def _softmax_pv(parts, extra_logit):
    m = None
    for s, _ in parts:
        pm = jnp.max(s, axis=-1, keepdims=True)
        m = pm if m is None else jnp.maximum(m, pm)
    if extra_logit is not None:
        m = jnp.maximum(m, extra_logit)
    denom = None
    acc = None
    for s, v in parts:
        e = jnp.exp(s - m)
        ps = jnp.sum(e, axis=-1, keepdims=True)
        denom = ps if denom is None else denom + ps
        pv = _dot(e.astype(BF16), v)
        acc = pv if acc is None else acc + pv
    if extra_logit is not None:
        denom = denom + jnp.exp(extra_logit - m)
    return acc / denom


def _pool_tile(c_ref, t0, n_tok, tq):
    y = c_ref[0, pl.ds(t0, tq), :]
    lo_start = pl.multiple_of(jnp.maximum(t0 - POOL_HALO, 0), POOL_HALO)
    hi_start = pl.multiple_of(jnp.minimum(t0 + tq, n_tok - POOL_HALO), POOL_HALO)
    lo = c_ref[0, pl.ds(lo_start, POOL_HALO), :]
    hi = c_ref[0, pl.ds(hi_start, POOL_HALO), :]
    lo = jnp.where(t0 > 0, lo, 0.0)
    hi = jnp.where(t0 + tq < n_tok, hi, 0.0)
    ypad = jnp.concatenate([lo, y, hi], axis=0)

    n = tq + 2 * POOL_HALO
    w2 = ypad + pltpu.roll(ypad, 1, axis=0)
    w4 = pltpu.roll(w2, 1, axis=0) + pltpu.roll(w2, n - 1, axis=0)
    w8 = pltpu.roll(w4, 2, axis=0) + pltpu.roll(w4, n - 2, axis=0)
    w16 = pltpu.roll(w8, 4, axis=0) + pltpu.roll(w8, n - 4, axis=0)
    sums = tuple(w[POOL_HALO:POOL_HALO + tq] for w in (w2, w4, w8, w16))

    lane = lax.broadcasted_iota(jnp.int32, (tq, GROUP_W), 1)
    pos = t0 + lax.broadcasted_iota(jnp.int32, (tq, GROUP_W), 0)
    pooled = None
    for gi, w in enumerate(C_POOLS):
        lo_i = jnp.clip(pos - w // 2, 0, n_tok)
        hi_i = jnp.clip(pos - w // 2 + w, 0, n_tok)
        mean = sums[gi] / (hi_i - lo_i).astype(F32)
        sel = (lane >= gi * C_GROUP_W) & (lane < (gi + 1) * C_GROUP_W)
        pooled = jnp.where(sel, mean, 0.0 if pooled is None else pooled)
    return pooled - y


def _mix_kernel(sink_ref, x_ref, g1_ref, qa_ref, ka_ref, va_ref, kca_ref, vca_ref, ob_ref, c_ref,
                wp_ref, cs_ref, qd_ref, kd_ref, vd_ref, kcd_ref, vcd_ref, bias_ref, wo_ref,
                o_ref, oa_scr, od_scr, *, local, n_tok):
    tq = x_ref.shape[1]
    ti = pl.program_id(1)
    t0 = pl.multiple_of(ti * tq, tq)

    for blk in range(tq // A_BLOCK):
        rows = slice(blk * A_BLOCK, (blk + 1) * A_BLOCK)
        qblk = qa_ref[0, rows, :]
        if local:
            nblk = ti * (tq // A_BLOCK) + blk
            kstart = pl.multiple_of(
                jnp.clip((nblk - 1) * A_BLOCK, 0, n_tok - 3 * A_BLOCK), A_BLOCK)
            kwin = ka_ref[0, pl.ds(kstart, 3 * A_BLOCK), :]
            vwin = va_ref[0, pl.ds(kstart, 3 * A_BLOCK), :]
            r = lax.broadcasted_iota(jnp.int32, (2 * A_BLOCK, 3 * A_BLOCK), 0)
            cidx = lax.broadcasted_iota(jnp.int32, (2 * A_BLOCK, 3 * A_BLOCK), 1)
            qpos = nblk * A_BLOCK + (r & (A_BLOCK - 1))
            valid = jnp.abs(kstart + cidx - qpos) <= A_WINDOW
        for kv in range(A_KV_HEADS):
            h0 = 2 * kv
            q2 = jnp.concatenate([qblk[:, h0 * 64:(h0 + 1) * 64],
                                  qblk[:, (h0 + 1) * 64:(h0 + 2) * 64]], axis=0)
            hsl = slice(kv * 64, (kv + 1) * 64)
            row = lax.broadcasted_iota(jnp.int32, (2 * A_BLOCK, 1), 0)
            sink = jnp.where(row < A_BLOCK, sink_ref[h0], sink_ref[h0 + 1])
            parts = []
            if local:
                s_loc = jnp.where(valid, _dot_nt(q2, kwin[:, hsl]), NEG_INF)
                parts.append((s_loc, vwin[:, hsl]))
            parts.append((_dot_nt(q2, kca_ref[0, :, hsl]), vca_ref[0, :, hsl]))
            o2 = _softmax_pv(parts, sink)
            oa_scr[rows, h0 * 64:(h0 + 1) * 64] = o2[:A_BLOCK]
            oa_scr[rows, (h0 + 1) * 64:(h0 + 2) * 64] = o2[A_BLOCK:]

    if local:
        n_rows = n_tok // GRID_W

        def row_body(i, carry):
            r = ti * (tq // GRID_W) + i
            kstart = pl.multiple_of(
                jnp.clip(r - D_WIN_ROWS // 2, 0, n_rows - D_WIN_ROWS) * GRID_W, GRID_W)
            q0 = pl.multiple_of(i * GRID_W, GRID_W)
            qrow = qd_ref[0, pl.ds(q0, GRID_W), :]
            kwin = kd_ref[0, pl.ds(kstart, D_WIN_ROWS * GRID_W), :]
            vwin = vd_ref[0, pl.ds(kstart, D_WIN_ROWS * GRID_W), :]
            for hd in range(D_HEADS):
                hsl = slice(hd * 64, (hd + 1) * 64)
                qh = qrow[:, hsl]
                s_loc = _dot_nt(qh, kwin[:, hsl]) + bias_ref[0, hd, pl.ds(q0, GRID_W), :]
                s_ctx = _dot_nt(qh, kcd_ref[0, :, hsl])
                o = _softmax_pv([(s_loc, vwin[:, hsl]), (s_ctx, vcd_ref[0, :, hsl])], None)
                od_scr[pl.ds(q0, GRID_W), hsl] = o
            return carry

        lax.fori_loop(0, tq // GRID_W, row_body, 0)
    else:
        qall = qd_ref[0]
        for hd in range(D_HEADS):
            hsl = slice(hd * 64, (hd + 1) * 64)
            s_ctx = _dot_nt(qall[:, hsl], kcd_ref[0, :, hsl])
            o = _softmax_pv([(s_ctx, vcd_ref[0, :, hsl])], None)
            od_scr[:, hsl] = o

    pooled = _pool_tile(c_ref, t0, n_tok, tq)
    out_c = _dot(pooled.astype(BF16), wp_ref[...]) * cs_ref[...]

    mix = (_dot(oa_scr[...].astype(BF16), wo_ref[0:256, :]) + _dot(ob_ref[0], wo_ref[256:512, :])
           + _dot(out_c.astype(BF16), wo_ref[512:768, :])
           + _dot(od_scr[...].astype(BF16), wo_ref[768:1024, :]))
    o_ref[0] = x_ref[0] + g1_ref[0] * mix


def _mixer(x, g1, qa, ka, va, kca, vca, ob, cin, wpool, cscale, qd, kd, vd, kcd, vcd, bias, w_out,
           sink, *, local):
    b, t, d = x.shape
    lc = kca.shape[1]
    tq = MIX_TQ
    n_tiles = t // tq
    tok = lambda bi, i: (bi, i, 0)
    per_b = lambda bi, i: (bi, 0, 0)
    const2 = lambda bi, i: (0, 0)

    def bias_map(bi, i):
        kind = jnp.where(i == 0, 1, jnp.where(i == n_tiles - 1, 2, 0))
        return (kind, 0, 0, 0)

    return pl.pallas_call(
        functools.partial(_mix_kernel, local=local, n_tok=t),
        out_shape=jax.ShapeDtypeStruct((b, t, d), F32),
        grid=(b, n_tiles),
        in_specs=[
            pl.BlockSpec(memory_space=pltpu.SMEM),
            pl.BlockSpec((1, tq, d), tok),
            pl.BlockSpec((1, 1, d), per_b),
            pl.BlockSpec((1, tq, 256), tok),
            pl.BlockSpec((1, t, 128), per_b),
            pl.BlockSpec((1, t, 128), per_b),
            pl.BlockSpec((1, lc, 128), per_b),
            pl.BlockSpec((1, lc, 128), per_b),
            pl.BlockSpec((1, tq, 256), tok),
            pl.BlockSpec((1, t, 256), per_b),
            pl.BlockSpec((256, 256), const2),
            pl.BlockSpec((1, 256), const2),
            pl.BlockSpec((1, tq, 256), tok),
            pl.BlockSpec((1, t, 256), per_b),
            pl.BlockSpec((1, t, 256), per_b),
            pl.BlockSpec((1, lc, 256), per_b),
            pl.BlockSpec((1, lc, 256), per_b),
            pl.BlockSpec((1, D_HEADS, tq, D_WIN_ROWS * GRID_W), bias_map),
            pl.BlockSpec((d, d), const2),
        ],
        out_specs=pl.BlockSpec((1, tq, d), tok),
        scratch_shapes=[pltpu.VMEM((tq, 256), F32), pltpu.VMEM((tq, 256), F32)],
        compiler_params=pltpu.CompilerParams(
            dimension_semantics=("arbitrary", "arbitrary"), vmem_limit_bytes=VMEM_LIMIT),
        name="mixer_latent" if local else "mixer_ctx",
    )(sink, x, g1, qa, ka, va, kca, vca, ob, cin, wpool, cscale, qd, kd, vd, kcd, vcd, bias, w_out)


def _ffn_kernel(x_ref, sh_ref, sc_ref, g2_ref, gn_ref, wg_ref, wu_ref, wd_ref, o_ref, h_scr, acc_scr):
    j = pl.program_id(2)

    @pl.when(j == 0)
    def _():
        xt = x_ref[0]
        ms = jnp.mean(xt * xt, axis=-1, keepdims=True)
        h = xt * lax.rsqrt(ms + NORM_EPS) * gn_ref[...]
        h_scr[...] = (h * (1.0 + sc_ref[0]) + sh_ref[0]).astype(BF16)
        acc_scr[...] = jnp.zeros_like(acc_scr)

    h = h_scr[...]
    gate = _dot(h, wg_ref[...])
    up = _dot(h, wu_ref[...])
    act = (gate * jax.nn.sigmoid(gate) * up).astype(BF16)
    acc_scr[...] += _dot(act, wd_ref[...])

    @pl.when(j == pl.num_programs(2) - 1)
    def _():
        o_ref[0] = x_ref[0] + g2_ref[0] * acc_scr[...]


def _ffn(x, shift, scale, g2, g_ffn, wg, wu, wd):
    b, t, d = x.shape
    tm = min(FFN_TM, t)
    tf = FFN_TF
    tok = lambda bi, i, j: (bi, i, 0)
    per_b = lambda bi, i, j: (bi, 0, 0)
    return pl.pallas_call(
        _ffn_kernel,
        out_shape=jax.ShapeDtypeStruct((b, t, d), F32),
        grid=(b, t // tm, FF_DIM // tf),
        in_specs=[
            pl.BlockSpec((1, tm, d), tok),
            pl.BlockSpec((1, 1, d), per_b),
            pl.BlockSpec((1, 1, d), per_b),
            pl.BlockSpec((1, 1, d), per_b),
            pl.BlockSpec((1, d), lambda bi, i, j: (0, 0)),
            pl.BlockSpec((d, tf), lambda bi, i, j: (0, j)),
            pl.BlockSpec((d, tf), lambda bi, i, j: (0, j)),
            pl.BlockSpec((tf, d), lambda bi, i, j: (j, 0)),
        ],
        out_specs=pl.BlockSpec((1, tm, d), tok),
        scratch_shapes=[pltpu.VMEM((tm, d), BF16), pltpu.VMEM((tm, d), F32)],
        compiler_params=pltpu.CompilerParams(
            dimension_semantics=("arbitrary", "arbitrary", "arbitrary"),
            vmem_limit_bytes=VMEM_LIMIT),
        name="ffn",
    )(x, shift, scale, g2, g_ffn, wg, wu, wd)


def _rope_tables(s):
    t = np.arange(s)
    half = 16
    inv = jnp.power(ROPE_BASE, -jnp.arange(half, dtype=F32) / half)
    ang_r = jnp.asarray(t // GRID_W, F32)[:, None] * inv[None, :]
    ang_c = jnp.asarray(t % GRID_W, F32)[:, None] * inv[None, :]
    cos = jnp.concatenate([jnp.cos(ang_r)] * 2 + [jnp.cos(ang_c)] * 2, axis=-1)
    sin = jnp.concatenate([-jnp.sin(ang_r), jnp.sin(ang_r), -jnp.sin(ang_c), jnp.sin(ang_c)], axis=-1)
    return cos, sin


def _neighbour_bias(rpb, tq):
    rows_per_tile = tq // GRID_W
    cidx = np.arange(GRID_W)
    col_start = np.clip(cidx - D_WIN_COLS // 2, 0, GRID_W - D_WIN_COLS)
    col_ok = (cidx[None, :] >= col_start[:, None]) & (cidx[None, :] < col_start[:, None] + D_WIN_COLS)
    coff = np.clip(cidx[None, :] - cidx[:, None], -(D_WIN_COLS - 1), D_WIN_COLS - 1) + (D_WIN_COLS - 1)
    i = np.arange(rows_per_tile)[:, None]
    j = np.arange(D_WIN_ROWS)[None, :]
    half = D_WIN_ROWS // 2
    droff = np.stack([np.broadcast_to(j - half, (rows_per_tile, D_WIN_ROWS)),
                      j - i,
                      j - D_WIN_ROWS + rows_per_tile - i], axis=0)
    roff = droff + (D_WIN_ROWS - 1)
    bias = rpb[:, roff[:, :, :, None, None], coff[None, None, None, :, :]].astype(F32)
    bias = jnp.where(col_ok[None, None, None, None, :, :], bias, NEG_INF)
    bias = jnp.transpose(bias, (1, 0, 2, 4, 3, 5))
    return bias.reshape(3, D_HEADS, tq, D_WIN_ROWS * GRID_W)


def kernel(x, c, ctx, c_ctx, w_mod, b_mod, g_mix, g_ffn, w_in, w_out, a_q_gain, a_k_gain, a_sink,
           b_v_gain, b_w_s, b_b_s, c_w_pool, c_scale, d_q_gain, d_k_gain, d_rpb, w_gate, w_up, w_down):
    bsz, s, d = x.shape
    lc = ctx.shape[1]
    n_rows = s // GRID_W
    assert s % MIX_TQ == 0 and n_rows >= 2 * D_WIN_ROWS and MIX_TQ // GRID_W == D_WIN_ROWS // 2

    cvec = jnp.zeros((MOD_ROWS, d), F32).at[:bsz].set(c).at[bsz].set(c_ctx)
    mod = _modulation(cvec, w_mod, b_mod)

    seg = jnp.asarray(np.kron(np.eye(4), np.full((64, 64), 1.0 / 64)), BF16)
    cos1, sin1 = _rope_tables(s)
    tables_lat = (jnp.tile(cos1, (1, 4)), jnp.tile(sin1, (1, 4)), jnp.tile(cos1, (1, 2)), jnp.tile(sin1, (1, 2)))
    tables_ctx = tuple(jnp.zeros((lc, w), F32) for w in (256, 256, 128, 128))

    xc = ctx
    for l in range(DEPTH):
        last = l == DEPTH - 1
        ml = mod[l, :bsz].reshape(bsz, 1, 6, d)
        sh1, sc1, g1, sh2, sc2, g2 = [ml[:, :, i, :] for i in range(6)]
        mc = jnp.broadcast_to(mod[l, bsz].reshape(1, 1, 6, d), (bsz, 1, 6, d))
        csh1, csc1, cg1, csh2, csc2, cg2 = [mc[:, :, i, :] for i in range(6)]

        w_in_l = w_in[l].astype(BF16)
        w_out_l = w_out[l].astype(BF16)
        wg_l, wu_l, wd_l = w_gate[l].astype(BF16), w_up[l].astype(BF16), w_down[l].astype(BF16)
        gmix = g_mix[l].reshape(1, d)
        gffn = g_ffn[l].reshape(1, d)
        aqg = jnp.tile(a_q_gain[l], 4).reshape(1, 256)
        akg = jnp.tile(a_k_gain[l], 2).reshape(1, 128)
        dqg = jnp.tile(d_q_gain[l], 4).reshape(1, 256)
        dkg = jnp.tile(d_k_gain[l], 4).reshape(1, 256)
        bvg = b_v_gain[l].reshape(1, 256)
        ws_cat = jnp.transpose(b_w_s[l], (1, 0, 2)).reshape(B_CHUNK, B_GROUPS * B_CHUNK).astype(BF16)
        bs_t = jnp.repeat(b_b_s[l].T, B_GROUP_W, axis=1)
        wpool = jax.scipy.linalg.block_diag(*[c_w_pool[l, g] for g in range(4)]).astype(BF16)
        cscale = c_scale[l].reshape(1, 256)
        bias = _neighbour_bias(d_rpb[l], MIX_TQ)
        sink = a_sink[l]

        common = (seg, aqg, akg, dqg, dkg)
        lat = _in_projection(x, sh1, sc1, gmix, w_in_l, *common, tables_lat, bvg, ws_cat, bs_t, use_rope=True)
        con = _in_projection(xc, csh1, csc1, gmix, w_in_l, *common, tables_ctx, bvg, ws_cat, bs_t, use_rope=False)
        qa, ka, va, ob, cin, qd, kd, vd = lat
        cqa, cka, cva, cob, ccin, cqd, ckd, cvd = con

        x = _mixer(x, g1, qa, ka, va, cka, cva, ob, cin, wpool, cscale, qd, kd, vd, ckd, cvd, bias,
                   w_out_l, sink, local=True)
        x = _ffn(x, sh2, sc2, g2, gffn, wg_l, wu_l, wd_l)
        if not last:
            xc = _mixer(xc, cg1, cqa, cka, cva, cka, cva, cob, ccin, wpool, cscale, cqd, ckd, cvd,
                        ckd, cvd, bias, w_out_l, sink, local=False)
            xc = _ffn(xc, csh2, csc2, cg2, gffn, wg_l, wu_l, wd_l)
    return x
```

```python
import functools

import jax
import jax.numpy as jnp
import numpy as np
from jax import lax
from jax.experimental import pallas as pl
from jax.experimental.pallas import tpu as pltpu

F32 = jnp.float32
BF16 = jnp.bfloat16

D_MODEL = 1024
DEPTH = 2
GRID_W = 64
HEAD_DIM = 64
GROUP_W = 256
A_HEADS = 4
A_KV_HEADS = 2
A_WINDOW = 128
A_BLOCK = 128
B_CHUNK = 128
B_GROUPS = 4
B_GROUP_W = 64
C_POOLS = (2, 4, 8, 16)
C_GROUP_W = 64
D_HEADS = 4
D_WIN_ROWS = 8
D_WIN_COLS = 16
FF_DIM = 2816
ROPE_BASE = 10000.0
NORM_EPS = 1e-6
NEG_INF = -1e30
IN_TOTAL = 2048
OFF_AQ, OFF_AK, OFF_AV, OFF_BU, OFF_BV, OFF_C, OFF_DQ, OFF_DK, OFF_DV = (
    0, 256, 384, 512, 768, 1024, 1280, 1536, 1792)

MOD_ROWS = 16
MOD_TN = 1536
IN_TM = 256
MIX_TQ = 256
FFN_TM = 512
FFN_TF = 1408
POOL_HALO = 8
VMEM_LIMIT = 56 * 1024 * 1024


def _dot(a, b):
    return jnp.dot(a, b, preferred_element_type=F32)


def _dot_nt(a, b):
    return lax.dot_general(a, b, (((1,), (1,)), ((), ())), preferred_element_type=F32)


def _mod_kernel(c_ref, w_ref, b_ref, o_ref):
    cv = c_ref[...]
    act = cv * jax.nn.sigmoid(cv)
    o_ref[0] = _dot(act.astype(BF16), w_ref[0].astype(BF16)) + b_ref[0]


def _modulation(cvec, w_mod, b_mod):
    depth, d, n = w_mod.shape
    return pl.pallas_call(
        _mod_kernel,
        out_shape=jax.ShapeDtypeStruct((depth, MOD_ROWS, n), F32),
        grid=(depth, n // MOD_TN),
        in_specs=[
            pl.BlockSpec((MOD_ROWS, d), lambda l, j: (0, 0)),
            pl.BlockSpec((1, d, MOD_TN), lambda l, j: (l, 0, j)),
            pl.BlockSpec((1, 1, MOD_TN), lambda l, j: (l, 0, j)),
        ],
        out_specs=pl.BlockSpec((1, MOD_ROWS, MOD_TN), lambda l, j: (l, 0, j)),
        compiler_params=pltpu.CompilerParams(
            dimension_semantics=("arbitrary", "arbitrary"), vmem_limit_bytes=VMEM_LIMIT),
        name="modulation",
    )(cvec, w_mod, b_mod.reshape(depth, 1, n))


def _head_norm(t, seg, gain):
    sq = t * t
    hi = sq.astype(BF16)
    lo = (sq - hi.astype(F32)).astype(BF16)
    ms = _dot(hi, seg) + _dot(lo, seg)
    return t * lax.rsqrt(ms + NORM_EPS) * gain


def _rope(y, cos, sin_signed):
    w = y.shape[-1]
    up = pltpu.roll(y, w - 16, axis=1)
    dn = pltpu.roll(y, 16, axis=1)
    lane = lax.broadcasted_iota(jnp.int32, y.shape, 1)
    swapped = jnp.where((lane & 31) < 16, up, dn)
    return y * cos + swapped * sin_signed


def _in_kernel(x_ref, sh_ref, sc_ref, g_ref, w_ref, seg_ref, aqg_ref, akg_ref, dqg_ref, dkg_ref,
               cosq_ref, sinq_ref, cosk_ref, sink_ref, bvg_ref, ws_ref, bs_ref,
               qa_ref, ka_ref, va_ref, ob_ref, c_ref, qd_ref, kd_ref, vd_ref, *, use_rope):
    xt = x_ref[0]
    ms = jnp.mean(xt * xt, axis=-1, keepdims=True)
    h = xt * lax.rsqrt(ms + NORM_EPS) * g_ref[...]
    h = (h * (1.0 + sc_ref[0]) + sh_ref[0]).astype(BF16)

    def proj(off, width):
        return _dot(h, w_ref[:, off:off + width])

    seg = seg_ref[...]
    qa = _head_norm(proj(OFF_AQ, 256), seg, aqg_ref[...])
    ka = _head_norm(proj(OFF_AK, 128), seg[:128, :128], akg_ref[...])
    if use_rope:
        qa = _rope(qa, cosq_ref[...], sinq_ref[...])
        ka = _rope(ka, cosk_ref[...], sink_ref[...])
    scale = HEAD_DIM ** -0.5
    qa_ref[0] = (qa * scale).astype(BF16)
    ka_ref[0] = ka.astype(BF16)
    va_ref[0] = proj(OFF_AV, 128).astype(BF16)

    qd = _head_norm(proj(OFF_DQ, 256), seg, dqg_ref[...])
    qd_ref[0] = (qd * scale).astype(BF16)
    kd_ref[0] = _head_norm(proj(OFF_DK, 256), seg, dkg_ref[...]).astype(BF16)
    vd_ref[0] = proj(OFF_DV, 256).astype(BF16)

    c_ref[0] = proj(OFF_C, 256)

    u = jax.nn.gelu(proj(OFF_BU, 256))
    v = jax.nn.gelu(proj(OFF_BV, 256))
    mu = jnp.mean(v, axis=-1, keepdims=True)
    vc = v - mu
    var = jnp.mean(vc * vc, axis=-1, keepdims=True)
    vn = (vc * lax.rsqrt(var + NORM_EPS) * bvg_ref[...]).astype(BF16)
    lane = lax.broadcasted_iota(jnp.int32, (B_CHUNK, GROUP_W), 1)
    zero = jnp.zeros((B_CHUNK, GROUP_W), BF16)
    tm = xt.shape[0]
    for ch in range(tm // B_CHUNK):
        rows = slice(ch * B_CHUNK, (ch + 1) * B_CHUNK)
        vch = vn[rows]
        stacked = jnp.concatenate(
            [jnp.where((lane >= g * B_GROUP_W) & (lane < (g + 1) * B_GROUP_W), vch, zero)
             for g in range(B_GROUPS)], axis=0)
        z = _dot(ws_ref[...], stacked) + bs_ref[...]
        ob_ref[0, rows, :] = (u[rows] * z).astype(BF16)


def _in_projection(x, shift, scale, g_mix, w_in, seg, aqg, akg, dqg, dkg, tables, bvg, ws_cat, bs_t,
                   *, use_rope):
    b, t, d = x.shape
    cosq, sinq, cosk, sink = tables
    tm = IN_TM
    tok = lambda bi, i: (bi, i, 0)
    per_b = lambda bi, i: (bi, 0, 0)
    const2 = lambda bi, i: (0, 0)
    pos2 = lambda bi, i: (i, 0)
    out_shapes = (
        jax.ShapeDtypeStruct((b, t, 256), BF16),
        jax.ShapeDtypeStruct((b, t, 128), BF16),
        jax.ShapeDtypeStruct((b, t, 128), BF16),
        jax.ShapeDtypeStruct((b, t, 256), BF16),
        jax.ShapeDtypeStruct((b, t, 256), F32),
        jax.ShapeDtypeStruct((b, t, 256), BF16),
        jax.ShapeDtypeStruct((b, t, 256), BF16),
        jax.ShapeDtypeStruct((b, t, 256), BF16),
    )
    out_specs = tuple(pl.BlockSpec((1, tm, s.shape[-1]), tok) for s in out_shapes)
    return pl.pallas_call(
        functools.partial(_in_kernel, use_rope=use_rope),
        out_shape=out_shapes,
        grid=(b, t // tm),
        in_specs=[
            pl.BlockSpec((1, tm, d), tok),
            pl.BlockSpec((1, 1, d), per_b),
            pl.BlockSpec((1, 1, d), per_b),
            pl.BlockSpec((1, d), const2),
            pl.BlockSpec((d, IN_TOTAL), const2),
            pl.BlockSpec((256, 256), const2),
            pl.BlockSpec((1, 256), const2),
            pl.BlockSpec((1, 128), const2),
            pl.BlockSpec((1, 256), const2),
            pl.BlockSpec((1, 256), const2),
            pl.BlockSpec((tm, 256), pos2),
            pl.BlockSpec((tm, 256), pos2),
            pl.BlockSpec((tm, 128), pos2),
            pl.BlockSpec((tm, 128), pos2),
            pl.BlockSpec((1, 256), const2),
            pl.BlockSpec((B_CHUNK, B_GROUPS * B_CHUNK), const2),
            pl.BlockSpec((B_CHUNK, GROUP_W), const2),
        ],
        out_specs=out_specs,
        compiler_params=pltpu.CompilerParams(
            dimension_semantics=("arbitrary", "arbitrary"), vmem_limit_bytes=VMEM_LIMIT),
        name="in_projection_rope" if use_rope else "in_projection_ctx",
    )(x, shift, scale, g_mix, w_in, seg, aqg, akg, dqg, dkg, cosq, sinq, cosk, sink, bvg, ws_cat, bs_t)


def _softmax_pv(parts, extra_logit):
    m = None
    for s, _ in parts:
        pm = jnp.max(s, axis=-1, keepdims=True)
        m = pm if m is None else jnp.maximum(m, pm)
    if extra_logit is not None:
        m = jnp.maximum(m, extra_logit)
    denom = None
    acc = None
    for s, v in parts:
        e = jnp.exp(s - m)
        ps = jnp.sum(e, axis=-1, keepdims=True)
        denom = ps if denom is None else denom + ps
        pv = _dot(e.astype(BF16), v)
        acc = pv if acc is None else acc + pv
    if extra_logit is not None:
        denom = denom + jnp.exp(extra_logit - m)
    return acc / denom


def _pool_tile(c_ref, t0, n_tok, tq):
    y = c_ref[0, pl.ds(t0, tq), :]
    lo_start = pl.multiple_of(jnp.maximum(t0 - POOL_HALO, 0), POOL_HALO)
    hi_start = pl.multiple_of(jnp.minimum(t0 + tq, n_tok - POOL_HALO), POOL_HALO)
    lo = c_ref[0, pl.ds(lo_start, POOL_HALO), :]
    hi = c_ref[0, pl.ds(hi_start, POOL_HALO), :]
    lo = jnp.where(t0 > 0, lo, 0.0)
    hi = jnp.where(t0 + tq < n_tok, hi, 0.0)
    ypad = jnp.concatenate([lo, y, hi], axis=0)

    n = tq + 2 * POOL_HALO
    w2 = ypad + pltpu.roll(ypad, 1, axis=0)
    w4 = pltpu.roll(w2, 1, axis=0) + pltpu.roll(w2, n - 1, axis=0)
    w8 = pltpu.roll(w4, 2, axis=0) + pltpu.roll(w4, n - 2, axis=0)
    w16 = pltpu.roll(w8, 4, axis=0) + pltpu.roll(w8, n - 4, axis=0)
    sums = tuple(w[POOL_HALO:POOL_HALO + tq] for w in (w2, w4, w8, w16))

    lane = lax.broadcasted_iota(jnp.int32, (tq, GROUP_W), 1)
    pos = t0 + lax.broadcasted_iota(jnp.int32, (tq, GROUP_W), 0)
    pooled = None
    for gi, w in enumerate(C_POOLS):
        lo_i = jnp.clip(pos - w // 2, 0, n_tok)
        hi_i = jnp.clip(pos - w // 2 + w, 0, n_tok)
        mean = sums[gi] / (hi_i - lo_i).astype(F32)
        sel = (lane >= gi * C_GROUP_W) & (lane < (gi + 1) * C_GROUP_W)
        pooled = jnp.where(sel, mean, 0.0 if pooled is None else pooled)
    return pooled - y


def _mix_kernel(sink_ref, x_ref, g1_ref, qa_ref, ka_ref, va_ref, kca_ref, vca_ref, ob_ref, c_ref,
                wp_ref, cs_ref, qd_ref, kd_ref, vd_ref, kcd_ref, vcd_ref, bias_ref, wo_ref,
                o_ref, oa_scr, od_scr, *, local, n_tok):
    tq = x_ref.shape[1]
    ti = pl.program_id(1)
    t0 = pl.multiple_of(ti * tq, tq)

    for blk in range(tq // A_BLOCK):
        rows = slice(blk * A_BLOCK, (blk + 1) * A_BLOCK)
        qblk = qa_ref[0, rows, :]
        if local:
            nblk = ti * (tq // A_BLOCK) + blk
            kstart = pl.multiple_of(
                jnp.clip((nblk - 1) * A_BLOCK, 0, n_tok - 3 * A_BLOCK), A_BLOCK)
            kwin = ka_ref[0, pl.ds(kstart, 3 * A_BLOCK), :]
            vwin = va_ref[0, pl.ds(kstart, 3 * A_BLOCK), :]
            r = lax.broadcasted_iota(jnp.int32, (2 * A_BLOCK, 3 * A_BLOCK), 0)
            cidx = lax.broadcasted_iota(jnp.int32, (2 * A_BLOCK, 3 * A_BLOCK), 1)
            qpos = nblk * A_BLOCK + (r & (A_BLOCK - 1))
            valid = jnp.abs(kstart + cidx - qpos) <= A_WINDOW
        for kv in range(A_KV_HEADS):
            h0 = 2 * kv
            q2 = jnp.concatenate([qblk[:, h0 * 64:(h0 + 1) * 64],
                                  qblk[:, (h0 + 1) * 64:(h0 + 2) * 64]], axis=0)
            hsl = slice(kv * 64, (kv + 1) * 64)
            row = lax.broadcasted_iota(jnp.int32, (2 * A_BLOCK, 1), 0)
            sink = jnp.where(row < A_BLOCK, sink_ref[h0], sink_ref[h0 + 1])
            parts = []
            if local:
                s_loc = jnp.where(valid, _dot_nt(q2, kwin[:, hsl]), NEG_INF)
                parts.append((s_loc, vwin[:, hsl]))
            parts.append((_dot_nt(q2, kca_ref[0, :, hsl]), vca_ref[0, :, hsl]))
            o2 = _softmax_pv(parts, sink)
            oa_scr[rows, h0 * 64:(h0 + 1) * 64] = o2[:A_BLOCK]
            oa_scr[rows, (h0 + 1) * 64:(h0 + 2) * 64] = o2[A_BLOCK:]

    if local:
        n_rows = n_tok // GRID_W

        def row_body(i, carry):
            r = ti * (tq // GRID_W) + i
            kstart = pl.multiple_of(
                jnp.clip(r - D_WIN_ROWS // 2, 0, n_rows - D_WIN_ROWS) * GRID_W, GRID_W)
            q0 = pl.multiple_of(i * GRID_W, GRID_W)
            qrow = qd_ref[0, pl.ds(q0, GRID_W), :]
            kwin = kd_ref[0, pl.ds(kstart, D_WIN_ROWS * GRID_W), :]
            vwin = vd_ref[0, pl.ds(kstart, D_WIN_ROWS * GRID_W), :]
            for hd in range(D_HEADS):
                hsl = slice(hd * 64, (hd + 1) * 64)
                qh = qrow[:, hsl]
                s_loc = _dot_nt(qh, kwin[:, hsl]) + bias_ref[0, hd, pl.ds(q0, GRID_W), :]
                s_ctx = _dot_nt(qh, kcd_ref[0, :, hsl])
                o = _softmax_pv([(s_loc, vwin[:, hsl]), (s_ctx, vcd_ref[0, :, hsl])], None)
                od_scr[pl.ds(q0, GRID_W), hsl] = o
            return carry

        lax.fori_loop(0, tq // GRID_W, row_body, 0)
    else:
        qall = qd_ref[0]
        for hd in range(D_HEADS):
            hsl = slice(hd * 64, (hd + 1) * 64)
            s_ctx = _dot_nt(qall[:, hsl], kcd_ref[0, :, hsl])
            o = _softmax_pv([(s_ctx, vcd_ref[0, :, hsl])], None)
            od_scr[:, hsl] = o

    pooled = _pool_tile(c_ref, t0, n_tok, tq)
    out_c = _dot(pooled.astype(BF16), wp_ref[...]) * cs_ref[...]

    mix = (_dot(oa_scr[...].astype(BF16), wo_ref[0:256, :]) + _dot(ob_ref[0], wo_ref[256:512, :])
           + _dot(out_c.astype(BF16), wo_ref[512:768, :])
           + _dot(od_scr[...].astype(BF16), wo_ref[768:1024, :]))
    o_ref[0] = x_ref[0] + g1_ref[0] * mix


def _mixer(x, g1, qa, ka, va, kca, vca, ob, cin, wpool, cscale, qd, kd, vd, kcd, vcd, bias, w_out,
           sink, *, local):
    b, t, d = x.shape
    lc = kca.shape[1]
    tq = MIX_TQ
    n_tiles = t // tq
    tok = lambda bi, i: (bi, i, 0)
    per_b = lambda bi, i: (bi, 0, 0)
    const2 = lambda bi, i: (0, 0)

    def bias_map(bi, i):
        kind = jnp.where(i == 0, 1, jnp.where(i == n_tiles - 1, 2, 0))
        return (kind, 0, 0, 0)

    return pl.pallas_call(
        functools.partial(_mix_kernel, local=local, n_tok=t),
        out_shape=jax.ShapeDtypeStruct((b, t, d), F32),
        grid=(b, n_tiles),
        in_specs=[
            pl.BlockSpec(memory_space=pltpu.SMEM),
            pl.BlockSpec((1, tq, d), tok),
            pl.BlockSpec((1, 1, d), per_b),
            pl.BlockSpec((1, tq, 256), tok),
            pl.BlockSpec((1, t, 128), per_b),
            pl.BlockSpec((1, t, 128), per_b),
            pl.BlockSpec((1, lc, 128), per_b),
            pl.BlockSpec((1, lc, 128), per_b),
            pl.BlockSpec((1, tq, 256), tok),
            pl.BlockSpec((1, t, 256), per_b),
            pl.BlockSpec((256, 256), const2),
            pl.BlockSpec((1, 256), const2),
            pl.BlockSpec((1, tq, 256), tok),
            pl.BlockSpec((1, t, 256), per_b),
            pl.BlockSpec((1, t, 256), per_b),
            pl.BlockSpec((1, lc, 256), per_b),
            pl.BlockSpec((1, lc, 256), per_b),
            pl.BlockSpec((1, D_HEADS, tq, D_WIN_ROWS * GRID_W), bias_map),
            pl.BlockSpec((d, d), const2),
        ],
        out_specs=pl.BlockSpec((1, tq, d), tok),
        scratch_shapes=[pltpu.VMEM((tq, 256), F32), pltpu.VMEM((tq, 256), F32)],
        compiler_params=pltpu.CompilerParams(
            dimension_semantics=("arbitrary", "arbitrary"), vmem_limit_bytes=VMEM_LIMIT),
        name="mixer_latent" if local else "mixer_ctx",
    )(sink, x, g1, qa, ka, va, kca, vca, ob, cin, wpool, cscale, qd, kd, vd, kcd, vcd, bias, w_out)


def _ffn_kernel(x_ref, sh_ref, sc_ref, g2_ref, gn_ref, wg_ref, wu_ref, wd_ref, o_ref, h_scr, acc_scr):
    j = pl.program_id(2)

    @pl.when(j == 0)
    def _():
        xt = x_ref[0]
        ms = jnp.mean(xt * xt, axis=-1, keepdims=True)
        h = xt * lax.rsqrt(ms + NORM_EPS) * gn_ref[...]
        h_scr[...] = (h * (1.0 + sc_ref[0]) + sh_ref[0]).astype(BF16)
        acc_scr[...] = jnp.zeros_like(acc_scr)

    h = h_scr[...]
    gate = _dot(h, wg_ref[...])
    up = _dot(h, wu_ref[...])
    act = (gate * jax.nn.sigmoid(gate) * up).astype(BF16)
    acc_scr[...] += _dot(act, wd_ref[...])

    @pl.when(j == pl.num_programs(2) - 1)
    def _():
        o_ref[0] = x_ref[0] + g2_ref[0] * acc_scr[...]


def _ffn(x, shift, scale, g2, g_ffn, wg, wu, wd):
    b, t, d = x.shape
    tm = min(FFN_TM, t)
    tf = FFN_TF
    tok = lambda bi, i, j: (bi, i, 0)
    per_b = lambda bi, i, j: (bi, 0, 0)
    return pl.pallas_call(
        _ffn_kernel,
        out_shape=jax.ShapeDtypeStruct((b, t, d), F32),
        grid=(b, t // tm, FF_DIM // tf),
        in_specs=[
            pl.BlockSpec((1, tm, d), tok),
            pl.BlockSpec((1, 1, d), per_b),
            pl.BlockSpec((1, 1, d), per_b),
            pl.BlockSpec((1, 1, d), per_b),
            pl.BlockSpec((1, d), lambda bi, i, j: (0, 0)),
            pl.BlockSpec((d, tf), lambda bi, i, j: (0, j)),
            pl.BlockSpec((d, tf), lambda bi, i, j: (0, j)),
            pl.BlockSpec((tf, d), lambda bi, i, j: (j, 0)),
        ],
        out_specs=pl.BlockSpec((1, tm, d), tok),
        scratch_shapes=[pltpu.VMEM((tm, d), BF16), pltpu.VMEM((tm, d), F32)],
        compiler_params=pltpu.CompilerParams(
            dimension_semantics=("arbitrary", "arbitrary", "arbitrary"),
            vmem_limit_bytes=VMEM_LIMIT),
        name="ffn",
    )(x, shift, scale, g2, g_ffn, wg, wu, wd)


def _rope_tables(s):
    t = np.arange(s)
    half = 16
    inv = jnp.power(ROPE_BASE, -jnp.arange(half, dtype=F32) / half)
    ang_r = jnp.asarray(t // GRID_W, F32)[:, None] * inv[None, :]
    ang_c = jnp.asarray(t % GRID_W, F32)[:, None] * inv[None, :]
    cos = jnp.concatenate([jnp.cos(ang_r)] * 2 + [jnp.cos(ang_c)] * 2, axis=-1)
    sin = jnp.concatenate([-jnp.sin(ang_r), jnp.sin(ang_r), -jnp.sin(ang_c), jnp.sin(ang_c)], axis=-1)
    return cos, sin


def _neighbour_bias(rpb, tq):
    rows_per_tile = tq // GRID_W
    cidx = np.arange(GRID_W)
    col_start = np.clip(cidx - D_WIN_COLS // 2, 0, GRID_W - D_WIN_COLS)
    col_ok = (cidx[None, :] >= col_start[:, None]) & (cidx[None, :] < col_start[:, None] + D_WIN_COLS)
    edge = GRID_W - D_WIN_COLS
    ext = jnp.concatenate([jnp.repeat(rpb[:, :, :1], edge, axis=2), rpb.astype(F32),
                           jnp.repeat(rpb[:, :, -1:], edge, axis=2)], axis=2)
    toeplitz = jnp.stack([ext[:, :, GRID_W - 1 - cq:2 * GRID_W - 1 - cq] for cq in range(GRID_W)], axis=2)
    toeplitz = jnp.where(col_ok[None, None], toeplitz, NEG_INF)
    half = D_WIN_ROWS // 2
    first_roff = ([half - 1] * rows_per_tile,
                  [D_WIN_ROWS - 1 - i for i in range(rows_per_tile)],
                  [rows_per_tile - 1 - i for i in range(rows_per_tile)])
    kinds = []
    for firsts in first_roff:
        slabs = [jnp.transpose(toeplitz[:, f:f + D_WIN_ROWS], (0, 2, 1, 3)) for f in firsts]
        kinds.append(jnp.stack(slabs, axis=1))
    bias = jnp.stack(kinds, axis=0)
    return bias.reshape(3, D_HEADS, tq, D_WIN_ROWS * GRID_W)


def kernel(x, c, ctx, c_ctx, w_mod, b_mod, g_mix, g_ffn, w_in, w_out, a_q_gain, a_k_gain, a_sink,
           b_v_gain, b_w_s, b_b_s, c_w_pool, c_scale, d_q_gain, d_k_gain, d_rpb, w_gate, w_up, w_down):
    bsz, s, d = x.shape
    lc = ctx.shape[1]
    n_rows = s // GRID_W
    assert s % MIX_TQ == 0 and n_rows >= 2 * D_WIN_ROWS and MIX_TQ // GRID_W == D_WIN_ROWS // 2

    cvec = jnp.zeros((MOD_ROWS, d), F32).at[:bsz].set(c).at[bsz].set(c_ctx)
    mod = _modulation(cvec, w_mod, b_mod)

    seg = jnp.asarray(np.kron(np.eye(4), np.full((64, 64), 1.0 / 64)), BF16)
    cos1, sin1 = _rope_tables(s)
    tables_lat = (jnp.tile(cos1, (1, 4)), jnp.tile(sin1, (1, 4)), jnp.tile(cos1, (1, 2)), jnp.tile(sin1, (1, 2)))
    tables_ctx = tuple(jnp.zeros((lc, w), F32) for w in (256, 256, 128, 128))

    xc = ctx
    for l in range(DEPTH):
        last = l == DEPTH - 1
        ml = mod[l, :bsz].reshape(bsz, 1, 6, d)
        sh1, sc1, g1, sh2, sc2, g2 = [ml[:, :, i, :] for i in range(6)]
        mc = jnp.broadcast_to(mod[l, bsz].reshape(1, 1, 6, d), (bsz, 1, 6, d))
        csh1, csc1, cg1, csh2, csc2, cg2 = [mc[:, :, i, :] for i in range(6)]

        w_in_l = w_in[l].astype(BF16)
        w_out_l = w_out[l].astype(BF16)
        wg_l, wu_l, wd_l = w_gate[l].astype(BF16), w_up[l].astype(BF16), w_down[l].astype(BF16)
        gmix = g_mix[l].reshape(1, d)
        gffn = g_ffn[l].reshape(1, d)
        aqg = jnp.tile(a_q_gain[l], 4).reshape(1, 256)
        akg = jnp.tile(a_k_gain[l], 2).reshape(1, 128)
        dqg = jnp.tile(d_q_gain[l], 4).reshape(1, 256)
        dkg = jnp.tile(d_k_gain[l], 4).reshape(1, 256)
        bvg = b_v_gain[l].reshape(1, 256)
        ws_cat = jnp.transpose(b_w_s[l], (1, 0, 2)).reshape(B_CHUNK, B_GROUPS * B_CHUNK).astype(BF16)
        bs_t = jnp.repeat(b_b_s[l].T, B_GROUP_W, axis=1)
        wpool = jax.scipy.linalg.block_diag(*[c_w_pool[l, g] for g in range(4)]).astype(BF16)
        cscale = c_scale[l].reshape(1, 256)
        bias = _neighbour_bias(d_rpb[l], MIX_TQ)
        sink = a_sink[l]

        common = (seg, aqg, akg, dqg, dkg)
        lat = _in_projection(x, sh1, sc1, gmix, w_in_l, *common, tables_lat, bvg, ws_cat, bs_t, use_rope=True)
        con = _in_projection(xc, csh1, csc1, gmix, w_in_l, *common, tables_ctx, bvg, ws_cat, bs_t, use_rope=False)
        qa, ka, va, ob, cin, qd, kd, vd = lat
        cqa, cka, cva, cob, ccin, cqd, ckd, cvd = con

        x = _mixer(x, g1, qa, ka, va, cka, cva, ob, cin, wpool, cscale, qd, kd, vd, ckd, cvd, bias,
                   w_out_l, sink, local=True)
        x = _ffn(x, sh2, sc2, g2, gffn, wg_l, wu_l, wd_l)
        if not last:
            xc = _mixer(xc, cg1, cqa, cka, cva, cka, cva, cob, ccin, wpool, cscale, cqd, ckd, cvd,
                        ckd, cvd, bias, w_out_l, sink, local=False)
            xc = _ffn(xc, csh2, csc2, cg2, gffn, wg_l, wu_l, wd_l)
    return x
```

```python
import functools

import jax
import jax.numpy as jnp
import numpy as np
from jax import lax
from jax.experimental import pallas as pl
from jax.experimental.pallas import tpu as pltpu

F32 = jnp.float32
BF16 = jnp.bfloat16

D_MODEL = 1024
DEPTH = 2
GRID_W = 64
HEAD_DIM = 64
GROUP_W = 256
A_HEADS = 4
A_KV_HEADS = 2
A_WINDOW = 128
A_BLOCK = 128
B_CHUNK = 128
B_GROUPS = 4
B_GROUP_W = 64
C_POOLS = (2, 4, 8, 16)
C_GROUP_W = 64
D_HEADS = 4
D_WIN_ROWS = 8
D_WIN_COLS = 16
FF_DIM = 2816
ROPE_BASE = 10000.0
NORM_EPS = 1e-6
NEG_INF = -1e30
IN_TOTAL = 2048
OFF_AQ, OFF_AK, OFF_AV, OFF_BU, OFF_BV, OFF_C, OFF_DQ, OFF_DK, OFF_DV = (
    0, 256, 384, 512, 768, 1024, 1280, 1536, 1792)

MOD_ROWS = 16
MOD_TN = 1536
IN_TM = 256
MIX_TQ = 256
FFN_TM = 512
FFN_TF = 1408
POOL_HALO = 8
VMEM_LIMIT = 56 * 1024 * 1024


def _dot(a, b):
    return jnp.dot(a, b, preferred_element_type=F32)


def _dot_nt(a, b):
    return lax.dot_general(a, b, (((1,), (1,)), ((), ())), preferred_element_type=F32)


def _mod_kernel(c_ref, w_ref, b_ref, o_ref):
    cv = c_ref[...]
    act = cv * jax.nn.sigmoid(cv)
    o_ref[0] = _dot(act.astype(BF16), w_ref[0].astype(BF16)) + b_ref[0]


def _modulation(cvec, w_mod, b_mod):
    depth, d, n = w_mod.shape
    return pl.pallas_call(
        _mod_kernel,
        out_shape=jax.ShapeDtypeStruct((depth, MOD_ROWS, n), F32),
        grid=(depth, n // MOD_TN),
        in_specs=[
            pl.BlockSpec((MOD_ROWS, d), lambda l, j: (0, 0)),
            pl.BlockSpec((1, d, MOD_TN), lambda l, j: (l, 0, j)),
            pl.BlockSpec((1, 1, MOD_TN), lambda l, j: (l, 0, j)),
        ],
        out_specs=pl.BlockSpec((1, MOD_ROWS, MOD_TN), lambda l, j: (l, 0, j)),
        compiler_params=pltpu.CompilerParams(
            dimension_semantics=("arbitrary", "arbitrary"), vmem_limit_bytes=VMEM_LIMIT),
        name="modulation",
    )(cvec, w_mod, b_mod.reshape(depth, 1, n))


def _head_norm(t, seg, gain):
    sq = t * t
    hi = sq.astype(BF16)
    lo = (sq - hi.astype(F32)).astype(BF16)
    ms = _dot(hi, seg) + _dot(lo, seg)
    return t * lax.rsqrt(ms + NORM_EPS) * gain


def _rope(y, cos, sin_signed):
    w = y.shape[-1]
    up = pltpu.roll(y, w - 16, axis=1)
    dn = pltpu.roll(y, 16, axis=1)
    lane = lax.broadcasted_iota(jnp.int32, y.shape, 1)
    swapped = jnp.where((lane & 31) < 16, up, dn)
    return y * cos + swapped * sin_signed


def _in_kernel(x_ref, sh_ref, sc_ref, g_ref, w_ref, seg_ref, aqg_ref, akg_ref, dqg_ref, dkg_ref,
               cosq_ref, sinq_ref, cosk_ref, sink_ref, bvg_ref, ws_ref, bs_ref,
               qa_ref, ka_ref, va_ref, ob_ref, c_ref, qd_ref, kd_ref, vd_ref, v_scr, *, use_rope):
    xt = x_ref[0]
    ms = jnp.mean(xt * xt, axis=-1, keepdims=True)
    h = xt * lax.rsqrt(ms + NORM_EPS) * g_ref[...]
    h = (h * (1.0 + sc_ref[0]) + sh_ref[0]).astype(BF16)

    def proj(off, width):
        return _dot(h, w_ref[:, off:off + width])

    seg = seg_ref[...]
    qa = _head_norm(proj(OFF_AQ, 256), seg, aqg_ref[...])
    ka = _head_norm(proj(OFF_AK, 128), seg[:128, :128], akg_ref[...])
    if use_rope:
        qa = _rope(qa, cosq_ref[...], sinq_ref[...])
        ka = _rope(ka, cosk_ref[...], sink_ref[...])
    scale = HEAD_DIM ** -0.5
    qa_ref[0] = (qa * scale).astype(BF16)
    ka_ref[0] = ka.astype(BF16)
    v_scr[:, 0:128] = proj(OFF_AV, 128)
    v_scr[:, 128:384] = proj(OFF_DV, 256)
    vat = v_scr[:, 0:128].T.astype(BF16)
    for j in range(vat.shape[1] // A_BLOCK):
        va_ref[0, j] = vat[:, j * A_BLOCK:(j + 1) * A_BLOCK]

    qd = _head_norm(proj(OFF_DQ, 256), seg, dqg_ref[...])
    qd_ref[0] = (qd * scale).astype(BF16)
    kd_ref[0] = _head_norm(proj(OFF_DK, 256), seg, dkg_ref[...]).astype(BF16)
    vd_ref[0, 0] = v_scr[:, 128:384].T.astype(BF16)

    c_ref[0] = proj(OFF_C, 256)

    u = jax.nn.gelu(proj(OFF_BU, 256))
    v = jax.nn.gelu(proj(OFF_BV, 256))
    mu = jnp.mean(v, axis=-1, keepdims=True)
    vc = v - mu
    var = jnp.mean(vc * vc, axis=-1, keepdims=True)
    vn = (vc * lax.rsqrt(var + NORM_EPS) * bvg_ref[...]).astype(BF16)
    lane = lax.broadcasted_iota(jnp.int32, (B_CHUNK, GROUP_W), 1)
    zero = jnp.zeros((B_CHUNK, GROUP_W), BF16)
    tm = xt.shape[0]
    for ch in range(tm // B_CHUNK):
        rows = slice(ch * B_CHUNK, (ch + 1) * B_CHUNK)
        vch = vn[rows]
        stacked = jnp.concatenate(
            [jnp.where((lane >= g * B_GROUP_W) & (lane < (g + 1) * B_GROUP_W), vch, zero)
             for g in range(B_GROUPS)], axis=0)
        z = _dot(ws_ref[...], stacked) + bs_ref[...]
        ob_ref[0, rows, :] = (u[rows] * z).astype(BF16)


def _in_projection(x, shift, scale, g_mix, w_in, seg, aqg, akg, dqg, dkg, tables, bvg, ws_cat, bs_t,
                   *, use_rope):
    b, t, d = x.shape
    cosq, sinq, cosk, sink = tables
    tm = IN_TM
    tok = lambda bi, i: (bi, i, 0)
    per_b = lambda bi, i: (bi, 0, 0)
    const2 = lambda bi, i: (0, 0)
    pos2 = lambda bi, i: (i, 0)
    assert tm == MIX_TQ and tm % A_BLOCK == 0
    out_shapes = (
        jax.ShapeDtypeStruct((b, t, 256), BF16),
        jax.ShapeDtypeStruct((b, t, 128), BF16),
        jax.ShapeDtypeStruct((b, t // A_BLOCK, 128, A_BLOCK), BF16),
        jax.ShapeDtypeStruct((b, t, 256), BF16),
        jax.ShapeDtypeStruct((b, t, 256), F32),
        jax.ShapeDtypeStruct((b, t, 256), BF16),
        jax.ShapeDtypeStruct((b, t, 256), BF16),
        jax.ShapeDtypeStruct((b, t // tm, 256, tm), BF16),
    )
    blk4 = lambda bi, i: (bi, i, 0, 0)
    out_specs = tuple(
        pl.BlockSpec((1, tm // A_BLOCK, 128, A_BLOCK), blk4) if k == 2 else
        pl.BlockSpec((1, 1, 256, tm), blk4) if k == 7 else
        pl.BlockSpec((1, tm, s.shape[-1]), tok)
        for k, s in enumerate(out_shapes))
    return pl.pallas_call(
        functools.partial(_in_kernel, use_rope=use_rope),
        out_shape=out_shapes,
        grid=(b, t // tm),
        in_specs=[
            pl.BlockSpec((1, tm, d), tok),
            pl.BlockSpec((1, 1, d), per_b),
            pl.BlockSpec((1, 1, d), per_b),
            pl.BlockSpec((1, d), const2),
            pl.BlockSpec((d, IN_TOTAL), const2),
            pl.BlockSpec((256, 256), const2),
            pl.BlockSpec((1, 256), const2),
            pl.BlockSpec((1, 128), const2),
            pl.BlockSpec((1, 256), const2),
            pl.BlockSpec((1, 256), const2),
            pl.BlockSpec((tm, 256), pos2),
            pl.BlockSpec((tm, 256), pos2),
            pl.BlockSpec((tm, 128), pos2),
            pl.BlockSpec((tm, 128), pos2),
            pl.BlockSpec((1, 256), const2),
            pl.BlockSpec((B_CHUNK, B_GROUPS * B_CHUNK), const2),
            pl.BlockSpec((B_CHUNK, GROUP_W), const2),
        ],
        out_specs=out_specs,
        scratch_shapes=[pltpu.VMEM((tm, 384), F32)],
        compiler_params=pltpu.CompilerParams(
            dimension_semantics=("arbitrary", "arbitrary"), vmem_limit_bytes=VMEM_LIMIT),
        name="in_projection_rope" if use_rope else "in_projection_ctx",
    )(x, shift, scale, g_mix, w_in, seg, aqg, akg, dqg, dkg, cosq, sinq, cosk, sink, bvg, ws_cat, bs_t)


def _softmax_pv_t(parts, extra_logit):
    m = None
    for s, _ in parts:
        pm = jnp.max(s, axis=0, keepdims=True)
        m = pm if m is None else jnp.maximum(m, pm)
    if extra_logit is not None:
        m = jnp.maximum(m, extra_logit)
    denom = None
    acc = None
    for s, vt in parts:
        e = jnp.exp(s - m)
        ps = jnp.sum(e, axis=0, keepdims=True)
        denom = ps if denom is None else denom + ps
        pv = _dot(vt, e.astype(BF16))
        acc = pv if acc is None else acc + pv
    if extra_logit is not None:
        denom = denom + jnp.exp(extra_logit - m)
    return acc / denom


def _head_lanes(q, head, width=HEAD_DIM):
    lane = lax.broadcasted_iota(jnp.int32, q.shape, 1)
    return jnp.where((lane >= head * width) & (lane < (head + 1) * width), q, jnp.zeros_like(q))


def _pool_tile(c_ref, t0, n_tok, tq):
    y = c_ref[0, pl.ds(t0, tq), :]
    lo_start = pl.multiple_of(jnp.maximum(t0 - POOL_HALO, 0), POOL_HALO)
    hi_start = pl.multiple_of(jnp.minimum(t0 + tq, n_tok - POOL_HALO), POOL_HALO)
    lo = c_ref[0, pl.ds(lo_start, POOL_HALO), :]
    hi = c_ref[0, pl.ds(hi_start, POOL_HALO), :]
    lo = jnp.where(t0 > 0, lo, 0.0)
    hi = jnp.where(t0 + tq < n_tok, hi, 0.0)
    ypad = jnp.concatenate([lo, y, hi], axis=0)

    n = tq + 2 * POOL_HALO
    w2 = ypad + pltpu.roll(ypad, 1, axis=0)
    w4 = pltpu.roll(w2, 1, axis=0) + pltpu.roll(w2, n - 1, axis=0)
    w8 = pltpu.roll(w4, 2, axis=0) + pltpu.roll(w4, n - 2, axis=0)
    w16 = pltpu.roll(w8, 4, axis=0) + pltpu.roll(w8, n - 4, axis=0)
    sums = tuple(w[POOL_HALO:POOL_HALO + tq] for w in (w2, w4, w8, w16))

    lane = lax.broadcasted_iota(jnp.int32, (tq, GROUP_W), 1)
    pos = t0 + lax.broadcasted_iota(jnp.int32, (tq, GROUP_W), 0)
    pooled = None
    for gi, w in enumerate(C_POOLS):
        lo_i = jnp.clip(pos - w // 2, 0, n_tok)
        hi_i = jnp.clip(pos - w // 2 + w, 0, n_tok)
        mean = sums[gi] / (hi_i - lo_i).astype(F32)
        sel = (lane >= gi * C_GROUP_W) & (lane < (gi + 1) * C_GROUP_W)
        pooled = jnp.where(sel, mean, 0.0 if pooled is None else pooled)
    return pooled - y


def _mix_kernel(sink_ref, x_ref, g1_ref, qa_ref, ka_ref, va_ref, kca_ref, vca_ref, ob_ref, c_ref,
                wp_ref, cs_ref, qd_ref, kd_ref, vd_ref, kcd_ref, vcd_ref, bias_ref, wo_ref,
                o_ref, oa_scr, od_scr, mixa_scr, mixd_scr, *, local, n_tok):
    tq = x_ref.shape[1]
    ti = pl.program_id(1)
    t0 = pl.multiple_of(ti * tq, tq)

    n_ctx_blk = vca_ref.shape[1]
    for blk in range(tq // A_BLOCK):
        cols = slice(blk * A_BLOCK, (blk + 1) * A_BLOCK)
        qblk = qa_ref[0, cols, :]
        qcat = jnp.concatenate([qblk[:, :128], qblk[:, 128:]], axis=0)
        if local:
            nblk = ti * (tq // A_BLOCK) + blk
            b0 = jnp.clip(nblk - 1, 0, n_tok // A_BLOCK - 3)
            kstart = pl.multiple_of(b0 * A_BLOCK, A_BLOCK)
            kwin = ka_ref[0, pl.ds(kstart, 3 * A_BLOCK), :]
            kk = lax.broadcasted_iota(jnp.int32, (3 * A_BLOCK, 2 * A_BLOCK), 0)
            qq = lax.broadcasted_iota(jnp.int32, (3 * A_BLOCK, 2 * A_BLOCK), 1)
            valid = jnp.abs(kk - (qq & (A_BLOCK - 1)) + (kstart - nblk * A_BLOCK)) <= A_WINDOW
        for kv in range(A_KV_HEADS):
            qm = _head_lanes(qcat, kv)
            rows = slice(kv * HEAD_DIM, (kv + 1) * HEAD_DIM)
            col = lax.broadcasted_iota(jnp.int32, (1, 2 * A_BLOCK), 1)
            sink = jnp.where(col < A_BLOCK, sink_ref[2 * kv], sink_ref[2 * kv + 1])
            parts = []
            if local:
                s_loc = jnp.where(valid, _dot_nt(kwin, qm), NEG_INF)
                vt = jnp.concatenate([va_ref[0, b0 + j, rows, :] for j in range(3)], axis=1)
                parts.append((s_loc, vt))
            vct = jnp.concatenate([vca_ref[0, j, rows, :] for j in range(n_ctx_blk)], axis=1)
            parts.append((_dot_nt(kca_ref[0], qm), vct))
            o2 = _softmax_pv_t(parts, sink)
            oa_scr[(2 * kv) * 64:(2 * kv + 1) * 64, cols] = o2[:, :A_BLOCK]
            oa_scr[(2 * kv + 1) * 64:(2 * kv + 2) * 64, cols] = o2[:, A_BLOCK:]

    qall = qd_ref[0]
    if local:
        d0 = jnp.clip(ti - 1, 0, n_tok // tq - 3)
        kwin_d = kd_ref[0, pl.ds(pl.multiple_of(d0 * tq, tq), 3 * tq), :]
    for hd in range(D_HEADS):
        qm = _head_lanes(qall, hd)
        rows = slice(hd * HEAD_DIM, (hd + 1) * HEAD_DIM)
        parts = []
        if local:
            s_loc = _dot_nt(kwin_d, qm) + bias_ref[0, hd]
            vt = jnp.concatenate([vd_ref[0, d0 + j, rows, :] for j in range(3)], axis=1)
            parts.append((s_loc, vt))
        parts.append((_dot_nt(kcd_ref[0], qm), vcd_ref[0, 0, rows, :]))
        od_scr[rows, :] = _softmax_pv_t(parts, None)

    pooled = _pool_tile(c_ref, t0, n_tok, tq)
    out_c = _dot(pooled.astype(BF16), wp_ref[...]) * cs_ref[...]

    mixa_scr[...] = oa_scr[...].T.astype(BF16)
    mixd_scr[...] = od_scr[...].T.astype(BF16)
    mix = (_dot(mixa_scr[...], wo_ref[0:256, :]) + _dot(ob_ref[0], wo_ref[256:512, :])
           + _dot(out_c.astype(BF16), wo_ref[512:768, :]) + _dot(mixd_scr[...], wo_ref[768:1024, :]))
    o_ref[0] = x_ref[0] + g1_ref[0] * mix


def _mixer(x, g1, qa, ka, va, kca, vca, ob, cin, wpool, cscale, qd, kd, vd, kcd, vcd, bias, w_out,
           sink, *, local):
    b, t, d = x.shape
    lc = kca.shape[1]
    tq = MIX_TQ
    n_tiles = t // tq
    tok = lambda bi, i: (bi, i, 0)
    per_b = lambda bi, i: (bi, 0, 0)
    per_b4 = lambda bi, i: (bi, 0, 0, 0)
    const2 = lambda bi, i: (0, 0)

    def bias_map(bi, i):
        kind = jnp.where(i == 0, 1, jnp.where(i == n_tiles - 1, 2, 0))
        return (kind, 0, 0, 0)

    return pl.pallas_call(
        functools.partial(_mix_kernel, local=local, n_tok=t),
        out_shape=jax.ShapeDtypeStruct((b, t, d), F32),
        grid=(b, n_tiles),
        in_specs=[
            pl.BlockSpec(memory_space=pltpu.SMEM),
            pl.BlockSpec((1, tq, d), tok),
            pl.BlockSpec((1, 1, d), per_b),
            pl.BlockSpec((1, tq, 256), tok),
            pl.BlockSpec((1, t, 128), per_b),
            pl.BlockSpec((1, t // A_BLOCK, 128, A_BLOCK), per_b4),
            pl.BlockSpec((1, lc, 128), per_b),
            pl.BlockSpec((1, lc // A_BLOCK, 128, A_BLOCK), per_b4),
            pl.BlockSpec((1, tq, 256), tok),
            pl.BlockSpec((1, t, 256), per_b),
            pl.BlockSpec((256, 256), const2),
            pl.BlockSpec((1, 256), const2),
            pl.BlockSpec((1, tq, 256), tok),
            pl.BlockSpec((1, t, 256), per_b),
            pl.BlockSpec((1, t // tq, 256, tq), per_b4),
            pl.BlockSpec((1, lc, 256), per_b),
            pl.BlockSpec((1, lc // tq, 256, tq), per_b4),
            pl.BlockSpec((1, D_HEADS, 3 * tq, tq), bias_map),
            pl.BlockSpec((d, d), const2),
        ],
        out_specs=pl.BlockSpec((1, tq, d), tok),
        scratch_shapes=[pltpu.VMEM((GROUP_W, tq), F32), pltpu.VMEM((GROUP_W, tq), F32),
                        pltpu.VMEM((tq, GROUP_W), BF16), pltpu.VMEM((tq, GROUP_W), BF16)],
        compiler_params=pltpu.CompilerParams(
            dimension_semantics=("arbitrary", "arbitrary"), vmem_limit_bytes=VMEM_LIMIT),
        name="mixer_latent" if local else "mixer_ctx",
    )(sink, x, g1, qa, ka, va, kca, vca, ob, cin, wpool, cscale, qd, kd, vd, kcd, vcd, bias, w_out)


def _ffn_kernel(x_ref, sh_ref, sc_ref, g2_ref, gn_ref, wg_ref, wu_ref, wd_ref, o_ref, h_scr, acc_scr):
    j = pl.program_id(2)

    @pl.when(j == 0)
    def _():
        xt = x_ref[0]
        ms = jnp.mean(xt * xt, axis=-1, keepdims=True)
        h = xt * lax.rsqrt(ms + NORM_EPS) * gn_ref[...]
        h_scr[...] = (h * (1.0 + sc_ref[0]) + sh_ref[0]).astype(BF16)
        acc_scr[...] = jnp.zeros_like(acc_scr)

    h = h_scr[...]
    gate = _dot(h, wg_ref[...])
    up = _dot(h, wu_ref[...])
    act = (gate * jax.nn.sigmoid(gate) * up).astype(BF16)
    acc_scr[...] += _dot(act, wd_ref[...])

    @pl.when(j == pl.num_programs(2) - 1)
    def _():
        o_ref[0] = x_ref[0] + g2_ref[0] * acc_scr[...]


def _ffn(x, shift, scale, g2, g_ffn, wg, wu, wd):
    b, t, d = x.shape
    tm = min(FFN_TM, t)
    tf = FFN_TF
    tok = lambda bi, i, j: (bi, i, 0)
    per_b = lambda bi, i, j: (bi, 0, 0)
    return pl.pallas_call(
        _ffn_kernel,
        out_shape=jax.ShapeDtypeStruct((b, t, d), F32),
        grid=(b, t // tm, FF_DIM // tf),
        in_specs=[
            pl.BlockSpec((1, tm, d), tok),
            pl.BlockSpec((1, 1, d), per_b),
            pl.BlockSpec((1, 1, d), per_b),
            pl.BlockSpec((1, 1, d), per_b),
            pl.BlockSpec((1, d), lambda bi, i, j: (0, 0)),
            pl.BlockSpec((d, tf), lambda bi, i, j: (0, j)),
            pl.BlockSpec((d, tf), lambda bi, i, j: (0, j)),
            pl.BlockSpec((tf, d), lambda bi, i, j: (j, 0)),
        ],
        out_specs=pl.BlockSpec((1, tm, d), tok),
        scratch_shapes=[pltpu.VMEM((tm, d), BF16), pltpu.VMEM((tm, d), F32)],
        compiler_params=pltpu.CompilerParams(
            dimension_semantics=("arbitrary", "arbitrary", "arbitrary"),
            vmem_limit_bytes=VMEM_LIMIT),
        name="ffn",
    )(x, shift, scale, g2, g_ffn, wg, wu, wd)


def _rope_tables(s):
    t = np.arange(s)
    half = 16
    inv = jnp.power(ROPE_BASE, -jnp.arange(half, dtype=F32) / half)
    ang_r = jnp.asarray(t // GRID_W, F32)[:, None] * inv[None, :]
    ang_c = jnp.asarray(t % GRID_W, F32)[:, None] * inv[None, :]
    cos = jnp.concatenate([jnp.cos(ang_r)] * 2 + [jnp.cos(ang_c)] * 2, axis=-1)
    sin = jnp.concatenate([-jnp.sin(ang_r), jnp.sin(ang_r), -jnp.sin(ang_c), jnp.sin(ang_c)], axis=-1)
    return cos, sin


def _neighbour_bias(rpb, tq, n_rows):
    rows_per_tile = tq // GRID_W
    cidx = np.arange(GRID_W)
    col_start = np.clip(cidx - D_WIN_COLS // 2, 0, GRID_W - D_WIN_COLS)
    col_ok = (cidx[None, :] >= col_start[:, None]) & (cidx[None, :] < col_start[:, None] + D_WIN_COLS)
    edge = GRID_W - D_WIN_COLS
    ext = jnp.concatenate([jnp.repeat(rpb[:, :, :1], edge, axis=2), rpb.astype(F32),
                           jnp.repeat(rpb[:, :, -1:], edge, axis=2)], axis=2)
    toeplitz = jnp.stack([ext[:, :, GRID_W - 1 - cq:2 * GRID_W - 1 - cq] for cq in range(GRID_W)], axis=2)
    toeplitz = jnp.where(col_ok[None, None], toeplitz, NEG_INF)
    toeplitz_t = jnp.swapaxes(toeplitz, 2, 3)
    masked = jnp.full((D_HEADS, GRID_W, GRID_W), NEG_INF, F32)
    win_rows = 3 * rows_per_tile
    kinds = []
    for r0, ws in ((2 * rows_per_tile, rows_per_tile), (0, 0), (n_rows - rows_per_tile, n_rows - win_rows)):
        cols = []
        for i in range(rows_per_tile):
            r = r0 + i
            rs = min(max(r - D_WIN_ROWS // 2, 0), n_rows - D_WIN_ROWS)
            blocks = [toeplitz_t[:, ws + jj - r + D_WIN_ROWS - 1] if rs <= ws + jj < rs + D_WIN_ROWS
                      else masked for jj in range(win_rows)]
            cols.append(jnp.concatenate(blocks, axis=1))
        kinds.append(jnp.concatenate(cols, axis=2))
    return jnp.stack(kinds, axis=0)


def kernel(x, c, ctx, c_ctx, w_mod, b_mod, g_mix, g_ffn, w_in, w_out, a_q_gain, a_k_gain, a_sink,
           b_v_gain, b_w_s, b_b_s, c_w_pool, c_scale, d_q_gain, d_k_gain, d_rpb, w_gate, w_up, w_down):
    bsz, s, d = x.shape
    lc = ctx.shape[1]
    n_rows = s // GRID_W
    assert s % MIX_TQ == 0 and n_rows >= 2 * D_WIN_ROWS and MIX_TQ // GRID_W == D_WIN_ROWS // 2

    cvec = jnp.zeros((MOD_ROWS, d), F32).at[:bsz].set(c).at[bsz].set(c_ctx)
    mod = _modulation(cvec, w_mod, b_mod)

    seg = jnp.asarray(np.kron(np.eye(4), np.full((64, 64), 1.0 / 64)), BF16)
    cos1, sin1 = _rope_tables(s)
    tables_lat = (jnp.tile(cos1, (1, 4)), jnp.tile(sin1, (1, 4)), jnp.tile(cos1, (1, 2)), jnp.tile(sin1, (1, 2)))
    tables_ctx = tuple(jnp.zeros((lc, w), F32) for w in (256, 256, 128, 128))

    xc = ctx
    for l in range(DEPTH):
        last = l == DEPTH - 1
        ml = mod[l, :bsz].reshape(bsz, 1, 6, d)
        sh1, sc1, g1, sh2, sc2, g2 = [ml[:, :, i, :] for i in range(6)]
        mc = jnp.broadcast_to(mod[l, bsz].reshape(1, 1, 6, d), (bsz, 1, 6, d))
        csh1, csc1, cg1, csh2, csc2, cg2 = [mc[:, :, i, :] for i in range(6)]

        w_in_l = jnp.concatenate([w_in[l][:, h * 64:(h + 1) * 64] for h in (0, 2, 1, 3)]
                                 + [w_in[l][:, OFF_AK:]], axis=1).astype(BF16)
        w_out_l = w_out[l].astype(BF16)
        wg_l, wu_l, wd_l = w_gate[l].astype(BF16), w_up[l].astype(BF16), w_down[l].astype(BF16)
        gmix = g_mix[l].reshape(1, d)
        gffn = g_ffn[l].reshape(1, d)
        aqg = jnp.tile(a_q_gain[l], 4).reshape(1, 256)
        akg = jnp.tile(a_k_gain[l], 2).reshape(1, 128)
        dqg = jnp.tile(d_q_gain[l], 4).reshape(1, 256)
        dkg = jnp.tile(d_k_gain[l], 4).reshape(1, 256)
        bvg = b_v_gain[l].reshape(1, 256)
        ws_cat = jnp.transpose(b_w_s[l], (1, 0, 2)).reshape(B_CHUNK, B_GROUPS * B_CHUNK).astype(BF16)
        bs_t = jnp.repeat(b_b_s[l].T, B_GROUP_W, axis=1)
        wpool = jax.scipy.linalg.block_diag(*[c_w_pool[l, g] for g in range(4)]).astype(BF16)
        cscale = c_scale[l].reshape(1, 256)
        bias = _neighbour_bias(d_rpb[l], MIX_TQ, n_rows)
        sink = a_sink[l]

        common = (seg, aqg, akg, dqg, dkg)
        lat = _in_projection(x, sh1, sc1, gmix, w_in_l, *common, tables_lat, bvg, ws_cat, bs_t, use_rope=True)
        con = _in_projection(xc, csh1, csc1, gmix, w_in_l, *common, tables_ctx, bvg, ws_cat, bs_t, use_rope=False)
        qa, ka, va, ob, cin, qd, kd, vd = lat
        cqa, cka, cva, cob, ccin, cqd, ckd, cvd = con

        x = _mixer(x, g1, qa, ka, va, cka, cva, ob, cin, wpool, cscale, qd, kd, vd, ckd, cvd, bias,
                   w_out_l, sink, local=True)
        x = _ffn(x, sh2, sc2, g2, gffn, wg_l, wu_l, wd_l)
        if not last:
            xc = _mixer(xc, cg1, cqa, cka, cva, cka, cva, cob, ccin, wpool, cscale, cqd, ckd, cvd,
                        ckd, cvd, bias, w_out_l, sink, local=False)
            xc = _ffn(xc, csh2, csc2, cg2, gffn, wg_l, wu_l, wd_l)
    return x
```

```python
import functools

import jax
import jax.numpy as jnp
import numpy as np
from jax import lax
from jax.experimental import pallas as pl
from jax.experimental.pallas import tpu as pltpu

F32 = jnp.float32
BF16 = jnp.bfloat16

D_MODEL = 1024
DEPTH = 2
GRID_W = 64
HEAD_DIM = 64
GROUP_W = 256
A_HEADS = 4
A_KV_HEADS = 2
A_WINDOW = 128
A_BLOCK = 128
B_CHUNK = 128
B_GROUPS = 4
B_GROUP_W = 64
C_POOLS = (2, 4, 8, 16)
C_GROUP_W = 64
D_HEADS = 4
D_WIN_ROWS = 8
D_WIN_COLS = 16
FF_DIM = 2816
ROPE_BASE = 10000.0
NORM_EPS = 1e-6
NEG_INF = -1e30
IN_TOTAL = 2048
OFF_AQ, OFF_AK, OFF_AV, OFF_BU, OFF_BV, OFF_C, OFF_DQ, OFF_DK, OFF_DV = (
    0, 256, 384, 512, 768, 1024, 1280, 1536, 1792)

MOD_ROWS = 16
MOD_TN = 1536
IN_TM = 512
MIX_TQ = 256
FFN_TM = 512
FFN_TF = 256
POOL_HALO = 8
VMEM_LIMIT = 56 * 1024 * 1024


def _dot(a, b):
    return jnp.dot(a, b, preferred_element_type=F32)


def _dot_nt(a, b):
    return lax.dot_general(a, b, (((1,), (1,)), ((), ())), preferred_element_type=F32)


def _mod_kernel(c_ref, w_ref, b_ref, o_ref):
    cv = c_ref[...]
    act = cv * jax.nn.sigmoid(cv)
    o_ref[0] = _dot(act.astype(BF16), w_ref[0].astype(BF16)) + b_ref[0]


def _modulation(cvec, w_mod, b_mod):
    depth, d, n = w_mod.shape
    return pl.pallas_call(
        _mod_kernel,
        out_shape=jax.ShapeDtypeStruct((depth, MOD_ROWS, n), F32),
        grid=(depth, n // MOD_TN),
        in_specs=[
            pl.BlockSpec((MOD_ROWS, d), lambda l, j: (0, 0)),
            pl.BlockSpec((1, d, MOD_TN), lambda l, j: (l, 0, j)),
            pl.BlockSpec((1, 1, MOD_TN), lambda l, j: (l, 0, j)),
        ],
        out_specs=pl.BlockSpec((1, MOD_ROWS, MOD_TN), lambda l, j: (l, 0, j)),
        compiler_params=pltpu.CompilerParams(
            dimension_semantics=("arbitrary", "arbitrary"), vmem_limit_bytes=VMEM_LIMIT),
        name="modulation",
    )(cvec, w_mod, b_mod.reshape(depth, 1, n))


def _head_norm(t, seg, gain):
    sq = t * t
    hi = sq.astype(BF16)
    lo = (sq - hi.astype(F32)).astype(BF16)
    ms = _dot(hi, seg) + _dot(lo, seg)
    return t * lax.rsqrt(ms + NORM_EPS) * gain


def _rope(y, cos, sin_signed):
    w = y.shape[-1]
    up = pltpu.roll(y, w - 16, axis=1)
    dn = pltpu.roll(y, 16, axis=1)
    lane = lax.broadcasted_iota(jnp.int32, y.shape, 1)
    swapped = jnp.where((lane & 31) < 16, up, dn)
    return y * cos + swapped * sin_signed


def _in_kernel(x_ref, sh_ref, sc_ref, g_ref, w_ref, seg_ref, aqg_ref, akg_ref, dqg_ref, dkg_ref,
               cosq_ref, sinq_ref, cosk_ref, sink_ref, bvg_ref, ws_ref, bs_ref,
               qa_ref, ka_ref, va_ref, ob_ref, c_ref, qd_ref, kd_ref, vd_ref, v_scr, *, use_rope):
    xt = x_ref[0]
    ms = jnp.mean(xt * xt, axis=-1, keepdims=True)
    h = xt * lax.rsqrt(ms + NORM_EPS) * g_ref[...]
    h = (h * (1.0 + sc_ref[0]) + sh_ref[0]).astype(BF16)

    p = _dot(h, w_ref[...])

    def proj(off, width):
        return p[:, off:off + width]

    seg = seg_ref[...]
    qa = _head_norm(proj(OFF_AQ, 256), seg, aqg_ref[...])
    ka = _head_norm(proj(OFF_AK, 128), seg[:128, :128], akg_ref[...])
    if use_rope:
        qa = _rope(qa, cosq_ref[...], sinq_ref[...])
        ka = _rope(ka, cosk_ref[...], sink_ref[...])
    scale = HEAD_DIM ** -0.5
    qa_ref[0] = (qa * scale).astype(BF16)
    ka_ref[0] = ka.astype(BF16)
    v_scr[:, 0:128] = proj(OFF_AV, 128)
    v_scr[:, 128:384] = proj(OFF_DV, 256)
    vat = v_scr[:, 0:128].T.astype(BF16)
    for j in range(vat.shape[1] // A_BLOCK):
        va_ref[0, j] = vat[:, j * A_BLOCK:(j + 1) * A_BLOCK]

    qd = _head_norm(proj(OFF_DQ, 256), seg, dqg_ref[...])
    qd_ref[0] = (qd * scale).astype(BF16)
    kd_ref[0] = _head_norm(proj(OFF_DK, 256), seg, dkg_ref[...]).astype(BF16)
    vdt = v_scr[:, 128:384].T.astype(BF16)
    for j in range(vdt.shape[1] // MIX_TQ):
        vd_ref[0, j] = vdt[:, j * MIX_TQ:(j + 1) * MIX_TQ]

    c_ref[0] = proj(OFF_C, 256)

    u = jax.nn.gelu(proj(OFF_BU, 256))
    v = jax.nn.gelu(proj(OFF_BV, 256))
    mu = jnp.mean(v, axis=-1, keepdims=True)
    vc = v - mu
    var = jnp.mean(vc * vc, axis=-1, keepdims=True)
    vn = (vc * lax.rsqrt(var + NORM_EPS) * bvg_ref[...]).astype(BF16)
    lane = lax.broadcasted_iota(jnp.int32, (B_CHUNK, GROUP_W), 1)
    zero = jnp.zeros((B_CHUNK, GROUP_W), BF16)
    tm = xt.shape[0]
    for ch in range(tm // B_CHUNK):
        rows = slice(ch * B_CHUNK, (ch + 1) * B_CHUNK)
        vch = vn[rows]
        stacked = jnp.concatenate(
            [jnp.where((lane >= g * B_GROUP_W) & (lane < (g + 1) * B_GROUP_W), vch, zero)
             for g in range(B_GROUPS)], axis=0)
        z = _dot(ws_ref[...], stacked) + bs_ref[...]
        ob_ref[0, rows, :] = (u[rows] * z).astype(BF16)


def _in_projection(x, shift, scale, g_mix, w_in, seg, aqg, akg, dqg, dkg, tables, bvg, ws_cat, bs_t,
                   *, use_rope):
    b, t, d = x.shape
    cosq, sinq, cosk, sink = tables
    tm = min(IN_TM, t)
    tok = lambda bi, i: (bi, i, 0)
    per_b = lambda bi, i: (bi, 0, 0)
    const2 = lambda bi, i: (0, 0)
    pos2 = lambda bi, i: (i, 0)
    assert tm % MIX_TQ == 0 and MIX_TQ % A_BLOCK == 0
    out_shapes = (
        jax.ShapeDtypeStruct((b, t, 256), BF16),
        jax.ShapeDtypeStruct((b, t, 128), BF16),
        jax.ShapeDtypeStruct((b, t // A_BLOCK, 128, A_BLOCK), BF16),
        jax.ShapeDtypeStruct((b, t, 256), BF16),
        jax.ShapeDtypeStruct((b, t, 256), F32),
        jax.ShapeDtypeStruct((b, t, 256), BF16),
        jax.ShapeDtypeStruct((b, t, 256), BF16),
        jax.ShapeDtypeStruct((b, t // MIX_TQ, 256, MIX_TQ), BF16),
    )
    blk4 = lambda bi, i: (bi, i, 0, 0)
    out_specs = tuple(
        pl.BlockSpec((1, tm // A_BLOCK, 128, A_BLOCK), blk4) if k == 2 else
        pl.BlockSpec((1, tm // MIX_TQ, 256, MIX_TQ), blk4) if k == 7 else
        pl.BlockSpec((1, tm, s.shape[-1]), tok)
        for k, s in enumerate(out_shapes))
    return pl.pallas_call(
        functools.partial(_in_kernel, use_rope=use_rope),
        out_shape=out_shapes,
        grid=(b, t // tm),
        in_specs=[
            pl.BlockSpec((1, tm, d), tok),
            pl.BlockSpec((1, 1, d), per_b),
            pl.BlockSpec((1, 1, d), per_b),
            pl.BlockSpec((1, d), const2),
            pl.BlockSpec((d, IN_TOTAL), const2),
            pl.BlockSpec((256, 256), const2),
            pl.BlockSpec((1, 256), const2),
            pl.BlockSpec((1, 128), const2),
            pl.BlockSpec((1, 256), const2),
            pl.BlockSpec((1, 256), const2),
            pl.BlockSpec((tm, 256), pos2),
            pl.BlockSpec((tm, 256), pos2),
            pl.BlockSpec((tm, 128), pos2),
            pl.BlockSpec((tm, 128), pos2),
            pl.BlockSpec((1, 256), const2),
            pl.BlockSpec((B_CHUNK, B_GROUPS * B_CHUNK), const2),
            pl.BlockSpec((B_CHUNK, GROUP_W), const2),
        ],
        out_specs=out_specs,
        scratch_shapes=[pltpu.VMEM((tm, 384), F32)],
        compiler_params=pltpu.CompilerParams(
            dimension_semantics=("arbitrary", "arbitrary"), vmem_limit_bytes=VMEM_LIMIT),
        name="in_projection_rope" if use_rope else "in_projection_ctx",
    )(x, shift, scale, g_mix, w_in, seg, aqg, akg, dqg, dkg, cosq, sinq, cosk, sink, bvg, ws_cat, bs_t)


def _softmax_pv_t(parts, extra_logit):
    m = None
    for s, _ in parts:
        pm = jnp.max(s, axis=0, keepdims=True)
        m = pm if m is None else jnp.maximum(m, pm)
    if extra_logit is not None:
        m = jnp.maximum(m, extra_logit)
    denom = None
    acc = None
    for s, vt in parts:
        e = jnp.exp(s - m)
        ps = jnp.sum(e, axis=0, keepdims=True)
        denom = ps if denom is None else denom + ps
        pv = _dot(vt, e.astype(BF16))
        acc = pv if acc is None else acc + pv
    if extra_logit is not None:
        denom = denom + jnp.exp(extra_logit - m)
    return acc / denom


def _head_lanes(q, head, width=HEAD_DIM):
    lane = lax.broadcasted_iota(jnp.int32, q.shape, 1)
    return jnp.where((lane >= head * width) & (lane < (head + 1) * width), q, jnp.zeros_like(q))


def _pool_tile(c_ref, t0, n_tok, tq):
    y = c_ref[0, pl.ds(t0, tq), :]
    lo_start = pl.multiple_of(jnp.maximum(t0 - POOL_HALO, 0), POOL_HALO)
    hi_start = pl.multiple_of(jnp.minimum(t0 + tq, n_tok - POOL_HALO), POOL_HALO)
    lo = c_ref[0, pl.ds(lo_start, POOL_HALO), :]
    hi = c_ref[0, pl.ds(hi_start, POOL_HALO), :]
    lo = jnp.where(t0 > 0, lo, 0.0)
    hi = jnp.where(t0 + tq < n_tok, hi, 0.0)
    ypad = jnp.concatenate([lo, y, hi], axis=0)

    n = tq + 2 * POOL_HALO
    w2 = ypad + pltpu.roll(ypad, 1, axis=0)
    w4 = pltpu.roll(w2, 1, axis=0) + pltpu.roll(w2, n - 1, axis=0)
    w8 = pltpu.roll(w4, 2, axis=0) + pltpu.roll(w4, n - 2, axis=0)
    w16 = pltpu.roll(w8, 4, axis=0) + pltpu.roll(w8, n - 4, axis=0)
    sums = tuple(w[POOL_HALO:POOL_HALO + tq] for w in (w2, w4, w8, w16))

    lane = lax.broadcasted_iota(jnp.int32, (tq, GROUP_W), 1)
    pos = t0 + lax.broadcasted_iota(jnp.int32, (tq, GROUP_W), 0)
    pooled = None
    for gi, w in enumerate(C_POOLS):
        lo_i = jnp.clip(pos - w // 2, 0, n_tok)
        hi_i = jnp.clip(pos - w // 2 + w, 0, n_tok)
        mean = sums[gi] / (hi_i - lo_i).astype(F32)
        sel = (lane >= gi * C_GROUP_W) & (lane < (gi + 1) * C_GROUP_W)
        pooled = jnp.where(sel, mean, 0.0 if pooled is None else pooled)
    return pooled - y


def _mix_kernel(sink_ref, x_ref, g1_ref, qa_ref, ka_ref, va_ref, kca_ref, vca_ref, ob_ref, c_ref,
                wp_ref, cs_ref, qd_ref, kd_ref, vd_ref, kcd_ref, vcd_ref, bias_ref, wo_ref,
                o_ref, oa_scr, od_scr, mixa_scr, mixd_scr, *, local, n_tok):
    tq = x_ref.shape[1]
    ti = pl.program_id(1)
    t0 = pl.multiple_of(ti * tq, tq)

    n_ctx_blk = vca_ref.shape[1]
    for blk in range(tq // A_BLOCK):
        cols = slice(blk * A_BLOCK, (blk + 1) * A_BLOCK)
        qblk = qa_ref[0, cols, :]
        qcat = jnp.concatenate([qblk[:, :128], qblk[:, 128:]], axis=0)
        if local:
            nblk = ti * (tq // A_BLOCK) + blk
            b0 = jnp.clip(nblk - 1, 0, n_tok // A_BLOCK - 3)
            kstart = pl.multiple_of(b0 * A_BLOCK, A_BLOCK)
            kwin = ka_ref[0, pl.ds(kstart, 3 * A_BLOCK), :]
            kk = lax.broadcasted_iota(jnp.int32, (3 * A_BLOCK, 2 * A_BLOCK), 0)
            qq = lax.broadcasted_iota(jnp.int32, (3 * A_BLOCK, 2 * A_BLOCK), 1)
            valid = jnp.abs(kk - (qq & (A_BLOCK - 1)) + (kstart - nblk * A_BLOCK)) <= A_WINDOW
        for kv in range(A_KV_HEADS):
            qm = _head_lanes(qcat, kv)
            rows = slice(kv * HEAD_DIM, (kv + 1) * HEAD_DIM)
            col = lax.broadcasted_iota(jnp.int32, (1, 2 * A_BLOCK), 1)
            sink = jnp.where(col < A_BLOCK, sink_ref[2 * kv], sink_ref[2 * kv + 1])
            parts = []
            if local:
                s_loc = jnp.where(valid, _dot_nt(kwin, qm), NEG_INF)
                vt = jnp.concatenate([va_ref[0, b0 + j, rows, :] for j in range(3)], axis=1)
                parts.append((s_loc, vt))
            vct = jnp.concatenate([vca_ref[0, j, rows, :] for j in range(n_ctx_blk)], axis=1)
            parts.append((_dot_nt(kca_ref[0], qm), vct))
            o2 = _softmax_pv_t(parts, sink)
            oa_scr[(2 * kv) * 64:(2 * kv + 1) * 64, cols] = o2[:, :A_BLOCK]
            oa_scr[(2 * kv + 1) * 64:(2 * kv + 2) * 64, cols] = o2[:, A_BLOCK:]

    qall = qd_ref[0]
    if local:
        d0 = jnp.clip(ti - 1, 0, n_tok // tq - 3)
        kwin_d = kd_ref[0, pl.ds(pl.multiple_of(d0 * tq, tq), 3 * tq), :]
    for hd in range(D_HEADS):
        qm = _head_lanes(qall, hd)
        rows = slice(hd * HEAD_DIM, (hd + 1) * HEAD_DIM)
        parts = []
        if local:
            s_loc = _dot_nt(kwin_d, qm) + bias_ref[0, hd]
            vt = jnp.concatenate([vd_ref[0, d0 + j, rows, :] for j in range(3)], axis=1)
            parts.append((s_loc, vt))
        parts.append((_dot_nt(kcd_ref[0], qm), vcd_ref[0, 0, rows, :]))
        od_scr[rows, :] = _softmax_pv_t(parts, None)

    pooled = _pool_tile(c_ref, t0, n_tok, tq)
    out_c = _dot(pooled.astype(BF16), wp_ref[...]) * cs_ref[...]

    mixa_scr[...] = oa_scr[...].T.astype(BF16)
    mixd_scr[...] = od_scr[...].T.astype(BF16)
    mix = (_dot(mixa_scr[...], wo_ref[0:256, :]) + _dot(ob_ref[0], wo_ref[256:512, :])
           + _dot(out_c.astype(BF16), wo_ref[512:768, :]) + _dot(mixd_scr[...], wo_ref[768:1024, :]))
    o_ref[0] = x_ref[0] + g1_ref[0] * mix


def _mixer(x, g1, qa, ka, va, kca, vca, ob, cin, wpool, cscale, qd, kd, vd, kcd, vcd, bias, w_out,
           sink, *, local):
    b, t, d = x.shape
    lc = kca.shape[1]
    tq = MIX_TQ
    n_tiles = t // tq
    tok = lambda bi, i: (bi, i, 0)
    per_b = lambda bi, i: (bi, 0, 0)
    per_b4 = lambda bi, i: (bi, 0, 0, 0)
    const2 = lambda bi, i: (0, 0)

    def bias_map(bi, i):
        kind = jnp.where(i == 0, 1, jnp.where(i == n_tiles - 1, 2, 0))
        return (kind, 0, 0, 0)

    return pl.pallas_call(
        functools.partial(_mix_kernel, local=local, n_tok=t),
        out_shape=jax.ShapeDtypeStruct((b, t, d), F32),
        grid=(b, n_tiles),
        in_specs=[
            pl.BlockSpec(memory_space=pltpu.SMEM),
            pl.BlockSpec((1, tq, d), tok),
            pl.BlockSpec((1, 1, d), per_b),
            pl.BlockSpec((1, tq, 256), tok),
            pl.BlockSpec((1, t, 128), per_b),
            pl.BlockSpec((1, t // A_BLOCK, 128, A_BLOCK), per_b4),
            pl.BlockSpec((1, lc, 128), per_b),
            pl.BlockSpec((1, lc // A_BLOCK, 128, A_BLOCK), per_b4),
            pl.BlockSpec((1, tq, 256), tok),
            pl.BlockSpec((1, t, 256), per_b),
            pl.BlockSpec((256, 256), const2),
            pl.BlockSpec((1, 256), const2),
            pl.BlockSpec((1, tq, 256), tok),
            pl.BlockSpec((1, t, 256), per_b),
            pl.BlockSpec((1, t // tq, 256, tq), per_b4),
            pl.BlockSpec((1, lc, 256), per_b),
            pl.BlockSpec((1, lc // tq, 256, tq), per_b4),
            pl.BlockSpec((1, D_HEADS, 3 * tq, tq), bias_map),
            pl.BlockSpec((d, d), const2),
        ],
        out_specs=pl.BlockSpec((1, tq, d), tok),
        scratch_shapes=[pltpu.VMEM((GROUP_W, tq), F32), pltpu.VMEM((GROUP_W, tq), F32),
                        pltpu.VMEM((tq, GROUP_W), BF16), pltpu.VMEM((tq, GROUP_W), BF16)],
        compiler_params=pltpu.CompilerParams(
            dimension_semantics=("arbitrary", "arbitrary"), vmem_limit_bytes=VMEM_LIMIT),
        name="mixer_latent" if local else "mixer_ctx",
    )(sink, x, g1, qa, ka, va, kca, vca, ob, cin, wpool, cscale, qd, kd, vd, kcd, vcd, bias, w_out)


def _ffn_kernel(x_ref, sh_ref, sc_ref, g2_ref, gn_ref, wg_ref, wu_ref, wd_ref, o_ref, act_scr):
    xt = x_ref[0]
    ms = jnp.mean(xt * xt, axis=-1, keepdims=True)
    h = xt * lax.rsqrt(ms + NORM_EPS) * gn_ref[...]
    h = (h * (1.0 + sc_ref[0]) + sh_ref[0]).astype(BF16)
    for c in range(FF_DIM // FFN_TF):
        cols = slice(c * FFN_TF, (c + 1) * FFN_TF)
        gate = _dot(h, wg_ref[:, cols])
        up = _dot(h, wu_ref[:, cols])
        act_scr[:, cols] = (gate * jax.nn.sigmoid(gate) * up).astype(BF16)
    o_ref[0] = xt + g2_ref[0] * _dot(act_scr[...], wd_ref[...])


def _ffn(x, shift, scale, g2, g_ffn, wg, wu, wd):
    b, t, d = x.shape
    tm = min(FFN_TM, t)
    tok = lambda bi, i: (bi, i, 0)
    per_b = lambda bi, i: (bi, 0, 0)
    const2 = lambda bi, i: (0, 0)
    resident = pl.Buffered(1)
    return pl.pallas_call(
        _ffn_kernel,
        out_shape=jax.ShapeDtypeStruct((b, t, d), F32),
        grid=(b, t // tm),
        in_specs=[
            pl.BlockSpec((1, tm, d), tok),
            pl.BlockSpec((1, 1, d), per_b),
            pl.BlockSpec((1, 1, d), per_b),
            pl.BlockSpec((1, 1, d), per_b),
            pl.BlockSpec((1, d), const2),
            pl.BlockSpec((d, FF_DIM), const2, pipeline_mode=resident),
            pl.BlockSpec((d, FF_DIM), const2, pipeline_mode=resident),
            pl.BlockSpec((FF_DIM, d), const2, pipeline_mode=resident),
        ],
        out_specs=pl.BlockSpec((1, tm, d), tok),
        scratch_shapes=[pltpu.VMEM((tm, FF_DIM), BF16)],
        compiler_params=pltpu.CompilerParams(
            dimension_semantics=("arbitrary", "arbitrary"), vmem_limit_bytes=VMEM_LIMIT),
        name="ffn",
    )(x, shift, scale, g2, g_ffn, wg, wu, wd)


def _rope_tables(s):
    t = np.arange(s)
    half = 16
    inv = jnp.power(ROPE_BASE, -jnp.arange(half, dtype=F32) / half)
    ang_r = jnp.asarray(t // GRID_W, F32)[:, None] * inv[None, :]
    ang_c = jnp.asarray(t % GRID_W, F32)[:, None] * inv[None, :]
    cos = jnp.concatenate([jnp.cos(ang_r)] * 2 + [jnp.cos(ang_c)] * 2, axis=-1)
    sin = jnp.concatenate([-jnp.sin(ang_r), jnp.sin(ang_r), -jnp.sin(ang_c), jnp.sin(ang_c)], axis=-1)
    return cos, sin


def _neighbour_bias(rpb, tq, n_rows):
    rows_per_tile = tq // GRID_W
    cidx = np.arange(GRID_W)
    col_start = np.clip(cidx - D_WIN_COLS // 2, 0, GRID_W - D_WIN_COLS)
    col_ok = (cidx[None, :] >= col_start[:, None]) & (cidx[None, :] < col_start[:, None] + D_WIN_COLS)
    edge = GRID_W - D_WIN_COLS
    ext = jnp.concatenate([jnp.repeat(rpb[:, :, :1], edge, axis=2), rpb.astype(F32),
                           jnp.repeat(rpb[:, :, -1:], edge, axis=2)], axis=2)
    toeplitz = jnp.stack([ext[:, :, GRID_W - 1 - cq:2 * GRID_W - 1 - cq] for cq in range(GRID_W)], axis=2)
    toeplitz = jnp.where(col_ok[None, None], toeplitz, NEG_INF)
    toeplitz_t = jnp.swapaxes(toeplitz, 2, 3)
    masked = jnp.full((D_HEADS, GRID_W, GRID_W), NEG_INF, F32)
    win_rows = 3 * rows_per_tile
    kinds = []
    for r0, ws in ((2 * rows_per_tile, rows_per_tile), (0, 0), (n_rows - rows_per_tile, n_rows - win_rows)):
        cols = []
        for i in range(rows_per_tile):
            r = r0 + i
            rs = min(max(r - D_WIN_ROWS // 2, 0), n_rows - D_WIN_ROWS)
            blocks = [toeplitz_t[:, ws + jj - r + D_WIN_ROWS - 1] if rs <= ws + jj < rs + D_WIN_ROWS
                      else masked for jj in range(win_rows)]
            cols.append(jnp.concatenate(blocks, axis=1))
        kinds.append(jnp.concatenate(cols, axis=2))
    return jnp.stack(kinds, axis=0)


def kernel(x, c, ctx, c_ctx, w_mod, b_mod, g_mix, g_ffn, w_in, w_out, a_q_gain, a_k_gain, a_sink,
           b_v_gain, b_w_s, b_b_s, c_w_pool, c_scale, d_q_gain, d_k_gain, d_rpb, w_gate, w_up, w_down):
    bsz, s, d = x.shape
    lc = ctx.shape[1]
    n_rows = s // GRID_W
    assert s % MIX_TQ == 0 and n_rows >= 2 * D_WIN_ROWS and MIX_TQ // GRID_W == D_WIN_ROWS // 2

    cvec = jnp.zeros((MOD_ROWS, d), F32).at[:bsz].set(c).at[bsz].set(c_ctx)
    mod = _modulation(cvec, w_mod, b_mod)

    seg = jnp.asarray(np.kron(np.eye(4), np.full((64, 64), 1.0 / 64)), BF16)
    cos1, sin1 = _rope_tables(s)
    tables_lat = (jnp.tile(cos1, (1, 4)), jnp.tile(sin1, (1, 4)), jnp.tile(cos1, (1, 2)), jnp.tile(sin1, (1, 2)))
    tables_ctx = tuple(jnp.zeros((lc, w), F32) for w in (256, 256, 128, 128))

    xc = ctx
    for l in range(DEPTH):
        last = l == DEPTH - 1
        ml = mod[l, :bsz].reshape(bsz, 1, 6, d)
        sh1, sc1, g1, sh2, sc2, g2 = [ml[:, :, i, :] for i in range(6)]
        mc = jnp.broadcast_to(mod[l, bsz].reshape(1, 1, 6, d), (bsz, 1, 6, d))
        csh1, csc1, cg1, csh2, csc2, cg2 = [mc[:, :, i, :] for i in range(6)]

        w_in_l = jnp.concatenate([w_in[l][:, h * 64:(h + 1) * 64] for h in (0, 2, 1, 3)]
                                 + [w_in[l][:, OFF_AK:]], axis=1).astype(BF16)
        w_out_l = w_out[l].astype(BF16)
        wg_l, wu_l, wd_l = w_gate[l].astype(BF16), w_up[l].astype(BF16), w_down[l].astype(BF16)
        gmix = g_mix[l].reshape(1, d)
        gffn = g_ffn[l].reshape(1, d)
        aqg = jnp.tile(a_q_gain[l], 4).reshape(1, 256)
        akg = jnp.tile(a_k_gain[l], 2).reshape(1, 128)
        dqg = jnp.tile(d_q_gain[l], 4).reshape(1, 256)
        dkg = jnp.tile(d_k_gain[l], 4).reshape(1, 256)
        bvg = b_v_gain[l].reshape(1, 256)
        ws_cat = jnp.transpose(b_w_s[l], (1, 0, 2)).reshape(B_CHUNK, B_GROUPS * B_CHUNK).astype(BF16)
        bs_t = jnp.repeat(b_b_s[l].T, B_GROUP_W, axis=1)
        wpool = jax.scipy.linalg.block_diag(*[c_w_pool[l, g] for g in range(4)]).astype(BF16)
        cscale = c_scale[l].reshape(1, 256)
        bias = _neighbour_bias(d_rpb[l], MIX_TQ, n_rows)
        sink = a_sink[l]

        common = (seg, aqg, akg, dqg, dkg)
        lat = _in_projection(x, sh1, sc1, gmix, w_in_l, *common, tables_lat, bvg, ws_cat, bs_t, use_rope=True)
        con = _in_projection(xc, csh1, csc1, gmix, w_in_l, *common, tables_ctx, bvg, ws_cat, bs_t, use_rope=False)
        qa, ka, va, ob, cin, qd, kd, vd = lat
        cqa, cka, cva, cob, ccin, cqd, ckd, cvd = con

        x = _mixer(x, g1, qa, ka, va, cka, cva, ob, cin, wpool, cscale, qd, kd, vd, ckd, cvd, bias,
                   w_out_l, sink, local=True)
        x = _ffn(x, sh2, sc2, g2, gffn, wg_l, wu_l, wd_l)
        if not last:
            xc = _mixer(xc, cg1, cqa, cka, cva, cka, cva, cob, ccin, wpool, cscale, cqd, ckd, cvd,
                        ckd, cvd, bias, w_out_l, sink, local=False)
            xc = _ffn(xc, csh2, csc2, cg2, gffn, wg_l, wu_l, wd_l)
    return x
```

```python
import functools

import jax
import jax.numpy as jnp
import numpy as np
from jax import lax
from jax.experimental import pallas as pl
from jax.experimental.pallas import tpu as pltpu

F32 = jnp.float32
BF16 = jnp.bfloat16

D_MODEL = 1024
DEPTH = 2
GRID_W = 64
HEAD_DIM = 64
GROUP_W = 256
A_HEADS = 4
A_KV_HEADS = 2
A_WINDOW = 128
A_BLOCK = 128
B_CHUNK = 128
B_GROUPS = 4
B_GROUP_W = 64
C_POOLS = (2, 4, 8, 16)
C_GROUP_W = 64
D_HEADS = 4
D_WIN_ROWS = 8
D_WIN_COLS = 16
FF_DIM = 2816
ROPE_BASE = 10000.0
NORM_EPS = 1e-6
NEG_INF = -1e30
LOG2E = 1.4426950408889634
IN_TOTAL = 2048
OFF_AQ, OFF_AK, OFF_AV, OFF_BU, OFF_BV, OFF_C, OFF_DQ, OFF_DK, OFF_DV = (
    0, 256, 384, 512, 768, 1024, 1280, 1536, 1792)

MOD_ROWS = 16
MOD_TN = 1536
IN_TM = 512
MIX_TQ = 256
FFN_TM = 512
FFN_TF = 256
POOL_HALO = 8
SCORE_LOOKAHEAD = 2
VMEM_LIMIT = 56 * 1024 * 1024


def _dot(a, b):
    return jnp.dot(a, b, preferred_element_type=F32)


def _dot_nt(a, b):
    return lax.dot_general(a, b, (((1,), (1,)), ((), ())), preferred_element_type=F32)


def _mod_kernel(c_ref, w_ref, b_ref, o_ref):
    cv = c_ref[...]
    act = cv * jax.nn.sigmoid(cv)
    o_ref[0] = _dot(act.astype(BF16), w_ref[0].astype(BF16)) + b_ref[0]


def _modulation(cvec, w_mod, b_mod):
    depth, d, n = w_mod.shape
    return pl.pallas_call(
        _mod_kernel,
        out_shape=jax.ShapeDtypeStruct((depth, MOD_ROWS, n), F32),
        grid=(depth, n // MOD_TN),
        in_specs=[
            pl.BlockSpec((MOD_ROWS, d), lambda l, j: (0, 0)),
            pl.BlockSpec((1, d, MOD_TN), lambda l, j: (l, 0, j)),
            pl.BlockSpec((1, 1, MOD_TN), lambda l, j: (l, 0, j)),
        ],
        out_specs=pl.BlockSpec((1, MOD_ROWS, MOD_TN), lambda l, j: (l, 0, j)),
        compiler_params=pltpu.CompilerParams(
            dimension_semantics=("arbitrary", "arbitrary"), vmem_limit_bytes=VMEM_LIMIT),
        name="modulation",
    )(cvec, w_mod, b_mod.reshape(depth, 1, n))


def _head_norm(t, seg, gain):
    sq = t * t
    hi = sq.astype(BF16)
    lo = (sq - hi.astype(F32)).astype(BF16)
    ms = _dot(hi, seg) + _dot(lo, seg)
    return t * lax.rsqrt(ms + NORM_EPS) * gain


def _rope(y, cos, sin_signed):
    w = y.shape[-1]
    up = pltpu.roll(y, w - 16, axis=1)
    dn = pltpu.roll(y, 16, axis=1)
    lane = lax.broadcasted_iota(jnp.int32, y.shape, 1)
    swapped = jnp.where((lane & 31) < 16, up, dn)
    return y * cos + swapped * sin_signed


def _in_kernel(x_ref, sh_ref, sc_ref, g_ref, w_ref, seg_ref, aqg_ref, akg_ref, dqg_ref, dkg_ref,
               cosq_ref, sinq_ref, cosk_ref, sink_ref, bvg_ref, ws_ref, bs_ref,
               qa_ref, ka_ref, va_ref, ob_ref, c_ref, qd_ref, kd_ref, vd_ref, v_scr, *, use_rope):
    xt = x_ref[0]
    ms = jnp.mean(xt * xt, axis=-1, keepdims=True)
    h = xt * lax.rsqrt(ms + NORM_EPS) * g_ref[...]
    h = (h * (1.0 + sc_ref[0]) + sh_ref[0]).astype(BF16)

    p = _dot(h, w_ref[...])

    def proj(off, width):
        return p[:, off:off + width]

    seg = seg_ref[...]
    qa = _head_norm(proj(OFF_AQ, 256), seg, aqg_ref[...])
    ka = _head_norm(proj(OFF_AK, 128), seg[:128, :128], akg_ref[...])
    if use_rope:
        qa = _rope(qa, cosq_ref[...], sinq_ref[...])
        ka = _rope(ka, cosk_ref[...], sink_ref[...])
    scale = HEAD_DIM ** -0.5 * LOG2E
    qa_ref[0] = (qa * scale).astype(BF16)
    ka_ref[0] = ka.astype(BF16)
    v_scr[:, 0:128] = proj(OFF_AV, 128)
    v_scr[:, 128:384] = proj(OFF_DV, 256)
    vat = v_scr[:, 0:128].T.astype(BF16)
    for j in range(vat.shape[1] // A_BLOCK):
        va_ref[0, j] = vat[:, j * A_BLOCK:(j + 1) * A_BLOCK]

    qd = _head_norm(proj(OFF_DQ, 256), seg, dqg_ref[...])
    qd_ref[0] = (qd * scale).astype(BF16)
    kd_ref[0] = _head_norm(proj(OFF_DK, 256), seg, dkg_ref[...]).astype(BF16)
    vdt = v_scr[:, 128:384].T.astype(BF16)
    for j in range(vdt.shape[1] // MIX_TQ):
        vd_ref[0, j] = vdt[:, j * MIX_TQ:(j + 1) * MIX_TQ]

    c_ref[0] = proj(OFF_C, 256)

    u = jax.nn.gelu(proj(OFF_BU, 256))
    v = jax.nn.gelu(proj(OFF_BV, 256))
    mu = jnp.mean(v, axis=-1, keepdims=True)
    vc = v - mu
    var = jnp.mean(vc * vc, axis=-1, keepdims=True)
    vn = (vc * lax.rsqrt(var + NORM_EPS) * bvg_ref[...]).astype(BF16)
    lane = lax.broadcasted_iota(jnp.int32, (B_CHUNK, GROUP_W), 1)
    zero = jnp.zeros((B_CHUNK, GROUP_W), BF16)
    tm = xt.shape[0]
    for ch in range(tm // B_CHUNK):
        rows = slice(ch * B_CHUNK, (ch + 1) * B_CHUNK)
        vch = vn[rows]
        stacked = jnp.concatenate(
            [jnp.where((lane >= g * B_GROUP_W) & (lane < (g + 1) * B_GROUP_W), vch, zero)
             for g in range(B_GROUPS)], axis=0)
        z = _dot(ws_ref[...], stacked) + bs_ref[...]
        ob_ref[0, rows, :] = (u[rows] * z).astype(BF16)


def _in_projection(x, shift, scale, g_mix, w_in, seg, aqg, akg, dqg, dkg, tables, bvg, ws_cat, bs_t,
                   *, use_rope):
    b, t, d = x.shape
    cosq, sinq, cosk, sink = tables
    tm = min(IN_TM, t)
    tok = lambda bi, i: (bi, i, 0)
    per_b = lambda bi, i: (bi, 0, 0)
    const2 = lambda bi, i: (0, 0)
    pos2 = lambda bi, i: (i, 0)
    assert tm % MIX_TQ == 0 and MIX_TQ % A_BLOCK == 0
    out_shapes = (
        jax.ShapeDtypeStruct((b, t, 256), BF16),
        jax.ShapeDtypeStruct((b, t, 128), BF16),
        jax.ShapeDtypeStruct((b, t // A_BLOCK, 128, A_BLOCK), BF16),
        jax.ShapeDtypeStruct((b, t, 256), BF16),
        jax.ShapeDtypeStruct((b, t, 256), F32),
        jax.ShapeDtypeStruct((b, t, 256), BF16),
        jax.ShapeDtypeStruct((b, t, 256), BF16),
        jax.ShapeDtypeStruct((b, t // MIX_TQ, 256, MIX_TQ), BF16),
    )
    blk4 = lambda bi, i: (bi, i, 0, 0)
    out_specs = tuple(
        pl.BlockSpec((1, tm // A_BLOCK, 128, A_BLOCK), blk4) if k == 2 else
        pl.BlockSpec((1, tm // MIX_TQ, 256, MIX_TQ), blk4) if k == 7 else
        pl.BlockSpec((1, tm, s.shape[-1]), tok)
        for k, s in enumerate(out_shapes))
    return pl.pallas_call(
        functools.partial(_in_kernel, use_rope=use_rope),
        out_shape=out_shapes,
        grid=(b, t // tm),
        in_specs=[
            pl.BlockSpec((1, tm, d), tok),
            pl.BlockSpec((1, 1, d), per_b),
            pl.BlockSpec((1, 1, d), per_b),
            pl.BlockSpec((1, d), const2),
            pl.BlockSpec((d, IN_TOTAL), const2),
            pl.BlockSpec((256, 256), const2),
            pl.BlockSpec((1, 256), const2),
            pl.BlockSpec((1, 128), const2),
            pl.BlockSpec((1, 256), const2),
            pl.BlockSpec((1, 256), const2),
            pl.BlockSpec((tm, 256), pos2),
            pl.BlockSpec((tm, 256), pos2),
            pl.BlockSpec((tm, 128), pos2),
            pl.BlockSpec((tm, 128), pos2),
            pl.BlockSpec((1, 256), const2),
            pl.BlockSpec((B_CHUNK, B_GROUPS * B_CHUNK), const2),
            pl.BlockSpec((B_CHUNK, GROUP_W), const2),
        ],
        out_specs=out_specs,
        scratch_shapes=[pltpu.VMEM((tm, 384), F32)],
        compiler_params=pltpu.CompilerParams(
            dimension_semantics=("arbitrary", "arbitrary"), vmem_limit_bytes=VMEM_LIMIT),
        name="in_projection_rope" if use_rope else "in_projection_ctx",
    )(x, shift, scale, g_mix, w_in, seg, aqg, akg, dqg, dkg, cosq, sinq, cosk, sink, bvg, ws_cat, bs_t)


def _softmax_pv_t(parts, extra_logit):
    m = None
    for s, _ in parts:
        pm = jnp.max(s, axis=0, keepdims=True)
        m = pm if m is None else jnp.maximum(m, pm)
    if extra_logit is not None:
        m = jnp.maximum(m, extra_logit)
    denom = None
    acc = None
    for s, vt in parts:
        e = jnp.exp2(s - m)
        ps = jnp.sum(e, axis=0, keepdims=True)
        denom = ps if denom is None else denom + ps
        pv = _dot(vt, e.astype(BF16))
        acc = pv if acc is None else acc + pv
    if extra_logit is not None:
        denom = denom + jnp.exp2(extra_logit - m)
    return acc / denom


def _head_lanes(q, head, width=HEAD_DIM):
    lane = lax.broadcasted_iota(jnp.int32, q.shape, 1)
    return jnp.where((lane >= head * width) & (lane < (head + 1) * width), q, jnp.zeros_like(q))


def _pool_tile(c_ref, t0, n_tok, tq):
    y = c_ref[0, pl.ds(t0, tq), :]
    lo_start = pl.multiple_of(jnp.maximum(t0 - POOL_HALO, 0), POOL_HALO)
    hi_start = pl.multiple_of(jnp.minimum(t0 + tq, n_tok - POOL_HALO), POOL_HALO)
    lo = c_ref[0, pl.ds(lo_start, POOL_HALO), :]
    hi = c_ref[0, pl.ds(hi_start, POOL_HALO), :]
    lo = jnp.where(t0 > 0, lo, 0.0)
    hi = jnp.where(t0 + tq < n_tok, hi, 0.0)
    ypad = jnp.concatenate([lo, y, hi], axis=0)

    n = tq + 2 * POOL_HALO
    w2 = ypad + pltpu.roll(ypad, 1, axis=0)
    w4 = pltpu.roll(w2, 1, axis=0) + pltpu.roll(w2, n - 1, axis=0)
    w8 = pltpu.roll(w4, 2, axis=0) + pltpu.roll(w4, n - 2, axis=0)
    w16 = pltpu.roll(w8, 4, axis=0) + pltpu.roll(w8, n - 4, axis=0)
    sums = tuple(w[POOL_HALO:POOL_HALO + tq] for w in (w2, w4, w8, w16))

    lane = lax.broadcasted_iota(jnp.int32, (tq, GROUP_W), 1)
    pos = t0 + lax.broadcasted_iota(jnp.int32, (tq, GROUP_W), 0)
    pooled = None
    for gi, w in enumerate(C_POOLS):
        lo_i = jnp.clip(pos - w // 2, 0, n_tok)
        hi_i = jnp.clip(pos - w // 2 + w, 0, n_tok)
        mean = sums[gi] / (hi_i - lo_i).astype(F32)
        sel = (lane >= gi * C_GROUP_W) & (lane < (gi + 1) * C_GROUP_W)
        pooled = jnp.where(sel, mean, 0.0 if pooled is None else pooled)
    return pooled - y


def _mix_kernel(sink_ref, x_ref, g1_ref, qa_ref, ka_ref, va_ref, kca_ref, vca_ref, ob_ref, c_ref,
                wp_ref, cs_ref, qd_ref, kd_ref, vd_ref, kcd_ref, vcd_ref, bias_ref, amask_ref, wo_ref,
                o_ref, oa_scr, od_scr, mixa_scr, mixd_scr, *, local, n_tok):
    tq = x_ref.shape[1]
    ti = pl.program_id(1)
    t0 = pl.multiple_of(ti * tq, tq)

    n_ctx_blk = vca_ref.shape[1]
    jobs = []

    def a_job(blk, kv):
        cols = slice(blk * A_BLOCK, (blk + 1) * A_BLOCK)
        rows = slice(kv * HEAD_DIM, (kv + 1) * HEAD_DIM)

        def scores():
            qblk = qa_ref[0, cols, :]
            qcat = jnp.concatenate([qblk[:, :128], qblk[:, 128:]], axis=0)
            qm = _head_lanes(qcat, kv)
            parts = []
            if local:
                nblk = ti * (tq // A_BLOCK) + blk
                b0 = jnp.clip(nblk - 1, 0, n_tok // A_BLOCK - 3)
                kstart = pl.multiple_of(b0 * A_BLOCK, A_BLOCK)
                kwin = ka_ref[0, pl.ds(kstart, 3 * A_BLOCK), :]
                last_blk = n_tok // A_BLOCK - 1
                kind = jnp.where(nblk == 0, 1, jnp.where(nblk == last_blk, 2, 0))
                s_loc = _dot_nt(kwin, qm) + amask_ref[kind]
                vt = jnp.concatenate([va_ref[0, b0 + j, rows, :] for j in range(3)], axis=1)
                parts.append((s_loc, vt))
            vct = jnp.concatenate([vca_ref[0, j, rows, :] for j in range(n_ctx_blk)], axis=1)
            parts.append((_dot_nt(kca_ref[0], qm), vct))
            return parts

        def finish(parts):
            col = lax.broadcasted_iota(jnp.int32, (1, 2 * A_BLOCK), 1)
            sink = jnp.where(col < A_BLOCK, sink_ref[2 * kv], sink_ref[2 * kv + 1])
            o2 = _softmax_pv_t(parts, sink)
            oa_scr[(2 * kv) * 64:(2 * kv + 1) * 64, cols] = o2[:, :A_BLOCK]
            oa_scr[(2 * kv + 1) * 64:(2 * kv + 2) * 64, cols] = o2[:, A_BLOCK:]

        return scores, finish

    def d_job(hd):
        rows = slice(hd * HEAD_DIM, (hd + 1) * HEAD_DIM)

        def scores():
            qm = _head_lanes(qd_ref[0], hd)
            parts = []
            if local:
                d0 = jnp.clip(ti - 1, 0, n_tok // tq - 3)
                kwin = kd_ref[0, pl.ds(pl.multiple_of(d0 * tq, tq), 3 * tq), :]
                s_loc = _dot_nt(kwin, qm) + bias_ref[0, hd]
                vt = jnp.concatenate([vd_ref[0, d0 + j, rows, :] for j in range(3)], axis=1)
                parts.append((s_loc, vt))
            parts.append((_dot_nt(kcd_ref[0], qm), vcd_ref[0, 0, rows, :]))
            return parts

        def finish(parts):
            od_scr[rows, :] = _softmax_pv_t(parts, None)

        return scores, finish

    for blk in range(tq // A_BLOCK):
        for kv in range(A_KV_HEADS):
            jobs.append(a_job(blk, kv))
    for hd in range(D_HEADS):
        jobs.append(d_job(hd))
    ahead = min(SCORE_LOOKAHEAD, len(jobs))
    pending = [jobs[j][0]() for j in range(ahead)]
    for j, (_, finish) in enumerate(jobs):
        if j + ahead < len(jobs):
            pending.append(jobs[j + ahead][0]())
        finish(pending.pop(0))

    pooled = _pool_tile(c_ref, t0, n_tok, tq)
    out_c = _dot(pooled.astype(BF16), wp_ref[...]) * cs_ref[...]

    mixa_scr[...] = oa_scr[...].T.astype(BF16)
    mixd_scr[...] = od_scr[...].T.astype(BF16)
    mix = (_dot(mixa_scr[...], wo_ref[0:256, :]) + _dot(ob_ref[0], wo_ref[256:512, :])
           + _dot(out_c.astype(BF16), wo_ref[512:768, :]) + _dot(mixd_scr[...], wo_ref[768:1024, :]))
    o_ref[0] = x_ref[0] + g1_ref[0] * mix


def _mixer(x, g1, qa, ka, va, kca, vca, ob, cin, wpool, cscale, qd, kd, vd, kcd, vcd, bias, amask,
           w_out, sink, *, local):
    b, t, d = x.shape
    lc = kca.shape[1]
    tq = MIX_TQ
    n_tiles = t // tq
    tok = lambda bi, i: (bi, i, 0)
    per_b = lambda bi, i: (bi, 0, 0)
    per_b4 = lambda bi, i: (bi, 0, 0, 0)
    const2 = lambda bi, i: (0, 0)

    def bias_map(bi, i):
        kind = jnp.where(i == 0, 1, jnp.where(i == n_tiles - 1, 2, 0))
        return (kind, 0, 0, 0)

    return pl.pallas_call(
        functools.partial(_mix_kernel, local=local, n_tok=t),
        out_shape=jax.ShapeDtypeStruct((b, t, d), F32),
        grid=(b, n_tiles),
        in_specs=[
            pl.BlockSpec(memory_space=pltpu.SMEM),
            pl.BlockSpec((1, tq, d), tok),
            pl.BlockSpec((1, 1, d), per_b),
            pl.BlockSpec((1, tq, 256), tok),
            pl.BlockSpec((1, t, 128), per_b),
            pl.BlockSpec((1, t // A_BLOCK, 128, A_BLOCK), per_b4),
            pl.BlockSpec((1, lc, 128), per_b),
            pl.BlockSpec((1, lc // A_BLOCK, 128, A_BLOCK), per_b4),
            pl.BlockSpec((1, tq, 256), tok),
            pl.BlockSpec((1, t, 256), per_b),
            pl.BlockSpec((256, 256), const2),
            pl.BlockSpec((1, 256), const2),
            pl.BlockSpec((1, tq, 256), tok),
            pl.BlockSpec((1, t, 256), per_b),
            pl.BlockSpec((1, t // tq, 256, tq), per_b4),
            pl.BlockSpec((1, lc, 256), per_b),
            pl.BlockSpec((1, lc // tq, 256, tq), per_b4),
            pl.BlockSpec((1, D_HEADS, 3 * tq, tq), bias_map),
            pl.BlockSpec((3, 3 * A_BLOCK, 2 * A_BLOCK), lambda bi, i: (0, 0, 0)),
            pl.BlockSpec((d, d), const2),
        ],
        out_specs=pl.BlockSpec((1, tq, d), tok),
        scratch_shapes=[pltpu.VMEM((GROUP_W, tq), F32), pltpu.VMEM((GROUP_W, tq), F32),
                        pltpu.VMEM((tq, GROUP_W), BF16), pltpu.VMEM((tq, GROUP_W), BF16)],
        compiler_params=pltpu.CompilerParams(
            dimension_semantics=("arbitrary", "arbitrary"), vmem_limit_bytes=VMEM_LIMIT),
        name="mixer_latent" if local else "mixer_ctx",
    )(sink, x, g1, qa, ka, va, kca, vca, ob, cin, wpool, cscale, qd, kd, vd, kcd, vcd, bias, amask,
      w_out)


def _ffn_kernel(x_ref, sh_ref, sc_ref, g2_ref, gn_ref, wg_ref, wu_ref, wd_ref, o_ref, act_scr):
    xt = x_ref[0]
    ms = jnp.mean(xt * xt, axis=-1, keepdims=True)
    h = xt * lax.rsqrt(ms + NORM_EPS) * gn_ref[...]
    h = (h * (1.0 + sc_ref[0]) + sh_ref[0]).astype(BF16)
    for c in range(FF_DIM // FFN_TF):
        cols = slice(c * FFN_TF, (c + 1) * FFN_TF)
        gate = _dot(h, wg_ref[:, cols])
        up = _dot(h, wu_ref[:, cols])
        act_scr[:, cols] = (gate * jax.nn.sigmoid(gate) * up).astype(BF16)
    o_ref[0] = xt + g2_ref[0] * _dot(act_scr[...], wd_ref[...])


def _ffn(x, shift, scale, g2, g_ffn, wg, wu, wd):
    b, t, d = x.shape
    tm = min(FFN_TM, t)
    tok = lambda bi, i: (bi, i, 0)
    per_b = lambda bi, i: (bi, 0, 0)
    const2 = lambda bi, i: (0, 0)
    resident = pl.Buffered(1)
    return pl.pallas_call(
        _ffn_kernel,
        out_shape=jax.ShapeDtypeStruct((b, t, d), F32),
        grid=(b, t // tm),
        in_specs=[
            pl.BlockSpec((1, tm, d), tok),
            pl.BlockSpec((1, 1, d), per_b),
            pl.BlockSpec((1, 1, d), per_b),
            pl.BlockSpec((1, 1, d), per_b),
            pl.BlockSpec((1, d), const2),
            pl.BlockSpec((d, FF_DIM), const2, pipeline_mode=resident),
            pl.BlockSpec((d, FF_DIM), const2, pipeline_mode=resident),
            pl.BlockSpec((FF_DIM, d), const2, pipeline_mode=resident),
        ],
        out_specs=pl.BlockSpec((1, tm, d), tok),
        scratch_shapes=[pltpu.VMEM((tm, FF_DIM), BF16)],
        compiler_params=pltpu.CompilerParams(
            dimension_semantics=("arbitrary", "arbitrary"), vmem_limit_bytes=VMEM_LIMIT),
        name="ffn",
    )(x, shift, scale, g2, g_ffn, wg, wu, wd)


def _rope_tables(s):
    t = np.arange(s)
    half = 16
    inv = jnp.power(ROPE_BASE, -jnp.arange(half, dtype=F32) / half)
    ang_r = jnp.asarray(t // GRID_W, F32)[:, None] * inv[None, :]
    ang_c = jnp.asarray(t % GRID_W, F32)[:, None] * inv[None, :]
    cos = jnp.concatenate([jnp.cos(ang_r)] * 2 + [jnp.cos(ang_c)] * 2, axis=-1)
    sin = jnp.concatenate([-jnp.sin(ang_r), jnp.sin(ang_r), -jnp.sin(ang_c), jnp.sin(ang_c)], axis=-1)
    return cos, sin


def _window_mask():
    kk = np.arange(3 * A_BLOCK)[:, None]
    qq = np.arange(2 * A_BLOCK)[None, :] % A_BLOCK
    kinds = [np.where(np.abs(kk - qq - back * A_BLOCK) <= A_WINDOW, 0.0, NEG_INF) for back in (1, 0, 2)]
    return jnp.asarray(np.stack(kinds), F32)


def _neighbour_bias(rpb, tq, n_rows):
    rows_per_tile = tq // GRID_W
    cidx = np.arange(GRID_W)
    col_start = np.clip(cidx - D_WIN_COLS // 2, 0, GRID_W - D_WIN_COLS)
    col_ok = (cidx[None, :] >= col_start[:, None]) & (cidx[None, :] < col_start[:, None] + D_WIN_COLS)
    edge = GRID_W - D_WIN_COLS
    ext = jnp.concatenate([jnp.repeat(rpb[:, :, :1], edge, axis=2), rpb.astype(F32),
                           jnp.repeat(rpb[:, :, -1:], edge, axis=2)], axis=2)
    toeplitz = jnp.stack([ext[:, :, GRID_W - 1 - cq:2 * GRID_W - 1 - cq] for cq in range(GRID_W)], axis=2)
    toeplitz = jnp.where(col_ok[None, None], toeplitz, NEG_INF)
    toeplitz_t = jnp.swapaxes(toeplitz, 2, 3)
    masked = jnp.full((D_HEADS, GRID_W, GRID_W), NEG_INF, F32)
    win_rows = 3 * rows_per_tile
    kinds = []
    for r0, ws in ((2 * rows_per_tile, rows_per_tile), (0, 0), (n_rows - rows_per_tile, n_rows - win_rows)):
        cols = []
        for i in range(rows_per_tile):
            r = r0 + i
            rs = min(max(r - D_WIN_ROWS // 2, 0), n_rows - D_WIN_ROWS)
            blocks = [toeplitz_t[:, ws + jj - r + D_WIN_ROWS - 1] if rs <= ws + jj < rs + D_WIN_ROWS
                      else masked for jj in range(win_rows)]
            cols.append(jnp.concatenate(blocks, axis=1))
        kinds.append(jnp.concatenate(cols, axis=2))
    return jnp.stack(kinds, axis=0)


def kernel(x, c, ctx, c_ctx, w_mod, b_mod, g_mix, g_ffn, w_in, w_out, a_q_gain, a_k_gain, a_sink,
           b_v_gain, b_w_s, b_b_s, c_w_pool, c_scale, d_q_gain, d_k_gain, d_rpb, w_gate, w_up, w_down):
    bsz, s, d = x.shape
    lc = ctx.shape[1]
    n_rows = s // GRID_W
    assert s % MIX_TQ == 0 and n_rows >= 2 * D_WIN_ROWS and MIX_TQ // GRID_W == D_WIN_ROWS // 2

    cvec = jnp.zeros((MOD_ROWS, d), F32).at[:bsz].set(c).at[bsz].set(c_ctx)
    mod = _modulation(cvec, w_mod, b_mod)

    seg = jnp.asarray(np.kron(np.eye(4), np.full((64, 64), 1.0 / 64)), BF16)
    cos1, sin1 = _rope_tables(s)
    tables_lat = (jnp.tile(cos1, (1, 4)), jnp.tile(sin1, (1, 4)), jnp.tile(cos1, (1, 2)), jnp.tile(sin1, (1, 2)))
    tables_ctx = tuple(jnp.zeros((lc, w), F32) for w in (256, 256, 128, 128))

    amask = _window_mask()

    xc = ctx
    for l in range(DEPTH):
        last = l == DEPTH - 1
        ml = mod[l, :bsz].reshape(bsz, 1, 6, d)
        sh1, sc1, g1, sh2, sc2, g2 = [ml[:, :, i, :] for i in range(6)]
        mc = jnp.broadcast_to(mod[l, bsz].reshape(1, 1, 6, d), (bsz, 1, 6, d))
        csh1, csc1, cg1, csh2, csc2, cg2 = [mc[:, :, i, :] for i in range(6)]

        w_in_l = jnp.concatenate([w_in[l][:, h * 64:(h + 1) * 64] for h in (0, 2, 1, 3)]
                                 + [w_in[l][:, OFF_AK:]], axis=1).astype(BF16)
        w_out_l = w_out[l].astype(BF16)
        wg_l, wu_l, wd_l = w_gate[l].astype(BF16), w_up[l].astype(BF16), w_down[l].astype(BF16)
        gmix = g_mix[l].reshape(1, d)
        gffn = g_ffn[l].reshape(1, d)
        aqg = jnp.tile(a_q_gain[l], 4).reshape(1, 256)
        akg = jnp.tile(a_k_gain[l], 2).reshape(1, 128)
        dqg = jnp.tile(d_q_gain[l], 4).reshape(1, 256)
        dkg = jnp.tile(d_k_gain[l], 4).reshape(1, 256)
        bvg = b_v_gain[l].reshape(1, 256)
        ws_cat = jnp.transpose(b_w_s[l], (1, 0, 2)).reshape(B_CHUNK, B_GROUPS * B_CHUNK).astype(BF16)
        bs_t = jnp.repeat(b_b_s[l].T, B_GROUP_W, axis=1)
        wpool = jax.scipy.linalg.block_diag(*[c_w_pool[l, g] for g in range(4)]).astype(BF16)
        cscale = c_scale[l].reshape(1, 256)
        bias = _neighbour_bias(d_rpb[l] * LOG2E, MIX_TQ, n_rows)
        sink = a_sink[l] * LOG2E

        common = (seg, aqg, akg, dqg, dkg)
        lat = _in_projection(x, sh1, sc1, gmix, w_in_l, *common, tables_lat, bvg, ws_cat, bs_t, use_rope=True)
        con = _in_projection(xc, csh1, csc1, gmix, w_in_l, *common, tables_ctx, bvg, ws_cat, bs_t, use_rope=False)
        qa, ka, va, ob, cin, qd, kd, vd = lat
        cqa, cka, cva, cob, ccin, cqd, ckd, cvd = con

        x = _mixer(x, g1, qa, ka, va, cka, cva, ob, cin, wpool, cscale, qd, kd, vd, ckd, cvd, bias,
                   amask, w_out_l, sink, local=True)
        x = _ffn(x, sh2, sc2, g2, gffn, wg_l, wu_l, wd_l)
        if not last:
            xc = _mixer(xc, cg1, cqa, cka, cva, cka, cva, cob, ccin, wpool, cscale, cqd, ckd, cvd,
                        ckd, cvd, bias, amask, w_out_l, sink, local=False)
            xc = _ffn(xc, csh2, csc2, cg2, gffn, wg_l, wu_l, wd_l)
    return x
```

```python
import functools

import jax
import jax.numpy as jnp
import numpy as np
from jax import lax
from jax.experimental import pallas as pl
from jax.experimental.pallas import tpu as pltpu

F32 = jnp.float32
BF16 = jnp.bfloat16

D_MODEL = 1024
DEPTH = 2
GRID_W = 64
HEAD_DIM = 64
GROUP_W = 256
A_HEADS = 4
A_KV_HEADS = 2
A_WINDOW = 128
A_BLOCK = 128
B_CHUNK = 128
B_GROUPS = 4
B_GROUP_W = 64
C_POOLS = (2, 4, 8, 16)
C_GROUP_W = 64
D_HEADS = 4
D_WIN_ROWS = 8
D_WIN_COLS = 16
FF_DIM = 2816
ROPE_BASE = 10000.0
NORM_EPS = 1e-6
NEG_INF = -1e30
LOG2E = 1.4426950408889634
IN_TOTAL = 2048
OFF_AQ, OFF_AK, OFF_AV, OFF_BU, OFF_BV, OFF_C, OFF_DQ, OFF_DK, OFF_DV = (
    0, 256, 384, 512, 768, 1024, 1280, 1536, 1792)

MOD_ROWS = 16
MOD_TN = 1536
IN_TM = 512
MIX_TQ = 256
FFN_TM = 1024
FFN_TF = 256
POOL_HALO = 8
SCORE_LOOKAHEAD = 2
VMEM_LIMIT = 56 * 1024 * 1024


def _dot(a, b):
    return jnp.dot(a, b, preferred_element_type=F32)


def _dot_nt(a, b):
    return lax.dot_general(a, b, (((1,), (1,)), ((), ())), preferred_element_type=F32)


def _mod_kernel(c_ref, w_ref, b_ref, o_ref):
    cv = c_ref[...]
    act = cv * jax.nn.sigmoid(cv)
    o_ref[0] = _dot(act.astype(BF16), w_ref[0].astype(BF16)) + b_ref[0]


def _modulation(cvec, w_mod, b_mod):
    depth, d, n = w_mod.shape
    return pl.pallas_call(
        _mod_kernel,
        out_shape=jax.ShapeDtypeStruct((depth, MOD_ROWS, n), F32),
        grid=(depth, n // MOD_TN),
        in_specs=[
            pl.BlockSpec((MOD_ROWS, d), lambda l, j: (0, 0)),
            pl.BlockSpec((1, d, MOD_TN), lambda l, j: (l, 0, j)),
            pl.BlockSpec((1, 1, MOD_TN), lambda l, j: (l, 0, j)),
        ],
        out_specs=pl.BlockSpec((1, MOD_ROWS, MOD_TN), lambda l, j: (l, 0, j)),
        compiler_params=pltpu.CompilerParams(
            dimension_semantics=("arbitrary", "arbitrary"), vmem_limit_bytes=VMEM_LIMIT),
        name="modulation",
    )(cvec, w_mod, b_mod.reshape(depth, 1, n))


def _head_norm(t, seg, gain):
    sq = t * t
    hi = sq.astype(BF16)
    lo = (sq - hi.astype(F32)).astype(BF16)
    ms = _dot(hi, seg) + _dot(lo, seg)
    return t * lax.rsqrt(ms + NORM_EPS) * gain


def _rope(y, cos, sin_signed):
    w = y.shape[-1]
    up = pltpu.roll(y, w - 16, axis=1)
    dn = pltpu.roll(y, 16, axis=1)
    lane = lax.broadcasted_iota(jnp.int32, y.shape, 1)
    swapped = jnp.where((lane & 31) < 16, up, dn)
    return y * cos + swapped * sin_signed


def _in_kernel(x_ref, sh_ref, sc_ref, g_ref, w_ref, seg_ref, aqg_ref, akg_ref, dqg_ref, dkg_ref,
               cosq_ref, sinq_ref, cosk_ref, sink_ref, bvg_ref, ws_ref, bs_ref,
               qa_ref, ka_ref, va_ref, ob_ref, c_ref, qd_ref, kd_ref, vd_ref, v_scr, *, use_rope):
    xt = x_ref[0]
    ms = jnp.mean(xt * xt, axis=-1, keepdims=True)
    h = xt * lax.rsqrt(ms + NORM_EPS) * g_ref[...]
    h = (h * (1.0 + sc_ref[...]) + sh_ref[...]).astype(BF16)

    p = _dot(h, w_ref[...])

    def proj(off, width):
        return p[:, off:off + width]

    seg = seg_ref[...]
    qa = _head_norm(proj(OFF_AQ, 256), seg, aqg_ref[...])
    ka = _head_norm(proj(OFF_AK, 128), seg[:128, :128], akg_ref[...])
    if use_rope:
        qa = _rope(qa, cosq_ref[...], sinq_ref[...])
        ka = _rope(ka, cosk_ref[...], sink_ref[...])
    scale = HEAD_DIM ** -0.5 * LOG2E
    qa_ref[0] = (qa * scale).astype(BF16)
    ka_ref[0] = ka.astype(BF16)
    v_scr[:, 0:128] = proj(OFF_AV, 128)
    v_scr[:, 128:384] = proj(OFF_DV, 256)
    vat = v_scr[:, 0:128].T.astype(BF16)
    for j in range(vat.shape[1] // A_BLOCK):
        va_ref[0, j] = vat[:, j * A_BLOCK:(j + 1) * A_BLOCK]

    qd = _head_norm(proj(OFF_DQ, 256), seg, dqg_ref[...])
    qd_ref[0] = (qd * scale).astype(BF16)
    kd_ref[0] = _head_norm(proj(OFF_DK, 256), seg, dkg_ref[...]).astype(BF16)
    vdt = v_scr[:, 128:384].T.astype(BF16)
    for j in range(vdt.shape[1] // MIX_TQ):
        vd_ref[0, j] = vdt[:, j * MIX_TQ:(j + 1) * MIX_TQ]

    c_ref[0] = proj(OFF_C, 256)

    u = jax.nn.gelu(proj(OFF_BU, 256))
    v = jax.nn.gelu(proj(OFF_BV, 256))
    mu = jnp.mean(v, axis=-1, keepdims=True)
    vc = v - mu
    var = jnp.mean(vc * vc, axis=-1, keepdims=True)
    vn = (vc * lax.rsqrt(var + NORM_EPS) * bvg_ref[...]).astype(BF16)
    lane = lax.broadcasted_iota(jnp.int32, (B_CHUNK, GROUP_W), 1)
    zero = jnp.zeros((B_CHUNK, GROUP_W), BF16)
    tm = xt.shape[0]
    for ch in range(tm // B_CHUNK):
        rows = slice(ch * B_CHUNK, (ch + 1) * B_CHUNK)
        vch = vn[rows]
        stacked = jnp.concatenate(
            [jnp.where((lane >= g * B_GROUP_W) & (lane < (g + 1) * B_GROUP_W), vch, zero)
             for g in range(B_GROUPS)], axis=0)
        z = _dot(ws_ref[...], stacked) + bs_ref[...]
        ob_ref[0, rows, :] = (u[rows] * z).astype(BF16)


def _mod_spec(layer, row, which):
    return pl.BlockSpec((None, None, None, 1, D_MODEL), lambda bi, *_: (layer, row(bi), which, 0, 0))


def _layer_spec(layer, shape):
    return pl.BlockSpec((None,) + tuple(shape), lambda *_: (layer,) + (0,) * len(shape))


def _in_projection(x, mod, row, layer, g_mix, w_in, seg, aqg, akg, dqg, dkg, tables, bvg, ws_cat, bs_t,
                   *, use_rope):
    b, t, d = x.shape
    cosq, sinq, cosk, sink = tables
    tm = min(IN_TM, t)
    tok = lambda bi, i: (bi, i, 0)
    const2 = lambda bi, i: (0, 0)
    pos2 = lambda bi, i: (i, 0)
    assert tm % MIX_TQ == 0 and MIX_TQ % A_BLOCK == 0
    out_shapes = (
        jax.ShapeDtypeStruct((b, t, 256), BF16),
        jax.ShapeDtypeStruct((b, t, 128), BF16),
        jax.ShapeDtypeStruct((b, t // A_BLOCK, 128, A_BLOCK), BF16),
        jax.ShapeDtypeStruct((b, t, 256), BF16),
        jax.ShapeDtypeStruct((b, t, 256), F32),
        jax.ShapeDtypeStruct((b, t, 256), BF16),
        jax.ShapeDtypeStruct((b, t, 256), BF16),
        jax.ShapeDtypeStruct((b, t // MIX_TQ, 256, MIX_TQ), BF16),
    )
    blk4 = lambda bi, i: (bi, i, 0, 0)
    out_specs = tuple(
        pl.BlockSpec((1, tm // A_BLOCK, 128, A_BLOCK), blk4) if k == 2 else
        pl.BlockSpec((1, tm // MIX_TQ, 256, MIX_TQ), blk4) if k == 7 else
        pl.BlockSpec((1, tm, s.shape[-1]), tok)
        for k, s in enumerate(out_shapes))
    return pl.pallas_call(
        functools.partial(_in_kernel, use_rope=use_rope),
        out_shape=out_shapes,
        grid=(b, t // tm),
        in_specs=[
            pl.BlockSpec((1, tm, d), tok),
            _mod_spec(layer, row, 0),
            _mod_spec(layer, row, 1),
            _layer_spec(layer, (1, d)),
            _layer_spec(layer, (d, IN_TOTAL)),
            pl.BlockSpec((256, 256), const2),
            _layer_spec(layer, (1, 256)),
            _layer_spec(layer, (1, 128)),
            _layer_spec(layer, (1, 256)),
            _layer_spec(layer, (1, 256)),
            pl.BlockSpec((tm, 256), pos2),
            pl.BlockSpec((tm, 256), pos2),
            pl.BlockSpec((tm, 128), pos2),
            pl.BlockSpec((tm, 128), pos2),
            _layer_spec(layer, (1, 256)),
            _layer_spec(layer, (B_CHUNK, B_GROUPS * B_CHUNK)),
            _layer_spec(layer, (B_CHUNK, GROUP_W)),
        ],
        out_specs=out_specs,
        scratch_shapes=[pltpu.VMEM((tm, 384), F32)],
        compiler_params=pltpu.CompilerParams(
            dimension_semantics=("arbitrary", "arbitrary"), vmem_limit_bytes=VMEM_LIMIT),
        name="in_projection_rope" if use_rope else "in_projection_ctx",
    )(x, mod, mod, g_mix, w_in, seg, aqg, akg, dqg, dkg, cosq, sinq, cosk, sink, bvg, ws_cat, bs_t)


def _softmax_pv_t(parts, extra_logit):
    m = None
    for s, _ in parts:
        pm = jnp.max(s, axis=0, keepdims=True)
        m = pm if m is None else jnp.maximum(m, pm)
    if extra_logit is not None:
        m = jnp.maximum(m, extra_logit)
    denom = None
    acc = None
    for s, vt in parts:
        e = jnp.exp2(s - m)
        ps = jnp.sum(e, axis=0, keepdims=True)
        denom = ps if denom is None else denom + ps
        pv = _dot(vt, e.astype(BF16))
        acc = pv if acc is None else acc + pv
    if extra_logit is not None:
        denom = denom + jnp.exp2(extra_logit - m)
    return acc / denom


def _head_lanes(q, head, width=HEAD_DIM):
    lane = lax.broadcasted_iota(jnp.int32, q.shape, 1)
    return jnp.where((lane >= head * width) & (lane < (head + 1) * width), q, jnp.zeros_like(q))


def _pool_tile(c_ref, t0, n_tok, tq):
    y = c_ref[0, pl.ds(t0, tq), :]
    lo_start = pl.multiple_of(jnp.maximum(t0 - POOL_HALO, 0), POOL_HALO)
    hi_start = pl.multiple_of(jnp.minimum(t0 + tq, n_tok - POOL_HALO), POOL_HALO)
    lo = c_ref[0, pl.ds(lo_start, POOL_HALO), :]
    hi = c_ref[0, pl.ds(hi_start, POOL_HALO), :]
    lo = jnp.where(t0 > 0, lo, 0.0)
    hi = jnp.where(t0 + tq < n_tok, hi, 0.0)
    ypad = jnp.concatenate([lo, y, hi], axis=0)

    n = tq + 2 * POOL_HALO
    w2 = ypad + pltpu.roll(ypad, 1, axis=0)
    w4 = pltpu.roll(w2, 1, axis=0) + pltpu.roll(w2, n - 1, axis=0)
    w8 = pltpu.roll(w4, 2, axis=0) + pltpu.roll(w4, n - 2, axis=0)
    w16 = pltpu.roll(w8, 4, axis=0) + pltpu.roll(w8, n - 4, axis=0)
    sums = tuple(w[POOL_HALO:POOL_HALO + tq] for w in (w2, w4, w8, w16))

    lane = lax.broadcasted_iota(jnp.int32, (tq, GROUP_W), 1)
    pos = t0 + lax.broadcasted_iota(jnp.int32, (tq, GROUP_W), 0)
    pooled = None
    for gi, w in enumerate(C_POOLS):
        lo_i = jnp.clip(pos - w // 2, 0, n_tok)
        hi_i = jnp.clip(pos - w // 2 + w, 0, n_tok)
        mean = sums[gi] / (hi_i - lo_i).astype(F32)
        sel = (lane >= gi * C_GROUP_W) & (lane < (gi + 1) * C_GROUP_W)
        pooled = jnp.where(sel, mean, 0.0 if pooled is None else pooled)
    return pooled - y


def _mix_kernel(sink_ref, x_ref, g1_ref, qa_ref, ka_ref, va_ref, kca_ref, vca_ref, ob_ref, c_ref,
                wp_ref, cs_ref, qd_ref, kd_ref, vd_ref, kcd_ref, vcd_ref, bias_ref, amask_ref, wo_ref,
                o_ref, oa_scr, od_scr, mixa_scr, mixd_scr, *, local, n_tok, layer):
    tq = x_ref.shape[1]
    ti = pl.program_id(1)
    t0 = pl.multiple_of(ti * tq, tq)

    n_ctx_blk = vca_ref.shape[1]
    jobs = []

    def a_job(blk, kv):
        cols = slice(blk * A_BLOCK, (blk + 1) * A_BLOCK)
        rows = slice(kv * HEAD_DIM, (kv + 1) * HEAD_DIM)

        def scores():
            qblk = qa_ref[0, cols, :]
            qcat = jnp.concatenate([qblk[:, :128], qblk[:, 128:]], axis=0)
            qm = _head_lanes(qcat, kv)
            parts = []
            if local:
                nblk = ti * (tq // A_BLOCK) + blk
                b0 = jnp.clip(nblk - 1, 0, n_tok // A_BLOCK - 3)
                kstart = pl.multiple_of(b0 * A_BLOCK, A_BLOCK)
                kwin = ka_ref[0, pl.ds(kstart, 3 * A_BLOCK), :]
                last_blk = n_tok // A_BLOCK - 1
                kind = jnp.where(nblk == 0, 1, jnp.where(nblk == last_blk, 2, 0))
                s_loc = _dot_nt(kwin, qm) + amask_ref[kind]
                vt = jnp.concatenate([va_ref[0, b0 + j, rows, :] for j in range(3)], axis=1)
                parts.append((s_loc, vt))
            vct = jnp.concatenate([vca_ref[0, j, rows, :] for j in range(n_ctx_blk)], axis=1)
            parts.append((_dot_nt(kca_ref[0], qm), vct))
            return parts

        def finish(parts):
            col = lax.broadcasted_iota(jnp.int32, (1, 2 * A_BLOCK), 1)
            sink = jnp.where(col < A_BLOCK, sink_ref[layer, 2 * kv], sink_ref[layer, 2 * kv + 1])
            o2 = _softmax_pv_t(parts, sink)
            oa_scr[(2 * kv) * 64:(2 * kv + 1) * 64, cols] = o2[:, :A_BLOCK]
            oa_scr[(2 * kv + 1) * 64:(2 * kv + 2) * 64, cols] = o2[:, A_BLOCK:]

        return scores, finish

    def d_job(hd):
        rows = slice(hd * HEAD_DIM, (hd + 1) * HEAD_DIM)

        def scores():
            qm = _head_lanes(qd_ref[0], hd)
            parts = []
            if local:
                d0 = jnp.clip(ti - 1, 0, n_tok // tq - 3)
                kwin = kd_ref[0, pl.ds(pl.multiple_of(d0 * tq, tq), 3 * tq), :]
                s_loc = _dot_nt(kwin, qm) + bias_ref[0, hd]
                vt = jnp.concatenate([vd_ref[0, d0 + j, rows, :] for j in range(3)], axis=1)
                parts.append((s_loc, vt))
            parts.append((_dot_nt(kcd_ref[0], qm), vcd_ref[0, 0, rows, :]))
            return parts

        def finish(parts):
            od_scr[rows, :] = _softmax_pv_t(parts, None)

        return scores, finish

    for blk in range(tq // A_BLOCK):
        for kv in range(A_KV_HEADS):
            jobs.append(a_job(blk, kv))
    for hd in range(D_HEADS):
        jobs.append(d_job(hd))
    ahead = min(SCORE_LOOKAHEAD, len(jobs))
    pending = [jobs[j][0]() for j in range(ahead)]
    for j, (_, finish) in enumerate(jobs):
        if j + ahead < len(jobs):
            pending.append(jobs[j + ahead][0]())
        finish(pending.pop(0))

    pooled = _pool_tile(c_ref, t0, n_tok, tq)
    out_c = _dot(pooled.astype(BF16), wp_ref[...]) * cs_ref[...]

    mixa_scr[...] = oa_scr[...].T.astype(BF16)
    mixd_scr[...] = od_scr[...].T.astype(BF16)
    mix = (_dot(mixa_scr[...], wo_ref[0:256, :]) + _dot(ob_ref[0], wo_ref[256:512, :])
           + _dot(out_c.astype(BF16), wo_ref[512:768, :]) + _dot(mixd_scr[...], wo_ref[768:1024, :]))
    o_ref[0] = x_ref[0] + g1_ref[...] * mix


def _mixer(x, mod, row, layer, qa, ka, va, kca, vca, ob, cin, wpool, cscale, qd, kd, vd, kcd, vcd, bias,
           amask, w_out, sink, *, local):
    b, t, d = x.shape
    lc = kca.shape[1]
    tq = MIX_TQ
    n_tiles = t // tq
    tok = lambda bi, i: (bi, i, 0)
    per_b = lambda bi, i: (bi, 0, 0)
    per_b4 = lambda bi, i: (bi, 0, 0, 0)
    const2 = lambda bi, i: (0, 0)

    def bias_map(bi, i):
        kind = jnp.where(i == 0, 1, jnp.where(i == n_tiles - 1, 2, 0))
        return (layer, kind, 0, 0, 0)

    return pl.pallas_call(
        functools.partial(_mix_kernel, local=local, n_tok=t, layer=layer),
        out_shape=jax.ShapeDtypeStruct((b, t, d), F32),
        grid=(b, n_tiles),
        in_specs=[
            pl.BlockSpec(memory_space=pltpu.SMEM),
            pl.BlockSpec((1, tq, d), tok),
            _mod_spec(layer, row, 2),
            pl.BlockSpec((1, tq, 256), tok),
            pl.BlockSpec((1, t, 128), per_b),
            pl.BlockSpec((1, t // A_BLOCK, 128, A_BLOCK), per_b4),
            pl.BlockSpec((1, lc, 128), per_b),
            pl.BlockSpec((1, lc // A_BLOCK, 128, A_BLOCK), per_b4),
            pl.BlockSpec((1, tq, 256), tok),
            pl.BlockSpec((1, t, 256), per_b),
            _layer_spec(layer, (256, 256)),
            _layer_spec(layer, (1, 256)),
            pl.BlockSpec((1, tq, 256), tok),
            pl.BlockSpec((1, t, 256), per_b),
            pl.BlockSpec((1, t // tq, 256, tq), per_b4),
            pl.BlockSpec((1, lc, 256), per_b),
            pl.BlockSpec((1, lc // tq, 256, tq), per_b4),
            pl.BlockSpec((None, 1, D_HEADS, 3 * tq, tq), bias_map),
            pl.BlockSpec((3, 3 * A_BLOCK, 2 * A_BLOCK), lambda bi, i: (0, 0, 0)),
            _layer_spec(layer, (d, d)),
        ],
        out_specs=pl.BlockSpec((1, tq, d), tok),
        scratch_shapes=[pltpu.VMEM((GROUP_W, tq), F32), pltpu.VMEM((GROUP_W, tq), F32),
                        pltpu.VMEM((tq, GROUP_W), BF16), pltpu.VMEM((tq, GROUP_W), BF16)],
        compiler_params=pltpu.CompilerParams(
            dimension_semantics=("arbitrary", "arbitrary"), vmem_limit_bytes=VMEM_LIMIT),
        name="mixer_latent" if local else "mixer_ctx",
    )(sink, x, mod, qa, ka, va, kca, vca, ob, cin, wpool, cscale, qd, kd, vd, kcd, vcd, bias, amask,
      w_out)


def _ffn_kernel(x_ref, sh_ref, sc_ref, g2_ref, gn_ref, wg_ref, wu_ref, wd_ref, o_ref, act_scr):
    xt = x_ref[0]
    ms = jnp.mean(xt * xt, axis=-1, keepdims=True)
    h = xt * lax.rsqrt(ms + NORM_EPS) * gn_ref[...]
    h = (h * (1.0 + sc_ref[...]) + sh_ref[...]).astype(BF16)
    for c in range(FF_DIM // FFN_TF):
        cols = slice(c * FFN_TF, (c + 1) * FFN_TF)
        gate = _dot(h, wg_ref[:, cols])
        up = _dot(h, wu_ref[:, cols])
        act_scr[:, cols] = (gate * jax.nn.sigmoid(gate) * up).astype(BF16)
    o_ref[0] = xt + g2_ref[...] * _dot(act_scr[...], wd_ref[...])


def _ffn(x, mod, row, layer, g_ffn, wg, wu, wd):
    b, t, d = x.shape
    tm = min(FFN_TM, t)
    tok = lambda bi, i: (bi, i, 0)
    per_layer = lambda bi, i: (layer, 0, 0)
    resident = pl.Buffered(1)
    return pl.pallas_call(
        _ffn_kernel,
        out_shape=jax.ShapeDtypeStruct((b, t, d), F32),
        grid=(b, t // tm),
        in_specs=[
            pl.BlockSpec((1, tm, d), tok),
            _mod_spec(layer, row, 3),
            _mod_spec(layer, row, 4),
            _mod_spec(layer, row, 5),
            _layer_spec(layer, (1, d)),
            pl.BlockSpec((None, d, FF_DIM), per_layer, pipeline_mode=resident),
            pl.BlockSpec((None, d, FF_DIM), per_layer, pipeline_mode=resident),
            pl.BlockSpec((None, FF_DIM, d), per_layer, pipeline_mode=resident),
        ],
        out_specs=pl.BlockSpec((1, tm, d), tok),
        scratch_shapes=[pltpu.VMEM((tm, FF_DIM), BF16)],
        compiler_params=pltpu.CompilerParams(
            dimension_semantics=("arbitrary", "arbitrary"), vmem_limit_bytes=VMEM_LIMIT),
        name="ffn",
    )(x, mod, mod, mod, g_ffn, wg, wu, wd)


def _rope_tables(s):
    t = np.arange(s)
    half = 16
    inv = np.power(np.float32(ROPE_BASE), -np.arange(half, dtype=np.float32) / half).astype(np.float32)
    ang_r = (t // GRID_W).astype(np.float32)[:, None] * inv[None, :]
    ang_c = (t % GRID_W).astype(np.float32)[:, None] * inv[None, :]
    cos = np.concatenate([np.cos(ang_r)] * 2 + [np.cos(ang_c)] * 2, axis=-1)
    sin = np.concatenate([-np.sin(ang_r), np.sin(ang_r), -np.sin(ang_c), np.sin(ang_c)], axis=-1)
    return cos.astype(np.float32), sin.astype(np.float32)


def _window_mask():
    kk = np.arange(3 * A_BLOCK)[:, None]
    qq = np.arange(2 * A_BLOCK)[None, :] % A_BLOCK
    kinds = [np.where(np.abs(kk - qq - back * A_BLOCK) <= A_WINDOW, 0.0, NEG_INF) for back in (1, 0, 2)]
    return jnp.asarray(np.stack(kinds), F32)


def _neighbour_bias(rpb, tq, n_rows):
    depth = rpb.shape[0]
    rows_per_tile = tq // GRID_W
    cidx = np.arange(GRID_W)
    col_start = np.clip(cidx - D_WIN_COLS // 2, 0, GRID_W - D_WIN_COLS)
    col_ok = (cidx[None, :] >= col_start[:, None]) & (cidx[None, :] < col_start[:, None] + D_WIN_COLS)
    edge = GRID_W - D_WIN_COLS
    lead = rpb.shape[:-1]
    ext = jnp.concatenate([jnp.broadcast_to(rpb[..., :1], lead + (edge,)), rpb.astype(F32),
                           jnp.broadcast_to(rpb[..., -1:], lead + (edge,))], axis=-1)
    wrapped = jnp.concatenate([ext[..., GRID_W - 1:], ext[..., :1], ext[..., :GRID_W - 1]], axis=-1)
    period = 2 * GRID_W
    flat = jnp.tile(wrapped, (1,) * len(lead) + (GRID_W,))[..., :GRID_W * (period - 1)]
    toeplitz = flat.reshape(lead + (GRID_W, period - 1))[..., :GRID_W]
    toeplitz = jnp.where(col_ok, toeplitz, NEG_INF)
    toeplitz_t = jnp.swapaxes(toeplitz, -1, -2)
    win_rows = 3 * rows_per_tile
    kinds = []
    for r0, ws in ((2 * rows_per_tile, rows_per_tile), (0, 0), (n_rows - rows_per_tile, n_rows - win_rows)):
        per_row = []
        for i in range(rows_per_tile):
            r = r0 + i
            rs = min(max(r - D_WIN_ROWS // 2, 0), n_rows - D_WIN_ROWS)
            before = rs - ws
            first = rs - r + D_WIN_ROWS - 1
            seen = toeplitz_t[:, :, first:first + D_WIN_ROWS]
            pad = [(0, 0, 0)] * 2 + [(before, win_rows - D_WIN_ROWS - before, 0)] + [(0, 0, 0)] * 2
            per_row.append(lax.pad(seen, jnp.asarray(NEG_INF, F32), pad))
        kinds.append(jnp.stack(per_row, axis=4))
    bias = jnp.stack(kinds, axis=1)
    return bias.reshape(depth, 3, D_HEADS, win_rows * GRID_W, tq)


def kernel(x, c, ctx, c_ctx, w_mod, b_mod, g_mix, g_ffn, w_in, w_out, a_q_gain, a_k_gain, a_sink,
           b_v_gain, b_w_s, b_b_s, c_w_pool, c_scale, d_q_gain, d_k_gain, d_rpb, w_gate, w_up, w_down):
    bsz, s, d = x.shape
    lc = ctx.shape[1]
    n_rows = s // GRID_W
    assert s % MIX_TQ == 0 and n_rows >= 2 * D_WIN_ROWS and MIX_TQ // GRID_W == D_WIN_ROWS // 2

    cvec = jnp.zeros((MOD_ROWS, d), F32).at[:bsz].set(c).at[bsz].set(c_ctx)
    mod = _modulation(cvec, w_mod, b_mod).reshape(DEPTH, MOD_ROWS, 6, 1, d)
    lat_row = lambda bi: bi
    ctx_row = lambda bi: bsz

    seg = jnp.asarray(np.kron(np.eye(4), np.full((64, 64), 1.0 / 64)), BF16)
    cos1, sin1 = _rope_tables(s)
    tables_lat = tuple(jnp.asarray(np.tile(tab, (1, reps)))
                       for tab, reps in ((cos1, 4), (sin1, 4), (cos1, 2), (sin1, 2)))
    tables_ctx = tuple(jnp.zeros((lc, w), F32) for w in (256, 256, 128, 128))

    amask = _window_mask()
    bias_all = _neighbour_bias(d_rpb * LOG2E, MIX_TQ, n_rows)

    w_in_b = jnp.concatenate([w_in[:, :, h * 64:(h + 1) * 64] for h in (0, 2, 1, 3)]
                             + [w_in[:, :, OFF_AK:]], axis=2).astype(BF16)
    w_out_b = w_out.astype(BF16)
    wg_b, wu_b, wd_b = w_gate.astype(BF16), w_up.astype(BF16), w_down.astype(BF16)
    gmix = g_mix.reshape(DEPTH, 1, d)
    gffn = g_ffn.reshape(DEPTH, 1, d)
    aqg = jnp.tile(a_q_gain, (1, 4)).reshape(DEPTH, 1, 256)
    akg = jnp.tile(a_k_gain, (1, 2)).reshape(DEPTH, 1, 128)
    dqg = jnp.tile(d_q_gain, (1, 4)).reshape(DEPTH, 1, 256)
    dkg = jnp.tile(d_k_gain, (1, 4)).reshape(DEPTH, 1, 256)
    bvg = b_v_gain.reshape(DEPTH, 1, 256)
    ws_cat = jnp.transpose(b_w_s, (0, 2, 1, 3)).reshape(DEPTH, B_CHUNK, B_GROUPS * B_CHUNK).astype(BF16)
    bs_t = jnp.repeat(jnp.swapaxes(b_b_s, 1, 2), B_GROUP_W, axis=2)
    wpool = jnp.einsum('lgcd,gh->lgchd', c_w_pool, jnp.eye(len(C_POOLS), dtype=F32)
                       ).reshape(DEPTH, GROUP_W, GROUP_W).astype(BF16)
    cscale = c_scale.reshape(DEPTH, 1, 256)
    sink = a_sink * LOG2E
    common = (gmix, w_in_b, seg, aqg, akg, dqg, dkg)

    xc = ctx
    for l in range(DEPTH):
        last = l == DEPTH - 1
        lat = _in_projection(x, mod, lat_row, l, *common, tables_lat, bvg, ws_cat, bs_t, use_rope=True)
        con = _in_projection(xc, mod, ctx_row, l, *common, tables_ctx, bvg, ws_cat, bs_t, use_rope=False)
        qa, ka, va, ob, cin, qd, kd, vd = lat
        cqa, cka, cva, cob, ccin, cqd, ckd, cvd = con

        x = _mixer(x, mod, lat_row, l, qa, ka, va, cka, cva, ob, cin, wpool, cscale, qd, kd, vd, ckd, cvd,
                   bias_all, amask, w_out_b, sink, local=True)
        x = _ffn(x, mod, lat_row, l, gffn, wg_b, wu_b, wd_b)
        if not last:
            xc = _mixer(xc, mod, ctx_row, l, cqa, cka, cva, cka, cva, cob, ccin, wpool, cscale, cqd, ckd,
                        cvd, ckd, cvd, bias_all, amask, w_out_b, sink, local=False)
            xc = _ffn(xc, mod, ctx_row, l, gffn, wg_b, wu_b, wd_b)
    return x
```

```python
import functools

import jax
import jax.numpy as jnp
import numpy as np
from jax import lax
from jax.experimental import pallas as pl
from jax.experimental.pallas import tpu as pltpu

F32 = jnp.float32
BF16 = jnp.bfloat16

D_MODEL = 1024
DEPTH = 2
GRID_W = 64
HEAD_DIM = 64
GROUP_W = 256
A_HEADS = 4
A_KV_HEADS = 2
A_WINDOW = 128
A_BLOCK = 128
B_CHUNK = 128
B_GROUPS = 4
B_GROUP_W = 64
C_POOLS = (2, 4, 8, 16)
C_GROUP_W = 64
D_HEADS = 4
D_WIN_ROWS = 8
D_WIN_COLS = 16
FF_DIM = 2816
ROPE_BASE = 10000.0
NORM_EPS = 1e-6
NEG_INF = -1e30
LOG2E = 1.4426950408889634
IN_TOTAL = 2048
OFF_AQ, OFF_AK, OFF_AV, OFF_BU, OFF_BV, OFF_C, OFF_DQ, OFF_DK, OFF_DV = (
    0, 256, 384, 512, 768, 1024, 1280, 1536, 1792)

MOD_ROWS = 16
MOD_TN = 1536
IN_TM = 512
MIX_TQ = 256
FFN_TM = 1024
FFN_TF = 256
POOL_HALO = 8
SCORE_LOOKAHEAD = 2
VMEM_LIMIT = 56 * 1024 * 1024


def _dot(a, b):
    return jnp.dot(a, b, preferred_element_type=F32)


def _dot_nt(a, b):
    return lax.dot_general(a, b, (((1,), (1,)), ((), ())), preferred_element_type=F32)


def _mod_kernel(c_ref, w_ref, b_ref, o_ref):
    cv = c_ref[...]
    act = cv * jax.nn.sigmoid(cv)
    o_ref[0] = _dot(act.astype(BF16), w_ref[0].astype(BF16)) + b_ref[0]


def _modulation(cvec, w_mod, b_mod):
    depth, d, n = w_mod.shape
    return pl.pallas_call(
        _mod_kernel,
        out_shape=jax.ShapeDtypeStruct((depth, MOD_ROWS, n), F32),
        grid=(depth, n // MOD_TN),
        in_specs=[
            pl.BlockSpec((MOD_ROWS, d), lambda l, j: (0, 0)),
            pl.BlockSpec((1, d, MOD_TN), lambda l, j: (l, 0, j)),
            pl.BlockSpec((1, 1, MOD_TN), lambda l, j: (l, 0, j)),
        ],
        out_specs=pl.BlockSpec((1, MOD_ROWS, MOD_TN), lambda l, j: (l, 0, j)),
        compiler_params=pltpu.CompilerParams(
            dimension_semantics=("arbitrary", "arbitrary"), vmem_limit_bytes=VMEM_LIMIT),
        name="modulation",
    )(cvec, w_mod, b_mod.reshape(depth, 1, n))


def _head_norm(t, seg, gain):
    sq = t * t
    hi = sq.astype(BF16)
    lo = (sq - hi.astype(F32)).astype(BF16)
    ms = _dot(hi, seg) + _dot(lo, seg)
    return t * lax.rsqrt(ms + NORM_EPS) * gain


def _rope(y, cos, sin_signed):
    w = y.shape[-1]
    up = pltpu.roll(y, w - 16, axis=1)
    dn = pltpu.roll(y, 16, axis=1)
    lane = lax.broadcasted_iota(jnp.int32, y.shape, 1)
    swapped = jnp.where((lane & 31) < 16, up, dn)
    return y * cos + swapped * sin_signed


def _swap_middle_heads(q):
    lane = lax.broadcasted_iota(jnp.int32, q.shape, 1)
    from_right = pltpu.roll(q, 3 * HEAD_DIM, axis=1)
    from_left = pltpu.roll(q, HEAD_DIM, axis=1)
    return jnp.where((lane >= HEAD_DIM) & (lane < 2 * HEAD_DIM), from_right,
                     jnp.where((lane >= 2 * HEAD_DIM) & (lane < 3 * HEAD_DIM), from_left, q))


def _in_kernel(x_ref, sh_ref, sc_ref, g_ref, w_ref, seg_ref, aqg_ref, akg_ref, dqg_ref, dkg_ref,
               cosq_ref, sinq_ref, cosk_ref, sink_ref, bvg_ref, ws_ref, bs_ref,
               qa_ref, ka_ref, va_ref, ob_ref, c_ref, qd_ref, kd_ref, vd_ref, v_scr, *, use_rope):
    xt = x_ref[0]
    ms = jnp.mean(xt * xt, axis=-1, keepdims=True)
    h = xt * lax.rsqrt(ms + NORM_EPS) * g_ref[...]
    h = (h * (1.0 + sc_ref[...]) + sh_ref[...]).astype(BF16)

    p = _dot(h, w_ref[...])

    def proj(off, width):
        return p[:, off:off + width]

    seg = seg_ref[...]
    qa = _head_norm(proj(OFF_AQ, 256), seg, aqg_ref[...])
    ka = _head_norm(proj(OFF_AK, 128), seg[:128, :128], akg_ref[...])
    if use_rope:
        qa = _rope(qa, cosq_ref[...], sinq_ref[...])
        ka = _rope(ka, cosk_ref[...], sink_ref[...])
    scale = HEAD_DIM ** -0.5 * LOG2E
    qa_ref[0] = (_swap_middle_heads(qa) * scale).astype(BF16)
    ka_ref[0] = ka.astype(BF16)
    v_scr[:, 0:128] = proj(OFF_AV, 128)
    v_scr[:, 128:384] = proj(OFF_DV, 256)
    vat = v_scr[:, 0:128].T.astype(BF16)
    for j in range(vat.shape[1] // A_BLOCK):
        va_ref[0, j] = vat[:, j * A_BLOCK:(j + 1) * A_BLOCK]

    qd = _head_norm(proj(OFF_DQ, 256), seg, dqg_ref[...])
    qd_ref[0] = (qd * scale).astype(BF16)
    kd_ref[0] = _head_norm(proj(OFF_DK, 256), seg, dkg_ref[...]).astype(BF16)
    vdt = v_scr[:, 128:384].T.astype(BF16)
    for j in range(vdt.shape[1] // MIX_TQ):
        vd_ref[0, j] = vdt[:, j * MIX_TQ:(j + 1) * MIX_TQ]

    c_ref[0] = proj(OFF_C, 256)

    u = jax.nn.gelu(proj(OFF_BU, 256))
    v = jax.nn.gelu(proj(OFF_BV, 256))
    mu = jnp.mean(v, axis=-1, keepdims=True)
    vc = v - mu
    var = jnp.mean(vc * vc, axis=-1, keepdims=True)
    vn = (vc * lax.rsqrt(var + NORM_EPS) * bvg_ref[...]).astype(BF16)
    lane = lax.broadcasted_iota(jnp.int32, (B_CHUNK, GROUP_W), 1)
    zero = jnp.zeros((B_CHUNK, GROUP_W), BF16)
    tm = xt.shape[0]
    for ch in range(tm // B_CHUNK):
        rows = slice(ch * B_CHUNK, (ch + 1) * B_CHUNK)
        vch = vn[rows]
        stacked = jnp.concatenate(
            [jnp.where((lane >= g * B_GROUP_W) & (lane < (g + 1) * B_GROUP_W), vch, zero)
             for g in range(B_GROUPS)], axis=0)
        z = _dot(ws_ref[...], stacked) + bs_ref[...]
        ob_ref[0, rows, :] = (u[rows] * z).astype(BF16)


def _mod_spec(layer, row, which):
    return pl.BlockSpec((None, None, None, 1, D_MODEL), lambda bi, *_: (layer, row(bi), which, 0, 0))


def _layer_spec(layer, shape):
    return pl.BlockSpec((None,) + tuple(shape), lambda *_: (layer,) + (0,) * len(shape))


def _in_projection(x, mod, row, layer, g_mix, w_in, seg, aqg, akg, dqg, dkg, tables, bvg, ws_cat, bs_t,
                   *, use_rope):
    b, t, d = x.shape
    cosq, sinq, cosk, sink = tables
    tm = min(IN_TM, t)
    tok = lambda bi, i: (bi, i, 0)
    const2 = lambda bi, i: (0, 0)
    pos2 = lambda bi, i: (i, 0)
    assert tm % MIX_TQ == 0 and MIX_TQ % A_BLOCK == 0
    out_shapes = (
        jax.ShapeDtypeStruct((b, t, 256), BF16),
        jax.ShapeDtypeStruct((b, t, 128), BF16),
        jax.ShapeDtypeStruct((b, t // A_BLOCK, 128, A_BLOCK), BF16),
        jax.ShapeDtypeStruct((b, t, 256), BF16),
        jax.ShapeDtypeStruct((b, t, 256), F32),
        jax.ShapeDtypeStruct((b, t, 256), BF16),
        jax.ShapeDtypeStruct((b, t, 256), BF16),
        jax.ShapeDtypeStruct((b, t // MIX_TQ, 256, MIX_TQ), BF16),
    )
    blk4 = lambda bi, i: (bi, i, 0, 0)
    out_specs = tuple(
        pl.BlockSpec((1, tm // A_BLOCK, 128, A_BLOCK), blk4) if k == 2 else
        pl.BlockSpec((1, tm // MIX_TQ, 256, MIX_TQ), blk4) if k == 7 else
        pl.BlockSpec((1, tm, s.shape[-1]), tok)
        for k, s in enumerate(out_shapes))
    return pl.pallas_call(
        functools.partial(_in_kernel, use_rope=use_rope),
        out_shape=out_shapes,
        grid=(b, t // tm),
        in_specs=[
            pl.BlockSpec((1, tm, d), tok),
            _mod_spec(layer, row, 0),
            _mod_spec(layer, row, 1),
            _layer_spec(layer, (1, d)),
            _layer_spec(layer, (d, IN_TOTAL)),
            pl.BlockSpec((256, 256), const2),
            _layer_spec(layer, (1, 256)),
            _layer_spec(layer, (1, 128)),
            _layer_spec(layer, (1, 256)),
            _layer_spec(layer, (1, 256)),
            pl.BlockSpec((tm, 256), pos2),
            pl.BlockSpec((tm, 256), pos2),
            pl.BlockSpec((tm, 128), pos2),
            pl.BlockSpec((tm, 128), pos2),
            _layer_spec(layer, (1, 256)),
            _layer_spec(layer, (B_CHUNK, B_GROUPS * B_CHUNK)),
            _layer_spec(layer, (B_CHUNK, GROUP_W)),
        ],
        out_specs=out_specs,
        scratch_shapes=[pltpu.VMEM((tm, 384), F32)],
        compiler_params=pltpu.CompilerParams(
            dimension_semantics=("arbitrary", "arbitrary"), vmem_limit_bytes=VMEM_LIMIT),
        name="in_projection_rope" if use_rope else "in_projection_ctx",
    )(x, mod, mod, g_mix, w_in, seg, aqg, akg, dqg, dkg, cosq, sinq, cosk, sink, bvg, ws_cat, bs_t)


def _softmax_pv_t(parts, extra_logit):
    m = None
    for s, _ in parts:
        pm = jnp.max(s, axis=0, keepdims=True)
        m = pm if m is None else jnp.maximum(m, pm)
    if extra_logit is not None:
        m = jnp.maximum(m, extra_logit)
    denom = None
    acc = None
    for s, vt in parts:
        e = jnp.exp2(s - m)
        ps = jnp.sum(e, axis=0, keepdims=True)
        denom = ps if denom is None else denom + ps
        pv = _dot(vt, e.astype(BF16))
        acc = pv if acc is None else acc + pv
    if extra_logit is not None:
        denom = denom + jnp.exp2(extra_logit - m)
    return acc / denom


def _head_lanes(q, head, width=HEAD_DIM):
    lane = lax.broadcasted_iota(jnp.int32, q.shape, 1)
    return jnp.where((lane >= head * width) & (lane < (head + 1) * width), q, jnp.zeros_like(q))


def _pool_tile(c_ref, t0, n_tok, tq):
    y = c_ref[0, pl.ds(t0, tq), :]
    lo_start = pl.multiple_of(jnp.maximum(t0 - POOL_HALO, 0), POOL_HALO)
    hi_start = pl.multiple_of(jnp.minimum(t0 + tq, n_tok - POOL_HALO), POOL_HALO)
    lo = c_ref[0, pl.ds(lo_start, POOL_HALO), :]
    hi = c_ref[0, pl.ds(hi_start, POOL_HALO), :]
    lo = jnp.where(t0 > 0, lo, 0.0)
    hi = jnp.where(t0 + tq < n_tok, hi, 0.0)
    ypad = jnp.concatenate([lo, y, hi], axis=0)

    n = tq + 2 * POOL_HALO
    w2 = ypad + pltpu.roll(ypad, 1, axis=0)
    w4 = pltpu.roll(w2, 1, axis=0) + pltpu.roll(w2, n - 1, axis=0)
    w8 = pltpu.roll(w4, 2, axis=0) + pltpu.roll(w4, n - 2, axis=0)
    w16 = pltpu.roll(w8, 4, axis=0) + pltpu.roll(w8, n - 4, axis=0)
    sums = tuple(w[POOL_HALO:POOL_HALO + tq] for w in (w2, w4, w8, w16))

    lane = lax.broadcasted_iota(jnp.int32, (tq, GROUP_W), 1)
    pos = t0 + lax.broadcasted_iota(jnp.int32, (tq, GROUP_W), 0)
    pooled = None
    for gi, w in enumerate(C_POOLS):
        lo_i = jnp.clip(pos - w // 2, 0, n_tok)
        hi_i = jnp.clip(pos - w // 2 + w, 0, n_tok)
        mean = sums[gi] / (hi_i - lo_i).astype(F32)
        sel = (lane >= gi * C_GROUP_W) & (lane < (gi + 1) * C_GROUP_W)
        pooled = jnp.where(sel, mean, 0.0 if pooled is None else pooled)
    return pooled - y


def _mix_kernel(sink_ref, x_ref, g1_ref, qa_ref, ka_ref, va_ref, kca_ref, vca_ref, ob_ref, c_ref,
                wp_ref, cs_ref, qd_ref, kd_ref, vd_ref, kcd_ref, vcd_ref, bias_ref, amask_ref, wo_ref,
                o_ref, oa_scr, od_scr, mixa_scr, mixd_scr, *, local, n_tok, layer):
    tq = x_ref.shape[1]
    ti = pl.program_id(1)
    t0 = pl.multiple_of(ti * tq, tq)

    if local:
        n_tiles = n_tok // tq
        half = D_WIN_ROWS // 2
        pair_entries = []
        for jj in range(3 * tq // GRID_W):
            row_entries = []
            for i in range(0, tq // GRID_W, 2):
                interior = PAIR_INTERIOR + (jj - half - i) + PAIR_COUNT // 2 - 1
                first = PAIR_EDGE + (jj - i) + PAIR_COUNT // 2 - 1 if jj < D_WIN_ROWS else PAIR_MASKED
                last = (PAIR_EDGE + (jj - D_WIN_ROWS - i) + PAIR_COUNT // 2 - 1 if jj >= half
                        else PAIR_MASKED)
                row_entries.append(jnp.where(ti == 0, first, jnp.where(ti == n_tiles - 1, last, interior)))
            pair_entries.append(row_entries)

    n_ctx_blk = vca_ref.shape[1]
    jobs = []

    def a_job(blk, kv):
        cols = slice(blk * A_BLOCK, (blk + 1) * A_BLOCK)
        rows = slice(kv * HEAD_DIM, (kv + 1) * HEAD_DIM)

        def scores():
            qblk = qa_ref[0, cols, :]
            qcat = jnp.concatenate([qblk[:, :128], qblk[:, 128:]], axis=0)
            qm = _head_lanes(qcat, kv)
            parts = []
            if local:
                nblk = ti * (tq // A_BLOCK) + blk
                b0 = jnp.clip(nblk - 1, 0, n_tok // A_BLOCK - 3)
                kstart = pl.multiple_of(b0 * A_BLOCK, A_BLOCK)
                kwin = ka_ref[0, pl.ds(kstart, 3 * A_BLOCK), :]
                last_blk = n_tok // A_BLOCK - 1
                kind = jnp.where(nblk == 0, 1, jnp.where(nblk == last_blk, 2, 0))
                s_loc = _dot_nt(kwin, qm) + amask_ref[kind]
                vt = jnp.concatenate([va_ref[0, b0 + j, rows, :] for j in range(3)], axis=1)
                parts.append((s_loc, vt))
            vct = jnp.concatenate([vca_ref[0, j, rows, :] for j in range(n_ctx_blk)], axis=1)
            parts.append((_dot_nt(kca_ref[0], qm), vct))
            return parts

        def finish(parts):
            col = lax.broadcasted_iota(jnp.int32, (1, 2 * A_BLOCK), 1)
            sink = jnp.where(col < A_BLOCK, sink_ref[layer, 2 * kv], sink_ref[layer, 2 * kv + 1])
            o2 = _softmax_pv_t(parts, sink)
            oa_scr[(2 * kv) * 64:(2 * kv + 1) * 64, cols] = o2[:, :A_BLOCK]
            oa_scr[(2 * kv + 1) * 64:(2 * kv + 2) * 64, cols] = o2[:, A_BLOCK:]

        return scores, finish

    def d_job(hd):
        rows = slice(hd * HEAD_DIM, (hd + 1) * HEAD_DIM)

        def scores():
            qm = _head_lanes(qd_ref[0], hd)
            parts = []
            if local:
                d0 = jnp.clip(ti - 1, 0, n_tok // tq - 3)
                kwin = kd_ref[0, pl.ds(pl.multiple_of(d0 * tq, tq), 3 * tq), :]
                bias = jnp.concatenate(
                    [jnp.concatenate([bias_ref[hd, e] for e in row_entries], axis=1)
                     for row_entries in pair_entries], axis=0)
                s_loc = _dot_nt(kwin, qm) + bias
                vt = jnp.concatenate([vd_ref[0, d0 + j, rows, :] for j in range(3)], axis=1)
                parts.append((s_loc, vt))
            parts.append((_dot_nt(kcd_ref[0], qm), vcd_ref[0, 0, rows, :]))
            return parts

        def finish(parts):
            od_scr[rows, :] = _softmax_pv_t(parts, None)

        return scores, finish

    for blk in range(tq // A_BLOCK):
        for kv in range(A_KV_HEADS):
            jobs.append(a_job(blk, kv))
    for hd in range(D_HEADS):
        jobs.append(d_job(hd))
    ahead = min(SCORE_LOOKAHEAD, len(jobs))
    pending = [jobs[j][0]() for j in range(ahead)]
    for j, (_, finish) in enumerate(jobs):
        if j + ahead < len(jobs):
            pending.append(jobs[j + ahead][0]())
        finish(pending.pop(0))

    pooled = _pool_tile(c_ref, t0, n_tok, tq)
    out_c = _dot(pooled.astype(BF16), wp_ref[...]) * cs_ref[...]

    mixa_scr[...] = oa_scr[...].T.astype(BF16)
    mixd_scr[...] = od_scr[...].T.astype(BF16)
    mix = (_dot(mixa_scr[...], wo_ref[0:256, :]) + _dot(ob_ref[0], wo_ref[256:512, :])
           + _dot(out_c.astype(BF16), wo_ref[512:768, :]) + _dot(mixd_scr[...], wo_ref[768:1024, :]))
    o_ref[0] = x_ref[0] + g1_ref[...] * mix


def _mixer(x, mod, row, layer, qa, ka, va, kca, vca, ob, cin, wpool, cscale, qd, kd, vd, kcd, vcd, bias,
           amask, w_out, sink, *, local):
    b, t, d = x.shape
    lc = kca.shape[1]
    tq = MIX_TQ
    n_tiles = t // tq
    tok = lambda bi, i: (bi, i, 0)
    per_b = lambda bi, i: (bi, 0, 0)
    per_b4 = lambda bi, i: (bi, 0, 0, 0)
    const2 = lambda bi, i: (0, 0)

    return pl.pallas_call(
        functools.partial(_mix_kernel, local=local, n_tok=t, layer=layer),
        out_shape=jax.ShapeDtypeStruct((b, t, d), F32),
        grid=(b, n_tiles),
        in_specs=[
            pl.BlockSpec(memory_space=pltpu.SMEM),
            pl.BlockSpec((1, tq, d), tok),
            _mod_spec(layer, row, 2),
            pl.BlockSpec((1, tq, 256), tok),
            pl.BlockSpec((1, t, 128), per_b),
            pl.BlockSpec((1, t // A_BLOCK, 128, A_BLOCK), per_b4),
            pl.BlockSpec((1, lc, 128), per_b),
            pl.BlockSpec((1, lc // A_BLOCK, 128, A_BLOCK), per_b4),
            pl.BlockSpec((1, tq, 256), tok),
            pl.BlockSpec((1, t, 256), per_b),
            _layer_spec(layer, (256, 256)),
            _layer_spec(layer, (1, 256)),
            pl.BlockSpec((1, tq, 256), tok),
            pl.BlockSpec((1, t, 256), per_b),
            pl.BlockSpec((1, t // tq, 256, tq), per_b4),
            pl.BlockSpec((1, lc, 256), per_b),
            pl.BlockSpec((1, lc // tq, 256, tq), per_b4),
            _layer_spec(layer, (D_HEADS, PAIR_MASKED + 1, GRID_W, 2 * GRID_W)),
            pl.BlockSpec((3, 3 * A_BLOCK, 2 * A_BLOCK), lambda bi, i: (0, 0, 0)),
            _layer_spec(layer, (d, d)),
        ],
        out_specs=pl.BlockSpec((1, tq, d), tok),
        scratch_shapes=[pltpu.VMEM((GROUP_W, tq), F32), pltpu.VMEM((GROUP_W, tq), F32),
                        pltpu.VMEM((tq, GROUP_W), BF16), pltpu.VMEM((tq, GROUP_W), BF16)],
        compiler_params=pltpu.CompilerParams(
            dimension_semantics=("arbitrary", "arbitrary"), vmem_limit_bytes=VMEM_LIMIT),
        name="mixer_latent" if local else "mixer_ctx",
    )(sink, x, mod, qa, ka, va, kca, vca, ob, cin, wpool, cscale, qd, kd, vd, kcd, vcd, bias, amask,
      w_out)


def _ffn_kernel(x_ref, sh_ref, sc_ref, g2_ref, gn_ref, wg_ref, wu_ref, wd_ref, o_ref, act_scr):
    xt = x_ref[0]
    ms = jnp.mean(xt * xt, axis=-1, keepdims=True)
    h = xt * lax.rsqrt(ms + NORM_EPS) * gn_ref[...]
    h = (h * (1.0 + sc_ref[...]) + sh_ref[...]).astype(BF16)
    for c in range(FF_DIM // FFN_TF):
        cols = slice(c * FFN_TF, (c + 1) * FFN_TF)
        gate = _dot(h, wg_ref[:, cols])
        up = _dot(h, wu_ref[:, cols])
        act_scr[:, cols] = (gate * jax.nn.sigmoid(gate) * up).astype(BF16)
    o_ref[0] = xt + g2_ref[...] * _dot(act_scr[...], wd_ref[...])


def _ffn(x, mod, row, layer, g_ffn, wg, wu, wd):
    b, t, d = x.shape
    tm = min(FFN_TM, t)
    tok = lambda bi, i: (bi, i, 0)
    per_layer = lambda bi, i: (layer, 0, 0)
    resident = pl.Buffered(1)
    return pl.pallas_call(
        _ffn_kernel,
        out_shape=jax.ShapeDtypeStruct((b, t, d), F32),
        grid=(b, t // tm),
        in_specs=[
            pl.BlockSpec((1, tm, d), tok),
            _mod_spec(layer, row, 3),
            _mod_spec(layer, row, 4),
            _mod_spec(layer, row, 5),
            _layer_spec(layer, (1, d)),
            pl.BlockSpec((None, d, FF_DIM), per_layer, pipeline_mode=resident),
            pl.BlockSpec((None, d, FF_DIM), per_layer, pipeline_mode=resident),
            pl.BlockSpec((None, FF_DIM, d), per_layer, pipeline_mode=resident),
        ],
        out_specs=pl.BlockSpec((1, tm, d), tok),
        scratch_shapes=[pltpu.VMEM((tm, FF_DIM), BF16)],
        compiler_params=pltpu.CompilerParams(
            dimension_semantics=("arbitrary", "arbitrary"), vmem_limit_bytes=VMEM_LIMIT),
        name="ffn",
    )(x, mod, mod, mod, g_ffn, wg, wu, wd)


def _rope_tables(s):
    t = np.arange(s)
    half = 16
    inv = np.power(np.float32(ROPE_BASE), -np.arange(half, dtype=np.float32) / half).astype(np.float32)
    ang_r = (t // GRID_W).astype(np.float32)[:, None] * inv[None, :]
    ang_c = (t % GRID_W).astype(np.float32)[:, None] * inv[None, :]
    cos = np.concatenate([np.cos(ang_r)] * 2 + [np.cos(ang_c)] * 2, axis=-1)
    sin = np.concatenate([-np.sin(ang_r), np.sin(ang_r), -np.sin(ang_c), np.sin(ang_c)], axis=-1)
    return cos.astype(np.float32), sin.astype(np.float32)


def _window_mask():
    kk = np.arange(3 * A_BLOCK)[:, None]
    qq = np.arange(2 * A_BLOCK)[None, :] % A_BLOCK
    kinds = [np.where(np.abs(kk - qq - back * A_BLOCK) <= A_WINDOW, 0.0, NEG_INF) for back in (1, 0, 2)]
    return jnp.asarray(np.stack(kinds), F32)


PAIR_COUNT = 2 * (D_WIN_ROWS - 1)
PAIR_INTERIOR = 0
PAIR_EDGE = PAIR_COUNT
PAIR_MASKED = 2 * PAIR_COUNT


def _neighbour_bias_pairs(rpb):
    cidx = np.arange(GRID_W)
    col_start = np.clip(cidx - D_WIN_COLS // 2, 0, GRID_W - D_WIN_COLS)
    col_ok_t = ((cidx[None, :] >= col_start[:, None]) & (cidx[None, :] < col_start[:, None] + D_WIN_COLS)).T
    coff_t = np.clip(cidx[:, None] - cidx[None, :], -(D_WIN_COLS - 1), D_WIN_COLS - 1) + (D_WIN_COLS - 1)
    select = (np.arange(2 * D_WIN_COLS - 1)[:, None, None] == coff_t[None]).astype(np.float32)
    blocks = lax.dot_general(rpb.astype(F32), jnp.asarray(select), (((3,), (0,)), ((), ())),
                             precision=lax.Precision.HIGHEST)
    blocks = jnp.where(col_ok_t, blocks, NEG_INF)
    d = np.arange(-(D_WIN_ROWS - 1), D_WIN_ROWS)
    seen = (d >= -(D_WIN_ROWS // 2)) & (d < D_WIN_ROWS // 2)
    interior = jnp.where(seen[:, None, None], blocks, NEG_INF)
    pair = lambda t: jnp.concatenate([t[:, :, 1:], t[:, :, :-1]], axis=-1)
    masked = jnp.full(blocks.shape[:2] + (1, GRID_W, 2 * GRID_W), NEG_INF, F32)
    return jnp.concatenate([pair(interior), pair(blocks), masked], axis=2)


def kernel(x, c, ctx, c_ctx, w_mod, b_mod, g_mix, g_ffn, w_in, w_out, a_q_gain, a_k_gain, a_sink,
           b_v_gain, b_w_s, b_b_s, c_w_pool, c_scale, d_q_gain, d_k_gain, d_rpb, w_gate, w_up, w_down):
    bsz, s, d = x.shape
    lc = ctx.shape[1]
    n_rows = s // GRID_W
    assert s % MIX_TQ == 0 and n_rows >= 2 * D_WIN_ROWS and MIX_TQ // GRID_W == D_WIN_ROWS // 2

    cvec = jnp.zeros((MOD_ROWS, d), F32).at[:bsz].set(c).at[bsz].set(c_ctx)
    mod = _modulation(cvec, w_mod, b_mod).reshape(DEPTH, MOD_ROWS, 6, 1, d)
    lat_row = lambda bi: bi
    ctx_row = lambda bi: bsz

    seg = jnp.asarray(np.kron(np.eye(4), np.full((64, 64), 1.0 / 64)), BF16)
    cos1, sin1 = _rope_tables(s)
    tables_lat = tuple(jnp.asarray(np.tile(tab, (1, reps)))
                       for tab, reps in ((cos1, 4), (sin1, 4), (cos1, 2), (sin1, 2)))
    tables_ctx = tuple(jnp.zeros((lc, w), F32) for w in (256, 256, 128, 128))

    amask = _window_mask()
    bias_all = _neighbour_bias_pairs(d_rpb * LOG2E)

    w_in_b = w_in.astype(BF16)
    w_out_b = w_out.astype(BF16)
    wg_b, wu_b, wd_b = w_gate.astype(BF16), w_up.astype(BF16), w_down.astype(BF16)
    gmix = g_mix.reshape(DEPTH, 1, d)
    gffn = g_ffn.reshape(DEPTH, 1, d)
    aqg = jnp.tile(a_q_gain, (1, 4)).reshape(DEPTH, 1, 256)
    akg = jnp.tile(a_k_gain, (1, 2)).reshape(DEPTH, 1, 128)
    dqg = jnp.tile(d_q_gain, (1, 4)).reshape(DEPTH, 1, 256)
    dkg = jnp.tile(d_k_gain, (1, 4)).reshape(DEPTH, 1, 256)
    bvg = b_v_gain.reshape(DEPTH, 1, 256)
    ws_cat = jnp.transpose(b_w_s, (0, 2, 1, 3)).reshape(DEPTH, B_CHUNK, B_GROUPS * B_CHUNK).astype(BF16)
    bs_t = jnp.repeat(jnp.swapaxes(b_b_s, 1, 2), B_GROUP_W, axis=2)
    wpool = jnp.einsum('lgcd,gh->lgchd', c_w_pool, jnp.eye(len(C_POOLS), dtype=F32)
                       ).reshape(DEPTH, GROUP_W, GROUP_W).astype(BF16)
    cscale = c_scale.reshape(DEPTH, 1, 256)
    sink = a_sink * LOG2E
    common = (gmix, w_in_b, seg, aqg, akg, dqg, dkg)

    xc = ctx
    for l in range(DEPTH):
        last = l == DEPTH - 1
        lat = _in_projection(x, mod, lat_row, l, *common, tables_lat, bvg, ws_cat, bs_t, use_rope=True)
        con = _in_projection(xc, mod, ctx_row, l, *common, tables_ctx, bvg, ws_cat, bs_t, use_rope=False)
        qa, ka, va, ob, cin, qd, kd, vd = lat
        cqa, cka, cva, cob, ccin, cqd, ckd, cvd = con

        x = _mixer(x, mod, lat_row, l, qa, ka, va, cka, cva, ob, cin, wpool, cscale, qd, kd, vd, ckd, cvd,
                   bias_all, amask, w_out_b, sink, local=True)
        x = _ffn(x, mod, lat_row, l, gffn, wg_b, wu_b, wd_b)
        if not last:
            xc = _mixer(xc, mod, ctx_row, l, cqa, cka, cva, cka, cva, cob, ccin, wpool, cscale, cqd, ckd,
                        cvd, ckd, cvd, bias_all, amask, w_out_b, sink, local=False)
            xc = _ffn(xc, mod, ctx_row, l, gffn, wg_b, wu_b, wd_b)
    return x
```

```python
import functools

import jax
import jax.numpy as jnp
import numpy as np
from jax import lax
from jax.experimental import pallas as pl
from jax.experimental.pallas import tpu as pltpu

F32 = jnp.float32
BF16 = jnp.bfloat16

D_MODEL = 1024
DEPTH = 2
GRID_W = 64
HEAD_DIM = 64
GROUP_W = 256
A_HEADS = 4
A_KV_HEADS = 2
A_WINDOW = 128
A_BLOCK = 128
B_CHUNK = 128
B_GROUPS = 4
B_GROUP_W = 64
C_POOLS = (2, 4, 8, 16)
C_GROUP_W = 64
D_HEADS = 4
D_WIN_ROWS = 8
D_WIN_COLS = 16
FF_DIM = 2816
ROPE_BASE = 10000.0
NORM_EPS = 1e-6
NEG_INF = -1e30
LOG2E = 1.4426950408889634
IN_TOTAL = 2048
OFF_AQ, OFF_AK, OFF_AV, OFF_BU, OFF_BV, OFF_C, OFF_DQ, OFF_DK, OFF_DV = (
    0, 256, 384, 512, 768, 1024, 1280, 1536, 1792)

MOD_ROWS = 16
MOD_TN = 1536
IN_TM = 512
MIX_TQ = 256
FFN_TM = 1024
FFN_TF = 256
BF16_SUBLANES = 16
POOL_HALO = 8
SCORE_LOOKAHEAD = 2
VMEM_LIMIT = 56 * 1024 * 1024


def _dot(a, b):
    return jnp.dot(a, b, preferred_element_type=F32)


def _dot_nt(a, b):
    return lax.dot_general(a, b, (((1,), (1,)), ((), ())), preferred_element_type=F32)


def _mod_kernel(c_ref, w_ref, b_ref, o_ref):
    cv = c_ref[...]
    act = cv * jax.nn.sigmoid(cv)
    o_ref[0] = _dot(act.astype(BF16), w_ref[0].astype(BF16)) + b_ref[0]


def _modulation(cvec, w_mod, b_mod):
    depth, d, n = w_mod.shape
    return pl.pallas_call(
        _mod_kernel,
        out_shape=jax.ShapeDtypeStruct((depth, MOD_ROWS, n), F32),
        grid=(depth, n // MOD_TN),
        in_specs=[
            pl.BlockSpec((MOD_ROWS, d), lambda l, j: (0, 0)),
            pl.BlockSpec((1, d, MOD_TN), lambda l, j: (l, 0, j)),
            pl.BlockSpec((1, 1, MOD_TN), lambda l, j: (l, 0, j)),
        ],
        out_specs=pl.BlockSpec((1, MOD_ROWS, MOD_TN), lambda l, j: (l, 0, j)),
        compiler_params=pltpu.CompilerParams(
            dimension_semantics=("arbitrary", "arbitrary"), vmem_limit_bytes=VMEM_LIMIT),
        name="modulation",
    )(cvec, w_mod, b_mod.reshape(depth, 1, n))


def _head_norm(t, seg, gain):
    sq = t * t
    hi = sq.astype(BF16)
    lo = (sq - hi.astype(F32)).astype(BF16)
    ms = _dot(hi, seg) + _dot(lo, seg)
    return t * lax.rsqrt(ms + NORM_EPS) * gain


def _rope(y, cos, sin_signed):
    w = y.shape[-1]
    up = pltpu.roll(y, w - 16, axis=1)
    dn = pltpu.roll(y, 16, axis=1)
    lane = lax.broadcasted_iota(jnp.int32, y.shape, 1)
    swapped = jnp.where((lane & 31) < 16, up, dn)
    return y * cos + swapped * sin_signed


def _swap_middle_heads(q):
    lane = lax.broadcasted_iota(jnp.int32, q.shape, 1)
    from_right = pltpu.roll(q, 3 * HEAD_DIM, axis=1)
    from_left = pltpu.roll(q, HEAD_DIM, axis=1)
    return jnp.where((lane >= HEAD_DIM) & (lane < 2 * HEAD_DIM), from_right,
                     jnp.where((lane >= 2 * HEAD_DIM) & (lane < 3 * HEAD_DIM), from_left, q))


def _in_kernel(x_ref, sh_ref, sc_ref, g_ref, w_ref, seg_ref, aqg_ref, akg_ref, dqg_ref, dkg_ref,
               cosq_ref, sinq_ref, cosk_ref, sink_ref, bvg_ref, ws_ref, bs_ref,
               qa_ref, ka_ref, va_ref, ob_ref, c_ref, qd_ref, kd_ref, vd_ref, v_scr, *, use_rope):
    xt = x_ref[0]
    ms = jnp.mean(xt * xt, axis=-1, keepdims=True)
    h = xt * lax.rsqrt(ms + NORM_EPS) * g_ref[...]
    h = (h * (1.0 + sc_ref[...]) + sh_ref[...]).astype(BF16)

    p = _dot(h, w_ref[...])

    def proj(off, width):
        return p[:, off:off + width]

    seg = seg_ref[...]
    qa = _head_norm(proj(OFF_AQ, 256), seg, aqg_ref[...])
    ka = _head_norm(proj(OFF_AK, 128), seg[:128, :128], akg_ref[...])
    if use_rope:
        qa = _rope(qa, cosq_ref[...], sinq_ref[...])
        ka = _rope(ka, cosk_ref[...], sink_ref[...])
    scale = HEAD_DIM ** -0.5 * LOG2E
    qa_ref[0] = (_swap_middle_heads(qa) * scale).astype(BF16)
    ka_ref[0] = ka.astype(BF16)
    v_scr[:, 0:128] = proj(OFF_AV, 128)
    v_scr[:, 128:384] = proj(OFF_DV, 256)
    vat = v_scr[:, 0:128].T.astype(BF16)
    for j in range(vat.shape[1] // A_BLOCK):
        va_ref[0, j] = vat[:, j * A_BLOCK:(j + 1) * A_BLOCK]

    qd = _head_norm(proj(OFF_DQ, 256), seg, dqg_ref[...])
    qd_ref[0] = (qd * scale).astype(BF16)
    kd_ref[0] = _head_norm(proj(OFF_DK, 256), seg, dkg_ref[...]).astype(BF16)
    vdt = v_scr[:, 128:384].T.astype(BF16)
    for j in range(vdt.shape[1] // MIX_TQ):
        vd_ref[0, j] = vdt[:, j * MIX_TQ:(j + 1) * MIX_TQ]

    c_ref[0] = proj(OFF_C, 256)

    u = jax.nn.gelu(proj(OFF_BU, 256))
    v = jax.nn.gelu(proj(OFF_BV, 256))
    mu = jnp.mean(v, axis=-1, keepdims=True)
    vc = v - mu
    var = jnp.mean(vc * vc, axis=-1, keepdims=True)
    vn = (vc * lax.rsqrt(var + NORM_EPS) * bvg_ref[...]).astype(BF16)
    lane = lax.broadcasted_iota(jnp.int32, (B_CHUNK, GROUP_W), 1)
    zero = jnp.zeros((B_CHUNK, GROUP_W), BF16)
    tm = xt.shape[0]
    for ch in range(tm // B_CHUNK):
        rows = slice(ch * B_CHUNK, (ch + 1) * B_CHUNK)
        vch = vn[rows]
        stacked = jnp.concatenate(
            [jnp.where((lane >= g * B_GROUP_W) & (lane < (g + 1) * B_GROUP_W), vch, zero)
             for g in range(B_GROUPS)], axis=0)
        z = _dot(ws_ref[...], stacked) + bs_ref[...]
        ob_ref[0, rows, :] = (u[rows] * z).astype(BF16)


def _mod_spec(layer, row, which):
    return pl.BlockSpec((None, None, None, 1, D_MODEL), lambda bi, *_: (layer, row(bi), which, 0, 0))


def _layer_spec(layer, shape):
    return pl.BlockSpec((None,) + tuple(shape), lambda *_: (layer,) + (0,) * len(shape))


def _in_projection(x, mod, row, layer, g_mix, w_in, seg, aqg, akg, dqg, dkg, tables, bvg, ws_cat, bs_t,
                   *, use_rope):
    b, t, d = x.shape
    cosq, sinq, cosk, sink = tables
    tm = min(IN_TM, t)
    tok = lambda bi, i: (bi, i, 0)
    const2 = lambda bi, i: (0, 0)
    pos2 = lambda bi, i: (i, 0)
    assert tm % MIX_TQ == 0 and MIX_TQ % A_BLOCK == 0
    out_shapes = (
        jax.ShapeDtypeStruct((b, t, 256), BF16),
        jax.ShapeDtypeStruct((b, t, 128), BF16),
        jax.ShapeDtypeStruct((b, t // A_BLOCK, 128, A_BLOCK), BF16),
        jax.ShapeDtypeStruct((b, t, 256), BF16),
        jax.ShapeDtypeStruct((b, t, 256), F32),
        jax.ShapeDtypeStruct((b, t, 256), BF16),
        jax.ShapeDtypeStruct((b, t, 256), BF16),
        jax.ShapeDtypeStruct((b, t // MIX_TQ, 256, MIX_TQ), BF16),
    )
    blk4 = lambda bi, i: (bi, i, 0, 0)
    out_specs = tuple(
        pl.BlockSpec((1, tm // A_BLOCK, 128, A_BLOCK), blk4) if k == 2 else
        pl.BlockSpec((1, tm // MIX_TQ, 256, MIX_TQ), blk4) if k == 7 else
        pl.BlockSpec((1, tm, s.shape[-1]), tok)
        for k, s in enumerate(out_shapes))
    return pl.pallas_call(
        functools.partial(_in_kernel, use_rope=use_rope),
        out_shape=out_shapes,
        grid=(b, t // tm),
        in_specs=[
            pl.BlockSpec((1, tm, d), tok),
            _mod_spec(layer, row, 0),
            _mod_spec(layer, row, 1),
            _layer_spec(layer, (1, d)),
            _layer_spec(layer, (d, IN_TOTAL)),
            pl.BlockSpec((256, 256), const2),
            _layer_spec(layer, (1, 256)),
            _layer_spec(layer, (1, 128)),
            _layer_spec(layer, (1, 256)),
            _layer_spec(layer, (1, 256)),
            pl.BlockSpec((tm, 256), pos2),
            pl.BlockSpec((tm, 256), pos2),
            pl.BlockSpec((tm, 128), pos2),
            pl.BlockSpec((tm, 128), pos2),
            _layer_spec(layer, (1, 256)),
            _layer_spec(layer, (B_CHUNK, B_GROUPS * B_CHUNK)),
            _layer_spec(layer, (B_CHUNK, GROUP_W)),
        ],
        out_specs=out_specs,
        scratch_shapes=[pltpu.VMEM((tm, 384), F32)],
        compiler_params=pltpu.CompilerParams(
            dimension_semantics=("arbitrary", "arbitrary"), vmem_limit_bytes=VMEM_LIMIT),
        name="in_projection_rope" if use_rope else "in_projection_ctx",
    )(x, mod, mod, g_mix, w_in, seg, aqg, akg, dqg, dkg, cosq, sinq, cosk, sink, bvg, ws_cat, bs_t)


def _softmax_pv_t(parts, extra_logit):
    m = None
    for s, _ in parts:
        pm = jnp.max(s, axis=0, keepdims=True)
        m = pm if m is None else jnp.maximum(m, pm)
    if extra_logit is not None:
        m = jnp.maximum(m, extra_logit)
    acc = None
    for s, vt in parts:
        dh, n = vt.shape
        vt_ones = jnp.concatenate([vt, jnp.ones((BF16_SUBLANES, n), BF16)], axis=0)
        pv = _dot(vt_ones, jnp.exp2(s - m).astype(BF16))
        acc = pv if acc is None else acc + pv
    denom = acc[dh:dh + 1]
    if extra_logit is not None:
        denom = denom + jnp.exp2(extra_logit - m)
    return acc[:dh] / denom


def _head_lanes(q, head, width=HEAD_DIM):
    lane = lax.broadcasted_iota(jnp.int32, q.shape, 1)
    return jnp.where((lane >= head * width) & (lane < (head + 1) * width), q, jnp.zeros_like(q))


def _pool_tile(c_ref, inv_ref, t0, n_tok, tq):
    y = c_ref[0, pl.ds(t0, tq), :]
    lo_start = pl.multiple_of(jnp.maximum(t0 - POOL_HALO, 0), POOL_HALO)
    hi_start = pl.multiple_of(jnp.minimum(t0 + tq, n_tok - POOL_HALO), POOL_HALO)
    lo = c_ref[0, pl.ds(lo_start, POOL_HALO), :]
    hi = c_ref[0, pl.ds(hi_start, POOL_HALO), :]
    lo = jnp.where(t0 > 0, lo, 0.0)
    hi = jnp.where(t0 + tq < n_tok, hi, 0.0)
    ypad = jnp.concatenate([lo, y, hi], axis=0)

    n = tq + 2 * POOL_HALO
    w2 = ypad + pltpu.roll(ypad, 1, axis=0)
    w4 = pltpu.roll(w2, 1, axis=0) + pltpu.roll(w2, n - 1, axis=0)
    w8 = pltpu.roll(w4, 2, axis=0) + pltpu.roll(w4, n - 2, axis=0)
    w16 = pltpu.roll(w8, 4, axis=0) + pltpu.roll(w8, n - 4, axis=0)
    sums = tuple(w[POOL_HALO:POOL_HALO + tq] for w in (w2, w4, w8, w16))

    lane = lax.broadcasted_iota(jnp.int32, (tq, GROUP_W), 1)
    total = sums[0]
    for gi in range(1, len(C_POOLS)):
        total = jnp.where(lane >= gi * C_GROUP_W, sums[gi], total)
    return total * inv_ref[...] - y


def _mix_kernel(sink_ref, x_ref, g1_ref, qa_ref, ka_ref, va_ref, kca_ref, vca_ref, ob_ref, c_ref,
                inv_ref, wp_ref, cs_ref, qd_ref, kd_ref, vd_ref, kcd_ref, vcd_ref, bias_ref, amask_ref, wo_ref,
                o_ref, oa_scr, od_scr, mixa_scr, mixd_scr, *, local, n_tok, layer):
    tq = x_ref.shape[1]
    ti = pl.program_id(1)
    t0 = pl.multiple_of(ti * tq, tq)

    pooled = _pool_tile(c_ref, inv_ref, t0, n_tok, tq)
    out_c = (_dot(pooled.astype(BF16), wp_ref[...]) * cs_ref[...]).astype(BF16)

    if local:
        n_tiles = n_tok // tq
        half = D_WIN_ROWS // 2
        pair_entries = []
        for jj in range(3 * tq // GRID_W):
            row_entries = []
            for i in range(0, tq // GRID_W, 2):
                interior = PAIR_INTERIOR + (jj - half - i) + PAIR_COUNT // 2 - 1
                first = PAIR_EDGE + (jj - i) + PAIR_COUNT // 2 - 1 if jj < D_WIN_ROWS else PAIR_MASKED
                last = (PAIR_EDGE + (jj - D_WIN_ROWS - i) + PAIR_COUNT // 2 - 1 if jj >= half
                        else PAIR_MASKED)
                row_entries.append(jnp.where(ti == 0, first, jnp.where(ti == n_tiles - 1, last, interior)))
            pair_entries.append(row_entries)

    n_ctx_blk = vca_ref.shape[1]
    jobs = []

    def a_job(blk):
        cols = slice(blk * A_BLOCK, (blk + 1) * A_BLOCK)

        def scores():
            qblk = qa_ref[0, cols, :]
            qcat = jnp.concatenate([qblk[:, :128], qblk[:, 128:]], axis=0)
            qm = jnp.concatenate([_head_lanes(qcat, kv) for kv in range(A_KV_HEADS)], axis=0)
            if local:
                nblk = ti * (tq // A_BLOCK) + blk
                b0 = jnp.clip(nblk - 1, 0, n_tok // A_BLOCK - 3)
                kstart = pl.multiple_of(b0 * A_BLOCK, A_BLOCK)
                kwin = ka_ref[0, pl.ds(kstart, 3 * A_BLOCK), :]
                last_blk = n_tok // A_BLOCK - 1
                kind = jnp.where(nblk == 0, 1, jnp.where(nblk == last_blk, 2, 0))
                s_loc = _dot_nt(kwin, qm)
            s_ctx = _dot_nt(kca_ref[0], qm)
            per_head = []
            for kv in range(A_KV_HEADS):
                rows = slice(kv * HEAD_DIM, (kv + 1) * HEAD_DIM)
                qsl = slice(kv * 2 * A_BLOCK, (kv + 1) * 2 * A_BLOCK)
                parts = []
                if local:
                    vt = jnp.concatenate([va_ref[0, b0 + j, rows, :] for j in range(3)], axis=1)
                    parts.append((s_loc[:, qsl] + amask_ref[kind], vt))
                vct = jnp.concatenate([vca_ref[0, j, rows, :] for j in range(n_ctx_blk)], axis=1)
                parts.append((s_ctx[:, qsl], vct))
                per_head.append(parts)
            return per_head

        def finish(per_head):
            for kv, parts in enumerate(per_head):
                col = lax.broadcasted_iota(jnp.int32, (1, 2 * A_BLOCK), 1)
                sink = jnp.where(col < A_BLOCK, sink_ref[layer, 2 * kv], sink_ref[layer, 2 * kv + 1])
                o2 = _softmax_pv_t(parts, sink)
                oa_scr[(2 * kv) * 64:(2 * kv + 1) * 64, cols] = o2[:, :A_BLOCK]
                oa_scr[(2 * kv + 1) * 64:(2 * kv + 2) * 64, cols] = o2[:, A_BLOCK:]

        return scores, finish

    def d_job(pair):
        heads = (2 * pair, 2 * pair + 1)

        def scores():
            qd = qd_ref[0]
            qm = jnp.concatenate([_head_lanes(qd, hd) for hd in heads], axis=0)
            if local:
                d0 = jnp.clip(ti - 1, 0, n_tok // tq - 3)
                kwin = kd_ref[0, pl.ds(pl.multiple_of(d0 * tq, tq), 3 * tq), :]
                s_loc = _dot_nt(kwin, qm)
            s_ctx = _dot_nt(kcd_ref[0], qm)
            per_head = []
            for k, hd in enumerate(heads):
                rows = slice(hd * HEAD_DIM, (hd + 1) * HEAD_DIM)
                qsl = slice(k * tq, (k + 1) * tq)
                parts = []
                if local:
                    bias = jnp.concatenate(
                        [jnp.concatenate([bias_ref[hd, e] for e in row_entries], axis=1)
                         for row_entries in pair_entries], axis=0)
                    vt = jnp.concatenate([vd_ref[0, d0 + j, rows, :] for j in range(3)], axis=1)
                    parts.append((s_loc[:, qsl] + bias, vt))
                parts.append((s_ctx[:, qsl], vcd_ref[0, 0, rows, :]))
                per_head.append(parts)
            return per_head

        def finish(per_head):
            for hd, parts in zip(heads, per_head):
                od_scr[hd * HEAD_DIM:(hd + 1) * HEAD_DIM, :] = _softmax_pv_t(parts, None)

        return scores, finish

    for blk in range(tq // A_BLOCK):
        jobs.append(a_job(blk))
    for pair in range(D_HEADS // 2):
        jobs.append(d_job(pair))
    ahead = min(SCORE_LOOKAHEAD, len(jobs))
    pending = [jobs[j][0]() for j in range(ahead)]
    for j, (_, finish) in enumerate(jobs):
        if j + ahead < len(jobs):
            pending.append(jobs[j + ahead][0]())
        finish(pending.pop(0))


    mixa_scr[...] = oa_scr[...].T.astype(BF16)
    mixd_scr[...] = od_scr[...].T.astype(BF16)
    mix = (_dot(mixa_scr[...], wo_ref[0:256, :]) + _dot(ob_ref[0], wo_ref[256:512, :])
           + _dot(out_c, wo_ref[512:768, :]) + _dot(mixd_scr[...], wo_ref[768:1024, :]))
    o_ref[0] = x_ref[0] + g1_ref[...] * mix


def _mixer(x, mod, row, layer, qa, ka, va, kca, vca, ob, cin, inv_cnt, wpool, cscale, qd, kd, vd, kcd, vcd, bias,
           amask, w_out, sink, *, local):
    b, t, d = x.shape
    lc = kca.shape[1]
    tq = MIX_TQ
    n_tiles = t // tq
    tok = lambda bi, i: (bi, i, 0)
    per_b = lambda bi, i: (bi, 0, 0)
    per_b4 = lambda bi, i: (bi, 0, 0, 0)
    const2 = lambda bi, i: (0, 0)

    return pl.pallas_call(
        functools.partial(_mix_kernel, local=local, n_tok=t, layer=layer),
        out_shape=jax.ShapeDtypeStruct((b, t, d), F32),
        grid=(b, n_tiles),
        in_specs=[
            pl.BlockSpec(memory_space=pltpu.SMEM),
            pl.BlockSpec((1, tq, d), tok),
            _mod_spec(layer, row, 2),
            pl.BlockSpec((1, tq, 256), tok),
            pl.BlockSpec((1, t, 128), per_b),
            pl.BlockSpec((1, t // A_BLOCK, 128, A_BLOCK), per_b4),
            pl.BlockSpec((1, lc, 128), per_b),
            pl.BlockSpec((1, lc // A_BLOCK, 128, A_BLOCK), per_b4),
            pl.BlockSpec((1, tq, 256), tok),
            pl.BlockSpec((1, t, 256), per_b),
            pl.BlockSpec((tq, 256), lambda bi, i: (i, 0)),
            _layer_spec(layer, (256, 256)),
            _layer_spec(layer, (1, 256)),
            pl.BlockSpec((1, tq, 256), tok),
            pl.BlockSpec((1, t, 256), per_b),
            pl.BlockSpec((1, t // tq, 256, tq), per_b4),
            pl.BlockSpec((1, lc, 256), per_b),
            pl.BlockSpec((1, lc // tq, 256, tq), per_b4),
            _layer_spec(layer, (D_HEADS, PAIR_MASKED + 1, GRID_W, 2 * GRID_W)),
            pl.BlockSpec((3, 3 * A_BLOCK, 2 * A_BLOCK), lambda bi, i: (0, 0, 0)),
            _layer_spec(layer, (d, d)),
        ],
        out_specs=pl.BlockSpec((1, tq, d), tok),
        scratch_shapes=[pltpu.VMEM((GROUP_W, tq), F32), pltpu.VMEM((GROUP_W, tq), F32),
                        pltpu.VMEM((tq, GROUP_W), BF16), pltpu.VMEM((tq, GROUP_W), BF16)],
        compiler_params=pltpu.CompilerParams(
            dimension_semantics=("arbitrary", "arbitrary"), vmem_limit_bytes=VMEM_LIMIT),
        name="mixer_latent" if local else "mixer_ctx",
    )(sink, x, mod, qa, ka, va, kca, vca, ob, cin, inv_cnt, wpool, cscale, qd, kd, vd, kcd, vcd, bias, amask,
      w_out)


def _ffn_kernel(x_ref, sh_ref, sc_ref, g2_ref, gn_ref, wg_ref, wu_ref, wd_ref, o_ref, act_scr):
    xt = x_ref[0]
    ms = jnp.mean(xt * xt, axis=-1, keepdims=True)
    h = xt * lax.rsqrt(ms + NORM_EPS) * gn_ref[...]
    h = (h * (1.0 + sc_ref[...]) + sh_ref[...]).astype(BF16)
    for c in range(FF_DIM // FFN_TF):
        cols = slice(c * FFN_TF, (c + 1) * FFN_TF)
        gate = _dot(h, wg_ref[:, cols])
        up = _dot(h, wu_ref[:, cols])
        act_scr[:, cols] = (gate * jax.nn.sigmoid(gate) * up).astype(BF16)
    o_ref[0] = xt + g2_ref[...] * _dot(act_scr[...], wd_ref[...])


def _ffn(x, mod, row, layer, g_ffn, wg, wu, wd):
    b, t, d = x.shape
    tm = min(FFN_TM, t)
    tok = lambda bi, i: (bi, i, 0)
    per_layer = lambda bi, i: (layer, 0, 0)
    resident = pl.Buffered(1)
    return pl.pallas_call(
        _ffn_kernel,
        out_shape=jax.ShapeDtypeStruct((b, t, d), F32),
        grid=(b, t // tm),
        in_specs=[
            pl.BlockSpec((1, tm, d), tok),
            _mod_spec(layer, row, 3),
            _mod_spec(layer, row, 4),
            _mod_spec(layer, row, 5),
            _layer_spec(layer, (1, d)),
            pl.BlockSpec((None, d, FF_DIM), per_layer, pipeline_mode=resident),
            pl.BlockSpec((None, d, FF_DIM), per_layer, pipeline_mode=resident),
            pl.BlockSpec((None, FF_DIM, d), per_layer, pipeline_mode=resident),
        ],
        out_specs=pl.BlockSpec((1, tm, d), tok),
        scratch_shapes=[pltpu.VMEM((tm, FF_DIM), BF16)],
        compiler_params=pltpu.CompilerParams(
            dimension_semantics=("arbitrary", "arbitrary"), vmem_limit_bytes=VMEM_LIMIT),
        name="ffn",
    )(x, mod, mod, mod, g_ffn, wg, wu, wd)


def _rope_tables(s):
    t = np.arange(s)
    half = 16
    inv = np.power(np.float32(ROPE_BASE), -np.arange(half, dtype=np.float32) / half).astype(np.float32)
    ang_r = (t // GRID_W).astype(np.float32)[:, None] * inv[None, :]
    ang_c = (t % GRID_W).astype(np.float32)[:, None] * inv[None, :]
    cos = np.concatenate([np.cos(ang_r)] * 2 + [np.cos(ang_c)] * 2, axis=-1)
    sin = np.concatenate([-np.sin(ang_r), np.sin(ang_r), -np.sin(ang_c), np.sin(ang_c)], axis=-1)
    return cos.astype(np.float32), sin.astype(np.float32)


def _pool_inverse_counts(n_tok):
    t = np.arange(n_tok)
    cols = []
    for w in C_POOLS:
        lo = np.clip(t - w // 2, 0, n_tok)
        hi = np.clip(t - w // 2 + w, 0, n_tok)
        cols.append(np.repeat((np.float32(1.0) / (hi - lo).astype(np.float32))[:, None], C_GROUP_W, axis=1))
    return jnp.asarray(np.concatenate(cols, axis=1), F32)


def _window_mask():
    kk = np.arange(3 * A_BLOCK)[:, None]
    qq = np.arange(2 * A_BLOCK)[None, :] % A_BLOCK
    kinds = [np.where(np.abs(kk - qq - back * A_BLOCK) <= A_WINDOW, 0.0, NEG_INF) for back in (1, 0, 2)]
    return jnp.asarray(np.stack(kinds), F32)


PAIR_COUNT = 2 * (D_WIN_ROWS - 1)
PAIR_INTERIOR = 0
PAIR_EDGE = PAIR_COUNT
PAIR_MASKED = 2 * PAIR_COUNT


def _neighbour_bias_pairs(rpb):
    cidx = np.arange(GRID_W)
    col_start = np.clip(cidx - D_WIN_COLS // 2, 0, GRID_W - D_WIN_COLS)
    col_ok_t = ((cidx[None, :] >= col_start[:, None]) & (cidx[None, :] < col_start[:, None] + D_WIN_COLS)).T
    coff_t = np.clip(cidx[:, None] - cidx[None, :], -(D_WIN_COLS - 1), D_WIN_COLS - 1) + (D_WIN_COLS - 1)
    select = (np.arange(2 * D_WIN_COLS - 1)[:, None, None] == coff_t[None]).astype(np.float32)
    blocks = lax.dot_general(rpb.astype(F32), jnp.asarray(select), (((3,), (0,)), ((), ())),
                             precision=lax.Precision.HIGHEST)
    blocks = jnp.where(col_ok_t, blocks, NEG_INF)
    d = np.arange(-(D_WIN_ROWS - 1), D_WIN_ROWS)
    seen = (d >= -(D_WIN_ROWS // 2)) & (d < D_WIN_ROWS // 2)
    interior = jnp.where(seen[:, None, None], blocks, NEG_INF)
    pair = lambda t: jnp.concatenate([t[:, :, 1:], t[:, :, :-1]], axis=-1)
    masked = jnp.full(blocks.shape[:2] + (1, GRID_W, 2 * GRID_W), NEG_INF, F32)
    return jnp.concatenate([pair(interior), pair(blocks), masked], axis=2)


def kernel(x, c, ctx, c_ctx, w_mod, b_mod, g_mix, g_ffn, w_in, w_out, a_q_gain, a_k_gain, a_sink,
           b_v_gain, b_w_s, b_b_s, c_w_pool, c_scale, d_q_gain, d_k_gain, d_rpb, w_gate, w_up, w_down):
    bsz, s, d = x.shape
    lc = ctx.shape[1]
    n_rows = s // GRID_W
    assert s % MIX_TQ == 0 and n_rows >= 2 * D_WIN_ROWS and MIX_TQ // GRID_W == D_WIN_ROWS // 2

    cvec = jnp.zeros((MOD_ROWS, d), F32).at[:bsz].set(c).at[bsz].set(c_ctx)
    mod = _modulation(cvec, w_mod, b_mod).reshape(DEPTH, MOD_ROWS, 6, 1, d)
    lat_row = lambda bi: bi
    ctx_row = lambda bi: bsz

    seg = jnp.asarray(np.kron(np.eye(4), np.full((64, 64), 1.0 / 64)), BF16)
    cos1, sin1 = _rope_tables(s)
    tables_lat = tuple(jnp.asarray(np.tile(tab, (1, reps)))
                       for tab, reps in ((cos1, 4), (sin1, 4), (cos1, 2), (sin1, 2)))
    tables_ctx = tuple(jnp.zeros((lc, w), F32) for w in (256, 256, 128, 128))

    amask = _window_mask()
    inv_lat, inv_ctx = _pool_inverse_counts(s), _pool_inverse_counts(lc)
    bias_all = _neighbour_bias_pairs(d_rpb * LOG2E)

    w_in_b = w_in.astype(BF16)
    w_out_b = w_out.astype(BF16)
    wg_b, wu_b, wd_b = w_gate.astype(BF16), w_up.astype(BF16), w_down.astype(BF16)
    gmix = g_mix.reshape(DEPTH, 1, d)
    gffn = g_ffn.reshape(DEPTH, 1, d)
    aqg = jnp.tile(a_q_gain, (1, 4)).reshape(DEPTH, 1, 256)
    akg = jnp.tile(a_k_gain, (1, 2)).reshape(DEPTH, 1, 128)
    dqg = jnp.tile(d_q_gain, (1, 4)).reshape(DEPTH, 1, 256)
    dkg = jnp.tile(d_k_gain, (1, 4)).reshape(DEPTH, 1, 256)
    bvg = b_v_gain.reshape(DEPTH, 1, 256)
    ws_cat = jnp.transpose(b_w_s, (0, 2, 1, 3)).reshape(DEPTH, B_CHUNK, B_GROUPS * B_CHUNK).astype(BF16)
    bs_t = jnp.repeat(jnp.swapaxes(b_b_s, 1, 2), B_GROUP_W, axis=2)
    wpool = jnp.einsum('lgcd,gh->lgchd', c_w_pool, jnp.eye(len(C_POOLS), dtype=F32)
                       ).reshape(DEPTH, GROUP_W, GROUP_W).astype(BF16)
    cscale = c_scale.reshape(DEPTH, 1, 256)
    sink = a_sink * LOG2E
    common = (gmix, w_in_b, seg, aqg, akg, dqg, dkg)

    xc = ctx
    for l in range(DEPTH):
        last = l == DEPTH - 1
        lat = _in_projection(x, mod, lat_row, l, *common, tables_lat, bvg, ws_cat, bs_t, use_rope=True)
        con = _in_projection(xc, mod, ctx_row, l, *common, tables_ctx, bvg, ws_cat, bs_t, use_rope=False)
        qa, ka, va, ob, cin, qd, kd, vd = lat
        cqa, cka, cva, cob, ccin, cqd, ckd, cvd = con

        x = _mixer(x, mod, lat_row, l, qa, ka, va, cka, cva, ob, cin, inv_lat, wpool, cscale, qd, kd, vd, ckd, cvd,
                   bias_all, amask, w_out_b, sink, local=True)
        x = _ffn(x, mod, lat_row, l, gffn, wg_b, wu_b, wd_b)
        if not last:
            xc = _mixer(xc, mod, ctx_row, l, cqa, cka, cva, cka, cva, cob, ccin, inv_ctx, wpool, cscale, cqd, ckd,
                        cvd, ckd, cvd, bias_all, amask, w_out_b, sink, local=False)
            xc = _ffn(xc, mod, ctx_row, l, gffn, wg_b, wu_b, wd_b)
    return x
```

```python
import functools

import jax
import jax.numpy as jnp
import numpy as np
from jax import lax
from jax.experimental import pallas as pl
from jax.experimental.pallas import tpu as pltpu

F32 = jnp.float32
BF16 = jnp.bfloat16

D_MODEL = 1024
DEPTH = 2
GRID_W = 64
HEAD_DIM = 64
GROUP_W = 256
A_HEADS = 4
A_KV_HEADS = 2
A_WINDOW = 128
A_BLOCK = 128
B_CHUNK = 128
B_GROUPS = 4
B_GROUP_W = 64
C_POOLS = (2, 4, 8, 16)
C_GROUP_W = 64
D_HEADS = 4
D_WIN_ROWS = 8
D_WIN_COLS = 16
FF_DIM = 2816
ROPE_BASE = 10000.0
NORM_EPS = 1e-6
NEG_INF = -1e30
LOG2E = 1.4426950408889634
IN_TOTAL = 2048
OFF_AQ, OFF_AK, OFF_AV, OFF_BU, OFF_BV, OFF_C, OFF_DQ, OFF_DK, OFF_DV = (
    0, 256, 384, 512, 768, 1024, 1280, 1536, 1792)

MOD_ROWS = 16
MOD_TN = 1536
IN_TM = 512
MIX_TQ = 256
FFN_TM = 1024
FFN_TF = 256
BF16_SUBLANES = 16
POOL_HALO = 8
MIX_STEP_TILES = 4
OUT_CHUNK = 256
SCORE_LOOKAHEAD = 2
VMEM_LIMIT = 56 * 1024 * 1024


def _dot(a, b):
    return jnp.dot(a, b, preferred_element_type=F32)


def _dot_nt(a, b):
    return lax.dot_general(a, b, (((1,), (1,)), ((), ())), preferred_element_type=F32)


def _mod_kernel(c_ref, w_ref, b_ref, o_ref):
    cv = c_ref[...]
    act = cv * jax.nn.sigmoid(cv)
    o_ref[0] = _dot(act.astype(BF16), w_ref[0].astype(BF16)) + b_ref[0]


def _modulation(cvec, w_mod, b_mod):
    depth, d, n = w_mod.shape
    return pl.pallas_call(
        _mod_kernel,
        out_shape=jax.ShapeDtypeStruct((depth, MOD_ROWS, n), F32),
        grid=(depth, n // MOD_TN),
        in_specs=[
            pl.BlockSpec((MOD_ROWS, d), lambda l, j: (0, 0)),
            pl.BlockSpec((1, d, MOD_TN), lambda l, j: (l, 0, j)),
            pl.BlockSpec((1, 1, MOD_TN), lambda l, j: (l, 0, j)),
        ],
        out_specs=pl.BlockSpec((1, MOD_ROWS, MOD_TN), lambda l, j: (l, 0, j)),
        compiler_params=pltpu.CompilerParams(
            dimension_semantics=("arbitrary", "arbitrary"), vmem_limit_bytes=VMEM_LIMIT),
        name="modulation",
    )(cvec, w_mod, b_mod.reshape(depth, 1, n))


def _head_norm(t, seg, gain):
    sq = t * t
    hi = sq.astype(BF16)
    lo = (sq - hi.astype(F32)).astype(BF16)
    ms = _dot(hi, seg) + _dot(lo, seg)
    return t * lax.rsqrt(ms + NORM_EPS) * gain


def _rope(y, cos, sin_signed):
    w = y.shape[-1]
    up = pltpu.roll(y, w - 16, axis=1)
    dn = pltpu.roll(y, 16, axis=1)
    lane = lax.broadcasted_iota(jnp.int32, y.shape, 1)
    swapped = jnp.where((lane & 31) < 16, up, dn)
    return y * cos + swapped * sin_signed


def _swap_middle_heads(q):
    lane = lax.broadcasted_iota(jnp.int32, q.shape, 1)
    from_right = pltpu.roll(q, 3 * HEAD_DIM, axis=1)
    from_left = pltpu.roll(q, HEAD_DIM, axis=1)
    return jnp.where((lane >= HEAD_DIM) & (lane < 2 * HEAD_DIM), from_right,
                     jnp.where((lane >= 2 * HEAD_DIM) & (lane < 3 * HEAD_DIM), from_left, q))


def _in_kernel(x_ref, sh_ref, sc_ref, g_ref, w_ref, seg_ref, aqg_ref, akg_ref, dqg_ref, dkg_ref,
               cosq_ref, sinq_ref, cosk_ref, sink_ref, bvg_ref, ws_ref, bs_ref,
               qa_ref, ka_ref, va_ref, ob_ref, c_ref, qd_ref, kd_ref, vd_ref, v_scr, *, use_rope):
    xt = x_ref[0]
    ms = jnp.mean(xt * xt, axis=-1, keepdims=True)
    h = xt * lax.rsqrt(ms + NORM_EPS) * g_ref[...]
    h = (h * (1.0 + sc_ref[...]) + sh_ref[...]).astype(BF16)

    p = _dot(h, w_ref[...])

    def proj(off, width):
        return p[:, off:off + width]

    seg = seg_ref[...]
    qa = _head_norm(proj(OFF_AQ, 256), seg, aqg_ref[...])
    ka = _head_norm(proj(OFF_AK, 128), seg[:128, :128], akg_ref[...])
    if use_rope:
        qa = _rope(qa, cosq_ref[...], sinq_ref[...])
        ka = _rope(ka, cosk_ref[...], sink_ref[...])
    scale = HEAD_DIM ** -0.5 * LOG2E
    qa_ref[0] = (_swap_middle_heads(qa) * scale).astype(BF16)
    ka_ref[0] = ka.astype(BF16)
    v_scr[:, 0:128] = proj(OFF_AV, 128)
    v_scr[:, 128:384] = proj(OFF_DV, 256)
    vat = v_scr[:, 0:128].T.astype(BF16)
    for j in range(vat.shape[1] // A_BLOCK):
        va_ref[0, j] = vat[:, j * A_BLOCK:(j + 1) * A_BLOCK]

    qd = _head_norm(proj(OFF_DQ, 256), seg, dqg_ref[...])
    qd_ref[0] = (qd * scale).astype(BF16)
    kd_ref[0] = _head_norm(proj(OFF_DK, 256), seg, dkg_ref[...]).astype(BF16)
    vdt = v_scr[:, 128:384].T.astype(BF16)
    for j in range(vdt.shape[1] // MIX_TQ):
        vd_ref[0, j] = vdt[:, j * MIX_TQ:(j + 1) * MIX_TQ]

    c_ref[0] = proj(OFF_C, 256)

    u = jax.nn.gelu(proj(OFF_BU, 256))
    v = jax.nn.gelu(proj(OFF_BV, 256))
    mu = jnp.mean(v, axis=-1, keepdims=True)
    vc = v - mu
    var = jnp.mean(vc * vc, axis=-1, keepdims=True)
    vn = (vc * lax.rsqrt(var + NORM_EPS) * bvg_ref[...]).astype(BF16)
    lane = lax.broadcasted_iota(jnp.int32, (B_CHUNK, GROUP_W), 1)
    zero = jnp.zeros((B_CHUNK, GROUP_W), BF16)
    tm = xt.shape[0]
    for ch in range(tm // B_CHUNK):
        rows = slice(ch * B_CHUNK, (ch + 1) * B_CHUNK)
        vch = vn[rows]
        stacked = jnp.concatenate(
            [jnp.where((lane >= g * B_GROUP_W) & (lane < (g + 1) * B_GROUP_W), vch, zero)
             for g in range(B_GROUPS)], axis=0)
        z = _dot(ws_ref[...], stacked) + bs_ref[...]
        ob_ref[0, rows, :] = (u[rows] * z).astype(BF16)


def _mod_spec(layer, row, which):
    return pl.BlockSpec((None, None, None, 1, D_MODEL), lambda bi, *_: (layer, row(bi), which, 0, 0))


def _layer_spec(layer, shape):
    return pl.BlockSpec((None,) + tuple(shape), lambda *_: (layer,) + (0,) * len(shape))


def _in_projection(x, mod, row, layer, g_mix, w_in, seg, aqg, akg, dqg, dkg, tables, bvg, ws_cat, bs_t,
                   *, use_rope):
    b, t, d = x.shape
    cosq, sinq, cosk, sink = tables
    tm = min(IN_TM, t)
    tok = lambda bi, i: (bi, i, 0)
    const2 = lambda bi, i: (0, 0)
    pos2 = lambda bi, i: (i, 0)
    assert tm % MIX_TQ == 0 and MIX_TQ % A_BLOCK == 0
    out_shapes = (
        jax.ShapeDtypeStruct((b, t, 256), BF16),
        jax.ShapeDtypeStruct((b, t, 128), BF16),
        jax.ShapeDtypeStruct((b, t // A_BLOCK, 128, A_BLOCK), BF16),
        jax.ShapeDtypeStruct((b, t, 256), BF16),
        jax.ShapeDtypeStruct((b, t, 256), F32),
        jax.ShapeDtypeStruct((b, t, 256), BF16),
        jax.ShapeDtypeStruct((b, t, 256), BF16),
        jax.ShapeDtypeStruct((b, t // MIX_TQ, 256, MIX_TQ), BF16),
    )
    blk4 = lambda bi, i: (bi, i, 0, 0)
    out_specs = tuple(
        pl.BlockSpec((1, tm // A_BLOCK, 128, A_BLOCK), blk4) if k == 2 else
        pl.BlockSpec((1, tm // MIX_TQ, 256, MIX_TQ), blk4) if k == 7 else
        pl.BlockSpec((1, tm, s.shape[-1]), tok)
        for k, s in enumerate(out_shapes))
    return pl.pallas_call(
        functools.partial(_in_kernel, use_rope=use_rope),
        out_shape=out_shapes,
        grid=(b, t // tm),
        in_specs=[
            pl.BlockSpec((1, tm, d), tok),
            _mod_spec(layer, row, 0),
            _mod_spec(layer, row, 1),
            _layer_spec(layer, (1, d)),
            _layer_spec(layer, (d, IN_TOTAL)),
            pl.BlockSpec((256, 256), const2),
            _layer_spec(layer, (1, 256)),
            _layer_spec(layer, (1, 128)),
            _layer_spec(layer, (1, 256)),
            _layer_spec(layer, (1, 256)),
            pl.BlockSpec((tm, 256), pos2),
            pl.BlockSpec((tm, 256), pos2),
            pl.BlockSpec((tm, 128), pos2),
            pl.BlockSpec((tm, 128), pos2),
            _layer_spec(layer, (1, 256)),
            _layer_spec(layer, (B_CHUNK, B_GROUPS * B_CHUNK)),
            _layer_spec(layer, (B_CHUNK, GROUP_W)),
        ],
        out_specs=out_specs,
        scratch_shapes=[pltpu.VMEM((tm, 384), F32)],
        compiler_params=pltpu.CompilerParams(
            dimension_semantics=("arbitrary", "arbitrary"), vmem_limit_bytes=VMEM_LIMIT),
        name="in_projection_rope" if use_rope else "in_projection_ctx",
    )(x, mod, mod, g_mix, w_in, seg, aqg, akg, dqg, dkg, cosq, sinq, cosk, sink, bvg, ws_cat, bs_t)


def _softmax_pv_t(parts, extra_logit):
    m = None
    for s, _ in parts:
        pm = jnp.max(s, axis=0, keepdims=True)
        m = pm if m is None else jnp.maximum(m, pm)
    if extra_logit is not None:
        m = jnp.maximum(m, extra_logit)
    acc = None
    for s, vt in parts:
        dh, n = vt.shape
        vt_ones = jnp.concatenate([vt, jnp.ones((BF16_SUBLANES, n), BF16)], axis=0)
        pv = _dot(vt_ones, jnp.exp2(s - m).astype(BF16))
        acc = pv if acc is None else acc + pv
    denom = acc[dh:dh + 1]
    if extra_logit is not None:
        denom = denom + jnp.exp2(extra_logit - m)
    return acc[:dh] / denom


def _head_lanes(q, head, width=HEAD_DIM):
    lane = lax.broadcasted_iota(jnp.int32, q.shape, 1)
    return jnp.where((lane >= head * width) & (lane < (head + 1) * width), q, jnp.zeros_like(q))


def _pool_tile(c_ref, inv, t0, n_tok, tq):
    y = c_ref[0, pl.ds(t0, tq), :]
    lo_start = pl.multiple_of(jnp.maximum(t0 - POOL_HALO, 0), POOL_HALO)
    hi_start = pl.multiple_of(jnp.minimum(t0 + tq, n_tok - POOL_HALO), POOL_HALO)
    lo = c_ref[0, pl.ds(lo_start, POOL_HALO), :]
    hi = c_ref[0, pl.ds(hi_start, POOL_HALO), :]
    lo = jnp.where(t0 > 0, lo, 0.0)
    hi = jnp.where(t0 + tq < n_tok, hi, 0.0)
    ypad = jnp.concatenate([lo, y, hi], axis=0)

    n = tq + 2 * POOL_HALO
    w2 = ypad + pltpu.roll(ypad, 1, axis=0)
    w4 = pltpu.roll(w2, 1, axis=0) + pltpu.roll(w2, n - 1, axis=0)
    w8 = pltpu.roll(w4, 2, axis=0) + pltpu.roll(w4, n - 2, axis=0)
    w16 = pltpu.roll(w8, 4, axis=0) + pltpu.roll(w8, n - 4, axis=0)
    sums = tuple(w[POOL_HALO:POOL_HALO + tq] for w in (w2, w4, w8, w16))

    lane = lax.broadcasted_iota(jnp.int32, (tq, GROUP_W), 1)
    total = sums[0]
    for gi in range(1, len(C_POOLS)):
        total = jnp.where(lane >= gi * C_GROUP_W, sums[gi], total)
    return total * inv - y


def _mix_kernel(*refs, local, n_tok, layer):
    x_ref = refs[1]
    deferred = []
    for sub in range(x_ref.shape[1] // MIX_TQ):
        _mix_subtile(*refs, sub=sub, deferred=deferred, local=local, n_tok=n_tok, layer=layer)
    while deferred:
        deferred.pop(0)()


def _mix_subtile(sink_ref, x_ref, g1_ref, qa_ref, ka_ref, va_ref, kca_ref, vca_ref, ob_ref, c_ref,
                 inv_ref, wp_ref, cs_ref, qd_ref, kd_ref, vd_ref, kcd_ref, vcd_ref, bias_ref, amask_ref, wo_ref,
                 o_ref, oa_scr, od_scr, mix_scr, *, sub, deferred, local, n_tok, layer):
    tq = MIX_TQ
    ti = pl.program_id(1) * (x_ref.shape[1] // tq) + sub
    t0 = pl.multiple_of(ti * tq, tq)
    tile = slice(sub * tq, (sub + 1) * tq)

    pooled = _pool_tile(c_ref, inv_ref[tile, :], t0, n_tok, tq)
    out_c = (_dot(pooled.astype(BF16), wp_ref[...]) * cs_ref[...]).astype(BF16)

    if local:
        n_tiles = n_tok // tq
        half = D_WIN_ROWS // 2
        pair_entries = []
        for jj in range(3 * tq // GRID_W):
            row_entries = []
            for i in range(0, tq // GRID_W, 2):
                interior = PAIR_INTERIOR + (jj - half - i) + PAIR_COUNT // 2 - 1
                first = PAIR_EDGE + (jj - i) + PAIR_COUNT // 2 - 1 if jj < D_WIN_ROWS else PAIR_MASKED
                last = (PAIR_EDGE + (jj - D_WIN_ROWS - i) + PAIR_COUNT // 2 - 1 if jj >= half
                        else PAIR_MASKED)
                row_entries.append(jnp.where(ti == 0, first, jnp.where(ti == n_tiles - 1, last, interior)))
            pair_entries.append(row_entries)

    n_ctx_blk = vca_ref.shape[1]
    jobs = []

    def a_job(blk):
        cols = slice(blk * A_BLOCK, (blk + 1) * A_BLOCK)

        def scores():
            qblk = qa_ref[0, sub * tq + blk * A_BLOCK:sub * tq + (blk + 1) * A_BLOCK, :]
            qcat = jnp.concatenate([qblk[:, :128], qblk[:, 128:]], axis=0)
            qm = jnp.concatenate([_head_lanes(qcat, kv) for kv in range(A_KV_HEADS)], axis=0)
            if local:
                nblk = ti * (tq // A_BLOCK) + blk
                b0 = jnp.clip(nblk - 1, 0, n_tok // A_BLOCK - 3)
                kstart = pl.multiple_of(b0 * A_BLOCK, A_BLOCK)
                kwin = ka_ref[0, pl.ds(kstart, 3 * A_BLOCK), :]
                last_blk = n_tok // A_BLOCK - 1
                kind = jnp.where(nblk == 0, 1, jnp.where(nblk == last_blk, 2, 0))
                s_loc = _dot_nt(kwin, qm)
            s_ctx = _dot_nt(kca_ref[0], qm)
            per_head = []
            for kv in range(A_KV_HEADS):
                rows = slice(kv * HEAD_DIM, (kv + 1) * HEAD_DIM)
                qsl = slice(kv * 2 * A_BLOCK, (kv + 1) * 2 * A_BLOCK)
                parts = []
                if local:
                    vt = jnp.concatenate([va_ref[0, b0 + j, rows, :] for j in range(3)], axis=1)
                    parts.append((s_loc[:, qsl] + amask_ref[kind], vt))
                vct = jnp.concatenate([vca_ref[0, j, rows, :] for j in range(n_ctx_blk)], axis=1)
                parts.append((s_ctx[:, qsl], vct))
                per_head.append(parts)
            return per_head

        def finish(per_head):
            for kv, parts in enumerate(per_head):
                col = lax.broadcasted_iota(jnp.int32, (1, 2 * A_BLOCK), 1)
                sink = jnp.where(col < A_BLOCK, sink_ref[layer, 2 * kv], sink_ref[layer, 2 * kv + 1])
                o2 = _softmax_pv_t(parts, sink)
                oa_scr[sub, (2 * kv) * 64:(2 * kv + 1) * 64, cols] = o2[:, :A_BLOCK]
                oa_scr[sub, (2 * kv + 1) * 64:(2 * kv + 2) * 64, cols] = o2[:, A_BLOCK:]

        return scores, finish

    def d_job(pair):
        heads = (2 * pair, 2 * pair + 1)

        def scores():
            qd = qd_ref[0, tile, :]
            qm = jnp.concatenate([_head_lanes(qd, hd) for hd in heads], axis=0)
            if local:
                d0 = jnp.clip(ti - 1, 0, n_tok // tq - 3)
                kwin = kd_ref[0, pl.ds(pl.multiple_of(d0 * tq, tq), 3 * tq), :]
                s_loc = _dot_nt(kwin, qm)
            s_ctx = _dot_nt(kcd_ref[0], qm)
            per_head = []
            for k, hd in enumerate(heads):
                rows = slice(hd * HEAD_DIM, (hd + 1) * HEAD_DIM)
                qsl = slice(k * tq, (k + 1) * tq)
                parts = []
                if local:
                    bias = jnp.concatenate(
                        [jnp.concatenate([bias_ref[hd, e] for e in row_entries], axis=1)
                         for row_entries in pair_entries], axis=0)
                    vt = jnp.concatenate([vd_ref[0, d0 + j, rows, :] for j in range(3)], axis=1)
                    parts.append((s_loc[:, qsl] + bias, vt))
                parts.append((s_ctx[:, qsl], vcd_ref[0, 0, rows, :]))
                per_head.append(parts)
            return per_head

        def finish(per_head):
            for hd, parts in zip(heads, per_head):
                od_scr[sub, hd * HEAD_DIM:(hd + 1) * HEAD_DIM, :] = _softmax_pv_t(parts, None)

        return scores, finish

    for blk in range(tq // A_BLOCK):
        jobs.append(a_job(blk))
    for pair in range(D_HEADS // 2):
        jobs.append(d_job(pair))
    ahead = min(SCORE_LOOKAHEAD, len(jobs))
    pending = [jobs[j][0]() for j in range(ahead)]
    for j, (_, finish) in enumerate(jobs):
        if j + ahead < len(jobs):
            pending.append(jobs[j + ahead][0]())
        if deferred:
            deferred.pop(0)()
        finish(pending.pop(0))

    mix_scr[sub, :, 0:256] = oa_scr[sub].T.astype(BF16)
    mix_scr[sub, :, 256:512] = ob_ref[0, tile, :]
    mix_scr[sub, :, 512:768] = out_c
    mix_scr[sub, :, 768:1024] = od_scr[sub].T.astype(BF16)

    def out_chunk(c):
        cols = slice(c * OUT_CHUNK, (c + 1) * OUT_CHUNK)

        def run():
            mix = _dot(mix_scr[sub], wo_ref[:, cols])
            o_ref[0, tile, cols] = x_ref[0, tile, cols] + g1_ref[:, cols] * mix

        return run

    deferred.extend(out_chunk(c) for c in range(D_MODEL // OUT_CHUNK))


def _mixer(x, mod, row, layer, qa, ka, va, kca, vca, ob, cin, inv_cnt, wpool, cscale, qd, kd, vd, kcd, vcd, bias,
           amask, w_out, sink, *, local):
    b, t, d = x.shape
    lc = kca.shape[1]
    tq = MIX_TQ
    ts = min(MIX_STEP_TILES * tq, t)
    n_sub = ts // tq
    tok = lambda bi, i: (bi, i, 0)
    per_b = lambda bi, i: (bi, 0, 0)
    per_b4 = lambda bi, i: (bi, 0, 0, 0)

    return pl.pallas_call(
        functools.partial(_mix_kernel, local=local, n_tok=t, layer=layer),
        out_shape=jax.ShapeDtypeStruct((b, t, d), F32),
        grid=(b, t // ts),
        in_specs=[
            pl.BlockSpec(memory_space=pltpu.SMEM),
            pl.BlockSpec((1, ts, d), tok),
            _mod_spec(layer, row, 2),
            pl.BlockSpec((1, ts, 256), tok),
            pl.BlockSpec((1, t, 128), per_b),
            pl.BlockSpec((1, t // A_BLOCK, 128, A_BLOCK), per_b4),
            pl.BlockSpec((1, lc, 128), per_b),
            pl.BlockSpec((1, lc // A_BLOCK, 128, A_BLOCK), per_b4),
            pl.BlockSpec((1, ts, 256), tok),
            pl.BlockSpec((1, t, 256), per_b),
            pl.BlockSpec((ts, 256), lambda bi, i: (i, 0)),
            _layer_spec(layer, (256, 256)),
            _layer_spec(layer, (1, 256)),
            pl.BlockSpec((1, ts, 256), tok),
            pl.BlockSpec((1, t, 256), per_b),
            pl.BlockSpec((1, t // tq, 256, tq), per_b4),
            pl.BlockSpec((1, lc, 256), per_b),
            pl.BlockSpec((1, lc // tq, 256, tq), per_b4),
            _layer_spec(layer, (D_HEADS, PAIR_MASKED + 1, GRID_W, 2 * GRID_W)),
            pl.BlockSpec((3, 3 * A_BLOCK, 2 * A_BLOCK), lambda bi, i: (0, 0, 0)),
            _layer_spec(layer, (d, d)),
        ],
        out_specs=pl.BlockSpec((1, ts, d), tok),
        scratch_shapes=[pltpu.VMEM((n_sub, GROUP_W, tq), F32), pltpu.VMEM((n_sub, GROUP_W, tq), F32),
                        pltpu.VMEM((n_sub, tq, d), BF16)],
        compiler_params=pltpu.CompilerParams(
            dimension_semantics=("arbitrary", "arbitrary"), vmem_limit_bytes=VMEM_LIMIT),
        name="mixer_latent" if local else "mixer_ctx",
    )(sink, x, mod, qa, ka, va, kca, vca, ob, cin, inv_cnt, wpool, cscale, qd, kd, vd, kcd, vcd, bias, amask,
      w_out)


def _ffn_kernel(x_ref, sh_ref, sc_ref, g2_ref, gn_ref, wg_ref, wu_ref, wd_ref, o_ref, act_scr):
    xt = x_ref[0]
    ms = jnp.mean(xt * xt, axis=-1, keepdims=True)
    h = xt * lax.rsqrt(ms + NORM_EPS) * gn_ref[...]
    h = (h * (1.0 + sc_ref[...]) + sh_ref[...]).astype(BF16)
    for c in range(FF_DIM // FFN_TF):
        cols = slice(c * FFN_TF, (c + 1) * FFN_TF)
        gate = _dot(h, wg_ref[:, cols])
        up = _dot(h, wu_ref[:, cols])
        act_scr[:, cols] = (gate * jax.nn.sigmoid(gate) * up).astype(BF16)
    o_ref[0] = xt + g2_ref[...] * _dot(act_scr[...], wd_ref[...])


def _ffn(x, mod, row, layer, g_ffn, wg, wu, wd):
    b, t, d = x.shape
    tm = min(FFN_TM, t)
    tok = lambda bi, i: (bi, i, 0)
    per_layer = lambda bi, i: (layer, 0, 0)
    resident = pl.Buffered(1)
    return pl.pallas_call(
        _ffn_kernel,
        out_shape=jax.ShapeDtypeStruct((b, t, d), F32),
        grid=(b, t // tm),
        in_specs=[
            pl.BlockSpec((1, tm, d), tok),
            _mod_spec(layer, row, 3),
            _mod_spec(layer, row, 4),
            _mod_spec(layer, row, 5),
            _layer_spec(layer, (1, d)),
            pl.BlockSpec((None, d, FF_DIM), per_layer, pipeline_mode=resident),
            pl.BlockSpec((None, d, FF_DIM), per_layer, pipeline_mode=resident),
            pl.BlockSpec((None, FF_DIM, d), per_layer, pipeline_mode=resident),
        ],
        out_specs=pl.BlockSpec((1, tm, d), tok),
        scratch_shapes=[pltpu.VMEM((tm, FF_DIM), BF16)],
        compiler_params=pltpu.CompilerParams(
            dimension_semantics=("arbitrary", "arbitrary"), vmem_limit_bytes=VMEM_LIMIT),
        name="ffn",
    )(x, mod, mod, mod, g_ffn, wg, wu, wd)


def _rope_tables(s):
    t = np.arange(s)
    half = 16
    inv = np.power(np.float32(ROPE_BASE), -np.arange(half, dtype=np.float32) / half).astype(np.float32)
    ang_r = (t // GRID_W).astype(np.float32)[:, None] * inv[None, :]
    ang_c = (t % GRID_W).astype(np.float32)[:, None] * inv[None, :]
    cos = np.concatenate([np.cos(ang_r)] * 2 + [np.cos(ang_c)] * 2, axis=-1)
    sin = np.concatenate([-np.sin(ang_r), np.sin(ang_r), -np.sin(ang_c), np.sin(ang_c)], axis=-1)
    return cos.astype(np.float32), sin.astype(np.float32)


def _pool_inverse_counts(n_tok):
    t = np.arange(n_tok)
    cols = []
    for w in C_POOLS:
        lo = np.clip(t - w // 2, 0, n_tok)
        hi = np.clip(t - w // 2 + w, 0, n_tok)
        cols.append(np.repeat((np.float32(1.0) / (hi - lo).astype(np.float32))[:, None], C_GROUP_W, axis=1))
    return jnp.asarray(np.concatenate(cols, axis=1), F32)


def _window_mask():
    kk = np.arange(3 * A_BLOCK)[:, None]
    qq = np.arange(2 * A_BLOCK)[None, :] % A_BLOCK
    kinds = [np.where(np.abs(kk - qq - back * A_BLOCK) <= A_WINDOW, 0.0, NEG_INF) for back in (1, 0, 2)]
    return jnp.asarray(np.stack(kinds), F32)


PAIR_COUNT = 2 * (D_WIN_ROWS - 1)
PAIR_INTERIOR = 0
PAIR_EDGE = PAIR_COUNT
PAIR_MASKED = 2 * PAIR_COUNT


def _neighbour_bias_pairs(rpb):
    cidx = np.arange(GRID_W)
    col_start = np.clip(cidx - D_WIN_COLS // 2, 0, GRID_W - D_WIN_COLS)
    col_ok_t = ((cidx[None, :] >= col_start[:, None]) & (cidx[None, :] < col_start[:, None] + D_WIN_COLS)).T
    coff_t = np.clip(cidx[:, None] - cidx[None, :], -(D_WIN_COLS - 1), D_WIN_COLS - 1) + (D_WIN_COLS - 1)
    select = (np.arange(2 * D_WIN_COLS - 1)[:, None, None] == coff_t[None]).astype(np.float32)
    blocks = lax.dot_general(rpb.astype(F32), jnp.asarray(select), (((3,), (0,)), ((), ())),
                             precision=lax.Precision.HIGHEST)
    blocks = jnp.where(col_ok_t, blocks, NEG_INF)
    d = np.arange(-(D_WIN_ROWS - 1), D_WIN_ROWS)
    seen = (d >= -(D_WIN_ROWS // 2)) & (d < D_WIN_ROWS // 2)
    interior = jnp.where(seen[:, None, None], blocks, NEG_INF)
    pair = lambda t: jnp.concatenate([t[:, :, 1:], t[:, :, :-1]], axis=-1)
    masked = jnp.full(blocks.shape[:2] + (1, GRID_W, 2 * GRID_W), NEG_INF, F32)
    return jnp.concatenate([pair(interior), pair(blocks), masked], axis=2)


def kernel(x, c, ctx, c_ctx, w_mod, b_mod, g_mix, g_ffn, w_in, w_out, a_q_gain, a_k_gain, a_sink,
           b_v_gain, b_w_s, b_b_s, c_w_pool, c_scale, d_q_gain, d_k_gain, d_rpb, w_gate, w_up, w_down):
    bsz, s, d = x.shape
    lc = ctx.shape[1]
    n_rows = s // GRID_W
    assert s % MIX_TQ == 0 and n_rows >= 2 * D_WIN_ROWS and MIX_TQ // GRID_W == D_WIN_ROWS // 2

    cvec = jnp.zeros((MOD_ROWS, d), F32).at[:bsz].set(c).at[bsz].set(c_ctx)
    mod = _modulation(cvec, w_mod, b_mod).reshape(DEPTH, MOD_ROWS, 6, 1, d)
    lat_row = lambda bi: bi
    ctx_row = lambda bi: bsz

    seg = jnp.asarray(np.kron(np.eye(4), np.full((64, 64), 1.0 / 64)), BF16)
    cos1, sin1 = _rope_tables(s)
    tables_lat = tuple(jnp.asarray(np.tile(tab, (1, reps)))
                       for tab, reps in ((cos1, 4), (sin1, 4), (cos1, 2), (sin1, 2)))
    tables_ctx = tuple(jnp.zeros((bsz * lc, w), F32) for w in (256, 256, 128, 128))

    amask = _window_mask()
    inv_lat, inv_ctx = _pool_inverse_counts(s), _pool_inverse_counts(lc)
    bias_all = _neighbour_bias_pairs(d_rpb * LOG2E)

    w_in_b = w_in.astype(BF16)
    w_out_b = w_out.astype(BF16)
    wg_b, wu_b, wd_b = w_gate.astype(BF16), w_up.astype(BF16), w_down.astype(BF16)
    gmix = g_mix.reshape(DEPTH, 1, d)
    gffn = g_ffn.reshape(DEPTH, 1, d)
    aqg = jnp.tile(a_q_gain, (1, 4)).reshape(DEPTH, 1, 256)
    akg = jnp.tile(a_k_gain, (1, 2)).reshape(DEPTH, 1, 128)
    dqg = jnp.tile(d_q_gain, (1, 4)).reshape(DEPTH, 1, 256)
    dkg = jnp.tile(d_k_gain, (1, 4)).reshape(DEPTH, 1, 256)
    bvg = b_v_gain.reshape(DEPTH, 1, 256)
    ws_cat = jnp.transpose(b_w_s, (0, 2, 1, 3)).reshape(DEPTH, B_CHUNK, B_GROUPS * B_CHUNK).astype(BF16)
    bs_t = jnp.repeat(jnp.swapaxes(b_b_s, 1, 2), B_GROUP_W, axis=2)
    wpool = jnp.einsum('lgcd,gh->lgchd', c_w_pool, jnp.eye(len(C_POOLS), dtype=F32)
                       ).reshape(DEPTH, GROUP_W, GROUP_W).astype(BF16)
    cscale = c_scale.reshape(DEPTH, 1, 256)
    sink = a_sink * LOG2E
    common = (gmix, w_in_b, seg, aqg, akg, dqg, dkg)

    xc = ctx
    for l in range(DEPTH):
        last = l == DEPTH - 1
        lat = _in_projection(x, mod, lat_row, l, *common, tables_lat, bvg, ws_cat, bs_t, use_rope=True)
        con = _in_projection(xc.reshape(1, bsz * lc, d), mod, ctx_row, l, *common, tables_ctx, bvg, ws_cat,
                             bs_t, use_rope=False)
        qa, ka, va, ob, cin, qd, kd, vd = lat
        cqa, cka, cva, cob, ccin, cqd, ckd, cvd = [
            o.reshape((bsz, o.shape[1] // bsz) + o.shape[2:]) for o in con]

        x = _mixer(x, mod, lat_row, l, qa, ka, va, cka, cva, ob, cin, inv_lat, wpool, cscale, qd, kd, vd, ckd, cvd,
                   bias_all, amask, w_out_b, sink, local=True)
        x = _ffn(x, mod, lat_row, l, gffn, wg_b, wu_b, wd_b)
        if not last:
            xc = _mixer(xc, mod, ctx_row, l, cqa, cka, cva, cka, cva, cob, ccin, inv_ctx, wpool, cscale, cqd, ckd,
                        cvd, ckd, cvd, bias_all, amask, w_out_b, sink, local=False)
            xc = _ffn(xc.reshape(1, bsz * lc, d), mod, ctx_row, l, gffn, wg_b, wu_b, wd_b).reshape(bsz, lc, d)
    return x
```

```python
import functools

import jax
import jax.numpy as jnp
import numpy as np
from jax import lax
from jax.experimental import pallas as pl
from jax.experimental.pallas import tpu as pltpu

F32 = jnp.float32
BF16 = jnp.bfloat16

D_MODEL = 1024
DEPTH = 2
GRID_W = 64
HEAD_DIM = 64
GROUP_W = 256
A_HEADS = 4
A_KV_HEADS = 2
A_WINDOW = 128
A_BLOCK = 128
B_CHUNK = 128
B_GROUPS = 4
B_GROUP_W = 64
C_POOLS = (2, 4, 8, 16)
C_GROUP_W = 64
D_HEADS = 4
D_WIN_ROWS = 8
D_WIN_COLS = 16
FF_DIM = 2816
ROPE_BASE = 10000.0
NORM_EPS = 1e-6
NEG_INF = -1e30
LOG2E = 1.4426950408889634
IN_TOTAL = 2048
OFF_AQ, OFF_AK, OFF_AV, OFF_BU, OFF_BV, OFF_C, OFF_DQ, OFF_DK, OFF_DV = (
    0, 256, 384, 512, 768, 1024, 1280, 1536, 1792)

MOD_ROWS = 16
MOD_TN = 1536
IN_TM = 512
MIX_TQ = 256
FFN_TM = 1024
FFN_TF = 256
BF16_SUBLANES = 16
POOL_HALO = 8
MIX_STEP_TILES = 4
OUT_CHUNK = 256
SCORE_LOOKAHEAD = 2
VMEM_LIMIT = 56 * 1024 * 1024


def _dot(a, b):
    return jnp.dot(a, b, preferred_element_type=F32)


def _dot_nt(a, b):
    return lax.dot_general(a, b, (((1,), (1,)), ((), ())), preferred_element_type=F32)


def _mod_kernel(c_ref, w_ref, b_ref, o_ref):
    cv = c_ref[...]
    act = cv * jax.nn.sigmoid(cv)
    o_ref[0] = _dot(act.astype(BF16), w_ref[0].astype(BF16)) + b_ref[0]


def _modulation(cvec, w_mod, b_mod):
    depth, d, n = w_mod.shape
    return pl.pallas_call(
        _mod_kernel,
        out_shape=jax.ShapeDtypeStruct((depth, MOD_ROWS, n), F32),
        grid=(depth, n // MOD_TN),
        in_specs=[
            pl.BlockSpec((MOD_ROWS, d), lambda l, j: (0, 0)),
            pl.BlockSpec((1, d, MOD_TN), lambda l, j: (l, 0, j)),
            pl.BlockSpec((1, 1, MOD_TN), lambda l, j: (l, 0, j)),
        ],
        out_specs=pl.BlockSpec((1, MOD_ROWS, MOD_TN), lambda l, j: (l, 0, j)),
        compiler_params=pltpu.CompilerParams(
            dimension_semantics=("arbitrary", "arbitrary"), vmem_limit_bytes=VMEM_LIMIT),
        name="modulation",
    )(cvec, w_mod, b_mod.reshape(depth, 1, n))


def _head_norm(t, seg, gain):
    sq = t * t
    hi = sq.astype(BF16)
    lo = (sq - hi.astype(F32)).astype(BF16)
    ms = _dot(hi, seg) + _dot(lo, seg)
    return t * lax.rsqrt(ms + NORM_EPS) * gain


def _rope(y, cos, sin_signed):
    w = y.shape[-1]
    up = pltpu.roll(y, w - 16, axis=1)
    dn = pltpu.roll(y, 16, axis=1)
    lane = lax.broadcasted_iota(jnp.int32, y.shape, 1)
    swapped = jnp.where((lane & 31) < 16, up, dn)
    return y * cos + swapped * sin_signed


def _swap_middle_heads(q):
    lane = lax.broadcasted_iota(jnp.int32, q.shape, 1)
    from_right = pltpu.roll(q, 3 * HEAD_DIM, axis=1)
    from_left = pltpu.roll(q, HEAD_DIM, axis=1)
    return jnp.where((lane >= HEAD_DIM) & (lane < 2 * HEAD_DIM), from_right,
                     jnp.where((lane >= 2 * HEAD_DIM) & (lane < 3 * HEAD_DIM), from_left, q))


def _in_kernel(x_ref, sh_ref, sc_ref, g_ref, w_ref, seg_ref, aqg_ref, akg_ref, dqg_ref, dkg_ref,
               cosq_ref, sinq_ref, cosk_ref, sink_ref, bvg_ref, ws_ref, bs_ref,
               qa_ref, ka_ref, va_ref, ob_ref, c_ref, qd_ref, kd_ref, vd_ref, v_scr, *, use_rope):
    xt = x_ref[0]
    ms = jnp.mean(xt * xt, axis=-1, keepdims=True)
    h = xt * lax.rsqrt(ms + NORM_EPS) * g_ref[...]
    h = (h * (1.0 + sc_ref[...]) + sh_ref[...]).astype(BF16)

    p = _dot(h, w_ref[...])

    def proj(off, width):
        return p[:, off:off + width]

    seg = seg_ref[...]
    qa = _head_norm(proj(OFF_AQ, 256), seg, aqg_ref[...])
    ka = _head_norm(proj(OFF_AK, 128), seg[:128, :128], akg_ref[...])
    if use_rope:
        qa = _rope(qa, cosq_ref[...], sinq_ref[...])
        ka = _rope(ka, cosk_ref[...], sink_ref[...])
    scale = HEAD_DIM ** -0.5 * LOG2E
    qa_ref[0] = (_swap_middle_heads(qa) * scale).astype(BF16)
    ka_ref[0] = ka.astype(BF16)
    v_scr[:, 0:128] = proj(OFF_AV, 128)
    v_scr[:, 128:384] = proj(OFF_DV, 256)
    vat = v_scr[:, 0:128].T.astype(BF16)
    for j in range(vat.shape[1] // A_BLOCK):
        va_ref[0, j] = vat[:, j * A_BLOCK:(j + 1) * A_BLOCK]

    qd = _head_norm(proj(OFF_DQ, 256), seg, dqg_ref[...])
    qd_ref[0] = (qd * scale).astype(BF16)
    kd_ref[0] = _head_norm(proj(OFF_DK, 256), seg, dkg_ref[...]).astype(BF16)
    vdt = v_scr[:, 128:384].T.astype(BF16)
    for j in range(vdt.shape[1] // MIX_TQ):
        vd_ref[0, j] = vdt[:, j * MIX_TQ:(j + 1) * MIX_TQ]

    c_ref[0] = proj(OFF_C, 256)

    u = jax.nn.gelu(proj(OFF_BU, 256))
    v = jax.nn.gelu(proj(OFF_BV, 256))
    mu = jnp.mean(v, axis=-1, keepdims=True)
    vc = v - mu
    var = jnp.mean(vc * vc, axis=-1, keepdims=True)
    vn = (vc * lax.rsqrt(var + NORM_EPS) * bvg_ref[...]).astype(BF16)
    lane = lax.broadcasted_iota(jnp.int32, (B_CHUNK, GROUP_W), 1)
    zero = jnp.zeros((B_CHUNK, GROUP_W), BF16)
    tm = xt.shape[0]
    for ch in range(tm // B_CHUNK):
        rows = slice(ch * B_CHUNK, (ch + 1) * B_CHUNK)
        vch = vn[rows]
        stacked = jnp.concatenate(
            [jnp.where((lane >= g * B_GROUP_W) & (lane < (g + 1) * B_GROUP_W), vch, zero)
             for g in range(B_GROUPS)], axis=0)
        z = _dot(ws_ref[...], stacked) + bs_ref[...]
        ob_ref[0, rows, :] = (u[rows] * z).astype(BF16)


def _mod_spec(layer, row, which, batch_axis=0):
    return pl.BlockSpec((None, None, None, 1, D_MODEL),
                        lambda *idx: (layer, row(idx[batch_axis]), which, 0, 0))


def _layer_spec(layer, shape):
    return pl.BlockSpec((None,) + tuple(shape), lambda *_: (layer,) + (0,) * len(shape))


def _in_projection(x, mod, row, layer, g_mix, w_in, seg, aqg, akg, dqg, dkg, tables, bvg, ws_cat, bs_t,
                   *, use_rope):
    b, t, d = x.shape
    cosq, sinq, cosk, sink = tables
    tm = min(IN_TM, t)
    tok = lambda i, bi: (bi, i, 0)
    const2 = lambda i, bi: (0, 0)
    pos2 = (lambda i, bi: (i, 0)) if use_rope else const2
    assert tm % MIX_TQ == 0 and MIX_TQ % A_BLOCK == 0
    out_shapes = (
        jax.ShapeDtypeStruct((b, t, 256), BF16),
        jax.ShapeDtypeStruct((b, t, 128), BF16),
        jax.ShapeDtypeStruct((b, t // A_BLOCK, 128, A_BLOCK), BF16),
        jax.ShapeDtypeStruct((b, t, 256), BF16),
        jax.ShapeDtypeStruct((b, t, 256), F32),
        jax.ShapeDtypeStruct((b, t, 256), BF16),
        jax.ShapeDtypeStruct((b, t, 256), BF16),
        jax.ShapeDtypeStruct((b, t // MIX_TQ, 256, MIX_TQ), BF16),
    )
    blk4 = lambda i, bi: (bi, i, 0, 0)
    out_specs = tuple(
        pl.BlockSpec((1, tm // A_BLOCK, 128, A_BLOCK), blk4) if k == 2 else
        pl.BlockSpec((1, tm // MIX_TQ, 256, MIX_TQ), blk4) if k == 7 else
        pl.BlockSpec((1, tm, s.shape[-1]), tok)
        for k, s in enumerate(out_shapes))
    return pl.pallas_call(
        functools.partial(_in_kernel, use_rope=use_rope),
        out_shape=out_shapes,
        grid=(t // tm, b),
        in_specs=[
            pl.BlockSpec((1, tm, d), tok),
            _mod_spec(layer, row, 0, batch_axis=1),
            _mod_spec(layer, row, 1, batch_axis=1),
            _layer_spec(layer, (1, d)),
            _layer_spec(layer, (d, IN_TOTAL)),
            pl.BlockSpec((256, 256), const2),
            _layer_spec(layer, (1, 256)),
            _layer_spec(layer, (1, 128)),
            _layer_spec(layer, (1, 256)),
            _layer_spec(layer, (1, 256)),
            pl.BlockSpec((tm, 256), pos2),
            pl.BlockSpec((tm, 256), pos2),
            pl.BlockSpec((tm, 128), pos2),
            pl.BlockSpec((tm, 128), pos2),
            _layer_spec(layer, (1, 256)),
            _layer_spec(layer, (B_CHUNK, B_GROUPS * B_CHUNK)),
            _layer_spec(layer, (B_CHUNK, GROUP_W)),
        ],
        out_specs=out_specs,
        scratch_shapes=[pltpu.VMEM((tm, 384), F32)],
        compiler_params=pltpu.CompilerParams(
            dimension_semantics=("arbitrary", "arbitrary"), vmem_limit_bytes=VMEM_LIMIT),
        name="in_projection_rope" if use_rope else "in_projection_ctx",
    )(x, mod, mod, g_mix, w_in, seg, aqg, akg, dqg, dkg, cosq, sinq, cosk, sink, bvg, ws_cat, bs_t)


def _softmax_pv_t(parts, extra_logit):
    m = None
    for s, _ in parts:
        pm = jnp.max(s, axis=0, keepdims=True)
        m = pm if m is None else jnp.maximum(m, pm)
    if extra_logit is not None:
        m = jnp.maximum(m, extra_logit)
    acc = None
    for s, vt in parts:
        dh, n = vt.shape
        vt_ones = jnp.concatenate([vt, jnp.ones((BF16_SUBLANES, n), BF16)], axis=0)
        pv = _dot(vt_ones, jnp.exp2(s - m).astype(BF16))
        acc = pv if acc is None else acc + pv
    denom = acc[dh:dh + 1]
    if extra_logit is not None:
        denom = denom + jnp.exp2(extra_logit - m)
    return acc[:dh] / denom


def _head_lanes(q, head, width=HEAD_DIM):
    lane = lax.broadcasted_iota(jnp.int32, q.shape, 1)
    return jnp.where((lane >= head * width) & (lane < (head + 1) * width), q, jnp.zeros_like(q))


def _pool_tile(c_ref, inv, t0, n_tok, tq):
    y = c_ref[0, pl.ds(t0, tq), :]
    lo_start = pl.multiple_of(jnp.maximum(t0 - POOL_HALO, 0), POOL_HALO)
    hi_start = pl.multiple_of(jnp.minimum(t0 + tq, n_tok - POOL_HALO), POOL_HALO)
    lo = c_ref[0, pl.ds(lo_start, POOL_HALO), :]
    hi = c_ref[0, pl.ds(hi_start, POOL_HALO), :]
    lo = jnp.where(t0 > 0, lo, 0.0)
    hi = jnp.where(t0 + tq < n_tok, hi, 0.0)
    ypad = jnp.concatenate([lo, y, hi], axis=0)

    n = tq + 2 * POOL_HALO
    w2 = ypad + pltpu.roll(ypad, 1, axis=0)
    w4 = pltpu.roll(w2, 1, axis=0) + pltpu.roll(w2, n - 1, axis=0)
    w8 = pltpu.roll(w4, 2, axis=0) + pltpu.roll(w4, n - 2, axis=0)
    w16 = pltpu.roll(w8, 4, axis=0) + pltpu.roll(w8, n - 4, axis=0)
    sums = tuple(w[POOL_HALO:POOL_HALO + tq] for w in (w2, w4, w8, w16))

    lane = lax.broadcasted_iota(jnp.int32, (tq, GROUP_W), 1)
    total = sums[0]
    for gi in range(1, len(C_POOLS)):
        total = jnp.where(lane >= gi * C_GROUP_W, sums[gi], total)
    return total * inv - y


def _mix_kernel(*refs, local, n_tok, layer):
    x_ref = refs[1]
    deferred = []
    for sub in range(x_ref.shape[1] // MIX_TQ):
        _mix_subtile(*refs, sub=sub, deferred=deferred, local=local, n_tok=n_tok, layer=layer)
    while deferred:
        deferred.pop(0)()


def _mix_subtile(sink_ref, x_ref, g1_ref, qa_ref, ka_ref, va_ref, kca_ref, vca_ref, ob_ref, c_ref,
                 inv_ref, wp_ref, cs_ref, qd_ref, kd_ref, vd_ref, kcd_ref, vcd_ref, bias_ref, amask_ref, wo_ref,
                 o_ref, oa_scr, od_scr, mix_scr, *, sub, deferred, local, n_tok, layer):
    tq = MIX_TQ
    ti = pl.program_id(1) * (x_ref.shape[1] // tq) + sub
    t0 = pl.multiple_of(ti * tq, tq)
    tile = slice(sub * tq, (sub + 1) * tq)

    pooled = _pool_tile(c_ref, inv_ref[tile, :], t0, n_tok, tq)
    out_c = (_dot(pooled.astype(BF16), wp_ref[...]) * cs_ref[...]).astype(BF16)

    if local:
        n_tiles = n_tok // tq
        half = D_WIN_ROWS // 2
        pair_entries = []
        for jj in range(3 * tq // GRID_W):
            row_entries = []
            for i in range(0, tq // GRID_W, 2):
                interior = PAIR_INTERIOR + (jj - half - i) + PAIR_COUNT // 2 - 1
                first = PAIR_EDGE + (jj - i) + PAIR_COUNT // 2 - 1 if jj < D_WIN_ROWS else PAIR_MASKED
                last = (PAIR_EDGE + (jj - D_WIN_ROWS - i) + PAIR_COUNT // 2 - 1 if jj >= half
                        else PAIR_MASKED)
                row_entries.append(jnp.where(ti == 0, first, jnp.where(ti == n_tiles - 1, last, interior)))
            pair_entries.append(row_entries)

    n_ctx_blk = vca_ref.shape[1]
    jobs = []

    def a_job(blk):
        cols = slice(blk * A_BLOCK, (blk + 1) * A_BLOCK)

        def scores():
            qblk = qa_ref[0, sub * tq + blk * A_BLOCK:sub * tq + (blk + 1) * A_BLOCK, :]
            qcat = jnp.concatenate([qblk[:, :128], qblk[:, 128:]], axis=0)
            qm = jnp.concatenate([_head_lanes(qcat, kv) for kv in range(A_KV_HEADS)], axis=0)
            if local:
                nblk = ti * (tq // A_BLOCK) + blk
                b0 = jnp.clip(nblk - 1, 0, n_tok // A_BLOCK - 3)
                kstart = pl.multiple_of(b0 * A_BLOCK, A_BLOCK)
                kwin = ka_ref[0, pl.ds(kstart, 3 * A_BLOCK), :]
                last_blk = n_tok // A_BLOCK - 1
                kind = jnp.where(nblk == 0, 1, jnp.where(nblk == last_blk, 2, 0))
                s_loc = _dot_nt(kwin, qm)
            s_ctx = _dot_nt(kca_ref[0], qm)
            per_head = []
            for kv in range(A_KV_HEADS):
                rows = slice(kv * HEAD_DIM, (kv + 1) * HEAD_DIM)
                qsl = slice(kv * 2 * A_BLOCK, (kv + 1) * 2 * A_BLOCK)
                parts = []
                if local:
                    vt = jnp.concatenate([va_ref[0, b0 + j, rows, :] for j in range(3)], axis=1)
                    parts.append((s_loc[:, qsl] + amask_ref[kind], vt))
                vct = jnp.concatenate([vca_ref[0, j, rows, :] for j in range(n_ctx_blk)], axis=1)
                parts.append((s_ctx[:, qsl], vct))
                per_head.append(parts)
            return per_head

        def finish(per_head):
            for kv, parts in enumerate(per_head):
                col = lax.broadcasted_iota(jnp.int32, (1, 2 * A_BLOCK), 1)
                sink = jnp.where(col < A_BLOCK, sink_ref[layer, 2 * kv], sink_ref[layer, 2 * kv + 1])
                o2 = _softmax_pv_t(parts, sink)
                oa_scr[sub, (2 * kv) * 64:(2 * kv + 1) * 64, cols] = o2[:, :A_BLOCK]
                oa_scr[sub, (2 * kv + 1) * 64:(2 * kv + 2) * 64, cols] = o2[:, A_BLOCK:]

        return scores, finish

    def d_job(pair):
        heads = (2 * pair, 2 * pair + 1)

        def scores():
            qd = qd_ref[0, tile, :]
            qm = jnp.concatenate([_head_lanes(qd, hd) for hd in heads], axis=0)
            if local:
                d0 = jnp.clip(ti - 1, 0, n_tok // tq - 3)
                kwin = kd_ref[0, pl.ds(pl.multiple_of(d0 * tq, tq), 3 * tq), :]
                s_loc = _dot_nt(kwin, qm)
            s_ctx = _dot_nt(kcd_ref[0], qm)
            per_head = []
            for k, hd in enumerate(heads):
                rows = slice(hd * HEAD_DIM, (hd + 1) * HEAD_DIM)
                qsl = slice(k * tq, (k + 1) * tq)
                parts = []
                if local:
                    bias = jnp.concatenate(
                        [jnp.concatenate([bias_ref[hd, e] for e in row_entries], axis=1)
                         for row_entries in pair_entries], axis=0)
                    vt = jnp.concatenate([vd_ref[0, d0 + j, rows, :] for j in range(3)], axis=1)
                    parts.append((s_loc[:, qsl] + bias, vt))
                parts.append((s_ctx[:, qsl], vcd_ref[0, 0, rows, :]))
                per_head.append(parts)
            return per_head

        def finish(per_head):
            for hd, parts in zip(heads, per_head):
                od_scr[sub, hd * HEAD_DIM:(hd + 1) * HEAD_DIM, :] = _softmax_pv_t(parts, None)

        return scores, finish

    for blk in range(tq // A_BLOCK):
        jobs.append(a_job(blk))
    for pair in range(D_HEADS // 2):
        jobs.append(d_job(pair))
    ahead = min(SCORE_LOOKAHEAD, len(jobs))
    pending = [jobs[j][0]() for j in range(ahead)]
    for j, (_, finish) in enumerate(jobs):
        if j + ahead < len(jobs):
            pending.append(jobs[j + ahead][0]())
        if deferred:
            deferred.pop(0)()
        finish(pending.pop(0))

    mix_scr[sub, :, 0:256] = oa_scr[sub].T.astype(BF16)
    mix_scr[sub, :, 256:512] = ob_ref[0, tile, :]
    mix_scr[sub, :, 512:768] = out_c
    mix_scr[sub, :, 768:1024] = od_scr[sub].T.astype(BF16)

    def out_chunk(c):
        cols = slice(c * OUT_CHUNK, (c + 1) * OUT_CHUNK)

        def run():
            mix = _dot(mix_scr[sub], wo_ref[:, cols])
            o_ref[0, tile, cols] = x_ref[0, tile, cols] + g1_ref[:, cols] * mix

        return run

    deferred.extend(out_chunk(c) for c in range(D_MODEL // OUT_CHUNK))


def _mixer(x, mod, row, layer, qa, ka, va, kca, vca, ob, cin, inv_cnt, wpool, cscale, qd, kd, vd, kcd, vcd, bias,
           amask, w_out, sink, *, local):
    b, t, d = x.shape
    lc = kca.shape[1]
    tq = MIX_TQ
    ts = min(MIX_STEP_TILES * tq, t)
    n_sub = ts // tq
    tok = lambda bi, i: (bi, i, 0)
    per_b = lambda bi, i: (bi, 0, 0)
    per_b4 = lambda bi, i: (bi, 0, 0, 0)

    return pl.pallas_call(
        functools.partial(_mix_kernel, local=local, n_tok=t, layer=layer),
        out_shape=jax.ShapeDtypeStruct((b, t, d), F32),
        grid=(b, t // ts),
        in_specs=[
            pl.BlockSpec(memory_space=pltpu.SMEM),
            pl.BlockSpec((1, ts, d), tok),
            _mod_spec(layer, row, 2),
            pl.BlockSpec((1, ts, 256), tok),
            pl.BlockSpec((1, t, 128), per_b),
            pl.BlockSpec((1, t // A_BLOCK, 128, A_BLOCK), per_b4),
            pl.BlockSpec((1, lc, 128), per_b),
            pl.BlockSpec((1, lc // A_BLOCK, 128, A_BLOCK), per_b4),
            pl.BlockSpec((1, ts, 256), tok),
            pl.BlockSpec((1, t, 256), per_b),
            pl.BlockSpec((ts, 256), lambda bi, i: (i, 0)),
            _layer_spec(layer, (256, 256)),
            _layer_spec(layer, (1, 256)),
            pl.BlockSpec((1, ts, 256), tok),
            pl.BlockSpec((1, t, 256), per_b),
            pl.BlockSpec((1, t // tq, 256, tq), per_b4),
            pl.BlockSpec((1, lc, 256), per_b),
            pl.BlockSpec((1, lc // tq, 256, tq), per_b4),
            _layer_spec(layer, (D_HEADS, PAIR_MASKED + 1, GRID_W, 2 * GRID_W)),
            pl.BlockSpec((3, 3 * A_BLOCK, 2 * A_BLOCK), lambda bi, i: (0, 0, 0)),
            _layer_spec(layer, (d, d)),
        ],
        out_specs=pl.BlockSpec((1, ts, d), tok),
        scratch_shapes=[pltpu.VMEM((n_sub, GROUP_W, tq), F32), pltpu.VMEM((n_sub, GROUP_W, tq), F32),
                        pltpu.VMEM((n_sub, tq, d), BF16)],
        compiler_params=pltpu.CompilerParams(
            dimension_semantics=("arbitrary", "arbitrary"), vmem_limit_bytes=VMEM_LIMIT),
        name="mixer_latent" if local else "mixer_ctx",
    )(sink, x, mod, qa, ka, va, kca, vca, ob, cin, inv_cnt, wpool, cscale, qd, kd, vd, kcd, vcd, bias, amask,
      w_out)


def _ffn_kernel(x_ref, sh_ref, sc_ref, g2_ref, gn_ref, wg_ref, wu_ref, wd_ref, o_ref, act_scr):
    xt = x_ref[0]
    ms = jnp.mean(xt * xt, axis=-1, keepdims=True)
    h = xt * lax.rsqrt(ms + NORM_EPS) * gn_ref[...]
    h = (h * (1.0 + sc_ref[...]) + sh_ref[...]).astype(BF16)
    for c in range(FF_DIM // FFN_TF):
        cols = slice(c * FFN_TF, (c + 1) * FFN_TF)
        gate = _dot(h, wg_ref[:, cols])
        up = _dot(h, wu_ref[:, cols])
        act_scr[:, cols] = (gate * jax.nn.sigmoid(gate) * up).astype(BF16)
    o_ref[0] = xt + g2_ref[...] * _dot(act_scr[...], wd_ref[...])


def _ffn(x, mod, row, layer, g_ffn, wg, wu, wd):
    b, t, d = x.shape
    tm = min(FFN_TM, t)
    tok = lambda bi, i: (bi, i, 0)
    per_layer = lambda bi, i: (layer, 0, 0)
    resident = pl.Buffered(1)
    return pl.pallas_call(
        _ffn_kernel,
        out_shape=jax.ShapeDtypeStruct((b, t, d), F32),
        grid=(b, t // tm),
        in_specs=[
            pl.BlockSpec((1, tm, d), tok),
            _mod_spec(layer, row, 3),
            _mod_spec(layer, row, 4),
            _mod_spec(layer, row, 5),
            _layer_spec(layer, (1, d)),
            pl.BlockSpec((None, d, FF_DIM), per_layer, pipeline_mode=resident),
            pl.BlockSpec((None, d, FF_DIM), per_layer, pipeline_mode=resident),
            pl.BlockSpec((None, FF_DIM, d), per_layer, pipeline_mode=resident),
        ],
        out_specs=pl.BlockSpec((1, tm, d), tok),
        scratch_shapes=[pltpu.VMEM((tm, FF_DIM), BF16)],
        compiler_params=pltpu.CompilerParams(
            dimension_semantics=("arbitrary", "arbitrary"), vmem_limit_bytes=VMEM_LIMIT),
        name="ffn",
    )(x, mod, mod, mod, g_ffn, wg, wu, wd)


def _rope_tables(s):
    t = np.arange(s)
    half = 16
    inv = np.power(np.float32(ROPE_BASE), -np.arange(half, dtype=np.float32) / half).astype(np.float32)
    ang_r = (t // GRID_W).astype(np.float32)[:, None] * inv[None, :]
    ang_c = (t % GRID_W).astype(np.float32)[:, None] * inv[None, :]
    cos = np.concatenate([np.cos(ang_r)] * 2 + [np.cos(ang_c)] * 2, axis=-1)
    sin = np.concatenate([-np.sin(ang_r), np.sin(ang_r), -np.sin(ang_c), np.sin(ang_c)], axis=-1)
    return cos.astype(np.float32), sin.astype(np.float32)


def _pool_inverse_counts(n_tok):
    t = np.arange(n_tok)
    cols = []
    for w in C_POOLS:
        lo = np.clip(t - w // 2, 0, n_tok)
        hi = np.clip(t - w // 2 + w, 0, n_tok)
        cols.append(np.repeat((np.float32(1.0) / (hi - lo).astype(np.float32))[:, None], C_GROUP_W, axis=1))
    return jnp.asarray(np.concatenate(cols, axis=1), F32)


def _window_mask():
    kk = np.arange(3 * A_BLOCK)[:, None]
    qq = np.arange(2 * A_BLOCK)[None, :] % A_BLOCK
    kinds = [np.where(np.abs(kk - qq - back * A_BLOCK) <= A_WINDOW, 0.0, NEG_INF) for back in (1, 0, 2)]
    return jnp.asarray(np.stack(kinds), F32)


PAIR_COUNT = 2 * (D_WIN_ROWS - 1)
PAIR_INTERIOR = 0
PAIR_EDGE = PAIR_COUNT
PAIR_MASKED = 2 * PAIR_COUNT


def _neighbour_bias_pairs(rpb):
    cidx = np.arange(GRID_W)
    col_start = np.clip(cidx - D_WIN_COLS // 2, 0, GRID_W - D_WIN_COLS)
    col_ok_t = ((cidx[None, :] >= col_start[:, None]) & (cidx[None, :] < col_start[:, None] + D_WIN_COLS)).T
    coff_t = np.clip(cidx[:, None] - cidx[None, :], -(D_WIN_COLS - 1), D_WIN_COLS - 1) + (D_WIN_COLS - 1)
    select = (np.arange(2 * D_WIN_COLS - 1)[:, None, None] == coff_t[None]).astype(np.float32)
    blocks = lax.dot_general(rpb.astype(F32), jnp.asarray(select), (((3,), (0,)), ((), ())),
                             precision=lax.Precision.HIGHEST)
    blocks = jnp.where(col_ok_t, blocks, NEG_INF)
    d = np.arange(-(D_WIN_ROWS - 1), D_WIN_ROWS)
    seen = (d >= -(D_WIN_ROWS // 2)) & (d < D_WIN_ROWS // 2)
    interior = jnp.where(seen[:, None, None], blocks, NEG_INF)
    pair = lambda t: jnp.concatenate([t[:, :, 1:], t[:, :, :-1]], axis=-1)
    masked = jnp.full(blocks.shape[:2] + (1, GRID_W, 2 * GRID_W), NEG_INF, F32)
    return jnp.concatenate([pair(interior), pair(blocks), masked], axis=2)


def kernel(x, c, ctx, c_ctx, w_mod, b_mod, g_mix, g_ffn, w_in, w_out, a_q_gain, a_k_gain, a_sink,
           b_v_gain, b_w_s, b_b_s, c_w_pool, c_scale, d_q_gain, d_k_gain, d_rpb, w_gate, w_up, w_down):
    bsz, s, d = x.shape
    lc = ctx.shape[1]
    n_rows = s // GRID_W
    assert s % MIX_TQ == 0 and n_rows >= 2 * D_WIN_ROWS and MIX_TQ // GRID_W == D_WIN_ROWS // 2

    cvec = jnp.zeros((MOD_ROWS, d), F32).at[:bsz].set(c).at[bsz].set(c_ctx)
    mod = _modulation(cvec, w_mod, b_mod).reshape(DEPTH, MOD_ROWS, 6, 1, d)
    lat_row = lambda bi: bi
    ctx_row = lambda bi: bsz

    seg = jnp.asarray(np.kron(np.eye(4), np.full((64, 64), 1.0 / 64)), BF16)
    cos1, sin1 = _rope_tables(s)
    tables_lat = tuple(jnp.asarray(np.tile(tab, (1, reps)))
                       for tab, reps in ((cos1, 4), (sin1, 4), (cos1, 2), (sin1, 2)))
    tables_ctx = tuple(jnp.zeros((min(IN_TM, bsz * lc), w), F32) for w in (256, 256, 128, 128))

    amask = _window_mask()
    inv_lat, inv_ctx = _pool_inverse_counts(s), _pool_inverse_counts(lc)
    bias_all = _neighbour_bias_pairs(d_rpb * LOG2E)

    w_in_b = w_in.astype(BF16)
    w_out_b = w_out.astype(BF16)
    wg_b, wu_b, wd_b = w_gate.astype(BF16), w_up.astype(BF16), w_down.astype(BF16)
    gmix = g_mix.reshape(DEPTH, 1, d)
    gffn = g_ffn.reshape(DEPTH, 1, d)
    aqg = jnp.tile(a_q_gain, (1, 4)).reshape(DEPTH, 1, 256)
    akg = jnp.tile(a_k_gain, (1, 2)).reshape(DEPTH, 1, 128)
    dqg = jnp.tile(d_q_gain, (1, 4)).reshape(DEPTH, 1, 256)
    dkg = jnp.tile(d_k_gain, (1, 4)).reshape(DEPTH, 1, 256)
    bvg = b_v_gain.reshape(DEPTH, 1, 256)
    ws_cat = jnp.transpose(b_w_s, (0, 2, 1, 3)).reshape(DEPTH, B_CHUNK, B_GROUPS * B_CHUNK).astype(BF16)
    bs_t = jnp.repeat(jnp.swapaxes(b_b_s, 1, 2), B_GROUP_W, axis=2)
    wpool = jnp.einsum('lgcd,gh->lgchd', c_w_pool, jnp.eye(len(C_POOLS), dtype=F32)
                       ).reshape(DEPTH, GROUP_W, GROUP_W).astype(BF16)
    cscale = c_scale.reshape(DEPTH, 1, 256)
    sink = a_sink * LOG2E
    common = (gmix, w_in_b, seg, aqg, akg, dqg, dkg)

    xc = ctx
    for l in range(DEPTH):
        last = l == DEPTH - 1
        lat = _in_projection(x, mod, lat_row, l, *common, tables_lat, bvg, ws_cat, bs_t, use_rope=True)
        con = _in_projection(xc.reshape(1, bsz * lc, d), mod, ctx_row, l, *common, tables_ctx, bvg, ws_cat,
                             bs_t, use_rope=False)
        qa, ka, va, ob, cin, qd, kd, vd = lat
        cqa, cka, cva, cob, ccin, cqd, ckd, cvd = [
            o.reshape((bsz, o.shape[1] // bsz) + o.shape[2:]) for o in con]

        x = _mixer(x, mod, lat_row, l, qa, ka, va, cka, cva, ob, cin, inv_lat, wpool, cscale, qd, kd, vd, ckd, cvd,
                   bias_all, amask, w_out_b, sink, local=True)
        x = _ffn(x, mod, lat_row, l, gffn, wg_b, wu_b, wd_b)
        if not last:
            xc = _mixer(xc, mod, ctx_row, l, cqa, cka, cva, cka, cva, cob, ccin, inv_ctx, wpool, cscale, cqd, ckd,
                        cvd, ckd, cvd, bias_all, amask, w_out_b, sink, local=False)
            xc = _ffn(xc.reshape(1, bsz * lc, d), mod, ctx_row, l, gffn, wg_b, wu_b, wd_b).reshape(bsz, lc, d)
    return x
```

```python
import functools

import jax
import jax.numpy as jnp
import numpy as np
from jax import lax
from jax.experimental import pallas as pl
from jax.experimental.pallas import tpu as pltpu

F32 = jnp.float32
BF16 = jnp.bfloat16

D_MODEL = 1024
DEPTH = 2
GRID_W = 64
HEAD_DIM = 64
GROUP_W = 256
A_HEADS = 4
A_KV_HEADS = 2
A_WINDOW = 128
A_BLOCK = 128
B_CHUNK = 128
B_GROUPS = 4
B_GROUP_W = 64
C_POOLS = (2, 4, 8, 16)
C_GROUP_W = 64
D_HEADS = 4
D_WIN_ROWS = 8
D_WIN_COLS = 16
FF_DIM = 2816
ROPE_BASE = 10000.0
NORM_EPS = 1e-6
NEG_INF = -1e30
LOG2E = 1.4426950408889634
IN_TOTAL = 2048
OFF_AQ, OFF_AK, OFF_AV, OFF_BU, OFF_BV, OFF_C, OFF_DQ, OFF_DK, OFF_DV = (
    0, 256, 384, 512, 768, 1024, 1280, 1536, 1792)

MOD_ROWS = 16
MOD_TN = 1536
IN_TM = 512
MIX_TQ = 256
FFN_TM = 1024
FFN_TF = 256
BF16_SUBLANES = 16
POOL_HALO = 8
MIX_STEP_TILES = 4
OUT_CHUNK = 256
SCORE_LOOKAHEAD = 2
VMEM_LIMIT = 56 * 1024 * 1024


def _dot(a, b):
    return jnp.dot(a, b, preferred_element_type=F32)


def _dot_nt(a, b):
    return lax.dot_general(a, b, (((1,), (1,)), ((), ())), preferred_element_type=F32)


def _mod_kernel(c_ref, w_ref, b_ref, o_ref):
    cv = c_ref[...]
    act = cv * jax.nn.sigmoid(cv)
    o_ref[0] = _dot(act.astype(BF16), w_ref[0].astype(BF16)) + b_ref[0]


def _modulation(cvec, w_mod, b_mod):
    depth, d, n = w_mod.shape
    return pl.pallas_call(
        _mod_kernel,
        out_shape=jax.ShapeDtypeStruct((depth, MOD_ROWS, n), F32),
        grid=(depth, n // MOD_TN),
        in_specs=[
            pl.BlockSpec((MOD_ROWS, d), lambda l, j: (0, 0)),
            pl.BlockSpec((1, d, MOD_TN), lambda l, j: (l, 0, j)),
            pl.BlockSpec((1, 1, MOD_TN), lambda l, j: (l, 0, j)),
        ],
        out_specs=pl.BlockSpec((1, MOD_ROWS, MOD_TN), lambda l, j: (l, 0, j)),
        compiler_params=pltpu.CompilerParams(
            dimension_semantics=("arbitrary", "arbitrary"), vmem_limit_bytes=VMEM_LIMIT),
        name="modulation",
    )(cvec, w_mod, b_mod.reshape(depth, 1, n))


def _head_norm(t, seg, gain):
    sq = t * t
    hi = sq.astype(BF16)
    lo = (sq - hi.astype(F32)).astype(BF16)
    ms = _dot(hi, seg) + _dot(lo, seg)
    return t * lax.rsqrt(ms + NORM_EPS) * gain


def _rope(y, cos, sin_signed):
    w = y.shape[-1]
    up = pltpu.roll(y, w - 16, axis=1)
    dn = pltpu.roll(y, 16, axis=1)
    lane = lax.broadcasted_iota(jnp.int32, y.shape, 1)
    swapped = jnp.where((lane & 31) < 16, up, dn)
    return y * cos + swapped * sin_signed


def _swap_middle_heads(q):
    lane = lax.broadcasted_iota(jnp.int32, q.shape, 1)
    from_right = pltpu.roll(q, 3 * HEAD_DIM, axis=1)
    from_left = pltpu.roll(q, HEAD_DIM, axis=1)
    return jnp.where((lane >= HEAD_DIM) & (lane < 2 * HEAD_DIM), from_right,
                     jnp.where((lane >= 2 * HEAD_DIM) & (lane < 3 * HEAD_DIM), from_left, q))


def _in_kernel(x_ref, sh_ref, sc_ref, g_ref, w_ref, seg_ref, aqg_ref, akg_ref, dqg_ref, dkg_ref,
               cosq_ref, sinq_ref, cosk_ref, sink_ref, bvg_ref, ws_ref, bs_ref,
               qa_ref, ka_ref, va_ref, ob_ref, c_ref, qd_ref, kd_ref, vd_ref, v_scr, *, use_rope):
    xt = x_ref[0]
    ms = jnp.mean(xt * xt, axis=-1, keepdims=True)
    h = xt * lax.rsqrt(ms + NORM_EPS) * g_ref[...]
    h = (h * (1.0 + sc_ref[...]) + sh_ref[...]).astype(BF16)

    p = _dot(h, w_ref[...])

    def proj(off, width):
        return p[:, off:off + width]

    seg = seg_ref[...]
    qa = _head_norm(proj(OFF_AQ, 256), seg, aqg_ref[...])
    ka = _head_norm(proj(OFF_AK, 128), seg[:128, :128], akg_ref[...])
    if use_rope:
        qa = _rope(qa, cosq_ref[...], sinq_ref[...])
        ka = _rope(ka, cosk_ref[...], sink_ref[...])
    scale = HEAD_DIM ** -0.5 * LOG2E
    qa_ref[0] = (_swap_middle_heads(qa) * scale).astype(BF16)
    ka_ref[0] = ka.astype(BF16)
    v_scr[:, 0:128] = proj(OFF_AV, 128)
    v_scr[:, 128:384] = proj(OFF_DV, 256)
    vat = v_scr[:, 0:128].T.astype(BF16)
    for j in range(vat.shape[1] // A_BLOCK):
        va_ref[0, j] = vat[:, j * A_BLOCK:(j + 1) * A_BLOCK]

    qd = _head_norm(proj(OFF_DQ, 256), seg, dqg_ref[...])
    qd_ref[0] = (qd * scale).astype(BF16)
    kd_ref[0] = _head_norm(proj(OFF_DK, 256), seg, dkg_ref[...]).astype(BF16)
    vdt = v_scr[:, 128:384].T.astype(BF16)
    for j in range(vdt.shape[1] // MIX_TQ):
        vd_ref[0, j] = vdt[:, j * MIX_TQ:(j + 1) * MIX_TQ]

    c_ref[0] = proj(OFF_C, 256)

    u = jax.nn.gelu(proj(OFF_BU, 256))
    v = jax.nn.gelu(proj(OFF_BV, 256))
    mu = jnp.mean(v, axis=-1, keepdims=True)
    vc = v - mu
    var = jnp.mean(vc * vc, axis=-1, keepdims=True)
    vn = (vc * lax.rsqrt(var + NORM_EPS) * bvg_ref[...]).astype(BF16)
    lane = lax.broadcasted_iota(jnp.int32, (B_CHUNK, GROUP_W), 1)
    zero = jnp.zeros((B_CHUNK, GROUP_W), BF16)
    tm = xt.shape[0]
    for ch in range(tm // B_CHUNK):
        rows = slice(ch * B_CHUNK, (ch + 1) * B_CHUNK)
        vch = vn[rows]
        stacked = jnp.concatenate(
            [jnp.where((lane >= g * B_GROUP_W) & (lane < (g + 1) * B_GROUP_W), vch, zero)
             for g in range(B_GROUPS)], axis=0)
        z = _dot(ws_ref[...], stacked) + bs_ref[...]
        ob_ref[0, rows, :] = (u[rows] * z).astype(BF16)


def _mod_spec(layer, row, which, batch_axis=0):
    return pl.BlockSpec((None, None, None, 1, D_MODEL),
                        lambda *idx: (layer, row(idx[batch_axis]), which, 0, 0))


def _layer_spec(layer, shape):
    return pl.BlockSpec((None,) + tuple(shape), lambda *_: (layer,) + (0,) * len(shape))


def _in_projection(x, mod, row, layer, g_mix, w_in, seg, aqg, akg, dqg, dkg, tables, bvg, ws_cat, bs_t,
                   *, use_rope):
    b, t, d = x.shape
    cosq, sinq, cosk, sink = tables
    tm = min(IN_TM, t)
    tok = lambda i, bi: (bi, i, 0)
    const2 = lambda i, bi: (0, 0)
    pos2 = (lambda i, bi: (i, 0)) if use_rope else const2
    assert tm % MIX_TQ == 0 and MIX_TQ % A_BLOCK == 0
    out_shapes = (
        jax.ShapeDtypeStruct((b, t, 256), BF16),
        jax.ShapeDtypeStruct((b, t, 128), BF16),
        jax.ShapeDtypeStruct((b, t // A_BLOCK, 128, A_BLOCK), BF16),
        jax.ShapeDtypeStruct((b, t, 256), BF16),
        jax.ShapeDtypeStruct((b, t, 256), F32),
        jax.ShapeDtypeStruct((b, t, 256), BF16),
        jax.ShapeDtypeStruct((b, t, 256), BF16),
        jax.ShapeDtypeStruct((b, t // MIX_TQ, 256, MIX_TQ), BF16),
    )
    blk4 = lambda i, bi: (bi, i, 0, 0)
    out_specs = tuple(
        pl.BlockSpec((1, tm // A_BLOCK, 128, A_BLOCK), blk4) if k == 2 else
        pl.BlockSpec((1, tm // MIX_TQ, 256, MIX_TQ), blk4) if k == 7 else
        pl.BlockSpec((1, tm, s.shape[-1]), tok)
        for k, s in enumerate(out_shapes))
    return pl.pallas_call(
        functools.partial(_in_kernel, use_rope=use_rope),
        out_shape=out_shapes,
        grid=(t // tm, b),
        in_specs=[
            pl.BlockSpec((1, tm, d), tok),
            _mod_spec(layer, row, 0, batch_axis=1),
            _mod_spec(layer, row, 1, batch_axis=1),
            _layer_spec(layer, (1, d)),
            _layer_spec(layer, (d, IN_TOTAL)),
            pl.BlockSpec((256, 256), const2),
            _layer_spec(layer, (1, 256)),
            _layer_spec(layer, (1, 128)),
            _layer_spec(layer, (1, 256)),
            _layer_spec(layer, (1, 256)),
            pl.BlockSpec((tm, 256), pos2),
            pl.BlockSpec((tm, 256), pos2),
            pl.BlockSpec((tm, 128), pos2),
            pl.BlockSpec((tm, 128), pos2),
            _layer_spec(layer, (1, 256)),
            _layer_spec(layer, (B_CHUNK, B_GROUPS * B_CHUNK)),
            _layer_spec(layer, (B_CHUNK, GROUP_W)),
        ],
        out_specs=out_specs,
        scratch_shapes=[pltpu.VMEM((tm, 384), F32)],
        compiler_params=pltpu.CompilerParams(
            dimension_semantics=("arbitrary", "arbitrary"), vmem_limit_bytes=VMEM_LIMIT),
        name="in_projection_rope" if use_rope else "in_projection_ctx",
    )(x, mod, mod, g_mix, w_in, seg, aqg, akg, dqg, dkg, cosq, sinq, cosk, sink, bvg, ws_cat, bs_t)


def _softmax_pv_t(parts, extra_logit):
    m = None
    for s, _ in parts:
        pm = jnp.max(s, axis=0, keepdims=True)
        m = pm if m is None else jnp.maximum(m, pm)
    if extra_logit is not None:
        m = jnp.maximum(m, extra_logit)
    acc = None
    for s, vt in parts:
        dh, n = vt.shape
        vt_ones = jnp.concatenate([vt, jnp.ones((BF16_SUBLANES, n), BF16)], axis=0)
        pv = _dot(vt_ones, jnp.exp2(s - m).astype(BF16))
        acc = pv if acc is None else acc + pv
    denom = acc[dh:dh + 1]
    if extra_logit is not None:
        denom = denom + jnp.exp2(extra_logit - m)
    return acc[:dh] / denom


def _head_lanes(q, head, width=HEAD_DIM):
    lane = lax.broadcasted_iota(jnp.int32, q.shape, 1)
    return jnp.where((lane >= head * width) & (lane < (head + 1) * width), q, jnp.zeros_like(q))


def _pool_tile(c_ref, inv, t0, n_tok, tq):
    y = c_ref[0, pl.ds(t0, tq), :]
    lo_start = pl.multiple_of(jnp.maximum(t0 - POOL_HALO, 0), POOL_HALO)
    hi_start = pl.multiple_of(jnp.minimum(t0 + tq, n_tok - POOL_HALO), POOL_HALO)
    lo = c_ref[0, pl.ds(lo_start, POOL_HALO), :]
    hi = c_ref[0, pl.ds(hi_start, POOL_HALO), :]
    lo = jnp.where(t0 > 0, lo, 0.0)
    hi = jnp.where(t0 + tq < n_tok, hi, 0.0)
    ypad = jnp.concatenate([lo, y, hi], axis=0)

    n = tq + 2 * POOL_HALO
    w2 = ypad + pltpu.roll(ypad, 1, axis=0)
    w4 = pltpu.roll(w2, 1, axis=0) + pltpu.roll(w2, n - 1, axis=0)
    w8 = pltpu.roll(w4, 2, axis=0) + pltpu.roll(w4, n - 2, axis=0)
    w16 = pltpu.roll(w8, 4, axis=0) + pltpu.roll(w8, n - 4, axis=0)
    sums = tuple(w[POOL_HALO:POOL_HALO + tq] for w in (w2, w4, w8, w16))

    lane = lax.broadcasted_iota(jnp.int32, (tq, GROUP_W), 1)
    total = sums[0]
    for gi in range(1, len(C_POOLS)):
        total = jnp.where(lane >= gi * C_GROUP_W, sums[gi], total)
    return total * inv - y


def _mix_kernel(*refs, local, n_tok, layer):
    x_ref = refs[1]
    deferred = []
    for sub in range(x_ref.shape[1] // MIX_TQ):
        _mix_subtile(*refs, sub=sub, deferred=deferred, local=local, n_tok=n_tok, layer=layer)
    while deferred:
        deferred.pop(0)()


def _mix_subtile(sink_ref, x_ref, g1_ref, qa_ref, ka_ref, va_ref, kca_ref, vca_ref, ob_ref, c_ref,
                 inv_ref, wp_ref, cs_ref, qd_ref, kd_ref, vd_ref, kcd_ref, vcd_ref, bias_ref, amask_ref, wo_ref,
                 o_ref, oa_scr, od_scr, mix_scr, *, sub, deferred, local, n_tok, layer):
    tq = MIX_TQ
    ti = pl.program_id(1) * (x_ref.shape[1] // tq) + sub
    t0 = pl.multiple_of(ti * tq, tq)
    tile = slice(sub * tq, (sub + 1) * tq)

    pooled = _pool_tile(c_ref, inv_ref[tile, :], t0, n_tok, tq)
    out_c = (_dot(pooled.astype(BF16), wp_ref[...]) * cs_ref[...]).astype(BF16)

    if local:
        n_tiles = n_tok // tq
        half = D_WIN_ROWS // 2
        pair_entries = []
        for jj in range(3 * tq // GRID_W):
            row_entries = []
            for i in range(0, tq // GRID_W, 2):
                interior = PAIR_INTERIOR + (jj - half - i) + PAIR_COUNT // 2 - 1
                first = PAIR_EDGE + (jj - i) + PAIR_COUNT // 2 - 1 if jj < D_WIN_ROWS else PAIR_MASKED
                last = (PAIR_EDGE + (jj - D_WIN_ROWS - i) + PAIR_COUNT // 2 - 1 if jj >= half
                        else PAIR_MASKED)
                row_entries.append(jnp.where(ti == 0, first, jnp.where(ti == n_tiles - 1, last, interior)))
            pair_entries.append(row_entries)

    n_ctx_blk = vca_ref.shape[1]
    jobs = []

    def a_job(blk):
        cols = slice(blk * A_BLOCK, (blk + 1) * A_BLOCK)

        def scores():
            qblk = qa_ref[0, sub * tq + blk * A_BLOCK:sub * tq + (blk + 1) * A_BLOCK, :]
            qcat = jnp.concatenate([qblk[:, :128], qblk[:, 128:]], axis=0)
            qm = jnp.concatenate([_head_lanes(qcat, kv) for kv in range(A_KV_HEADS)], axis=0)
            if local:
                nblk = ti * (tq // A_BLOCK) + blk
                b0 = jnp.clip(nblk - 1, 0, n_tok // A_BLOCK - 3)
                kstart = pl.multiple_of(b0 * A_BLOCK, A_BLOCK)
                kwin = ka_ref[0, pl.ds(kstart, 3 * A_BLOCK), :]
                last_blk = n_tok // A_BLOCK - 1
                kind = jnp.where(nblk == 0, 1, jnp.where(nblk == last_blk, 2, 0))
                s_loc = _dot_nt(kwin, qm)
            s_ctx = _dot_nt(kca_ref[0], qm)
            per_head = []
            for kv in range(A_KV_HEADS):
                rows = slice(kv * HEAD_DIM, (kv + 1) * HEAD_DIM)
                qsl = slice(kv * 2 * A_BLOCK, (kv + 1) * 2 * A_BLOCK)
                parts = []
                if local:
                    s_win = s_loc[:, qsl] + amask_ref[kind]
                    for j in range(3):
                        parts.append((s_win[j * A_BLOCK:(j + 1) * A_BLOCK], va_ref[0, b0 + j, rows, :]))
                for j in range(n_ctx_blk):
                    parts.append((s_ctx[j * A_BLOCK:(j + 1) * A_BLOCK, qsl], vca_ref[0, j, rows, :]))
                per_head.append(parts)
            return per_head

        def finish(per_head):
            for kv, parts in enumerate(per_head):
                col = lax.broadcasted_iota(jnp.int32, (1, 2 * A_BLOCK), 1)
                sink = jnp.where(col < A_BLOCK, sink_ref[layer, 2 * kv], sink_ref[layer, 2 * kv + 1])
                o2 = _softmax_pv_t(parts, sink)
                oa_scr[sub, (2 * kv) * 64:(2 * kv + 1) * 64, cols] = o2[:, :A_BLOCK]
                oa_scr[sub, (2 * kv + 1) * 64:(2 * kv + 2) * 64, cols] = o2[:, A_BLOCK:]

        return scores, finish

    def d_job(pair):
        heads = (2 * pair, 2 * pair + 1)

        def scores():
            qd = qd_ref[0, tile, :]
            qm = jnp.concatenate([_head_lanes(qd, hd) for hd in heads], axis=0)
            if local:
                d0 = jnp.clip(ti - 1, 0, n_tok // tq - 3)
                kwin = kd_ref[0, pl.ds(pl.multiple_of(d0 * tq, tq), 3 * tq), :]
                s_loc = _dot_nt(kwin, qm)
            s_ctx = _dot_nt(kcd_ref[0], qm)
            per_head = []
            for k, hd in enumerate(heads):
                rows = slice(hd * HEAD_DIM, (hd + 1) * HEAD_DIM)
                qsl = slice(k * tq, (k + 1) * tq)
                parts = []
                if local:
                    bias = jnp.concatenate(
                        [jnp.concatenate([bias_ref[hd, e] for e in row_entries], axis=1)
                         for row_entries in pair_entries], axis=0)
                    s_win = s_loc[:, qsl] + bias
                    for j in range(3):
                        parts.append((s_win[j * tq:(j + 1) * tq], vd_ref[0, d0 + j, rows, :]))
                parts.append((s_ctx[:, qsl], vcd_ref[0, 0, rows, :]))
                per_head.append(parts)
            return per_head

        def finish(per_head):
            for hd, parts in zip(heads, per_head):
                od_scr[sub, hd * HEAD_DIM:(hd + 1) * HEAD_DIM, :] = _softmax_pv_t(parts, None)

        return scores, finish

    for blk in range(tq // A_BLOCK):
        jobs.append(a_job(blk))
    for pair in range(D_HEADS // 2):
        jobs.append(d_job(pair))
    ahead = min(SCORE_LOOKAHEAD, len(jobs))
    pending = [jobs[j][0]() for j in range(ahead)]
    for j, (_, finish) in enumerate(jobs):
        if j + ahead < len(jobs):
            pending.append(jobs[j + ahead][0]())
        if deferred:
            deferred.pop(0)()
        finish(pending.pop(0))

    mix_scr[sub, :, 0:256] = oa_scr[sub].T.astype(BF16)
    mix_scr[sub, :, 256:512] = ob_ref[0, tile, :]
    mix_scr[sub, :, 512:768] = out_c
    mix_scr[sub, :, 768:1024] = od_scr[sub].T.astype(BF16)

    def out_chunk(c):
        cols = slice(c * OUT_CHUNK, (c + 1) * OUT_CHUNK)

        def run():
            mix = _dot(mix_scr[sub], wo_ref[:, cols])
            o_ref[0, tile, cols] = x_ref[0, tile, cols] + g1_ref[:, cols] * mix

        return run

    deferred.extend(out_chunk(c) for c in range(D_MODEL // OUT_CHUNK))


def _mixer(x, mod, row, layer, qa, ka, va, kca, vca, ob, cin, inv_cnt, wpool, cscale, qd, kd, vd, kcd, vcd, bias,
           amask, w_out, sink, *, local):
    b, t, d = x.shape
    lc = kca.shape[1]
    tq = MIX_TQ
    ts = min(MIX_STEP_TILES * tq, t)
    n_sub = ts // tq
    tok = lambda bi, i: (bi, i, 0)
    per_b = lambda bi, i: (bi, 0, 0)
    per_b4 = lambda bi, i: (bi, 0, 0, 0)

    return pl.pallas_call(
        functools.partial(_mix_kernel, local=local, n_tok=t, layer=layer),
        out_shape=jax.ShapeDtypeStruct((b, t, d), F32),
        grid=(b, t // ts),
        in_specs=[
            pl.BlockSpec(memory_space=pltpu.SMEM),
            pl.BlockSpec((1, ts, d), tok),
            _mod_spec(layer, row, 2),
            pl.BlockSpec((1, ts, 256), tok),
            pl.BlockSpec((1, t, 128), per_b),
            pl.BlockSpec((1, t // A_BLOCK, 128, A_BLOCK), per_b4),
            pl.BlockSpec((1, lc, 128), per_b),
            pl.BlockSpec((1, lc // A_BLOCK, 128, A_BLOCK), per_b4),
            pl.BlockSpec((1, ts, 256), tok),
            pl.BlockSpec((1, t, 256), per_b),
            pl.BlockSpec((ts, 256), lambda bi, i: (i, 0)),
            _layer_spec(layer, (256, 256)),
            _layer_spec(layer, (1, 256)),
            pl.BlockSpec((1, ts, 256), tok),
            pl.BlockSpec((1, t, 256), per_b),
            pl.BlockSpec((1, t // tq, 256, tq), per_b4),
            pl.BlockSpec((1, lc, 256), per_b),
            pl.BlockSpec((1, lc // tq, 256, tq), per_b4),
            _layer_spec(layer, (D_HEADS, PAIR_MASKED + 1, GRID_W, 2 * GRID_W)),
            pl.BlockSpec((3, 3 * A_BLOCK, 2 * A_BLOCK), lambda bi, i: (0, 0, 0)),
            _layer_spec(layer, (d, d)),
        ],
        out_specs=pl.BlockSpec((1, ts, d), tok),
        scratch_shapes=[pltpu.VMEM((n_sub, GROUP_W, tq), F32), pltpu.VMEM((n_sub, GROUP_W, tq), F32),
                        pltpu.VMEM((n_sub, tq, d), BF16)],
        compiler_params=pltpu.CompilerParams(
            dimension_semantics=("arbitrary", "arbitrary"), vmem_limit_bytes=VMEM_LIMIT),
        name="mixer_latent" if local else "mixer_ctx",
    )(sink, x, mod, qa, ka, va, kca, vca, ob, cin, inv_cnt, wpool, cscale, qd, kd, vd, kcd, vcd, bias, amask,
      w_out)


def _ffn_kernel(x_ref, sh_ref, sc_ref, g2_ref, gn_ref, wg_ref, wu_ref, wd_ref, o_ref, act_scr):
    xt = x_ref[0]
    ms = jnp.mean(xt * xt, axis=-1, keepdims=True)
    h = xt * lax.rsqrt(ms + NORM_EPS) * gn_ref[...]
    h = (h * (1.0 + sc_ref[...]) + sh_ref[...]).astype(BF16)
    for c in range(FF_DIM // FFN_TF):
        cols = slice(c * FFN_TF, (c + 1) * FFN_TF)
        gate = _dot(h, wg_ref[:, cols])
        up = _dot(h, wu_ref[:, cols])
        act_scr[:, cols] = (gate * jax.nn.sigmoid(gate) * up).astype(BF16)
    o_ref[0] = xt + g2_ref[...] * _dot(act_scr[...], wd_ref[...])


def _ffn(x, mod, row, layer, g_ffn, wg, wu, wd):
    b, t, d = x.shape
    tm = min(FFN_TM, t)
    tok = lambda bi, i: (bi, i, 0)
    per_layer = lambda bi, i: (layer, 0, 0)
    resident = pl.Buffered(1)
    return pl.pallas_call(
        _ffn_kernel,
        out_shape=jax.ShapeDtypeStruct((b, t, d), F32),
        grid=(b, t // tm),
        in_specs=[
            pl.BlockSpec((1, tm, d), tok),
            _mod_spec(layer, row, 3),
            _mod_spec(layer, row, 4),
            _mod_spec(layer, row, 5),
            _layer_spec(layer, (1, d)),
            pl.BlockSpec((None, d, FF_DIM), per_layer, pipeline_mode=resident),
            pl.BlockSpec((None, d, FF_DIM), per_layer, pipeline_mode=resident),
            pl.BlockSpec((None, FF_DIM, d), per_layer, pipeline_mode=resident),
        ],
        out_specs=pl.BlockSpec((1, tm, d), tok),
        scratch_shapes=[pltpu.VMEM((tm, FF_DIM), BF16)],
        compiler_params=pltpu.CompilerParams(
            dimension_semantics=("arbitrary", "arbitrary"), vmem_limit_bytes=VMEM_LIMIT),
        name="ffn",
    )(x, mod, mod, mod, g_ffn, wg, wu, wd)


def _rope_tables(s):
    t = np.arange(s)
    half = 16
    inv = np.power(np.float32(ROPE_BASE), -np.arange(half, dtype=np.float32) / half).astype(np.float32)
    ang_r = (t // GRID_W).astype(np.float32)[:, None] * inv[None, :]
    ang_c = (t % GRID_W).astype(np.float32)[:, None] * inv[None, :]
    cos = np.concatenate([np.cos(ang_r)] * 2 + [np.cos(ang_c)] * 2, axis=-1)
    sin = np.concatenate([-np.sin(ang_r), np.sin(ang_r), -np.sin(ang_c), np.sin(ang_c)], axis=-1)
    return cos.astype(np.float32), sin.astype(np.float32)


def _pool_inverse_counts(n_tok):
    t = np.arange(n_tok)
    cols = []
    for w in C_POOLS:
        lo = np.clip(t - w // 2, 0, n_tok)
        hi = np.clip(t - w // 2 + w, 0, n_tok)
        cols.append(np.repeat((np.float32(1.0) / (hi - lo).astype(np.float32))[:, None], C_GROUP_W, axis=1))
    return jnp.asarray(np.concatenate(cols, axis=1), F32)


def _window_mask():
    kk = np.arange(3 * A_BLOCK)[:, None]
    qq = np.arange(2 * A_BLOCK)[None, :] % A_BLOCK
    kinds = [np.where(np.abs(kk - qq - back * A_BLOCK) <= A_WINDOW, 0.0, NEG_INF) for back in (1, 0, 2)]
    return jnp.asarray(np.stack(kinds), F32)


PAIR_COUNT = 2 * (D_WIN_ROWS - 1)
PAIR_INTERIOR = 0
PAIR_EDGE = PAIR_COUNT
PAIR_MASKED = 2 * PAIR_COUNT


def _neighbour_bias_pairs(rpb):
    cidx = np.arange(GRID_W)
    col_start = np.clip(cidx - D_WIN_COLS // 2, 0, GRID_W - D_WIN_COLS)
    col_ok_t = ((cidx[None, :] >= col_start[:, None]) & (cidx[None, :] < col_start[:, None] + D_WIN_COLS)).T
    coff_t = np.clip(cidx[:, None] - cidx[None, :], -(D_WIN_COLS - 1), D_WIN_COLS - 1) + (D_WIN_COLS - 1)
    select = (np.arange(2 * D_WIN_COLS - 1)[:, None, None] == coff_t[None]).astype(np.float32)
    blocks = lax.dot_general(rpb.astype(F32), jnp.asarray(select), (((3,), (0,)), ((), ())),
                             precision=lax.Precision.HIGHEST)
    blocks = jnp.where(col_ok_t, blocks, NEG_INF)
    d = np.arange(-(D_WIN_ROWS - 1), D_WIN_ROWS)
    seen = (d >= -(D_WIN_ROWS // 2)) & (d < D_WIN_ROWS // 2)
    interior = jnp.where(seen[:, None, None], blocks, NEG_INF)
    pair = lambda t: jnp.concatenate([t[:, :, 1:], t[:, :, :-1]], axis=-1)
    masked = jnp.full(blocks.shape[:2] + (1, GRID_W, 2 * GRID_W), NEG_INF, F32)
    return jnp.concatenate([pair(interior), pair(blocks), masked], axis=2)


def kernel(x, c, ctx, c_ctx, w_mod, b_mod, g_mix, g_ffn, w_in, w_out, a_q_gain, a_k_gain, a_sink,
           b_v_gain, b_w_s, b_b_s, c_w_pool, c_scale, d_q_gain, d_k_gain, d_rpb, w_gate, w_up, w_down):
    bsz, s, d = x.shape
    lc = ctx.shape[1]
    n_rows = s // GRID_W
    assert s % MIX_TQ == 0 and n_rows >= 2 * D_WIN_ROWS and MIX_TQ // GRID_W == D_WIN_ROWS // 2

    cvec = jnp.zeros((MOD_ROWS, d), F32).at[:bsz].set(c).at[bsz].set(c_ctx)
    mod = _modulation(cvec, w_mod, b_mod).reshape(DEPTH, MOD_ROWS, 6, 1, d)
    lat_row = lambda bi: bi
    ctx_row = lambda bi: bsz

    seg = jnp.asarray(np.kron(np.eye(4), np.full((64, 64), 1.0 / 64)), BF16)
    cos1, sin1 = _rope_tables(s)
    tables_lat = tuple(jnp.asarray(np.tile(tab, (1, reps)))
                       for tab, reps in ((cos1, 4), (sin1, 4), (cos1, 2), (sin1, 2)))
    tables_ctx = tuple(jnp.zeros((min(IN_TM, bsz * lc), w), F32) for w in (256, 256, 128, 128))

    amask = _window_mask()
    inv_lat, inv_ctx = _pool_inverse_counts(s), _pool_inverse_counts(lc)
    bias_all = _neighbour_bias_pairs(d_rpb * LOG2E)

    w_in_b = w_in.astype(BF16)
    w_out_b = w_out.astype(BF16)
    wg_b, wu_b, wd_b = w_gate.astype(BF16), w_up.astype(BF16), w_down.astype(BF16)
    gmix = g_mix.reshape(DEPTH, 1, d)
    gffn = g_ffn.reshape(DEPTH, 1, d)
    aqg = jnp.tile(a_q_gain, (1, 4)).reshape(DEPTH, 1, 256)
    akg = jnp.tile(a_k_gain, (1, 2)).reshape(DEPTH, 1, 128)
    dqg = jnp.tile(d_q_gain, (1, 4)).reshape(DEPTH, 1, 256)
    dkg = jnp.tile(d_k_gain, (1, 4)).reshape(DEPTH, 1, 256)
    bvg = b_v_gain.reshape(DEPTH, 1, 256)
    ws_cat = jnp.transpose(b_w_s, (0, 2, 1, 3)).reshape(DEPTH, B_CHUNK, B_GROUPS * B_CHUNK).astype(BF16)
    bs_t = jnp.repeat(jnp.swapaxes(b_b_s, 1, 2), B_GROUP_W, axis=2)
    wpool = jnp.einsum('lgcd,gh->lgchd', c_w_pool, jnp.eye(len(C_POOLS), dtype=F32)
                       ).reshape(DEPTH, GROUP_W, GROUP_W).astype(BF16)
    cscale = c_scale.reshape(DEPTH, 1, 256)
    sink = a_sink * LOG2E
    common = (gmix, w_in_b, seg, aqg, akg, dqg, dkg)

    xc = ctx
    for l in range(DEPTH):
        last = l == DEPTH - 1
        lat = _in_projection(x, mod, lat_row, l, *common, tables_lat, bvg, ws_cat, bs_t, use_rope=True)
        con = _in_projection(xc.reshape(1, bsz * lc, d), mod, ctx_row, l, *common, tables_ctx, bvg, ws_cat,
                             bs_t, use_rope=False)
        qa, ka, va, ob, cin, qd, kd, vd = lat
        cqa, cka, cva, cob, ccin, cqd, ckd, cvd = [
            o.reshape((bsz, o.shape[1] // bsz) + o.shape[2:]) for o in con]

        x = _mixer(x, mod, lat_row, l, qa, ka, va, cka, cva, ob, cin, inv_lat, wpool, cscale, qd, kd, vd, ckd, cvd,
                   bias_all, amask, w_out_b, sink, local=True)
        x = _ffn(x, mod, lat_row, l, gffn, wg_b, wu_b, wd_b)
        if not last:
            xc = _mixer(xc, mod, ctx_row, l, cqa, cka, cva, cka, cva, cob, ccin, inv_ctx, wpool, cscale, cqd, ckd,
                        cvd, ckd, cvd, bias_all, amask, w_out_b, sink, local=False)
            xc = _ffn(xc.reshape(1, bsz * lc, d), mod, ctx_row, l, gffn, wg_b, wu_b, wd_b).reshape(bsz, lc, d)
    return x
```

```python
import functools

import jax
import jax.numpy as jnp
import numpy as np
from jax import lax
from jax.experimental import pallas as pl
from jax.experimental.pallas import tpu as pltpu

F32 = jnp.float32
BF16 = jnp.bfloat16

D_MODEL = 1024
DEPTH = 2
GRID_W = 64
HEAD_DIM = 64
GROUP_W = 256
A_HEADS = 4
A_KV_HEADS = 2
A_WINDOW = 128
A_BLOCK = 128
B_CHUNK = 128
B_GROUPS = 4
B_GROUP_W = 64
C_POOLS = (2, 4, 8, 16)
C_GROUP_W = 64
D_HEADS = 4
D_WIN_ROWS = 8
D_WIN_COLS = 16
FF_DIM = 2816
ROPE_BASE = 10000.0
NORM_EPS = 1e-6
NEG_INF = -1e30
LOG2E = 1.4426950408889634
IN_TOTAL = 2048
OFF_AQ, OFF_AK, OFF_AV, OFF_BU, OFF_BV, OFF_C, OFF_DQ, OFF_DK, OFF_DV = (
    0, 256, 384, 512, 768, 1024, 1280, 1536, 1792)

MOD_ROWS = 16
MOD_TN = 1536
IN_TM = 512
MIX_TQ = 256
FFN_TM = 1024
FFN_TF = 256
BF16_SUBLANES = 16
POOL_HALO = 8
MIX_STEP_TILES = 4
OUT_GROUP = 2
OUT_CHUNK = 256
SCORE_LOOKAHEAD = 2
VMEM_LIMIT = 56 * 1024 * 1024


def _dot(a, b):
    return jnp.dot(a, b, preferred_element_type=F32)


def _dot_nt(a, b):
    return lax.dot_general(a, b, (((1,), (1,)), ((), ())), preferred_element_type=F32)


def _mod_kernel(c_ref, w_ref, b_ref, o_ref):
    cv = c_ref[...]
    act = cv * jax.nn.sigmoid(cv)
    o_ref[0] = _dot(act.astype(BF16), w_ref[0].astype(BF16)) + b_ref[0]


def _modulation(cvec, w_mod, b_mod):
    depth, d, n = w_mod.shape
    return pl.pallas_call(
        _mod_kernel,
        out_shape=jax.ShapeDtypeStruct((depth, MOD_ROWS, n), F32),
        grid=(depth, n // MOD_TN),
        in_specs=[
            pl.BlockSpec((MOD_ROWS, d), lambda l, j: (0, 0)),
            pl.BlockSpec((1, d, MOD_TN), lambda l, j: (l, 0, j)),
            pl.BlockSpec((1, 1, MOD_TN), lambda l, j: (l, 0, j)),
        ],
        out_specs=pl.BlockSpec((1, MOD_ROWS, MOD_TN), lambda l, j: (l, 0, j)),
        compiler_params=pltpu.CompilerParams(
            dimension_semantics=("arbitrary", "arbitrary"), vmem_limit_bytes=VMEM_LIMIT),
        name="modulation",
    )(cvec, w_mod, b_mod.reshape(depth, 1, n))


def _head_norm(t, seg, gain):
    sq = t * t
    hi = sq.astype(BF16)
    lo = (sq - hi.astype(F32)).astype(BF16)
    ms = _dot(hi, seg) + _dot(lo, seg)
    return t * lax.rsqrt(ms + NORM_EPS) * gain


def _rope(y, cos, sin_signed):
    w = y.shape[-1]
    up = pltpu.roll(y, w - 16, axis=1)
    dn = pltpu.roll(y, 16, axis=1)
    lane = lax.broadcasted_iota(jnp.int32, y.shape, 1)
    swapped = jnp.where((lane & 31) < 16, up, dn)
    return y * cos + swapped * sin_signed


def _swap_middle_heads(q):
    lane = lax.broadcasted_iota(jnp.int32, q.shape, 1)
    from_right = pltpu.roll(q, 3 * HEAD_DIM, axis=1)
    from_left = pltpu.roll(q, HEAD_DIM, axis=1)
    return jnp.where((lane >= HEAD_DIM) & (lane < 2 * HEAD_DIM), from_right,
                     jnp.where((lane >= 2 * HEAD_DIM) & (lane < 3 * HEAD_DIM), from_left, q))


def _in_kernel(x_ref, sh_ref, sc_ref, g_ref, w_ref, seg_ref, aqg_ref, akg_ref, dqg_ref, dkg_ref,
               cosq_ref, sinq_ref, cosk_ref, sink_ref, bvg_ref, ws_ref, bs_ref,
               qa_ref, ka_ref, va_ref, ob_ref, c_ref, qd_ref, kd_ref, vd_ref, v_scr, *, use_rope):
    xt = x_ref[0]
    ms = jnp.mean(xt * xt, axis=-1, keepdims=True)
    h = xt * lax.rsqrt(ms + NORM_EPS) * g_ref[...]
    h = (h * (1.0 + sc_ref[...]) + sh_ref[...]).astype(BF16)

    p = _dot(h, w_ref[...])

    def proj(off, width):
        return p[:, off:off + width]

    seg = seg_ref[...]
    qa = _head_norm(proj(OFF_AQ, 256), seg, aqg_ref[...])
    ka = _head_norm(proj(OFF_AK, 128), seg[:128, :128], akg_ref[...])
    if use_rope:
        qa = _rope(qa, cosq_ref[...], sinq_ref[...])
        ka = _rope(ka, cosk_ref[...], sink_ref[...])
    scale = HEAD_DIM ** -0.5 * LOG2E
    qa_ref[0] = (_swap_middle_heads(qa) * scale).astype(BF16)
    ka_ref[0] = ka.astype(BF16)
    v_scr[:, 0:128] = proj(OFF_AV, 128)
    v_scr[:, 128:384] = proj(OFF_DV, 256)
    vat = v_scr[:, 0:128].T.astype(BF16)
    for j in range(vat.shape[1] // A_BLOCK):
        va_ref[0, j] = vat[:, j * A_BLOCK:(j + 1) * A_BLOCK]

    qd = _head_norm(proj(OFF_DQ, 256), seg, dqg_ref[...])
    qd_ref[0] = (qd * scale).astype(BF16)
    kd_ref[0] = _head_norm(proj(OFF_DK, 256), seg, dkg_ref[...]).astype(BF16)
    vdt = v_scr[:, 128:384].T.astype(BF16)
    for j in range(vdt.shape[1] // MIX_TQ):
        vd_ref[0, j] = vdt[:, j * MIX_TQ:(j + 1) * MIX_TQ]

    c_ref[0] = proj(OFF_C, 256)

    u = jax.nn.gelu(proj(OFF_BU, 256))
    v = jax.nn.gelu(proj(OFF_BV, 256))
    mu = jnp.mean(v, axis=-1, keepdims=True)
    vc = v - mu
    var = jnp.mean(vc * vc, axis=-1, keepdims=True)
    vn = (vc * lax.rsqrt(var + NORM_EPS) * bvg_ref[...]).astype(BF16)
    lane = lax.broadcasted_iota(jnp.int32, (B_CHUNK, GROUP_W), 1)
    zero = jnp.zeros((B_CHUNK, GROUP_W), BF16)
    tm = xt.shape[0]
    for ch in range(tm // B_CHUNK):
        rows = slice(ch * B_CHUNK, (ch + 1) * B_CHUNK)
        vch = vn[rows]
        stacked = jnp.concatenate(
            [jnp.where((lane >= g * B_GROUP_W) & (lane < (g + 1) * B_GROUP_W), vch, zero)
             for g in range(B_GROUPS)], axis=0)
        z = _dot(ws_ref[...], stacked) + bs_ref[...]
        ob_ref[0, rows, :] = (u[rows] * z).astype(BF16)


def _mod_spec(layer, row, which, batch_axis=0):
    return pl.BlockSpec((None, None, None, 1, D_MODEL),
                        lambda *idx: (layer, row(idx[batch_axis]), which, 0, 0))


def _layer_spec(layer, shape):
    return pl.BlockSpec((None,) + tuple(shape), lambda *_: (layer,) + (0,) * len(shape))


def _in_projection(x, mod, row, layer, g_mix, w_in, seg, aqg, akg, dqg, dkg, tables, bvg, ws_cat, bs_t,
                   *, use_rope):
    b, t, d = x.shape
    cosq, sinq, cosk, sink = tables
    tm = min(IN_TM, t)
    tok = lambda i, bi: (bi, i, 0)
    const2 = lambda i, bi: (0, 0)
    pos2 = (lambda i, bi: (i, 0)) if use_rope else const2
    assert tm % MIX_TQ == 0 and MIX_TQ % A_BLOCK == 0
    out_shapes = (
        jax.ShapeDtypeStruct((b, t, 256), BF16),
        jax.ShapeDtypeStruct((b, t, 128), BF16),
        jax.ShapeDtypeStruct((b, t // A_BLOCK, 128, A_BLOCK), BF16),
        jax.ShapeDtypeStruct((b, t, 256), BF16),
        jax.ShapeDtypeStruct((b, t, 256), F32),
        jax.ShapeDtypeStruct((b, t, 256), BF16),
        jax.ShapeDtypeStruct((b, t, 256), BF16),
        jax.ShapeDtypeStruct((b, t // MIX_TQ, 256, MIX_TQ), BF16),
    )
    blk4 = lambda i, bi: (bi, i, 0, 0)
    out_specs = tuple(
        pl.BlockSpec((1, tm // A_BLOCK, 128, A_BLOCK), blk4) if k == 2 else
        pl.BlockSpec((1, tm // MIX_TQ, 256, MIX_TQ), blk4) if k == 7 else
        pl.BlockSpec((1, tm, s.shape[-1]), tok)
        for k, s in enumerate(out_shapes))
    return pl.pallas_call(
        functools.partial(_in_kernel, use_rope=use_rope),
        out_shape=out_shapes,
        grid=(t // tm, b),
        in_specs=[
            pl.BlockSpec((1, tm, d), tok),
            _mod_spec(layer, row, 0, batch_axis=1),
            _mod_spec(layer, row, 1, batch_axis=1),
            _layer_spec(layer, (1, d)),
            _layer_spec(layer, (d, IN_TOTAL)),
            pl.BlockSpec((256, 256), const2),
            _layer_spec(layer, (1, 256)),
            _layer_spec(layer, (1, 128)),
            _layer_spec(layer, (1, 256)),
            _layer_spec(layer, (1, 256)),
            pl.BlockSpec((tm, 256), pos2),
            pl.BlockSpec((tm, 256), pos2),
            pl.BlockSpec((tm, 128), pos2),
            pl.BlockSpec((tm, 128), pos2),
            _layer_spec(layer, (1, 256)),
            _layer_spec(layer, (B_CHUNK, B_GROUPS * B_CHUNK)),
            _layer_spec(layer, (B_CHUNK, GROUP_W)),
        ],
        out_specs=out_specs,
        scratch_shapes=[pltpu.VMEM((tm, 384), F32)],
        compiler_params=pltpu.CompilerParams(
            dimension_semantics=("arbitrary", "arbitrary"), vmem_limit_bytes=VMEM_LIMIT),
        name="in_projection_rope" if use_rope else "in_projection_ctx",
    )(x, mod, mod, g_mix, w_in, seg, aqg, akg, dqg, dkg, cosq, sinq, cosk, sink, bvg, ws_cat, bs_t)


def _softmax_pv_t(parts, extra_logit):
    m = None
    for s, _ in parts:
        pm = jnp.max(s, axis=0, keepdims=True)
        m = pm if m is None else jnp.maximum(m, pm)
    if extra_logit is not None:
        m = jnp.maximum(m, extra_logit)
    acc = None
    for s, vt in parts:
        dh, n = vt.shape
        vt_ones = jnp.concatenate([vt, jnp.ones((BF16_SUBLANES, n), BF16)], axis=0)
        pv = _dot(vt_ones, jnp.exp2(s - m).astype(BF16))
        acc = pv if acc is None else acc + pv
    denom = acc[dh:dh + 1]
    if extra_logit is not None:
        denom = denom + jnp.exp2(extra_logit - m)
    return acc[:dh] / denom


def _head_lanes(q, head, width=HEAD_DIM):
    lane = lax.broadcasted_iota(jnp.int32, q.shape, 1)
    return jnp.where((lane >= head * width) & (lane < (head + 1) * width), q, jnp.zeros_like(q))


def _pool_tile(c_ref, inv, t0, n_tok, tq):
    y = c_ref[0, pl.ds(t0, tq), :]
    lo_start = pl.multiple_of(jnp.maximum(t0 - POOL_HALO, 0), POOL_HALO)
    hi_start = pl.multiple_of(jnp.minimum(t0 + tq, n_tok - POOL_HALO), POOL_HALO)
    lo = c_ref[0, pl.ds(lo_start, POOL_HALO), :]
    hi = c_ref[0, pl.ds(hi_start, POOL_HALO), :]
    lo = jnp.where(t0 > 0, lo, 0.0)
    hi = jnp.where(t0 + tq < n_tok, hi, 0.0)
    ypad = jnp.concatenate([lo, y, hi], axis=0)

    n = tq + 2 * POOL_HALO
    w2 = ypad + pltpu.roll(ypad, 1, axis=0)
    w4 = pltpu.roll(w2, 1, axis=0) + pltpu.roll(w2, n - 1, axis=0)
    w8 = pltpu.roll(w4, 2, axis=0) + pltpu.roll(w4, n - 2, axis=0)
    w16 = pltpu.roll(w8, 4, axis=0) + pltpu.roll(w8, n - 4, axis=0)
    sums = tuple(w[POOL_HALO:POOL_HALO + tq] for w in (w2, w4, w8, w16))

    lane = lax.broadcasted_iota(jnp.int32, (tq, GROUP_W), 1)
    total = sums[0]
    for gi in range(1, len(C_POOLS)):
        total = jnp.where(lane >= gi * C_GROUP_W, sums[gi], total)
    return total * inv - y


def _mix_kernel(*refs, local, n_tok, layer):
    x_ref = refs[1]
    deferred = []
    for sub in range(x_ref.shape[1] // MIX_TQ):
        _mix_subtile(*refs, sub=sub, deferred=deferred, local=local, n_tok=n_tok, layer=layer)
    while deferred:
        deferred.pop(0)()


def _mix_subtile(sink_ref, x_ref, g1_ref, qa_ref, ka_ref, va_ref, kca_ref, vca_ref, ob_ref, c_ref,
                 inv_ref, wp_ref, cs_ref, qd_ref, kd_ref, vd_ref, kcd_ref, vcd_ref, bias_ref, amask_ref, wo_ref,
                 o_ref, oa_scr, od_scr, mix_scr, *, sub, deferred, local, n_tok, layer):
    tq = MIX_TQ
    ti = pl.program_id(1) * (x_ref.shape[1] // tq) + sub
    t0 = pl.multiple_of(ti * tq, tq)
    tile = slice(sub * tq, (sub + 1) * tq)

    pooled = _pool_tile(c_ref, inv_ref[tile, :], t0, n_tok, tq)
    out_c = (_dot(pooled.astype(BF16), wp_ref[...]) * cs_ref[...]).astype(BF16)

    if local:
        n_tiles = n_tok // tq
        half = D_WIN_ROWS // 2
        pair_entries = []
        for jj in range(3 * tq // GRID_W):
            row_entries = []
            for i in range(0, tq // GRID_W, 2):
                interior = PAIR_INTERIOR + (jj - half - i) + PAIR_COUNT // 2 - 1
                first = PAIR_EDGE + (jj - i) + PAIR_COUNT // 2 - 1 if jj < D_WIN_ROWS else PAIR_MASKED
                last = (PAIR_EDGE + (jj - D_WIN_ROWS - i) + PAIR_COUNT // 2 - 1 if jj >= half
                        else PAIR_MASKED)
                row_entries.append(jnp.where(ti == 0, first, jnp.where(ti == n_tiles - 1, last, interior)))
            pair_entries.append(row_entries)

    n_ctx_blk = vca_ref.shape[1]
    jobs = []

    def a_job(blk):
        cols = slice(blk * A_BLOCK, (blk + 1) * A_BLOCK)

        def scores():
            qblk = qa_ref[0, sub * tq + blk * A_BLOCK:sub * tq + (blk + 1) * A_BLOCK, :]
            qcat = jnp.concatenate([qblk[:, :128], qblk[:, 128:]], axis=0)
            qm = jnp.concatenate([_head_lanes(qcat, kv) for kv in range(A_KV_HEADS)], axis=0)
            if local:
                nblk = ti * (tq // A_BLOCK) + blk
                b0 = jnp.clip(nblk - 1, 0, n_tok // A_BLOCK - 3)
                kstart = pl.multiple_of(b0 * A_BLOCK, A_BLOCK)
                kwin = ka_ref[0, pl.ds(kstart, 3 * A_BLOCK), :]
                last_blk = n_tok // A_BLOCK - 1
                kind = jnp.where(nblk == 0, 1, jnp.where(nblk == last_blk, 2, 0))
                s_loc = _dot_nt(kwin, qm)
            s_ctx = _dot_nt(kca_ref[0], qm)
            per_head = []
            for kv in range(A_KV_HEADS):
                rows = slice(kv * HEAD_DIM, (kv + 1) * HEAD_DIM)
                qsl = slice(kv * 2 * A_BLOCK, (kv + 1) * 2 * A_BLOCK)
                parts = []
                if local:
                    s_win = s_loc[:, qsl] + amask_ref[kind]
                    for j in range(3):
                        parts.append((s_win[j * A_BLOCK:(j + 1) * A_BLOCK], va_ref[0, b0 + j, rows, :]))
                for j in range(n_ctx_blk):
                    parts.append((s_ctx[j * A_BLOCK:(j + 1) * A_BLOCK, qsl], vca_ref[0, j, rows, :]))
                per_head.append(parts)
            return per_head

        def finish(per_head):
            for kv, parts in enumerate(per_head):
                col = lax.broadcasted_iota(jnp.int32, (1, 2 * A_BLOCK), 1)
                sink = jnp.where(col < A_BLOCK, sink_ref[layer, 2 * kv], sink_ref[layer, 2 * kv + 1])
                o2 = _softmax_pv_t(parts, sink)
                oa_scr[sub, (2 * kv) * 64:(2 * kv + 1) * 64, cols] = o2[:, :A_BLOCK]
                oa_scr[sub, (2 * kv + 1) * 64:(2 * kv + 2) * 64, cols] = o2[:, A_BLOCK:]

        return scores, finish

    def d_job(pair):
        heads = (2 * pair, 2 * pair + 1)

        def scores():
            qd = qd_ref[0, tile, :]
            qm = jnp.concatenate([_head_lanes(qd, hd) for hd in heads], axis=0)
            if local:
                d0 = jnp.clip(ti - 1, 0, n_tok // tq - 3)
                kwin = kd_ref[0, pl.ds(pl.multiple_of(d0 * tq, tq), 3 * tq), :]
                s_loc = _dot_nt(kwin, qm)
            s_ctx = _dot_nt(kcd_ref[0], qm)
            per_head = []
            for k, hd in enumerate(heads):
                rows = slice(hd * HEAD_DIM, (hd + 1) * HEAD_DIM)
                qsl = slice(k * tq, (k + 1) * tq)
                parts = []
                if local:
                    bias = jnp.concatenate(
                        [jnp.concatenate([bias_ref[hd, e] for e in row_entries], axis=1)
                         for row_entries in pair_entries], axis=0)
                    s_win = s_loc[:, qsl] + bias
                    for j in range(3):
                        parts.append((s_win[j * tq:(j + 1) * tq], vd_ref[0, d0 + j, rows, :]))
                parts.append((s_ctx[:, qsl], vcd_ref[0, 0, rows, :]))
                per_head.append(parts)
            return per_head

        def finish(per_head):
            for hd, parts in zip(heads, per_head):
                od_scr[sub, hd * HEAD_DIM:(hd + 1) * HEAD_DIM, :] = _softmax_pv_t(parts, None)

        return scores, finish

    for blk in range(tq // A_BLOCK):
        jobs.append(a_job(blk))
    for pair in range(D_HEADS // 2):
        jobs.append(d_job(pair))
    ahead = min(SCORE_LOOKAHEAD, len(jobs))
    pending = [jobs[j][0]() for j in range(ahead)]
    for j, (_, finish) in enumerate(jobs):
        if j + ahead < len(jobs):
            pending.append(jobs[j + ahead][0]())
        if deferred:
            deferred.pop(0)()
        finish(pending.pop(0))

    mix_scr[tile, 0:256] = oa_scr[sub].T.astype(BF16)
    mix_scr[tile, 256:512] = ob_ref[0, tile, :]
    mix_scr[tile, 512:768] = out_c
    mix_scr[tile, 768:1024] = od_scr[sub].T.astype(BF16)

    n_sub = x_ref.shape[1] // tq
    group = min(OUT_GROUP, n_sub)
    if (sub + 1) % group:
        return
    first = sub + 1 - group
    rows = slice(first * tq, (sub + 1) * tq)

    def out_chunk(c):
        cols = slice(c * OUT_CHUNK, (c + 1) * OUT_CHUNK)

        def run():
            mix = _dot(mix_scr[rows, :], wo_ref[:, cols])
            o_ref[0, rows, cols] = x_ref[0, rows, cols] + g1_ref[:, cols] * mix

        return run

    deferred.extend(out_chunk(c) for c in range(D_MODEL // OUT_CHUNK))


def _mixer(x, mod, row, layer, qa, ka, va, kca, vca, ob, cin, inv_cnt, wpool, cscale, qd, kd, vd, kcd, vcd, bias,
           amask, w_out, sink, *, local):
    b, t, d = x.shape
    lc = kca.shape[1]
    tq = MIX_TQ
    ts = min(MIX_STEP_TILES * tq, t)
    n_sub = ts // tq
    tok = lambda bi, i: (bi, i, 0)
    per_b = lambda bi, i: (bi, 0, 0)
    per_b4 = lambda bi, i: (bi, 0, 0, 0)

    return pl.pallas_call(
        functools.partial(_mix_kernel, local=local, n_tok=t, layer=layer),
        out_shape=jax.ShapeDtypeStruct((b, t, d), F32),
        grid=(b, t // ts),
        in_specs=[
            pl.BlockSpec(memory_space=pltpu.SMEM),
            pl.BlockSpec((1, ts, d), tok),
            _mod_spec(layer, row, 2),
            pl.BlockSpec((1, ts, 256), tok),
            pl.BlockSpec((1, t, 128), per_b),
            pl.BlockSpec((1, t // A_BLOCK, 128, A_BLOCK), per_b4),
            pl.BlockSpec((1, lc, 128), per_b),
            pl.BlockSpec((1, lc // A_BLOCK, 128, A_BLOCK), per_b4),
            pl.BlockSpec((1, ts, 256), tok),
            pl.BlockSpec((1, t, 256), per_b),
            pl.BlockSpec((ts, 256), lambda bi, i: (i, 0)),
            _layer_spec(layer, (256, 256)),
            _layer_spec(layer, (1, 256)),
            pl.BlockSpec((1, ts, 256), tok),
            pl.BlockSpec((1, t, 256), per_b),
            pl.BlockSpec((1, t // tq, 256, tq), per_b4),
            pl.BlockSpec((1, lc, 256), per_b),
            pl.BlockSpec((1, lc // tq, 256, tq), per_b4),
            _layer_spec(layer, (D_HEADS, PAIR_MASKED + 1, GRID_W, 2 * GRID_W)),
            pl.BlockSpec((3, 3 * A_BLOCK, 2 * A_BLOCK), lambda bi, i: (0, 0, 0)),
            _layer_spec(layer, (d, d)),
        ],
        out_specs=pl.BlockSpec((1, ts, d), tok),
        scratch_shapes=[pltpu.VMEM((n_sub, GROUP_W, tq), F32), pltpu.VMEM((n_sub, GROUP_W, tq), F32),
                        pltpu.VMEM((ts, d), BF16)],
        compiler_params=pltpu.CompilerParams(
            dimension_semantics=("arbitrary", "arbitrary"), vmem_limit_bytes=VMEM_LIMIT),
        name="mixer_latent" if local else "mixer_ctx",
    )(sink, x, mod, qa, ka, va, kca, vca, ob, cin, inv_cnt, wpool, cscale, qd, kd, vd, kcd, vcd, bias, amask,
      w_out)


def _ffn_kernel(x_ref, sh_ref, sc_ref, g2_ref, gn_ref, wg_ref, wu_ref, wd_ref, o_ref, act_scr):
    xt = x_ref[0]
    ms = jnp.mean(xt * xt, axis=-1, keepdims=True)
    h = xt * lax.rsqrt(ms + NORM_EPS) * gn_ref[...]
    h = (h * (1.0 + sc_ref[...]) + sh_ref[...]).astype(BF16)
    for c in range(FF_DIM // FFN_TF):
        cols = slice(c * FFN_TF, (c + 1) * FFN_TF)
        gate = _dot(h, wg_ref[:, cols])
        up = _dot(h, wu_ref[:, cols])
        act_scr[:, cols] = (gate * jax.nn.sigmoid(gate) * up).astype(BF16)
    o_ref[0] = xt + g2_ref[...] * _dot(act_scr[...], wd_ref[...])


def _ffn(x, mod, row, layer, g_ffn, wg, wu, wd):
    b, t, d = x.shape
    tm = min(FFN_TM, t)
    tok = lambda bi, i: (bi, i, 0)
    per_layer = lambda bi, i: (layer, 0, 0)
    resident = pl.Buffered(1)
    return pl.pallas_call(
        _ffn_kernel,
        out_shape=jax.ShapeDtypeStruct((b, t, d), F32),
        grid=(b, t // tm),
        in_specs=[
            pl.BlockSpec((1, tm, d), tok),
            _mod_spec(layer, row, 3),
            _mod_spec(layer, row, 4),
            _mod_spec(layer, row, 5),
            _layer_spec(layer, (1, d)),
            pl.BlockSpec((None, d, FF_DIM), per_layer, pipeline_mode=resident),
            pl.BlockSpec((None, d, FF_DIM), per_layer, pipeline_mode=resident),
            pl.BlockSpec((None, FF_DIM, d), per_layer, pipeline_mode=resident),
        ],
        out_specs=pl.BlockSpec((1, tm, d), tok),
        scratch_shapes=[pltpu.VMEM((tm, FF_DIM), BF16)],
        compiler_params=pltpu.CompilerParams(
            dimension_semantics=("arbitrary", "arbitrary"), vmem_limit_bytes=VMEM_LIMIT),
        name="ffn",
    )(x, mod, mod, mod, g_ffn, wg, wu, wd)


def _rope_tables(s):
    t = np.arange(s)
    half = 16
    inv = np.power(np.float32(ROPE_BASE), -np.arange(half, dtype=np.float32) / half).astype(np.float32)
    ang_r = (t // GRID_W).astype(np.float32)[:, None] * inv[None, :]
    ang_c = (t % GRID_W).astype(np.float32)[:, None] * inv[None, :]
    cos = np.concatenate([np.cos(ang_r)] * 2 + [np.cos(ang_c)] * 2, axis=-1)
    sin = np.concatenate([-np.sin(ang_r), np.sin(ang_r), -np.sin(ang_c), np.sin(ang_c)], axis=-1)
    return cos.astype(np.float32), sin.astype(np.float32)


def _pool_inverse_counts(n_tok):
    t = np.arange(n_tok)
    cols = []
    for w in C_POOLS:
        lo = np.clip(t - w // 2, 0, n_tok)
        hi = np.clip(t - w // 2 + w, 0, n_tok)
        cols.append(np.repeat((np.float32(1.0) / (hi - lo).astype(np.float32))[:, None], C_GROUP_W, axis=1))
    return jnp.asarray(np.concatenate(cols, axis=1), F32)


def _window_mask():
    kk = np.arange(3 * A_BLOCK)[:, None]
    qq = np.arange(2 * A_BLOCK)[None, :] % A_BLOCK
    kinds = [np.where(np.abs(kk - qq - back * A_BLOCK) <= A_WINDOW, 0.0, NEG_INF) for back in (1, 0, 2)]
    return jnp.asarray(np.stack(kinds), F32)


PAIR_COUNT = 2 * (D_WIN_ROWS - 1)
PAIR_INTERIOR = 0
PAIR_EDGE = PAIR_COUNT
PAIR_MASKED = 2 * PAIR_COUNT


def _neighbour_bias_pairs(rpb):
    cidx = np.arange(GRID_W)
    col_start = np.clip(cidx - D_WIN_COLS // 2, 0, GRID_W - D_WIN_COLS)
    col_ok_t = ((cidx[None, :] >= col_start[:, None]) & (cidx[None, :] < col_start[:, None] + D_WIN_COLS)).T
    coff_t = np.clip(cidx[:, None] - cidx[None, :], -(D_WIN_COLS - 1), D_WIN_COLS - 1) + (D_WIN_COLS - 1)
    select = (np.arange(2 * D_WIN_COLS - 1)[:, None, None] == coff_t[None]).astype(np.float32)
    blocks = lax.dot_general(rpb.astype(F32), jnp.asarray(select), (((3,), (0,)), ((), ())),
                             precision=lax.Precision.HIGHEST)
    blocks = jnp.where(col_ok_t, blocks, NEG_INF)
    d = np.arange(-(D_WIN_ROWS - 1), D_WIN_ROWS)
    seen = (d >= -(D_WIN_ROWS // 2)) & (d < D_WIN_ROWS // 2)
    interior = jnp.where(seen[:, None, None], blocks, NEG_INF)
    pair = lambda t: jnp.concatenate([t[:, :, 1:], t[:, :, :-1]], axis=-1)
    masked = jnp.full(blocks.shape[:2] + (1, GRID_W, 2 * GRID_W), NEG_INF, F32)
    return jnp.concatenate([pair(interior), pair(blocks), masked], axis=2)


def kernel(x, c, ctx, c_ctx, w_mod, b_mod, g_mix, g_ffn, w_in, w_out, a_q_gain, a_k_gain, a_sink,
           b_v_gain, b_w_s, b_b_s, c_w_pool, c_scale, d_q_gain, d_k_gain, d_rpb, w_gate, w_up, w_down):
    bsz, s, d = x.shape
    lc = ctx.shape[1]
    n_rows = s // GRID_W
    assert s % MIX_TQ == 0 and n_rows >= 2 * D_WIN_ROWS and MIX_TQ // GRID_W == D_WIN_ROWS // 2

    cvec = jnp.zeros((MOD_ROWS, d), F32).at[:bsz].set(c).at[bsz].set(c_ctx)
    mod = _modulation(cvec, w_mod, b_mod).reshape(DEPTH, MOD_ROWS, 6, 1, d)
    lat_row = lambda bi: bi
    ctx_row = lambda bi: bsz

    seg = jnp.asarray(np.kron(np.eye(4), np.full((64, 64), 1.0 / 64)), BF16)
    cos1, sin1 = _rope_tables(s)
    tables_lat = tuple(jnp.asarray(np.tile(tab, (1, reps)))
                       for tab, reps in ((cos1, 4), (sin1, 4), (cos1, 2), (sin1, 2)))
    tables_ctx = tuple(jnp.zeros((min(IN_TM, bsz * lc), w), F32) for w in (256, 256, 128, 128))

    amask = _window_mask()
    inv_lat, inv_ctx = _pool_inverse_counts(s), _pool_inverse_counts(lc)
    bias_all = _neighbour_bias_pairs(d_rpb * LOG2E)

    w_in_b = w_in.astype(BF16)
    w_out_b = w_out.astype(BF16)
    wg_b, wu_b, wd_b = w_gate.astype(BF16), w_up.astype(BF16), w_down.astype(BF16)
    gmix = g_mix.reshape(DEPTH, 1, d)
    gffn = g_ffn.reshape(DEPTH, 1, d)
    aqg = jnp.tile(a_q_gain, (1, 4)).reshape(DEPTH, 1, 256)
    akg = jnp.tile(a_k_gain, (1, 2)).reshape(DEPTH, 1, 128)
    dqg = jnp.tile(d_q_gain, (1, 4)).reshape(DEPTH, 1, 256)
    dkg = jnp.tile(d_k_gain, (1, 4)).reshape(DEPTH, 1, 256)
    bvg = b_v_gain.reshape(DEPTH, 1, 256)
    ws_cat = jnp.transpose(b_w_s, (0, 2, 1, 3)).reshape(DEPTH, B_CHUNK, B_GROUPS * B_CHUNK).astype(BF16)
    bs_t = jnp.repeat(jnp.swapaxes(b_b_s, 1, 2), B_GROUP_W, axis=2)
    wpool = jnp.einsum('lgcd,gh->lgchd', c_w_pool, jnp.eye(len(C_POOLS), dtype=F32)
                       ).reshape(DEPTH, GROUP_W, GROUP_W).astype(BF16)
    cscale = c_scale.reshape(DEPTH, 1, 256)
    sink = a_sink * LOG2E
    common = (gmix, w_in_b, seg, aqg, akg, dqg, dkg)

    xc = ctx
    for l in range(DEPTH):
        last = l == DEPTH - 1
        lat = _in_projection(x, mod, lat_row, l, *common, tables_lat, bvg, ws_cat, bs_t, use_rope=True)
        con = _in_projection(xc.reshape(1, bsz * lc, d), mod, ctx_row, l, *common, tables_ctx, bvg, ws_cat,
                             bs_t, use_rope=False)
        qa, ka, va, ob, cin, qd, kd, vd = lat
        cqa, cka, cva, cob, ccin, cqd, ckd, cvd = [
            o.reshape((bsz, o.shape[1] // bsz) + o.shape[2:]) for o in con]

        x = _mixer(x, mod, lat_row, l, qa, ka, va, cka, cva, ob, cin, inv_lat, wpool, cscale, qd, kd, vd, ckd, cvd,
                   bias_all, amask, w_out_b, sink, local=True)
        x = _ffn(x, mod, lat_row, l, gffn, wg_b, wu_b, wd_b)
        if not last:
            xc = _mixer(xc, mod, ctx_row, l, cqa, cka, cva, cka, cva, cob, ccin, inv_ctx, wpool, cscale, cqd, ckd,
                        cvd, ckd, cvd, bias_all, amask, w_out_b, sink, local=False)
            xc = _ffn(xc.reshape(1, bsz * lc, d), mod, ctx_row, l, gffn, wg_b, wu_b, wd_b).reshape(bsz, lc, d)
    return x
```

```python
import functools

import jax
import jax.numpy as jnp
import numpy as np
from jax import lax
from jax.experimental import pallas as pl
from jax.experimental.pallas import tpu as pltpu

F32 = jnp.float32
BF16 = jnp.bfloat16

D_MODEL = 1024
DEPTH = 2
GRID_W = 64
HEAD_DIM = 64
GROUP_W = 256
A_HEADS = 4
A_KV_HEADS = 2
A_WINDOW = 128
A_BLOCK = 128
B_CHUNK = 128
B_GROUPS = 4
B_GROUP_W = 64
C_POOLS = (2, 4, 8, 16)
C_GROUP_W = 64
D_HEADS = 4
D_WIN_ROWS = 8
D_WIN_COLS = 16
FF_DIM = 2816
ROPE_BASE = 10000.0
NORM_EPS = 1e-6
NEG_INF = -1e30
LOG2E = 1.4426950408889634
IN_TOTAL = 2048
OFF_AQ, OFF_AK, OFF_AV, OFF_BU, OFF_BV, OFF_C, OFF_DQ, OFF_DK, OFF_DV = (
    0, 256, 384, 512, 768, 1024, 1280, 1536, 1792)

MOD_ROWS = 16
MOD_TN = 1536
IN_TM = 512
MIX_TQ = 256
FFN_TM = 1024
FFN_TF = 256
BF16_SUBLANES = 16
POOL_HALO = 8
MIX_STEP_TILES = 4
OUT_CHUNK = 256
SCORE_LOOKAHEAD = 2
VMEM_LIMIT = 56 * 1024 * 1024


def _dot(a, b):
    return jnp.dot(a, b, preferred_element_type=F32)


def _dot_nt(a, b):
    return lax.dot_general(a, b, (((1,), (1,)), ((), ())), preferred_element_type=F32)


def _mod_kernel(c_ref, w_ref, b_ref, o_ref):
    cv = c_ref[...]
    act = cv * jax.nn.sigmoid(cv)
    o_ref[0] = _dot(act.astype(BF16), w_ref[0].astype(BF16)) + b_ref[0]


def _modulation(cvec, w_mod, b_mod):
    depth, d, n = w_mod.shape
    return pl.pallas_call(
        _mod_kernel,
        out_shape=jax.ShapeDtypeStruct((depth, MOD_ROWS, n), F32),
        grid=(depth, n // MOD_TN),
        in_specs=[
            pl.BlockSpec((MOD_ROWS, d), lambda l, j: (0, 0)),
            pl.BlockSpec((1, d, MOD_TN), lambda l, j: (l, 0, j)),
            pl.BlockSpec((1, 1, MOD_TN), lambda l, j: (l, 0, j)),
        ],
        out_specs=pl.BlockSpec((1, MOD_ROWS, MOD_TN), lambda l, j: (l, 0, j)),
        compiler_params=pltpu.CompilerParams(
            dimension_semantics=("arbitrary", "arbitrary"), vmem_limit_bytes=VMEM_LIMIT),
        name="modulation",
    )(cvec, w_mod, b_mod.reshape(depth, 1, n))


def _head_norm(t, seg, gain):
    sq = t * t
    hi = sq.astype(BF16)
    lo = (sq - hi.astype(F32)).astype(BF16)
    ms = _dot(hi, seg) + _dot(lo, seg)
    return t * lax.rsqrt(ms + NORM_EPS) * gain


def _rope(y, cos, sin_signed):
    w = y.shape[-1]
    up = pltpu.roll(y, w - 16, axis=1)
    dn = pltpu.roll(y, 16, axis=1)
    lane = lax.broadcasted_iota(jnp.int32, y.shape, 1)
    swapped = jnp.where((lane & 31) < 16, up, dn)
    return y * cos + swapped * sin_signed


def _swap_middle_heads(q):
    lane = lax.broadcasted_iota(jnp.int32, q.shape, 1)
    from_right = pltpu.roll(q, 3 * HEAD_DIM, axis=1)
    from_left = pltpu.roll(q, HEAD_DIM, axis=1)
    return jnp.where((lane >= HEAD_DIM) & (lane < 2 * HEAD_DIM), from_right,
                     jnp.where((lane >= 2 * HEAD_DIM) & (lane < 3 * HEAD_DIM), from_left, q))


def _in_kernel(x_ref, sh_ref, sc_ref, g_ref, w_ref, seg_ref, aqg_ref, akg_ref, dqg_ref, dkg_ref,
               cosq_ref, sinq_ref, cosk_ref, sink_ref, bvg_ref, ws_ref, bs_ref,
               qa_ref, ka_ref, va_ref, ob_ref, c_ref, qd_ref, kd_ref, vd_ref, v_scr, *, use_rope):
    xt = x_ref[0]
    ms = jnp.mean(xt * xt, axis=-1, keepdims=True)
    h = xt * lax.rsqrt(ms + NORM_EPS) * g_ref[...]
    h = (h * (1.0 + sc_ref[...]) + sh_ref[...]).astype(BF16)

    group_w = IN_TOTAL // 4
    assert (OFF_BU, OFF_C, OFF_DK) == (group_w, 2 * group_w, 3 * group_w)

    def group(k):
        return _dot(h, w_ref[:, k * group_w:(k + 1) * group_w])

    seg = seg_ref[...]
    scale = HEAD_DIM ** -0.5 * LOG2E
    tm = xt.shape[0]

    def finish_a(p):
        qa = _head_norm(p[:, 0:256], seg, aqg_ref[...])
        ka = _head_norm(p[:, 256:384], seg[:128, :128], akg_ref[...])
        if use_rope:
            qa = _rope(qa, cosq_ref[...], sinq_ref[...])
            ka = _rope(ka, cosk_ref[...], sink_ref[...])
        qa_ref[0] = (_swap_middle_heads(qa) * scale).astype(BF16)
        ka_ref[0] = ka.astype(BF16)
        v_scr[:, 0:128] = p[:, 384:512]
        vat = v_scr[:, 0:128].T.astype(BF16)
        for j in range(tm // A_BLOCK):
            va_ref[0, j] = vat[:, j * A_BLOCK:(j + 1) * A_BLOCK]

    def finish_b(p):
        u = jax.nn.gelu(p[:, 0:256])
        v = jax.nn.gelu(p[:, 256:512])
        mu = jnp.mean(v, axis=-1, keepdims=True)
        vc = v - mu
        var = jnp.mean(vc * vc, axis=-1, keepdims=True)
        vn = (vc * lax.rsqrt(var + NORM_EPS) * bvg_ref[...]).astype(BF16)
        lane = lax.broadcasted_iota(jnp.int32, (B_CHUNK, GROUP_W), 1)
        zero = jnp.zeros((B_CHUNK, GROUP_W), BF16)
        for ch in range(tm // B_CHUNK):
            rows = slice(ch * B_CHUNK, (ch + 1) * B_CHUNK)
            vch = vn[rows]
            stacked = jnp.concatenate(
                [jnp.where((lane >= g * B_GROUP_W) & (lane < (g + 1) * B_GROUP_W), vch, zero)
                 for g in range(B_GROUPS)], axis=0)
            z = _dot(ws_ref[...], stacked) + bs_ref[...]
            ob_ref[0, rows, :] = (u[rows] * z).astype(BF16)

    def finish_c(p):
        c_ref[0] = p[:, 0:256]
        qd = _head_norm(p[:, 256:512], seg, dqg_ref[...])
        qd_ref[0] = (qd * scale).astype(BF16)

    def finish_d(p):
        kd_ref[0] = _head_norm(p[:, 0:256], seg, dkg_ref[...]).astype(BF16)
        v_scr[:, 128:384] = p[:, 256:512]
        vdt = v_scr[:, 128:384].T.astype(BF16)
        for j in range(tm // MIX_TQ):
            vd_ref[0, j] = vdt[:, j * MIX_TQ:(j + 1) * MIX_TQ]

    order = ((1, finish_b), (0, finish_a), (3, finish_d), (2, finish_c))
    pending = group(order[0][0])
    for pos, (_, finish) in enumerate(order):
        upcoming = group(order[pos + 1][0]) if pos + 1 < len(order) else None
        finish(pending)
        pending = upcoming


def _mod_spec(layer, row, which, batch_axis=0):
    return pl.BlockSpec((None, None, None, 1, D_MODEL),
                        lambda *idx: (layer, row(idx[batch_axis]), which, 0, 0))


def _layer_spec(layer, shape):
    return pl.BlockSpec((None,) + tuple(shape), lambda *_: (layer,) + (0,) * len(shape))


def _in_projection(x, mod, row, layer, g_mix, w_in, seg, aqg, akg, dqg, dkg, tables, bvg, ws_cat, bs_t,
                   *, use_rope):
    b, t, d = x.shape
    cosq, sinq, cosk, sink = tables
    tm = min(IN_TM, t)
    tok = lambda i, bi: (bi, i, 0)
    const2 = lambda i, bi: (0, 0)
    pos2 = (lambda i, bi: (i, 0)) if use_rope else const2
    assert tm % MIX_TQ == 0 and MIX_TQ % A_BLOCK == 0
    out_shapes = (
        jax.ShapeDtypeStruct((b, t, 256), BF16),
        jax.ShapeDtypeStruct((b, t, 128), BF16),
        jax.ShapeDtypeStruct((b, t // A_BLOCK, 128, A_BLOCK), BF16),
        jax.ShapeDtypeStruct((b, t, 256), BF16),
        jax.ShapeDtypeStruct((b, t, 256), F32),
        jax.ShapeDtypeStruct((b, t, 256), BF16),
        jax.ShapeDtypeStruct((b, t, 256), BF16),
        jax.ShapeDtypeStruct((b, t // MIX_TQ, 256, MIX_TQ), BF16),
    )
    blk4 = lambda i, bi: (bi, i, 0, 0)
    out_specs = tuple(
        pl.BlockSpec((1, tm // A_BLOCK, 128, A_BLOCK), blk4) if k == 2 else
        pl.BlockSpec((1, tm // MIX_TQ, 256, MIX_TQ), blk4) if k == 7 else
        pl.BlockSpec((1, tm, s.shape[-1]), tok)
        for k, s in enumerate(out_shapes))
    return pl.pallas_call(
        functools.partial(_in_kernel, use_rope=use_rope),
        out_shape=out_shapes,
        grid=(t // tm, b),
        in_specs=[
            pl.BlockSpec((1, tm, d), tok),
            _mod_spec(layer, row, 0, batch_axis=1),
            _mod_spec(layer, row, 1, batch_axis=1),
            _layer_spec(layer, (1, d)),
            _layer_spec(layer, (d, IN_TOTAL)),
            pl.BlockSpec((256, 256), const2),
            _layer_spec(layer, (1, 256)),
            _layer_spec(layer, (1, 128)),
            _layer_spec(layer, (1, 256)),
            _layer_spec(layer, (1, 256)),
            pl.BlockSpec((tm, 256), pos2),
            pl.BlockSpec((tm, 256), pos2),
            pl.BlockSpec((tm, 128), pos2),
            pl.BlockSpec((tm, 128), pos2),
            _layer_spec(layer, (1, 256)),
            _layer_spec(layer, (B_CHUNK, B_GROUPS * B_CHUNK)),
            _layer_spec(layer, (B_CHUNK, GROUP_W)),
        ],
        out_specs=out_specs,
        scratch_shapes=[pltpu.VMEM((tm, 384), F32)],
        compiler_params=pltpu.CompilerParams(
            dimension_semantics=("arbitrary", "arbitrary"), vmem_limit_bytes=VMEM_LIMIT),
        name="in_projection_rope" if use_rope else "in_projection_ctx",
    )(x, mod, mod, g_mix, w_in, seg, aqg, akg, dqg, dkg, cosq, sinq, cosk, sink, bvg, ws_cat, bs_t)


def _softmax_pv_t(parts, extra_logit):
    m = None
    for s, _ in parts:
        pm = jnp.max(s, axis=0, keepdims=True)
        m = pm if m is None else jnp.maximum(m, pm)
    if extra_logit is not None:
        m = jnp.maximum(m, extra_logit)
    acc = None
    for s, vt in parts:
        dh, n = vt.shape
        vt_ones = jnp.concatenate([vt, jnp.ones((BF16_SUBLANES, n), BF16)], axis=0)
        pv = _dot(vt_ones, jnp.exp2(s - m).astype(BF16))
        acc = pv if acc is None else acc + pv
    denom = acc[dh:dh + 1]
    if extra_logit is not None:
        denom = denom + jnp.exp2(extra_logit - m)
    return acc[:dh] / denom


def _head_lanes(q, head, width=HEAD_DIM):
    lane = lax.broadcasted_iota(jnp.int32, q.shape, 1)
    return jnp.where((lane >= head * width) & (lane < (head + 1) * width), q, jnp.zeros_like(q))


def _pool_tile(c_ref, inv, t0, n_tok, tq):
    y = c_ref[0, pl.ds(t0, tq), :]
    lo_start = pl.multiple_of(jnp.maximum(t0 - POOL_HALO, 0), POOL_HALO)
    hi_start = pl.multiple_of(jnp.minimum(t0 + tq, n_tok - POOL_HALO), POOL_HALO)
    lo = c_ref[0, pl.ds(lo_start, POOL_HALO), :]
    hi = c_ref[0, pl.ds(hi_start, POOL_HALO), :]
    lo = jnp.where(t0 > 0, lo, 0.0)
    hi = jnp.where(t0 + tq < n_tok, hi, 0.0)
    ypad = jnp.concatenate([lo, y, hi], axis=0)

    n = tq + 2 * POOL_HALO
    w2 = ypad + pltpu.roll(ypad, 1, axis=0)
    w4 = pltpu.roll(w2, 1, axis=0) + pltpu.roll(w2, n - 1, axis=0)
    w8 = pltpu.roll(w4, 2, axis=0) + pltpu.roll(w4, n - 2, axis=0)
    w16 = pltpu.roll(w8, 4, axis=0) + pltpu.roll(w8, n - 4, axis=0)
    sums = tuple(w[POOL_HALO:POOL_HALO + tq] for w in (w2, w4, w8, w16))

    lane = lax.broadcasted_iota(jnp.int32, (tq, GROUP_W), 1)
    total = sums[0]
    for gi in range(1, len(C_POOLS)):
        total = jnp.where(lane >= gi * C_GROUP_W, sums[gi], total)
    return total * inv - y


def _mix_kernel(*refs, local, n_tok, layer):
    x_ref = refs[1]
    deferred = []
    for sub in range(x_ref.shape[1] // MIX_TQ):
        _mix_subtile(*refs, sub=sub, deferred=deferred, local=local, n_tok=n_tok, layer=layer)
    while deferred:
        deferred.pop(0)()


def _mix_subtile(sink_ref, x_ref, g1_ref, qa_ref, ka_ref, va_ref, kca_ref, vca_ref, ob_ref, c_ref,
                 inv_ref, wp_ref, cs_ref, qd_ref, kd_ref, vd_ref, kcd_ref, vcd_ref, bias_ref, amask_ref, wo_ref,
                 o_ref, oa_scr, od_scr, mix_scr, *, sub, deferred, local, n_tok, layer):
    tq = MIX_TQ
    ti = pl.program_id(1) * (x_ref.shape[1] // tq) + sub
    t0 = pl.multiple_of(ti * tq, tq)
    tile = slice(sub * tq, (sub + 1) * tq)

    pooled = _pool_tile(c_ref, inv_ref[tile, :], t0, n_tok, tq)
    out_c = (_dot(pooled.astype(BF16), wp_ref[...]) * cs_ref[...]).astype(BF16)

    if local:
        n_tiles = n_tok // tq
        half = D_WIN_ROWS // 2
        pair_entries = []
        for jj in range(3 * tq // GRID_W):
            row_entries = []
            for i in range(0, tq // GRID_W, 2):
                interior = PAIR_INTERIOR + (jj - half - i) + PAIR_COUNT // 2 - 1
                first = PAIR_EDGE + (jj - i) + PAIR_COUNT // 2 - 1 if jj < D_WIN_ROWS else PAIR_MASKED
                last = (PAIR_EDGE + (jj - D_WIN_ROWS - i) + PAIR_COUNT // 2 - 1 if jj >= half
                        else PAIR_MASKED)
                row_entries.append(jnp.where(ti == 0, first, jnp.where(ti == n_tiles - 1, last, interior)))
            pair_entries.append(row_entries)

    n_ctx_blk = vca_ref.shape[1]
    jobs = []

    def a_job(blk):
        cols = slice(blk * A_BLOCK, (blk + 1) * A_BLOCK)

        def scores():
            qblk = qa_ref[0, sub * tq + blk * A_BLOCK:sub * tq + (blk + 1) * A_BLOCK, :]
            qcat = jnp.concatenate([qblk[:, :128], qblk[:, 128:]], axis=0)
            qm = jnp.concatenate([_head_lanes(qcat, kv) for kv in range(A_KV_HEADS)], axis=0)
            if local:
                nblk = ti * (tq // A_BLOCK) + blk
                b0 = jnp.clip(nblk - 1, 0, n_tok // A_BLOCK - 3)
                kstart = pl.multiple_of(b0 * A_BLOCK, A_BLOCK)
                kwin = ka_ref[0, pl.ds(kstart, 3 * A_BLOCK), :]
                last_blk = n_tok // A_BLOCK - 1
                kind = jnp.where(nblk == 0, 1, jnp.where(nblk == last_blk, 2, 0))
                s_loc = _dot_nt(kwin, qm)
            s_ctx = _dot_nt(kca_ref[0], qm)
            per_head = []
            for kv in range(A_KV_HEADS):
                rows = slice(kv * HEAD_DIM, (kv + 1) * HEAD_DIM)
                qsl = slice(kv * 2 * A_BLOCK, (kv + 1) * 2 * A_BLOCK)
                parts = []
                if local:
                    s_win = s_loc[:, qsl] + amask_ref[kind]
                    for j in range(3):
                        parts.append((s_win[j * A_BLOCK:(j + 1) * A_BLOCK], va_ref[0, b0 + j, rows, :]))
                for j in range(n_ctx_blk):
                    parts.append((s_ctx[j * A_BLOCK:(j + 1) * A_BLOCK, qsl], vca_ref[0, j, rows, :]))
                per_head.append(parts)
            return per_head

        def finish(per_head):
            for kv, parts in enumerate(per_head):
                col = lax.broadcasted_iota(jnp.int32, (1, 2 * A_BLOCK), 1)
                sink = jnp.where(col < A_BLOCK, sink_ref[layer, 2 * kv], sink_ref[layer, 2 * kv + 1])
                o2 = _softmax_pv_t(parts, sink)
                oa_scr[sub, (2 * kv) * 64:(2 * kv + 1) * 64, cols] = o2[:, :A_BLOCK]
                oa_scr[sub, (2 * kv + 1) * 64:(2 * kv + 2) * 64, cols] = o2[:, A_BLOCK:]

        return scores, finish

    def d_job(pair):
        heads = (2 * pair, 2 * pair + 1)

        def scores():
            qd = qd_ref[0, tile, :]
            qm = jnp.concatenate([_head_lanes(qd, hd) for hd in heads], axis=0)
            if local:
                d0 = jnp.clip(ti - 1, 0, n_tok // tq - 3)
                kwin = kd_ref[0, pl.ds(pl.multiple_of(d0 * tq, tq), 3 * tq), :]
                s_loc = _dot_nt(kwin, qm)
            s_ctx = _dot_nt(kcd_ref[0], qm)
            per_head = []
            for k, hd in enumerate(heads):
                rows = slice(hd * HEAD_DIM, (hd + 1) * HEAD_DIM)
                qsl = slice(k * tq, (k + 1) * tq)
                parts = []
                if local:
                    bias = jnp.concatenate(
                        [jnp.concatenate([bias_ref[hd, e] for e in row_entries], axis=1)
                         for row_entries in pair_entries], axis=0)
                    s_win = s_loc[:, qsl] + bias
                    for j in range(3):
                        parts.append((s_win[j * tq:(j + 1) * tq], vd_ref[0, d0 + j, rows, :]))
                parts.append((s_ctx[:, qsl], vcd_ref[0, 0, rows, :]))
                per_head.append(parts)
            return per_head

        def finish(per_head):
            for hd, parts in zip(heads, per_head):
                od_scr[sub, hd * HEAD_DIM:(hd + 1) * HEAD_DIM, :] = _softmax_pv_t(parts, None)

        return scores, finish

    for blk in range(tq // A_BLOCK):
        jobs.append(a_job(blk))
    for pair in range(D_HEADS // 2):
        jobs.append(d_job(pair))
    ahead = min(SCORE_LOOKAHEAD, len(jobs))
    pending = [jobs[j][0]() for j in range(ahead)]
    for j, (_, finish) in enumerate(jobs):
        if j + ahead < len(jobs):
            pending.append(jobs[j + ahead][0]())
        if deferred:
            deferred.pop(0)()
        finish(pending.pop(0))

    mix_scr[sub, :, 0:256] = oa_scr[sub].T.astype(BF16)
    mix_scr[sub, :, 256:512] = ob_ref[0, tile, :]
    mix_scr[sub, :, 512:768] = out_c
    mix_scr[sub, :, 768:1024] = od_scr[sub].T.astype(BF16)

    def out_chunk(c):
        cols = slice(c * OUT_CHUNK, (c + 1) * OUT_CHUNK)

        def run():
            mix = _dot(mix_scr[sub], wo_ref[:, cols])
            o_ref[0, tile, cols] = x_ref[0, tile, cols] + g1_ref[:, cols] * mix

        return run

    deferred.extend(out_chunk(c) for c in range(D_MODEL // OUT_CHUNK))


def _mixer(x, mod, row, layer, qa, ka, va, kca, vca, ob, cin, inv_cnt, wpool, cscale, qd, kd, vd, kcd, vcd, bias,
           amask, w_out, sink, *, local):
    b, t, d = x.shape
    lc = kca.shape[1]
    tq = MIX_TQ
    ts = min(MIX_STEP_TILES * tq, t)
    n_sub = ts // tq
    tok = lambda bi, i: (bi, i, 0)
    per_b = lambda bi, i: (bi, 0, 0)
    per_b4 = lambda bi, i: (bi, 0, 0, 0)

    return pl.pallas_call(
        functools.partial(_mix_kernel, local=local, n_tok=t, layer=layer),
        out_shape=jax.ShapeDtypeStruct((b, t, d), F32),
        grid=(b, t // ts),
        in_specs=[
            pl.BlockSpec(memory_space=pltpu.SMEM),
            pl.BlockSpec((1, ts, d), tok),
            _mod_spec(layer, row, 2),
            pl.BlockSpec((1, ts, 256), tok),
            pl.BlockSpec((1, t, 128), per_b),
            pl.BlockSpec((1, t // A_BLOCK, 128, A_BLOCK), per_b4),
            pl.BlockSpec((1, lc, 128), per_b),
            pl.BlockSpec((1, lc // A_BLOCK, 128, A_BLOCK), per_b4),
            pl.BlockSpec((1, ts, 256), tok),
            pl.BlockSpec((1, t, 256), per_b),
            pl.BlockSpec((ts, 256), lambda bi, i: (i, 0)),
            _layer_spec(layer, (256, 256)),
            _layer_spec(layer, (1, 256)),
            pl.BlockSpec((1, ts, 256), tok),
            pl.BlockSpec((1, t, 256), per_b),
            pl.BlockSpec((1, t // tq, 256, tq), per_b4),
            pl.BlockSpec((1, lc, 256), per_b),
            pl.BlockSpec((1, lc // tq, 256, tq), per_b4),
            _layer_spec(layer, (D_HEADS, PAIR_MASKED + 1, GRID_W, 2 * GRID_W)),
            pl.BlockSpec((3, 3 * A_BLOCK, 2 * A_BLOCK), lambda bi, i: (0, 0, 0)),
            _layer_spec(layer, (d, d)),
        ],
        out_specs=pl.BlockSpec((1, ts, d), tok),
        scratch_shapes=[pltpu.VMEM((n_sub, GROUP_W, tq), F32), pltpu.VMEM((n_sub, GROUP_W, tq), F32),
                        pltpu.VMEM((n_sub, tq, d), BF16)],
        compiler_params=pltpu.CompilerParams(
            dimension_semantics=("arbitrary", "arbitrary"), vmem_limit_bytes=VMEM_LIMIT),
        name="mixer_latent" if local else "mixer_ctx",
    )(sink, x, mod, qa, ka, va, kca, vca, ob, cin, inv_cnt, wpool, cscale, qd, kd, vd, kcd, vcd, bias, amask,
      w_out)


def _ffn_kernel(x_ref, sh_ref, sc_ref, g2_ref, gn_ref, wg_ref, wu_ref, wd_ref, o_ref, act_scr):
    xt = x_ref[0]
    ms = jnp.mean(xt * xt, axis=-1, keepdims=True)
    h = xt * lax.rsqrt(ms + NORM_EPS) * gn_ref[...]
    h = (h * (1.0 + sc_ref[...]) + sh_ref[...]).astype(BF16)
    for c in range(FF_DIM // FFN_TF):
        cols = slice(c * FFN_TF, (c + 1) * FFN_TF)
        gate = _dot(h, wg_ref[:, cols])
        up = _dot(h, wu_ref[:, cols])
        act_scr[:, cols] = (gate * jax.nn.sigmoid(gate) * up).astype(BF16)
    o_ref[0] = xt + g2_ref[...] * _dot(act_scr[...], wd_ref[...])


def _ffn(x, mod, row, layer, g_ffn, wg, wu, wd):
    b, t, d = x.shape
    tm = min(FFN_TM, t)
    tok = lambda bi, i: (bi, i, 0)
    per_layer = lambda bi, i: (layer, 0, 0)
    resident = pl.Buffered(1)
    return pl.pallas_call(
        _ffn_kernel,
        out_shape=jax.ShapeDtypeStruct((b, t, d), F32),
        grid=(b, t // tm),
        in_specs=[
            pl.BlockSpec((1, tm, d), tok),
            _mod_spec(layer, row, 3),
            _mod_spec(layer, row, 4),
            _mod_spec(layer, row, 5),
            _layer_spec(layer, (1, d)),
            pl.BlockSpec((None, d, FF_DIM), per_layer, pipeline_mode=resident),
            pl.BlockSpec((None, d, FF_DIM), per_layer, pipeline_mode=resident),
            pl.BlockSpec((None, FF_DIM, d), per_layer, pipeline_mode=resident),
        ],
        out_specs=pl.BlockSpec((1, tm, d), tok),
        scratch_shapes=[pltpu.VMEM((tm, FF_DIM), BF16)],
        compiler_params=pltpu.CompilerParams(
            dimension_semantics=("arbitrary", "arbitrary"), vmem_limit_bytes=VMEM_LIMIT),
        name="ffn",
    )(x, mod, mod, mod, g_ffn, wg, wu, wd)


def _rope_tables(s):
    t = np.arange(s)
    half = 16
    inv = np.power(np.float32(ROPE_BASE), -np.arange(half, dtype=np.float32) / half).astype(np.float32)
    ang_r = (t // GRID_W).astype(np.float32)[:, None] * inv[None, :]
    ang_c = (t % GRID_W).astype(np.float32)[:, None] * inv[None, :]
    cos = np.concatenate([np.cos(ang_r)] * 2 + [np.cos(ang_c)] * 2, axis=-1)
    sin = np.concatenate([-np.sin(ang_r), np.sin(ang_r), -np.sin(ang_c), np.sin(ang_c)], axis=-1)
    return cos.astype(np.float32), sin.astype(np.float32)


def _pool_inverse_counts(n_tok):
    t = np.arange(n_tok)
    cols = []
    for w in C_POOLS:
        lo = np.clip(t - w // 2, 0, n_tok)
        hi = np.clip(t - w // 2 + w, 0, n_tok)
        cols.append(np.repeat((np.float32(1.0) / (hi - lo).astype(np.float32))[:, None], C_GROUP_W, axis=1))
    return jnp.asarray(np.concatenate(cols, axis=1), F32)


def _window_mask():
    kk = np.arange(3 * A_BLOCK)[:, None]
    qq = np.arange(2 * A_BLOCK)[None, :] % A_BLOCK
    kinds = [np.where(np.abs(kk - qq - back * A_BLOCK) <= A_WINDOW, 0.0, NEG_INF) for back in (1, 0, 2)]
    return jnp.asarray(np.stack(kinds), F32)


PAIR_COUNT = 2 * (D_WIN_ROWS - 1)
PAIR_INTERIOR = 0
PAIR_EDGE = PAIR_COUNT
PAIR_MASKED = 2 * PAIR_COUNT


def _neighbour_bias_pairs(rpb):
    cidx = np.arange(GRID_W)
    col_start = np.clip(cidx - D_WIN_COLS // 2, 0, GRID_W - D_WIN_COLS)
    col_ok_t = ((cidx[None, :] >= col_start[:, None]) & (cidx[None, :] < col_start[:, None] + D_WIN_COLS)).T
    coff_t = np.clip(cidx[:, None] - cidx[None, :], -(D_WIN_COLS - 1), D_WIN_COLS - 1) + (D_WIN_COLS - 1)
    select = (np.arange(2 * D_WIN_COLS - 1)[:, None, None] == coff_t[None]).astype(np.float32)
    blocks = lax.dot_general(rpb.astype(F32), jnp.asarray(select), (((3,), (0,)), ((), ())),
                             precision=lax.Precision.HIGHEST)
    blocks = jnp.where(col_ok_t, blocks, NEG_INF)
    d = np.arange(-(D_WIN_ROWS - 1), D_WIN_ROWS)
    seen = (d >= -(D_WIN_ROWS // 2)) & (d < D_WIN_ROWS // 2)
    interior = jnp.where(seen[:, None, None], blocks, NEG_INF)
    pair = lambda t: jnp.concatenate([t[:, :, 1:], t[:, :, :-1]], axis=-1)
    masked = jnp.full(blocks.shape[:2] + (1, GRID_W, 2 * GRID_W), NEG_INF, F32)
    return jnp.concatenate([pair(interior), pair(blocks), masked], axis=2)


def kernel(x, c, ctx, c_ctx, w_mod, b_mod, g_mix, g_ffn, w_in, w_out, a_q_gain, a_k_gain, a_sink,
           b_v_gain, b_w_s, b_b_s, c_w_pool, c_scale, d_q_gain, d_k_gain, d_rpb, w_gate, w_up, w_down):
    bsz, s, d = x.shape
    lc = ctx.shape[1]
    n_rows = s // GRID_W
    assert s % MIX_TQ == 0 and n_rows >= 2 * D_WIN_ROWS and MIX_TQ // GRID_W == D_WIN_ROWS // 2

    cvec = jnp.zeros((MOD_ROWS, d), F32).at[:bsz].set(c).at[bsz].set(c_ctx)
    mod = _modulation(cvec, w_mod, b_mod).reshape(DEPTH, MOD_ROWS, 6, 1, d)
    lat_row = lambda bi: bi
    ctx_row = lambda bi: bsz

    seg = jnp.asarray(np.kron(np.eye(4), np.full((64, 64), 1.0 / 64)), BF16)
    cos1, sin1 = _rope_tables(s)
    tables_lat = tuple(jnp.asarray(np.tile(tab, (1, reps)))
                       for tab, reps in ((cos1, 4), (sin1, 4), (cos1, 2), (sin1, 2)))
    tables_ctx = tuple(jnp.zeros((min(IN_TM, bsz * lc), w), F32) for w in (256, 256, 128, 128))

    amask = _window_mask()
    inv_lat, inv_ctx = _pool_inverse_counts(s), _pool_inverse_counts(lc)
    bias_all = _neighbour_bias_pairs(d_rpb * LOG2E)

    w_in_b = w_in.astype(BF16)
    w_out_b = w_out.astype(BF16)
    wg_b, wu_b, wd_b = w_gate.astype(BF16), w_up.astype(BF16), w_down.astype(BF16)
    gmix = g_mix.reshape(DEPTH, 1, d)
    gffn = g_ffn.reshape(DEPTH, 1, d)
    aqg = jnp.tile(a_q_gain, (1, 4)).reshape(DEPTH, 1, 256)
    akg = jnp.tile(a_k_gain, (1, 2)).reshape(DEPTH, 1, 128)
    dqg = jnp.tile(d_q_gain, (1, 4)).reshape(DEPTH, 1, 256)
    dkg = jnp.tile(d_k_gain, (1, 4)).reshape(DEPTH, 1, 256)
    bvg = b_v_gain.reshape(DEPTH, 1, 256)
    ws_cat = jnp.transpose(b_w_s, (0, 2, 1, 3)).reshape(DEPTH, B_CHUNK, B_GROUPS * B_CHUNK).astype(BF16)
    bs_t = jnp.repeat(jnp.swapaxes(b_b_s, 1, 2), B_GROUP_W, axis=2)
    wpool = jnp.einsum('lgcd,gh->lgchd', c_w_pool, jnp.eye(len(C_POOLS), dtype=F32)
                       ).reshape(DEPTH, GROUP_W, GROUP_W).astype(BF16)
    cscale = c_scale.reshape(DEPTH, 1, 256)
    sink = a_sink * LOG2E
    common = (gmix, w_in_b, seg, aqg, akg, dqg, dkg)

    xc = ctx
    for l in range(DEPTH):
        last = l == DEPTH - 1
        lat = _in_projection(x, mod, lat_row, l, *common, tables_lat, bvg, ws_cat, bs_t, use_rope=True)
        con = _in_projection(xc.reshape(1, bsz * lc, d), mod, ctx_row, l, *common, tables_ctx, bvg, ws_cat,
                             bs_t, use_rope=False)
        qa, ka, va, ob, cin, qd, kd, vd = lat
        cqa, cka, cva, cob, ccin, cqd, ckd, cvd = [
            o.reshape((bsz, o.shape[1] // bsz) + o.shape[2:]) for o in con]

        x = _mixer(x, mod, lat_row, l, qa, ka, va, cka, cva, ob, cin, inv_lat, wpool, cscale, qd, kd, vd, ckd, cvd,
                   bias_all, amask, w_out_b, sink, local=True)
        x = _ffn(x, mod, lat_row, l, gffn, wg_b, wu_b, wd_b)
        if not last:
            xc = _mixer(xc, mod, ctx_row, l, cqa, cka, cva, cka, cva, cob, ccin, inv_ctx, wpool, cscale, cqd, ckd,
                        cvd, ckd, cvd, bias_all, amask, w_out_b, sink, local=False)
            xc = _ffn(xc.reshape(1, bsz * lc, d), mod, ctx_row, l, gffn, wg_b, wu_b, wd_b).reshape(bsz, lc, d)
    return x
```

```python
import functools

import jax
import jax.numpy as jnp
import numpy as np
from jax import lax
from jax.experimental import pallas as pl
from jax.experimental.pallas import tpu as pltpu

F32 = jnp.float32
BF16 = jnp.bfloat16

D_MODEL = 1024
DEPTH = 2
GRID_W = 64
HEAD_DIM = 64
GROUP_W = 256
A_HEADS = 4
A_KV_HEADS = 2
A_WINDOW = 128
A_BLOCK = 128
B_CHUNK = 128
B_GROUPS = 4
B_GROUP_W = 64
C_POOLS = (2, 4, 8, 16)
C_GROUP_W = 64
D_HEADS = 4
D_WIN_ROWS = 8
D_WIN_COLS = 16
FF_DIM = 2816
ROPE_BASE = 10000.0
NORM_EPS = 1e-6
NEG_INF = -1e30
LOG2E = 1.4426950408889634
IN_TOTAL = 2048
OFF_AQ, OFF_AK, OFF_AV, OFF_BU, OFF_BV, OFF_C, OFF_DQ, OFF_DK, OFF_DV = (
    0, 256, 384, 512, 768, 1024, 1280, 1536, 1792)

MOD_ROWS = 16
MOD_TN = D_MODEL
IN_TM = 512
MIX_TQ = 256
FFN_TM = 1024
FFN_TF = 256
BF16_SUBLANES = 16
POOL_HALO = 8
MIX_STEP_TILES = 4
OUT_CHUNK = 256
SCORE_LOOKAHEAD = 2
VMEM_LIMIT = 56 * 1024 * 1024


def _dot(a, b):
    return jnp.dot(a, b, preferred_element_type=F32)


def _dot_nt(a, b):
    return lax.dot_general(a, b, (((1,), (1,)), ((), ())), preferred_element_type=F32)


def _mod_kernel(c_ref, w_ref, b_ref, o_ref):
    cv = c_ref[...]
    act = cv * jax.nn.sigmoid(cv)
    o_ref[:, 0, :] = _dot(act.astype(BF16), w_ref[0].astype(BF16)) + b_ref[0]


def _modulation(cvec, w_mod, b_mod):
    depth, d, n = w_mod.shape
    return pl.pallas_call(
        _mod_kernel,
        out_shape=jax.ShapeDtypeStruct((depth, MOD_ROWS, n // MOD_TN, 1, MOD_TN), F32),
        grid=(depth, n // MOD_TN),
        in_specs=[
            pl.BlockSpec((MOD_ROWS, d), lambda l, j: (0, 0)),
            pl.BlockSpec((1, d, MOD_TN), lambda l, j: (l, 0, j)),
            pl.BlockSpec((1, 1, MOD_TN), lambda l, j: (l, 0, j)),
        ],
        out_specs=pl.BlockSpec((None, MOD_ROWS, None, 1, MOD_TN), lambda l, j: (l, 0, j, 0, 0)),
        compiler_params=pltpu.CompilerParams(
            dimension_semantics=("arbitrary", "arbitrary"), vmem_limit_bytes=VMEM_LIMIT),
        name="modulation",
    )(cvec, w_mod, b_mod.reshape(depth, 1, n))


def _head_norm(t, seg, gain):
    sq = t * t
    hi = sq.astype(BF16)
    lo = (sq - hi.astype(F32)).astype(BF16)
    ms = _dot(hi, seg) + _dot(lo, seg)
    return t * lax.rsqrt(ms + NORM_EPS) * gain


def _rope(y, cos, sin_signed):
    w = y.shape[-1]
    up = pltpu.roll(y, w - 16, axis=1)
    dn = pltpu.roll(y, 16, axis=1)
    lane = lax.broadcasted_iota(jnp.int32, y.shape, 1)
    swapped = jnp.where((lane & 31) < 16, up, dn)
    return y * cos + swapped * sin_signed


def _swap_middle_heads(q):
    lane = lax.broadcasted_iota(jnp.int32, q.shape, 1)
    from_right = pltpu.roll(q, 3 * HEAD_DIM, axis=1)
    from_left = pltpu.roll(q, HEAD_DIM, axis=1)
    return jnp.where((lane >= HEAD_DIM) & (lane < 2 * HEAD_DIM), from_right,
                     jnp.where((lane >= 2 * HEAD_DIM) & (lane < 3 * HEAD_DIM), from_left, q))


def _in_kernel(x_ref, sh_ref, sc_ref, g_ref, w_ref, seg_ref, aqg_ref, akg_ref, dqg_ref, dkg_ref,
               cosq_ref, sinq_ref, cosk_ref, sink_ref, bvg_ref, ws_ref, bs_ref,
               qa_ref, ka_ref, va_ref, ob_ref, c_ref, qd_ref, kd_ref, vd_ref, v_scr, *, use_rope):
    xt = x_ref[0]
    ms = jnp.mean(xt * xt, axis=-1, keepdims=True)
    h = xt * lax.rsqrt(ms + NORM_EPS) * g_ref[...]
    h = (h * (1.0 + sc_ref[...]) + sh_ref[...]).astype(BF16)

    group_w = IN_TOTAL // 4
    assert (OFF_BU, OFF_C, OFF_DK) == (group_w, 2 * group_w, 3 * group_w)

    def group(k):
        return _dot(h, w_ref[:, k * group_w:(k + 1) * group_w])

    seg = seg_ref[...]
    scale = HEAD_DIM ** -0.5 * LOG2E
    tm = xt.shape[0]

    def finish_a(p):
        qa = _head_norm(p[:, 0:256], seg, aqg_ref[...])
        ka = _head_norm(p[:, 256:384], seg[:128, :128], akg_ref[...])
        if use_rope:
            qa = _rope(qa, cosq_ref[...], sinq_ref[...])
            ka = _rope(ka, cosk_ref[...], sink_ref[...])
        qa_ref[0] = (_swap_middle_heads(qa) * scale).astype(BF16)
        ka_ref[0] = ka.astype(BF16)
        v_scr[:, 0:128] = p[:, 384:512]
        vat = v_scr[:, 0:128].T.astype(BF16)
        for j in range(tm // A_BLOCK):
            va_ref[0, j] = vat[:, j * A_BLOCK:(j + 1) * A_BLOCK]

    def finish_b(p):
        u = jax.nn.gelu(p[:, 0:256])
        v = jax.nn.gelu(p[:, 256:512])
        mu = jnp.mean(v, axis=-1, keepdims=True)
        vc = v - mu
        var = jnp.mean(vc * vc, axis=-1, keepdims=True)
        vn = (vc * lax.rsqrt(var + NORM_EPS) * bvg_ref[...]).astype(BF16)
        lane = lax.broadcasted_iota(jnp.int32, (B_CHUNK, GROUP_W), 1)
        zero = jnp.zeros((B_CHUNK, GROUP_W), BF16)
        for ch in range(tm // B_CHUNK):
            rows = slice(ch * B_CHUNK, (ch + 1) * B_CHUNK)
            vch = vn[rows]
            stacked = jnp.concatenate(
                [jnp.where((lane >= g * B_GROUP_W) & (lane < (g + 1) * B_GROUP_W), vch, zero)
                 for g in range(B_GROUPS)], axis=0)
            z = _dot(ws_ref[...], stacked) + bs_ref[...]
            ob_ref[0, rows, :] = (u[rows] * z).astype(BF16)

    def finish_c(p):
        c_ref[0] = p[:, 0:256]
        qd = _head_norm(p[:, 256:512], seg, dqg_ref[...])
        qd_ref[0] = (qd * scale).astype(BF16)

    def finish_d(p):
        kd_ref[0] = _head_norm(p[:, 0:256], seg, dkg_ref[...]).astype(BF16)
        v_scr[:, 128:384] = p[:, 256:512]
        vdt = v_scr[:, 128:384].T.astype(BF16)
        for j in range(tm // MIX_TQ):
            vd_ref[0, j] = vdt[:, j * MIX_TQ:(j + 1) * MIX_TQ]

    order = ((1, finish_b), (0, finish_a), (3, finish_d), (2, finish_c))
    pending = group(order[0][0])
    for pos, (_, finish) in enumerate(order):
        upcoming = group(order[pos + 1][0]) if pos + 1 < len(order) else None
        finish(pending)
        pending = upcoming


def _mod_spec(layer, row, which, batch_axis=0):
    return pl.BlockSpec((None, None, None, 1, D_MODEL),
                        lambda *idx: (layer, row(idx[batch_axis]), which, 0, 0))


def _layer_spec(layer, shape):
    return pl.BlockSpec((None,) + tuple(shape), lambda *_: (layer,) + (0,) * len(shape))


def _in_projection(x, mod, row, layer, g_mix, w_in, seg, aqg, akg, dqg, dkg, tables, bvg, ws_cat, bs_t,
                   *, use_rope):
    b, t, d = x.shape
    cosq, sinq, cosk, sink = tables
    tm = min(IN_TM, t)
    tok = lambda i, bi: (bi, i, 0)
    const2 = lambda i, bi: (0, 0)
    pos2 = (lambda i, bi: (i, 0)) if use_rope else const2
    assert tm % MIX_TQ == 0 and MIX_TQ % A_BLOCK == 0
    out_shapes = (
        jax.ShapeDtypeStruct((b, t, 256), BF16),
        jax.ShapeDtypeStruct((b, t, 128), BF16),
        jax.ShapeDtypeStruct((b, t // A_BLOCK, 128, A_BLOCK), BF16),
        jax.ShapeDtypeStruct((b, t, 256), BF16),
        jax.ShapeDtypeStruct((b, t, 256), F32),
        jax.ShapeDtypeStruct((b, t, 256), BF16),
        jax.ShapeDtypeStruct((b, t, 256), BF16),
        jax.ShapeDtypeStruct((b, t // MIX_TQ, 256, MIX_TQ), BF16),
    )
    blk4 = lambda i, bi: (bi, i, 0, 0)
    out_specs = tuple(
        pl.BlockSpec((1, tm // A_BLOCK, 128, A_BLOCK), blk4) if k == 2 else
        pl.BlockSpec((1, tm // MIX_TQ, 256, MIX_TQ), blk4) if k == 7 else
        pl.BlockSpec((1, tm, s.shape[-1]), tok)
        for k, s in enumerate(out_shapes))
    return pl.pallas_call(
        functools.partial(_in_kernel, use_rope=use_rope),
        out_shape=out_shapes,
        grid=(t // tm, b),
        in_specs=[
            pl.BlockSpec((1, tm, d), tok),
            _mod_spec(layer, row, 0, batch_axis=1),
            _mod_spec(layer, row, 1, batch_axis=1),
            _layer_spec(layer, (1, d)),
            _layer_spec(layer, (d, IN_TOTAL)),
            pl.BlockSpec((256, 256), const2),
            _layer_spec(layer, (1, 256)),
            _layer_spec(layer, (1, 128)),
            _layer_spec(layer, (1, 256)),
            _layer_spec(layer, (1, 256)),
            pl.BlockSpec((tm, 256), pos2),
            pl.BlockSpec((tm, 256), pos2),
            pl.BlockSpec((tm, 128), pos2),
            pl.BlockSpec((tm, 128), pos2),
            _layer_spec(layer, (1, 256)),
            _layer_spec(layer, (B_CHUNK, B_GROUPS * B_CHUNK)),
            _layer_spec(layer, (B_CHUNK, GROUP_W)),
        ],
        out_specs=out_specs,
        scratch_shapes=[pltpu.VMEM((tm, 384), F32)],
        compiler_params=pltpu.CompilerParams(
            dimension_semantics=("arbitrary", "arbitrary"), vmem_limit_bytes=VMEM_LIMIT),
        name="in_projection_rope" if use_rope else "in_projection_ctx",
    )(x, mod, mod, g_mix, w_in, seg, aqg, akg, dqg, dkg, cosq, sinq, cosk, sink, bvg, ws_cat, bs_t)


def _ctx_kv_kernel(x_ref, sh_ref, sc_ref, g_ref, w_ref, seg_ref, akg_ref, dkg_ref,
                   ka_ref, va_ref, kd_ref, vd_ref, v_scr):
    xt = x_ref[0]
    tm = xt.shape[0]
    ms = jnp.mean(xt * xt, axis=-1, keepdims=True)
    h = xt * lax.rsqrt(ms + NORM_EPS) * g_ref[...]
    h = (h * (1.0 + sc_ref[...]) + sh_ref[...]).astype(BF16)
    seg = seg_ref[...]
    pa = _dot(h, w_ref[:, OFF_AK:OFF_BU])
    pd = _dot(h, w_ref[:, OFF_DK:IN_TOTAL])
    ka_ref[0] = _head_norm(pa[:, 0:128], seg[:128, :128], akg_ref[...]).astype(BF16)
    kd_ref[0] = _head_norm(pd[:, 0:256], seg, dkg_ref[...]).astype(BF16)
    v_scr[:, 0:128] = pa[:, 128:256]
    v_scr[:, 128:384] = pd[:, 256:512]
    vat = v_scr[:, 0:128].T.astype(BF16)
    for j in range(tm // A_BLOCK):
        va_ref[0, j] = vat[:, j * A_BLOCK:(j + 1) * A_BLOCK]
    vdt = v_scr[:, 128:384].T.astype(BF16)
    for j in range(tm // MIX_TQ):
        vd_ref[0, j] = vdt[:, j * MIX_TQ:(j + 1) * MIX_TQ]


def _ctx_kv_projection(x, mod, row, layer, g_mix, w_in, seg, akg, dkg):
    b, t, d = x.shape
    tm = min(IN_TM, t)
    tok = lambda i, bi: (bi, i, 0)
    blk4 = lambda i, bi: (bi, i, 0, 0)
    out_shapes = (
        jax.ShapeDtypeStruct((b, t, 128), BF16),
        jax.ShapeDtypeStruct((b, t // A_BLOCK, 128, A_BLOCK), BF16),
        jax.ShapeDtypeStruct((b, t, 256), BF16),
        jax.ShapeDtypeStruct((b, t // MIX_TQ, 256, MIX_TQ), BF16),
    )
    return pl.pallas_call(
        _ctx_kv_kernel,
        out_shape=out_shapes,
        grid=(t // tm, b),
        in_specs=[
            pl.BlockSpec((1, tm, d), tok),
            _mod_spec(layer, row, 0, batch_axis=1),
            _mod_spec(layer, row, 1, batch_axis=1),
            _layer_spec(layer, (1, d)),
            _layer_spec(layer, (d, IN_TOTAL)),
            pl.BlockSpec((256, 256), lambda i, bi: (0, 0)),
            _layer_spec(layer, (1, 128)),
            _layer_spec(layer, (1, 256)),
        ],
        out_specs=(pl.BlockSpec((1, tm, 128), tok),
                   pl.BlockSpec((1, tm // A_BLOCK, 128, A_BLOCK), blk4),
                   pl.BlockSpec((1, tm, 256), tok),
                   pl.BlockSpec((1, tm // MIX_TQ, 256, MIX_TQ), blk4)),
        scratch_shapes=[pltpu.VMEM((tm, 384), F32)],
        compiler_params=pltpu.CompilerParams(
            dimension_semantics=("arbitrary", "arbitrary"), vmem_limit_bytes=VMEM_LIMIT),
        name="ctx_kv_projection",
    )(x, mod, mod, g_mix, w_in, seg, akg, dkg)


def _softmax_pv_t(parts, extra_logit):
    m = None
    for s, _ in parts:
        pm = jnp.max(s, axis=0, keepdims=True)
        m = pm if m is None else jnp.maximum(m, pm)
    if extra_logit is not None:
        m = jnp.maximum(m, extra_logit)
    acc = None
    for s, vt in parts:
        dh, n = vt.shape
        vt_ones = jnp.concatenate([vt, jnp.ones((BF16_SUBLANES, n), BF16)], axis=0)
        pv = _dot(vt_ones, jnp.exp2(s - m).astype(BF16))
        acc = pv if acc is None else acc + pv
    denom = acc[dh:dh + 1]
    if extra_logit is not None:
        denom = denom + jnp.exp2(extra_logit - m)
    return acc[:dh] / denom


def _head_lanes(q, head, width=HEAD_DIM):
    lane = lax.broadcasted_iota(jnp.int32, q.shape, 1)
    return jnp.where((lane >= head * width) & (lane < (head + 1) * width), q, jnp.zeros_like(q))


def _pool_tile(c_ref, inv, t0, n_tok, tq):
    y = c_ref[0, pl.ds(t0, tq), :]
    lo_start = pl.multiple_of(jnp.maximum(t0 - POOL_HALO, 0), POOL_HALO)
    hi_start = pl.multiple_of(jnp.minimum(t0 + tq, n_tok - POOL_HALO), POOL_HALO)
    lo = c_ref[0, pl.ds(lo_start, POOL_HALO), :]
    hi = c_ref[0, pl.ds(hi_start, POOL_HALO), :]
    lo = jnp.where(t0 > 0, lo, 0.0)
    hi = jnp.where(t0 + tq < n_tok, hi, 0.0)
    ypad = jnp.concatenate([lo, y, hi], axis=0)

    n = tq + 2 * POOL_HALO
    w2 = ypad + pltpu.roll(ypad, 1, axis=0)
    w4 = pltpu.roll(w2, 1, axis=0) + pltpu.roll(w2, n - 1, axis=0)
    w8 = pltpu.roll(w4, 2, axis=0) + pltpu.roll(w4, n - 2, axis=0)
    w16 = pltpu.roll(w8, 4, axis=0) + pltpu.roll(w8, n - 4, axis=0)
    sums = tuple(w[POOL_HALO:POOL_HALO + tq] for w in (w2, w4, w8, w16))

    lane = lax.broadcasted_iota(jnp.int32, (tq, GROUP_W), 1)
    total = sums[0]
    for gi in range(1, len(C_POOLS)):
        total = jnp.where(lane >= gi * C_GROUP_W, sums[gi], total)
    return total * inv - y


def _mix_kernel(*refs, local, n_tok, layer):
    x_ref = refs[1]
    deferred = []
    for sub in range(x_ref.shape[1] // MIX_TQ):
        _mix_subtile(*refs, sub=sub, deferred=deferred, local=local, n_tok=n_tok, layer=layer)
    while deferred:
        deferred.pop(0)()


def _mix_subtile(sink_ref, x_ref, g1_ref, qa_ref, ka_ref, va_ref, kca_ref, vca_ref, ob_ref, c_ref,
                 inv_ref, wp_ref, cs_ref, qd_ref, kd_ref, vd_ref, kcd_ref, vcd_ref, bias_ref, amask_ref, wo_ref,
                 o_ref, oa_scr, od_scr, mix_scr, *, sub, deferred, local, n_tok, layer):
    tq = MIX_TQ
    ti = pl.program_id(1) * (x_ref.shape[1] // tq) + sub
    t0 = pl.multiple_of(ti * tq, tq)
    tile = slice(sub * tq, (sub + 1) * tq)

    pooled = _pool_tile(c_ref, inv_ref[tile, :], t0, n_tok, tq)
    out_c = (_dot(pooled.astype(BF16), wp_ref[...]) * cs_ref[...]).astype(BF16)

    if local:
        n_tiles = n_tok // tq
        half = D_WIN_ROWS // 2
        pair_entries = []
        for jj in range(3 * tq // GRID_W):
            row_entries = []
            for i in range(0, tq // GRID_W, 2):
                interior = PAIR_INTERIOR + (jj - half - i) + PAIR_COUNT // 2 - 1
                first = PAIR_EDGE + (jj - i) + PAIR_COUNT // 2 - 1 if jj < D_WIN_ROWS else PAIR_MASKED
                last = (PAIR_EDGE + (jj - D_WIN_ROWS - i) + PAIR_COUNT // 2 - 1 if jj >= half
                        else PAIR_MASKED)
                row_entries.append(jnp.where(ti == 0, first, jnp.where(ti == n_tiles - 1, last, interior)))
            pair_entries.append(row_entries)

    n_ctx_blk = vca_ref.shape[1]
    jobs = []

    def a_job(blk):
        cols = slice(blk * A_BLOCK, (blk + 1) * A_BLOCK)

        def scores():
            qblk = qa_ref[0, sub * tq + blk * A_BLOCK:sub * tq + (blk + 1) * A_BLOCK, :]
            qcat = jnp.concatenate([qblk[:, :128], qblk[:, 128:]], axis=0)
            qm = jnp.concatenate([_head_lanes(qcat, kv) for kv in range(A_KV_HEADS)], axis=0)
            if local:
                nblk = ti * (tq // A_BLOCK) + blk
                b0 = jnp.clip(nblk - 1, 0, n_tok // A_BLOCK - 3)
                kstart = pl.multiple_of(b0 * A_BLOCK, A_BLOCK)
                kwin = ka_ref[0, pl.ds(kstart, 3 * A_BLOCK), :]
                last_blk = n_tok // A_BLOCK - 1
                kind = jnp.where(nblk == 0, 1, jnp.where(nblk == last_blk, 2, 0))
                s_loc = _dot_nt(kwin, qm)
            s_ctx = _dot_nt(kca_ref[0], qm)
            per_head = []
            for kv in range(A_KV_HEADS):
                rows = slice(kv * HEAD_DIM, (kv + 1) * HEAD_DIM)
                qsl = slice(kv * 2 * A_BLOCK, (kv + 1) * 2 * A_BLOCK)
                parts = []
                if local:
                    s_win = s_loc[:, qsl] + amask_ref[kind]
                    for j in range(3):
                        parts.append((s_win[j * A_BLOCK:(j + 1) * A_BLOCK], va_ref[0, b0 + j, rows, :]))
                for j in range(n_ctx_blk):
                    parts.append((s_ctx[j * A_BLOCK:(j + 1) * A_BLOCK, qsl], vca_ref[0, j, rows, :]))
                per_head.append(parts)
            return per_head

        def finish(per_head):
            for kv, parts in enumerate(per_head):
                col = lax.broadcasted_iota(jnp.int32, (1, 2 * A_BLOCK), 1)
                sink = jnp.where(col < A_BLOCK, sink_ref[layer, 2 * kv], sink_ref[layer, 2 * kv + 1])
                o2 = _softmax_pv_t(parts, sink)
                oa_scr[sub, (2 * kv) * 64:(2 * kv + 1) * 64, cols] = o2[:, :A_BLOCK]
                oa_scr[sub, (2 * kv + 1) * 64:(2 * kv + 2) * 64, cols] = o2[:, A_BLOCK:]

        return scores, finish

    def d_job(pair):
        heads = (2 * pair, 2 * pair + 1)

        def scores():
            qd = qd_ref[0, tile, :]
            qm = jnp.concatenate([_head_lanes(qd, hd) for hd in heads], axis=0)
            if local:
                d0 = jnp.clip(ti - 1, 0, n_tok // tq - 3)
                kwin = kd_ref[0, pl.ds(pl.multiple_of(d0 * tq, tq), 3 * tq), :]
                s_loc = _dot_nt(kwin, qm)
            s_ctx = _dot_nt(kcd_ref[0], qm)
            per_head = []
            for k, hd in enumerate(heads):
                rows = slice(hd * HEAD_DIM, (hd + 1) * HEAD_DIM)
                qsl = slice(k * tq, (k + 1) * tq)
                parts = []
                if local:
                    bias = jnp.concatenate(
                        [jnp.concatenate([bias_ref[hd, e] for e in row_entries], axis=1)
                         for row_entries in pair_entries], axis=0)
                    s_win = s_loc[:, qsl] + bias
                    for j in range(3):
                        parts.append((s_win[j * tq:(j + 1) * tq], vd_ref[0, d0 + j, rows, :]))
                parts.append((s_ctx[:, qsl], vcd_ref[0, 0, rows, :]))
                per_head.append(parts)
            return per_head

        def finish(per_head):
            for hd, parts in zip(heads, per_head):
                od_scr[sub, hd * HEAD_DIM:(hd + 1) * HEAD_DIM, :] = _softmax_pv_t(parts, None)

        return scores, finish

    for blk in range(tq // A_BLOCK):
        jobs.append(a_job(blk))
    for pair in range(D_HEADS // 2):
        jobs.append(d_job(pair))
    ahead = min(SCORE_LOOKAHEAD, len(jobs))
    pending = [jobs[j][0]() for j in range(ahead)]
    for j, (_, finish) in enumerate(jobs):
        if j + ahead < len(jobs):
            pending.append(jobs[j + ahead][0]())
        if deferred:
            deferred.pop(0)()
        finish(pending.pop(0))

    mix_scr[sub, :, 0:256] = oa_scr[sub].T.astype(BF16)
    mix_scr[sub, :, 256:512] = ob_ref[0, tile, :]
    mix_scr[sub, :, 512:768] = out_c
    mix_scr[sub, :, 768:1024] = od_scr[sub].T.astype(BF16)

    def out_chunk(c):
        cols = slice(c * OUT_CHUNK, (c + 1) * OUT_CHUNK)

        def run():
            mix = _dot(mix_scr[sub], wo_ref[:, cols])
            o_ref[0, tile, cols] = x_ref[0, tile, cols] + g1_ref[:, cols] * mix

        return run

    deferred.extend(out_chunk(c) for c in range(D_MODEL // OUT_CHUNK))


def _mixer(x, mod, row, layer, qa, ka, va, kca, vca, ob, cin, inv_cnt, wpool, cscale, qd, kd, vd, kcd, vcd, bias,
           amask, w_out, sink, *, local):
    b, t, d = x.shape
    lc = kca.shape[1]
    tq = MIX_TQ
    ts = min(MIX_STEP_TILES * tq, t)
    n_sub = ts // tq
    tok = lambda bi, i: (bi, i, 0)
    per_b = lambda bi, i: (bi, 0, 0)
    per_b4 = lambda bi, i: (bi, 0, 0, 0)

    return pl.pallas_call(
        functools.partial(_mix_kernel, local=local, n_tok=t, layer=layer),
        out_shape=jax.ShapeDtypeStruct((b, t, d), F32),
        grid=(b, t // ts),
        in_specs=[
            pl.BlockSpec(memory_space=pltpu.SMEM),
            pl.BlockSpec((1, ts, d), tok),
            _mod_spec(layer, row, 2),
            pl.BlockSpec((1, ts, 256), tok),
            pl.BlockSpec((1, t, 128), per_b),
            pl.BlockSpec((1, t // A_BLOCK, 128, A_BLOCK), per_b4),
            pl.BlockSpec((1, lc, 128), per_b),
            pl.BlockSpec((1, lc // A_BLOCK, 128, A_BLOCK), per_b4),
            pl.BlockSpec((1, ts, 256), tok),
            pl.BlockSpec((1, t, 256), per_b),
            pl.BlockSpec((ts, 256), lambda bi, i: (i, 0)),
            _layer_spec(layer, (256, 256)),
            _layer_spec(layer, (1, 256)),
            pl.BlockSpec((1, ts, 256), tok),
            pl.BlockSpec((1, t, 256), per_b),
            pl.BlockSpec((1, t // tq, 256, tq), per_b4),
            pl.BlockSpec((1, lc, 256), per_b),
            pl.BlockSpec((1, lc // tq, 256, tq), per_b4),
            _layer_spec(layer, (D_HEADS, PAIR_MASKED + 1, GRID_W, 2 * GRID_W)),
            pl.BlockSpec((3, 3 * A_BLOCK, 2 * A_BLOCK), lambda bi, i: (0, 0, 0)),
            _layer_spec(layer, (d, d)),
        ],
        out_specs=pl.BlockSpec((1, ts, d), tok),
        scratch_shapes=[pltpu.VMEM((n_sub, GROUP_W, tq), F32), pltpu.VMEM((n_sub, GROUP_W, tq), F32),
                        pltpu.VMEM((n_sub, tq, d), BF16)],
        compiler_params=pltpu.CompilerParams(
            dimension_semantics=("arbitrary", "arbitrary"), vmem_limit_bytes=VMEM_LIMIT),
        name="mixer_latent" if local else "mixer_ctx",
    )(sink, x, mod, qa, ka, va, kca, vca, ob, cin, inv_cnt, wpool, cscale, qd, kd, vd, kcd, vcd, bias, amask,
      w_out)


def _ffn_kernel(x_ref, sh_ref, sc_ref, g2_ref, gn_ref, wg_ref, wu_ref, wd_ref, o_ref, act_scr):
    xt = x_ref[0]
    ms = jnp.mean(xt * xt, axis=-1, keepdims=True)
    h = xt * lax.rsqrt(ms + NORM_EPS) * gn_ref[...]
    h = (h * (1.0 + sc_ref[...]) + sh_ref[...]).astype(BF16)
    for c in range(FF_DIM // FFN_TF):
        cols = slice(c * FFN_TF, (c + 1) * FFN_TF)
        gate = _dot(h, wg_ref[:, cols])
        up = _dot(h, wu_ref[:, cols])
        act_scr[:, cols] = (gate * jax.nn.sigmoid(gate) * up).astype(BF16)
    o_ref[0] = xt + g2_ref[...] * _dot(act_scr[...], wd_ref[...])


def _ffn(x, mod, row, layer, g_ffn, wg, wu, wd):
    b, t, d = x.shape
    tm = min(FFN_TM, t)
    tok = lambda bi, i: (bi, i, 0)
    per_layer = lambda bi, i: (layer, 0, 0)
    resident = pl.Buffered(1)
    return pl.pallas_call(
        _ffn_kernel,
        out_shape=jax.ShapeDtypeStruct((b, t, d), F32),
        grid=(b, t // tm),
        in_specs=[
            pl.BlockSpec((1, tm, d), tok),
            _mod_spec(layer, row, 3),
            _mod_spec(layer, row, 4),
            _mod_spec(layer, row, 5),
            _layer_spec(layer, (1, d)),
            pl.BlockSpec((None, d, FF_DIM), per_layer, pipeline_mode=resident),
            pl.BlockSpec((None, d, FF_DIM), per_layer, pipeline_mode=resident),
            pl.BlockSpec((None, FF_DIM, d), per_layer, pipeline_mode=resident),
        ],
        out_specs=pl.BlockSpec((1, tm, d), tok),
        scratch_shapes=[pltpu.VMEM((tm, FF_DIM), BF16)],
        compiler_params=pltpu.CompilerParams(
            dimension_semantics=("arbitrary", "arbitrary"), vmem_limit_bytes=VMEM_LIMIT),
        name="ffn",
    )(x, mod, mod, mod, g_ffn, wg, wu, wd)


def _rope_tables(s):
    t = np.arange(s)
    half = 16
    inv = np.power(np.float32(ROPE_BASE), -np.arange(half, dtype=np.float32) / half).astype(np.float32)
    ang_r = (t // GRID_W).astype(np.float32)[:, None] * inv[None, :]
    ang_c = (t % GRID_W).astype(np.float32)[:, None] * inv[None, :]
    cos = np.concatenate([np.cos(ang_r)] * 2 + [np.cos(ang_c)] * 2, axis=-1)
    sin = np.concatenate([-np.sin(ang_r), np.sin(ang_r), -np.sin(ang_c), np.sin(ang_c)], axis=-1)
    return cos.astype(np.float32), sin.astype(np.float32)


def _pool_inverse_counts(n_tok):
    t = np.arange(n_tok)
    cols = []
    for w in C_POOLS:
        lo = np.clip(t - w // 2, 0, n_tok)
        hi = np.clip(t - w // 2 + w, 0, n_tok)
        cols.append(np.repeat((np.float32(1.0) / (hi - lo).astype(np.float32))[:, None], C_GROUP_W, axis=1))
    return jnp.asarray(np.concatenate(cols, axis=1), F32)


def _window_mask():
    kk = np.arange(3 * A_BLOCK)[:, None]
    qq = np.arange(2 * A_BLOCK)[None, :] % A_BLOCK
    kinds = [np.where(np.abs(kk - qq - back * A_BLOCK) <= A_WINDOW, 0.0, NEG_INF) for back in (1, 0, 2)]
    return jnp.asarray(np.stack(kinds), F32)


PAIR_COUNT = 2 * (D_WIN_ROWS - 1)
PAIR_INTERIOR = 0
PAIR_EDGE = PAIR_COUNT
PAIR_MASKED = 2 * PAIR_COUNT


def _neighbour_bias_pairs(rpb):
    cidx = np.arange(GRID_W)
    col_start = np.clip(cidx - D_WIN_COLS // 2, 0, GRID_W - D_WIN_COLS)
    col_ok_t = ((cidx[None, :] >= col_start[:, None]) & (cidx[None, :] < col_start[:, None] + D_WIN_COLS)).T
    coff_t = np.clip(cidx[:, None] - cidx[None, :], -(D_WIN_COLS - 1), D_WIN_COLS - 1) + (D_WIN_COLS - 1)
    select = (np.arange(2 * D_WIN_COLS - 1)[:, None, None] == coff_t[None]).astype(np.float32)
    blocks = lax.dot_general(rpb.astype(F32), jnp.asarray(select), (((3,), (0,)), ((), ())),
                             precision=lax.Precision.HIGHEST)
    blocks = jnp.where(col_ok_t, blocks, NEG_INF)
    d = np.arange(-(D_WIN_ROWS - 1), D_WIN_ROWS)
    seen = (d >= -(D_WIN_ROWS // 2)) & (d < D_WIN_ROWS // 2)
    both = jnp.stack([jnp.where(seen[:, None, None], blocks, NEG_INF), blocks], axis=2)
    pairs = jnp.concatenate([both[:, :, :, 1:], both[:, :, :, :-1]], axis=-1)
    table = pairs.reshape(pairs.shape[:2] + (2 * PAIR_COUNT, GRID_W, 2 * GRID_W))
    no_pad = (0, 0, 0)
    return lax.pad(table, jnp.asarray(NEG_INF, F32), (no_pad, no_pad, (0, 1, 0), no_pad, no_pad))


def kernel(x, c, ctx, c_ctx, w_mod, b_mod, g_mix, g_ffn, w_in, w_out, a_q_gain, a_k_gain, a_sink,
           b_v_gain, b_w_s, b_b_s, c_w_pool, c_scale, d_q_gain, d_k_gain, d_rpb, w_gate, w_up, w_down):
    bsz, s, d = x.shape
    lc = ctx.shape[1]
    n_rows = s // GRID_W
    assert s % MIX_TQ == 0 and n_rows >= 2 * D_WIN_ROWS and MIX_TQ // GRID_W == D_WIN_ROWS // 2

    cvec = jnp.zeros((MOD_ROWS, d), F32).at[:bsz].set(c).at[bsz].set(c_ctx)
    mod = _modulation(cvec, w_mod, b_mod)
    lat_row = lambda bi: bi
    ctx_row = lambda bi: bsz

    seg = jnp.asarray(np.kron(np.eye(4), np.full((64, 64), 1.0 / 64)), BF16)
    cos1, sin1 = _rope_tables(s)
    tables_lat = tuple(jnp.asarray(np.tile(tab, (1, reps)))
                       for tab, reps in ((cos1, 4), (sin1, 4), (cos1, 2), (sin1, 2)))
    tables_ctx = tuple(jnp.zeros((min(IN_TM, bsz * lc), w), F32) for w in (256, 256, 128, 128))

    amask = _window_mask()
    inv_lat, inv_ctx = _pool_inverse_counts(s), _pool_inverse_counts(lc)
    bias_all = _neighbour_bias_pairs(d_rpb * LOG2E)

    w_in_b = w_in.astype(BF16)
    w_out_b = w_out.astype(BF16)
    wg_b, wu_b, wd_b = w_gate.astype(BF16), w_up.astype(BF16), w_down.astype(BF16)
    gmix = g_mix.reshape(DEPTH, 1, d)
    gffn = g_ffn.reshape(DEPTH, 1, d)
    aqg = jnp.tile(a_q_gain, (1, 4)).reshape(DEPTH, 1, 256)
    akg = jnp.tile(a_k_gain, (1, 2)).reshape(DEPTH, 1, 128)
    dqg = jnp.tile(d_q_gain, (1, 4)).reshape(DEPTH, 1, 256)
    dkg = jnp.tile(d_k_gain, (1, 4)).reshape(DEPTH, 1, 256)
    bvg = b_v_gain.reshape(DEPTH, 1, 256)
    ws_cat = jnp.transpose(b_w_s, (0, 2, 1, 3)).reshape(DEPTH, B_CHUNK, B_GROUPS * B_CHUNK).astype(BF16)
    bs_t = jnp.repeat(jnp.swapaxes(b_b_s, 1, 2), B_GROUP_W, axis=2)
    wpool = jnp.einsum('lgcd,gh->lgchd', c_w_pool, jnp.eye(len(C_POOLS), dtype=F32)
                       ).reshape(DEPTH, GROUP_W, GROUP_W).astype(BF16)
    cscale = c_scale.reshape(DEPTH, 1, 256)
    sink = a_sink * LOG2E
    common = (gmix, w_in_b, seg, aqg, akg, dqg, dkg)

    xc = ctx
    for l in range(DEPTH):
        last = l == DEPTH - 1
        lat = _in_projection(x, mod, lat_row, l, *common, tables_lat, bvg, ws_cat, bs_t, use_rope=True)
        xc_flat = xc.reshape(1, bsz * lc, d)
        per_batch = lambda o: o.reshape((bsz, o.shape[1] // bsz) + o.shape[2:])
        if last:
            cka, cva, ckd, cvd = map(per_batch, _ctx_kv_projection(
                xc_flat, mod, ctx_row, l, gmix, w_in_b, seg, akg, dkg))
        else:
            cqa, cka, cva, cob, ccin, cqd, ckd, cvd = map(per_batch, _in_projection(
                xc_flat, mod, ctx_row, l, *common, tables_ctx, bvg, ws_cat, bs_t, use_rope=False))
        qa, ka, va, ob, cin, qd, kd, vd = lat

        x = _mixer(x, mod, lat_row, l, qa, ka, va, cka, cva, ob, cin, inv_lat, wpool, cscale, qd, kd, vd, ckd, cvd,
                   bias_all, amask, w_out_b, sink, local=True)
        x = _ffn(x, mod, lat_row, l, gffn, wg_b, wu_b, wd_b)
        if not last:
            xc = _mixer(xc, mod, ctx_row, l, cqa, cka, cva, cka, cva, cob, ccin, inv_ctx, wpool, cscale, cqd, ckd,
                        cvd, ckd, cvd, bias_all, amask, w_out_b, sink, local=False)
            xc = _ffn(xc.reshape(1, bsz * lc, d), mod, ctx_row, l, gffn, wg_b, wu_b, wd_b).reshape(bsz, lc, d)
    return x
```

```python
import functools

import jax
import jax.numpy as jnp
import numpy as np
from jax import lax
from jax.experimental import pallas as pl
from jax.experimental.pallas import tpu as pltpu

F32 = jnp.float32
BF16 = jnp.bfloat16

D_MODEL = 1024
DEPTH = 2
GRID_W = 64
HEAD_DIM = 64
GROUP_W = 256
A_HEADS = 4
A_KV_HEADS = 2
A_WINDOW = 128
A_BLOCK = 128
B_CHUNK = 128
B_GROUPS = 4
B_GROUP_W = 64
C_POOLS = (2, 4, 8, 16)
C_GROUP_W = 64
D_HEADS = 4
D_WIN_ROWS = 8
D_WIN_COLS = 16
FF_DIM = 2816
ROPE_BASE = 10000.0
NORM_EPS = 1e-6
NEG_INF = -1e30
LOG2E = 1.4426950408889634
IN_TOTAL = 2048
OFF_AQ, OFF_AK, OFF_AV, OFF_BU, OFF_BV, OFF_C, OFF_DQ, OFF_DK, OFF_DV = (
    0, 256, 384, 512, 768, 1024, 1280, 1536, 1792)

MOD_ROWS = 16
MOD_TN = D_MODEL
IN_TM = 512
MIX_TQ = 256
FFN_TM = 1024
FFN_TF = 256
BF16_SUBLANES = 16
POOL_HALO = 8
MIX_STEP_TILES = 4
OUT_CHUNK = 256
SCORE_LOOKAHEAD = 2
VMEM_LIMIT = 56 * 1024 * 1024


def _dot(a, b):
    return jnp.dot(a, b, preferred_element_type=F32)


def _dot_nt(a, b):
    return lax.dot_general(a, b, (((1,), (1,)), ((), ())), preferred_element_type=F32)


def _norm_modulate(xt, gain, scale, shift):
    ms = jnp.mean(xt * xt, axis=-1, keepdims=True)
    return (xt * lax.rsqrt(ms + NORM_EPS) * (gain * (1.0 + scale)) + shift).astype(BF16)


def _mod_kernel(c_ref, w_ref, b_ref, o_ref):
    cv = c_ref[...]
    act = cv * jax.nn.sigmoid(cv)
    o_ref[:, 0, :] = _dot(act.astype(BF16), w_ref[0].astype(BF16)) + b_ref[0]


def _modulation(cvec, w_mod, b_mod):
    depth, d, n = w_mod.shape
    return pl.pallas_call(
        _mod_kernel,
        out_shape=jax.ShapeDtypeStruct((depth, MOD_ROWS, n // MOD_TN, 1, MOD_TN), F32),
        grid=(depth, n // MOD_TN),
        in_specs=[
            pl.BlockSpec((MOD_ROWS, d), lambda l, j: (0, 0)),
            pl.BlockSpec((1, d, MOD_TN), lambda l, j: (l, 0, j)),
            pl.BlockSpec((1, 1, MOD_TN), lambda l, j: (l, 0, j)),
        ],
        out_specs=pl.BlockSpec((None, MOD_ROWS, None, 1, MOD_TN), lambda l, j: (l, 0, j, 0, 0)),
        compiler_params=pltpu.CompilerParams(
            dimension_semantics=("arbitrary", "arbitrary"), vmem_limit_bytes=VMEM_LIMIT),
        name="modulation",
    )(cvec, w_mod, b_mod.reshape(depth, 1, n))


def _head_norm(t, seg, gain):
    sq = t * t
    hi = sq.astype(BF16)
    lo = (sq - hi.astype(F32)).astype(BF16)
    ms = _dot(hi, seg) + _dot(lo, seg)
    return t * lax.rsqrt(ms + NORM_EPS) * gain


def _rope(y, cos, sin_signed):
    w = y.shape[-1]
    up = pltpu.roll(y, w - 16, axis=1)
    dn = pltpu.roll(y, 16, axis=1)
    lane = lax.broadcasted_iota(jnp.int32, y.shape, 1)
    swapped = jnp.where((lane & 31) < 16, up, dn)
    return y * cos + swapped * sin_signed


def _swap_middle_heads(q):
    lane = lax.broadcasted_iota(jnp.int32, q.shape, 1)
    from_right = pltpu.roll(q, 3 * HEAD_DIM, axis=1)
    from_left = pltpu.roll(q, HEAD_DIM, axis=1)
    return jnp.where((lane >= HEAD_DIM) & (lane < 2 * HEAD_DIM), from_right,
                     jnp.where((lane >= 2 * HEAD_DIM) & (lane < 3 * HEAD_DIM), from_left, q))


def _in_kernel(x_ref, sh_ref, sc_ref, g_ref, w_ref, seg_ref, aqg_ref, akg_ref, dqg_ref, dkg_ref,
               cosq_ref, sinq_ref, cosk_ref, sink_ref, bvg_ref, ws_ref, bs_ref,
               qa_ref, ka_ref, va_ref, ob_ref, c_ref, qd_ref, kd_ref, vd_ref, v_scr, *, use_rope):
    xt = x_ref[0]
    h = _norm_modulate(xt, g_ref[...], sc_ref[...], sh_ref[...])

    group_w = IN_TOTAL // 4
    assert (OFF_BU, OFF_C, OFF_DK) == (group_w, 2 * group_w, 3 * group_w)

    def group(k):
        return _dot(h, w_ref[:, k * group_w:(k + 1) * group_w])

    seg = seg_ref[...]
    scale = HEAD_DIM ** -0.5 * LOG2E
    tm = xt.shape[0]

    def finish_a(p):
        qa = _head_norm(p[:, 0:256], seg, aqg_ref[...])
        ka = _head_norm(p[:, 256:384], seg[:128, :128], akg_ref[...])
        if use_rope:
            qa = _rope(qa, cosq_ref[...], sinq_ref[...])
            ka = _rope(ka, cosk_ref[...], sink_ref[...])
        qa_ref[0] = (_swap_middle_heads(qa) * scale).astype(BF16)
        ka_ref[0] = ka.astype(BF16)
        v_scr[:, 0:128] = p[:, 384:512]
        vat = v_scr[:, 0:128].T.astype(BF16)
        for j in range(tm // A_BLOCK):
            va_ref[0, j] = vat[:, j * A_BLOCK:(j + 1) * A_BLOCK]

    def finish_b(p):
        u = jax.nn.gelu(p[:, 0:256])
        v = jax.nn.gelu(p[:, 256:512])
        mu = jnp.mean(v, axis=-1, keepdims=True)
        vc = v - mu
        var = jnp.mean(vc * vc, axis=-1, keepdims=True)
        vn = (vc * lax.rsqrt(var + NORM_EPS) * bvg_ref[...]).astype(BF16)
        lane = lax.broadcasted_iota(jnp.int32, (B_CHUNK, GROUP_W), 1)
        zero = jnp.zeros((B_CHUNK, GROUP_W), BF16)
        for ch in range(tm // B_CHUNK):
            rows = slice(ch * B_CHUNK, (ch + 1) * B_CHUNK)
            vch = vn[rows]
            stacked = jnp.concatenate(
                [jnp.where((lane >= g * B_GROUP_W) & (lane < (g + 1) * B_GROUP_W), vch, zero)
                 for g in range(B_GROUPS)], axis=0)
            z = _dot(ws_ref[...], stacked) + bs_ref[...]
            ob_ref[0, rows, :] = (u[rows] * z).astype(BF16)

    def finish_c(p):
        c_ref[0] = p[:, 0:256]
        qd = _head_norm(p[:, 256:512], seg, dqg_ref[...])
        qd_ref[0] = (qd * scale).astype(BF16)

    def finish_d(p):
        kd_ref[0] = _head_norm(p[:, 0:256], seg, dkg_ref[...]).astype(BF16)
        v_scr[:, 128:384] = p[:, 256:512]
        vdt = v_scr[:, 128:384].T.astype(BF16)
        for j in range(tm // MIX_TQ):
            vd_ref[0, j] = vdt[:, j * MIX_TQ:(j + 1) * MIX_TQ]

    order = ((1, finish_b), (0, finish_a), (3, finish_d), (2, finish_c))
    pending = group(order[0][0])
    for pos, (_, finish) in enumerate(order):
        upcoming = group(order[pos + 1][0]) if pos + 1 < len(order) else None
        finish(pending)
        pending = upcoming


def _mod_spec(layer, row, which, batch_axis=0):
    return pl.BlockSpec((None, None, None, 1, D_MODEL),
                        lambda *idx: (layer, row(idx[batch_axis]), which, 0, 0))


def _layer_spec(layer, shape):
    return pl.BlockSpec((None,) + tuple(shape), lambda *_: (layer,) + (0,) * len(shape))


def _in_projection(x, mod, row, layer, g_mix, w_in, seg, aqg, akg, dqg, dkg, tables, bvg, ws_cat, bs_t,
                   *, use_rope):
    b, t, d = x.shape
    cosq, sinq, cosk, sink = tables
    tm = min(IN_TM, t)
    tok = lambda i, bi: (bi, i, 0)
    const2 = lambda i, bi: (0, 0)
    pos2 = (lambda i, bi: (i, 0)) if use_rope else const2
    assert tm % MIX_TQ == 0 and MIX_TQ % A_BLOCK == 0
    out_shapes = (
        jax.ShapeDtypeStruct((b, t, 256), BF16),
        jax.ShapeDtypeStruct((b, t, 128), BF16),
        jax.ShapeDtypeStruct((b, t // A_BLOCK, 128, A_BLOCK), BF16),
        jax.ShapeDtypeStruct((b, t, 256), BF16),
        jax.ShapeDtypeStruct((b, t, 256), F32),
        jax.ShapeDtypeStruct((b, t, 256), BF16),
        jax.ShapeDtypeStruct((b, t, 256), BF16),
        jax.ShapeDtypeStruct((b, t // MIX_TQ, 256, MIX_TQ), BF16),
    )
    blk4 = lambda i, bi: (bi, i, 0, 0)
    out_specs = tuple(
        pl.BlockSpec((1, tm // A_BLOCK, 128, A_BLOCK), blk4) if k == 2 else
        pl.BlockSpec((1, tm // MIX_TQ, 256, MIX_TQ), blk4) if k == 7 else
        pl.BlockSpec((1, tm, s.shape[-1]), tok)
        for k, s in enumerate(out_shapes))
    return pl.pallas_call(
        functools.partial(_in_kernel, use_rope=use_rope),
        out_shape=out_shapes,
        grid=(t // tm, b),
        in_specs=[
            pl.BlockSpec((1, tm, d), tok),
            _mod_spec(layer, row, 0, batch_axis=1),
            _mod_spec(layer, row, 1, batch_axis=1),
            _layer_spec(layer, (1, d)),
            _layer_spec(layer, (d, IN_TOTAL)),
            pl.BlockSpec((256, 256), const2),
            _layer_spec(layer, (1, 256)),
            _layer_spec(layer, (1, 128)),
            _layer_spec(layer, (1, 256)),
            _layer_spec(layer, (1, 256)),
            pl.BlockSpec((tm, 256), pos2),
            pl.BlockSpec((tm, 256), pos2),
            pl.BlockSpec((tm, 128), pos2),
            pl.BlockSpec((tm, 128), pos2),
            _layer_spec(layer, (1, 256)),
            _layer_spec(layer, (B_CHUNK, B_GROUPS * B_CHUNK)),
            _layer_spec(layer, (B_CHUNK, GROUP_W)),
        ],
        out_specs=out_specs,
        scratch_shapes=[pltpu.VMEM((tm, 384), F32)],
        compiler_params=pltpu.CompilerParams(
            dimension_semantics=("arbitrary", "arbitrary"), vmem_limit_bytes=VMEM_LIMIT),
        name="in_projection_rope" if use_rope else "in_projection_ctx",
    )(x, mod, mod, g_mix, w_in, seg, aqg, akg, dqg, dkg, cosq, sinq, cosk, sink, bvg, ws_cat, bs_t)


def _ctx_kv_kernel(x_ref, sh_ref, sc_ref, g_ref, w_ref, seg_ref, akg_ref, dkg_ref,
                   ka_ref, va_ref, kd_ref, vd_ref, v_scr):
    xt = x_ref[0]
    tm = xt.shape[0]
    h = _norm_modulate(xt, g_ref[...], sc_ref[...], sh_ref[...])
    seg = seg_ref[...]
    pa = _dot(h, w_ref[:, OFF_AK:OFF_BU])
    pd = _dot(h, w_ref[:, OFF_DK:IN_TOTAL])
    ka_ref[0] = _head_norm(pa[:, 0:128], seg[:128, :128], akg_ref[...]).astype(BF16)
    kd_ref[0] = _head_norm(pd[:, 0:256], seg, dkg_ref[...]).astype(BF16)
    v_scr[:, 0:128] = pa[:, 128:256]
    v_scr[:, 128:384] = pd[:, 256:512]
    vat = v_scr[:, 0:128].T.astype(BF16)
    for j in range(tm // A_BLOCK):
        va_ref[0, j] = vat[:, j * A_BLOCK:(j + 1) * A_BLOCK]
    vdt = v_scr[:, 128:384].T.astype(BF16)
    for j in range(tm // MIX_TQ):
        vd_ref[0, j] = vdt[:, j * MIX_TQ:(j + 1) * MIX_TQ]


def _ctx_kv_projection(x, mod, row, layer, g_mix, w_in, seg, akg, dkg):
    b, t, d = x.shape
    tm = min(IN_TM, t)
    tok = lambda i, bi: (bi, i, 0)
    blk4 = lambda i, bi: (bi, i, 0, 0)
    out_shapes = (
        jax.ShapeDtypeStruct((b, t, 128), BF16),
        jax.ShapeDtypeStruct((b, t // A_BLOCK, 128, A_BLOCK), BF16),
        jax.ShapeDtypeStruct((b, t, 256), BF16),
        jax.ShapeDtypeStruct((b, t // MIX_TQ, 256, MIX_TQ), BF16),
    )
    return pl.pallas_call(
        _ctx_kv_kernel,
        out_shape=out_shapes,
        grid=(t // tm, b),
        in_specs=[
            pl.BlockSpec((1, tm, d), tok),
            _mod_spec(layer, row, 0, batch_axis=1),
            _mod_spec(layer, row, 1, batch_axis=1),
            _layer_spec(layer, (1, d)),
            _layer_spec(layer, (d, IN_TOTAL)),
            pl.BlockSpec((256, 256), lambda i, bi: (0, 0)),
            _layer_spec(layer, (1, 128)),
            _layer_spec(layer, (1, 256)),
        ],
        out_specs=(pl.BlockSpec((1, tm, 128), tok),
                   pl.BlockSpec((1, tm // A_BLOCK, 128, A_BLOCK), blk4),
                   pl.BlockSpec((1, tm, 256), tok),
                   pl.BlockSpec((1, tm // MIX_TQ, 256, MIX_TQ), blk4)),
        scratch_shapes=[pltpu.VMEM((tm, 384), F32)],
        compiler_params=pltpu.CompilerParams(
            dimension_semantics=("arbitrary", "arbitrary"), vmem_limit_bytes=VMEM_LIMIT),
        name="ctx_kv_projection",
    )(x, mod, mod, g_mix, w_in, seg, akg, dkg)


def _softmax_pv_t(parts, extra_logit):
    m = None
    for s, _ in parts:
        pm = jnp.max(s, axis=0, keepdims=True)
        m = pm if m is None else jnp.maximum(m, pm)
    if extra_logit is not None:
        m = jnp.maximum(m, extra_logit)
    acc = None
    for s, vt in parts:
        dh, n = vt.shape
        vt_ones = jnp.concatenate([vt, jnp.ones((BF16_SUBLANES, n), BF16)], axis=0)
        pv = _dot(vt_ones, jnp.exp2(s - m).astype(BF16))
        acc = pv if acc is None else acc + pv
    denom = acc[dh:dh + 1]
    if extra_logit is not None:
        denom = denom + jnp.exp2(extra_logit - m)
    return acc[:dh] / denom


def _head_lanes(q, head, width=HEAD_DIM):
    lane = lax.broadcasted_iota(jnp.int32, q.shape, 1)
    return jnp.where((lane >= head * width) & (lane < (head + 1) * width), q, jnp.zeros_like(q))


def _pool_tile(c_ref, inv, t0, n_tok, tq):
    y = c_ref[0, pl.ds(t0, tq), :]
    lo_start = pl.multiple_of(jnp.maximum(t0 - POOL_HALO, 0), POOL_HALO)
    hi_start = pl.multiple_of(jnp.minimum(t0 + tq, n_tok - POOL_HALO), POOL_HALO)
    lo = c_ref[0, pl.ds(lo_start, POOL_HALO), :]
    hi = c_ref[0, pl.ds(hi_start, POOL_HALO), :]
    lo = jnp.where(t0 > 0, lo, 0.0)
    hi = jnp.where(t0 + tq < n_tok, hi, 0.0)
    ypad = jnp.concatenate([lo, y, hi], axis=0)

    n = tq + 2 * POOL_HALO
    w2 = ypad + pltpu.roll(ypad, 1, axis=0)
    w4 = pltpu.roll(w2, 1, axis=0) + pltpu.roll(w2, n - 1, axis=0)
    w8 = pltpu.roll(w4, 2, axis=0) + pltpu.roll(w4, n - 2, axis=0)
    w16 = pltpu.roll(w8, 4, axis=0) + pltpu.roll(w8, n - 4, axis=0)
    sums = tuple(w[POOL_HALO:POOL_HALO + tq] for w in (w2, w4, w8, w16))

    lane = lax.broadcasted_iota(jnp.int32, (tq, GROUP_W), 1)
    total = sums[0]
    for gi in range(1, len(C_POOLS)):
        total = jnp.where(lane >= gi * C_GROUP_W, sums[gi], total)
    return total * inv - y


def _mix_kernel(*refs, local, n_tok, layer):
    x_ref = refs[1]
    deferred = []
    for sub in range(x_ref.shape[1] // MIX_TQ):
        _mix_subtile(*refs, sub=sub, deferred=deferred, local=local, n_tok=n_tok, layer=layer)
    while deferred:
        deferred.pop(0)()


def _mix_subtile(sink_ref, x_ref, g1_ref, qa_ref, ka_ref, va_ref, kca_ref, vca_ref, ob_ref, c_ref,
                 inv_ref, wp_ref, cs_ref, qd_ref, kd_ref, vd_ref, kcd_ref, vcd_ref, bias_ref, amask_ref, wo_ref,
                 o_ref, oa_scr, od_scr, mix_scr, *, sub, deferred, local, n_tok, layer):
    tq = MIX_TQ
    ti = pl.program_id(1) * (x_ref.shape[1] // tq) + sub
    t0 = pl.multiple_of(ti * tq, tq)
    tile = slice(sub * tq, (sub + 1) * tq)

    pooled = _pool_tile(c_ref, inv_ref[tile, :], t0, n_tok, tq)
    out_c = (_dot(pooled.astype(BF16), wp_ref[...]) * cs_ref[...]).astype(BF16)

    if local:
        n_tiles = n_tok // tq
        half = D_WIN_ROWS // 2
        pair_entries = []
        for jj in range(3 * tq // GRID_W):
            row_entries = []
            for i in range(0, tq // GRID_W, 2):
                interior = PAIR_INTERIOR + (jj - half - i) + PAIR_COUNT // 2 - 1
                first = PAIR_EDGE + (jj - i) + PAIR_COUNT // 2 - 1 if jj < D_WIN_ROWS else PAIR_MASKED
                last = (PAIR_EDGE + (jj - D_WIN_ROWS - i) + PAIR_COUNT // 2 - 1 if jj >= half
                        else PAIR_MASKED)
                row_entries.append(jnp.where(ti == 0, first, jnp.where(ti == n_tiles - 1, last, interior)))
            pair_entries.append(row_entries)

    n_ctx_blk = vca_ref.shape[1]
    jobs = []

    def a_job(blk):
        cols = slice(blk * A_BLOCK, (blk + 1) * A_BLOCK)

        def scores():
            qblk = qa_ref[0, sub * tq + blk * A_BLOCK:sub * tq + (blk + 1) * A_BLOCK, :]
            qcat = jnp.concatenate([qblk[:, :128], qblk[:, 128:]], axis=0)
            qm = jnp.concatenate([_head_lanes(qcat, kv) for kv in range(A_KV_HEADS)], axis=0)
            if local:
                nblk = ti * (tq // A_BLOCK) + blk
                b0 = jnp.clip(nblk - 1, 0, n_tok // A_BLOCK - 3)
                kstart = pl.multiple_of(b0 * A_BLOCK, A_BLOCK)
                kwin = ka_ref[0, pl.ds(kstart, 3 * A_BLOCK), :]
                last_blk = n_tok // A_BLOCK - 1
                kind = jnp.where(nblk == 0, 1, jnp.where(nblk == last_blk, 2, 0))
                s_loc = _dot_nt(kwin, qm)
            s_ctx = _dot_nt(kca_ref[0], qm)
            per_head = []
            for kv in range(A_KV_HEADS):
                rows = slice(kv * HEAD_DIM, (kv + 1) * HEAD_DIM)
                qsl = slice(kv * 2 * A_BLOCK, (kv + 1) * 2 * A_BLOCK)
                parts = []
                if local:
                    s_win = s_loc[:, qsl] + amask_ref[kind]
                    for j in range(3):
                        parts.append((s_win[j * A_BLOCK:(j + 1) * A_BLOCK], va_ref[0, b0 + j, rows, :]))
                for j in range(n_ctx_blk):
                    parts.append((s_ctx[j * A_BLOCK:(j + 1) * A_BLOCK, qsl], vca_ref[0, j, rows, :]))
                per_head.append(parts)
            return per_head

        def finish(per_head):
            for kv, parts in enumerate(per_head):
                col = lax.broadcasted_iota(jnp.int32, (1, 2 * A_BLOCK), 1)
                sink = jnp.where(col < A_BLOCK, sink_ref[layer, 2 * kv], sink_ref[layer, 2 * kv + 1])
                o2 = _softmax_pv_t(parts, sink)
                oa_scr[sub, (2 * kv) * 64:(2 * kv + 1) * 64, cols] = o2[:, :A_BLOCK]
                oa_scr[sub, (2 * kv + 1) * 64:(2 * kv + 2) * 64, cols] = o2[:, A_BLOCK:]

        return scores, finish

    def d_job(pair):
        heads = (2 * pair, 2 * pair + 1)

        def scores():
            qd = qd_ref[0, tile, :]
            qm = jnp.concatenate([_head_lanes(qd, hd) for hd in heads], axis=0)
            if local:
                d0 = jnp.clip(ti - 1, 0, n_tok // tq - 3)
                kwin = kd_ref[0, pl.ds(pl.multiple_of(d0 * tq, tq), 3 * tq), :]
                s_loc = _dot_nt(kwin, qm)
            s_ctx = _dot_nt(kcd_ref[0], qm)
            per_head = []
            for k, hd in enumerate(heads):
                rows = slice(hd * HEAD_DIM, (hd + 1) * HEAD_DIM)
                qsl = slice(k * tq, (k + 1) * tq)
                parts = []
                if local:
                    bias = jnp.concatenate(
                        [jnp.concatenate([bias_ref[hd, e] for e in row_entries], axis=1)
                         for row_entries in pair_entries], axis=0)
                    s_win = s_loc[:, qsl] + bias
                    for j in range(3):
                        parts.append((s_win[j * tq:(j + 1) * tq], vd_ref[0, d0 + j, rows, :]))
                parts.append((s_ctx[:, qsl], vcd_ref[0, 0, rows, :]))
                per_head.append(parts)
            return per_head

        def finish(per_head):
            for hd, parts in zip(heads, per_head):
                od_scr[sub, hd * HEAD_DIM:(hd + 1) * HEAD_DIM, :] = _softmax_pv_t(parts, None)

        return scores, finish

    for blk in range(tq // A_BLOCK):
        jobs.append(a_job(blk))
    for pair in range(D_HEADS // 2):
        jobs.append(d_job(pair))
    ahead = min(SCORE_LOOKAHEAD, len(jobs))
    pending = [jobs[j][0]() for j in range(ahead)]
    for j, (_, finish) in enumerate(jobs):
        if j + ahead < len(jobs):
            pending.append(jobs[j + ahead][0]())
        if deferred:
            deferred.pop(0)()
        finish(pending.pop(0))

    mix_scr[sub, :, 0:256] = oa_scr[sub].T.astype(BF16)
    mix_scr[sub, :, 256:512] = ob_ref[0, tile, :]
    mix_scr[sub, :, 512:768] = out_c
    mix_scr[sub, :, 768:1024] = od_scr[sub].T.astype(BF16)

    def out_chunk(c):
        cols = slice(c * OUT_CHUNK, (c + 1) * OUT_CHUNK)

        def run():
            mix = _dot(mix_scr[sub], wo_ref[:, cols])
            o_ref[0, tile, cols] = x_ref[0, tile, cols] + g1_ref[:, cols] * mix

        return run

    deferred.extend(out_chunk(c) for c in range(D_MODEL // OUT_CHUNK))


def _mixer(x, mod, row, layer, qa, ka, va, kca, vca, ob, cin, inv_cnt, wpool, cscale, qd, kd, vd, kcd, vcd, bias,
           amask, w_out, sink, *, local):
    b, t, d = x.shape
    lc = kca.shape[1]
    tq = MIX_TQ
    ts = min(MIX_STEP_TILES * tq, t)
    n_sub = ts // tq
    tok = lambda bi, i: (bi, i, 0)
    per_b = lambda bi, i: (bi, 0, 0)
    per_b4 = lambda bi, i: (bi, 0, 0, 0)

    return pl.pallas_call(
        functools.partial(_mix_kernel, local=local, n_tok=t, layer=layer),
        out_shape=jax.ShapeDtypeStruct((b, t, d), F32),
        grid=(b, t // ts),
        in_specs=[
            pl.BlockSpec(memory_space=pltpu.SMEM),
            pl.BlockSpec((1, ts, d), tok),
            _mod_spec(layer, row, 2),
            pl.BlockSpec((1, ts, 256), tok),
            pl.BlockSpec((1, t, 128), per_b),
            pl.BlockSpec((1, t // A_BLOCK, 128, A_BLOCK), per_b4),
            pl.BlockSpec((1, lc, 128), per_b),
            pl.BlockSpec((1, lc // A_BLOCK, 128, A_BLOCK), per_b4),
            pl.BlockSpec((1, ts, 256), tok),
            pl.BlockSpec((1, t, 256), per_b),
            pl.BlockSpec((ts, 256), lambda bi, i: (i, 0)),
            _layer_spec(layer, (256, 256)),
            _layer_spec(layer, (1, 256)),
            pl.BlockSpec((1, ts, 256), tok),
            pl.BlockSpec((1, t, 256), per_b),
            pl.BlockSpec((1, t // tq, 256, tq), per_b4),
            pl.BlockSpec((1, lc, 256), per_b),
            pl.BlockSpec((1, lc // tq, 256, tq), per_b4),
            _layer_spec(layer, (D_HEADS, PAIR_MASKED + 1, GRID_W, 2 * GRID_W)),
            pl.BlockSpec((3, 3 * A_BLOCK, 2 * A_BLOCK), lambda bi, i: (0, 0, 0)),
            _layer_spec(layer, (d, d)),
        ],
        out_specs=pl.BlockSpec((1, ts, d), tok),
        scratch_shapes=[pltpu.VMEM((n_sub, GROUP_W, tq), F32), pltpu.VMEM((n_sub, GROUP_W, tq), F32),
                        pltpu.VMEM((n_sub, tq, d), BF16)],
        compiler_params=pltpu.CompilerParams(
            dimension_semantics=("arbitrary", "arbitrary"), vmem_limit_bytes=VMEM_LIMIT),
        name="mixer_latent" if local else "mixer_ctx",
    )(sink, x, mod, qa, ka, va, kca, vca, ob, cin, inv_cnt, wpool, cscale, qd, kd, vd, kcd, vcd, bias, amask,
      w_out)


def _ffn_kernel(x_ref, sh_ref, sc_ref, g2_ref, gn_ref, wg_ref, wu_ref, wd_ref, o_ref, act_scr):
    xt = x_ref[0]
    h = _norm_modulate(xt, gn_ref[...], sc_ref[...], sh_ref[...])
    for c in range(FF_DIM // FFN_TF):
        cols = slice(c * FFN_TF, (c + 1) * FFN_TF)
        gate = _dot(h, wg_ref[:, cols])
        up = _dot(h, wu_ref[:, cols])
        act_scr[:, cols] = (gate * jax.nn.sigmoid(gate) * up).astype(BF16)
    o_ref[0] = xt + g2_ref[...] * _dot(act_scr[...], wd_ref[...])


def _ffn(x, mod, row, layer, g_ffn, wg, wu, wd):
    b, t, d = x.shape
    tm = min(FFN_TM, t)
    tok = lambda bi, i: (bi, i, 0)
    per_layer = lambda bi, i: (layer, 0, 0)
    resident = pl.Buffered(1)
    return pl.pallas_call(
        _ffn_kernel,
        out_shape=jax.ShapeDtypeStruct((b, t, d), F32),
        grid=(b, t // tm),
        in_specs=[
            pl.BlockSpec((1, tm, d), tok),
            _mod_spec(layer, row, 3),
            _mod_spec(layer, row, 4),
            _mod_spec(layer, row, 5),
            _layer_spec(layer, (1, d)),
            pl.BlockSpec((None, d, FF_DIM), per_layer, pipeline_mode=resident),
            pl.BlockSpec((None, d, FF_DIM), per_layer, pipeline_mode=resident),
            pl.BlockSpec((None, FF_DIM, d), per_layer, pipeline_mode=resident),
        ],
        out_specs=pl.BlockSpec((1, tm, d), tok),
        scratch_shapes=[pltpu.VMEM((tm, FF_DIM), BF16)],
        compiler_params=pltpu.CompilerParams(
            dimension_semantics=("arbitrary", "arbitrary"), vmem_limit_bytes=VMEM_LIMIT),
        name="ffn",
    )(x, mod, mod, mod, g_ffn, wg, wu, wd)


def _rope_tables(s):
    t = np.arange(s)
    half = 16
    inv = np.power(np.float32(ROPE_BASE), -np.arange(half, dtype=np.float32) / half).astype(np.float32)
    ang_r = (t // GRID_W).astype(np.float32)[:, None] * inv[None, :]
    ang_c = (t % GRID_W).astype(np.float32)[:, None] * inv[None, :]
    cos = np.concatenate([np.cos(ang_r)] * 2 + [np.cos(ang_c)] * 2, axis=-1)
    sin = np.concatenate([-np.sin(ang_r), np.sin(ang_r), -np.sin(ang_c), np.sin(ang_c)], axis=-1)
    return cos.astype(np.float32), sin.astype(np.float32)


def _pool_inverse_counts(n_tok):
    t = np.arange(n_tok)
    cols = []
    for w in C_POOLS:
        lo = np.clip(t - w // 2, 0, n_tok)
        hi = np.clip(t - w // 2 + w, 0, n_tok)
        cols.append(np.repeat((np.float32(1.0) / (hi - lo).astype(np.float32))[:, None], C_GROUP_W, axis=1))
    return jnp.asarray(np.concatenate(cols, axis=1), F32)


def _window_mask():
    kk = np.arange(3 * A_BLOCK)[:, None]
    qq = np.arange(2 * A_BLOCK)[None, :] % A_BLOCK
    kinds = [np.where(np.abs(kk - qq - back * A_BLOCK) <= A_WINDOW, 0.0, NEG_INF) for back in (1, 0, 2)]
    return jnp.asarray(np.stack(kinds), F32)


PAIR_COUNT = 2 * (D_WIN_ROWS - 1)
PAIR_INTERIOR = 0
PAIR_EDGE = PAIR_COUNT
PAIR_MASKED = 2 * PAIR_COUNT


def _neighbour_bias_pairs(rpb):
    cidx = np.arange(GRID_W)
    col_start = np.clip(cidx - D_WIN_COLS // 2, 0, GRID_W - D_WIN_COLS)
    col_ok_t = ((cidx[None, :] >= col_start[:, None]) & (cidx[None, :] < col_start[:, None] + D_WIN_COLS)).T
    coff_t = np.clip(cidx[:, None] - cidx[None, :], -(D_WIN_COLS - 1), D_WIN_COLS - 1) + (D_WIN_COLS - 1)
    select = (np.arange(2 * D_WIN_COLS - 1)[:, None, None] == coff_t[None]).astype(np.float32)
    blocks = lax.dot_general(rpb.astype(F32), jnp.asarray(select), (((3,), (0,)), ((), ())),
                             precision=lax.Precision.HIGHEST)
    blocks = jnp.where(col_ok_t, blocks, NEG_INF)
    d = np.arange(-(D_WIN_ROWS - 1), D_WIN_ROWS)
    seen = (d >= -(D_WIN_ROWS // 2)) & (d < D_WIN_ROWS // 2)
    interior = jnp.where(seen[:, None, None], blocks, NEG_INF)
    pair = lambda t: jnp.concatenate([t[:, :, 1:], t[:, :, :-1]], axis=-1)
    masked = jnp.full(blocks.shape[:2] + (1, GRID_W, 2 * GRID_W), NEG_INF, F32)
    return jnp.concatenate([pair(interior), pair(blocks), masked], axis=2)


def kernel(x, c, ctx, c_ctx, w_mod, b_mod, g_mix, g_ffn, w_in, w_out, a_q_gain, a_k_gain, a_sink,
           b_v_gain, b_w_s, b_b_s, c_w_pool, c_scale, d_q_gain, d_k_gain, d_rpb, w_gate, w_up, w_down):
    bsz, s, d = x.shape
    lc = ctx.shape[1]
    n_rows = s // GRID_W
    assert s % MIX_TQ == 0 and n_rows >= 2 * D_WIN_ROWS and MIX_TQ // GRID_W == D_WIN_ROWS // 2

    cvec = jnp.zeros((MOD_ROWS, d), F32).at[:bsz].set(c).at[bsz].set(c_ctx)
    mod = _modulation(cvec, w_mod, b_mod)
    lat_row = lambda bi: bi
    ctx_row = lambda bi: bsz

    seg = jnp.asarray(np.kron(np.eye(4), np.full((64, 64), 1.0 / 64)), BF16)
    cos1, sin1 = _rope_tables(s)
    tables_lat = tuple(jnp.asarray(np.tile(tab, (1, reps)))
                       for tab, reps in ((cos1, 4), (sin1, 4), (cos1, 2), (sin1, 2)))
    tables_ctx = tuple(jnp.zeros((min(IN_TM, bsz * lc), w), F32) for w in (256, 256, 128, 128))

    amask = _window_mask()
    inv_lat, inv_ctx = _pool_inverse_counts(s), _pool_inverse_counts(lc)
    bias_all = _neighbour_bias_pairs(d_rpb * LOG2E)

    w_in_b = w_in.astype(BF16)
    w_out_b = w_out.astype(BF16)
    wg_b, wu_b, wd_b = w_gate.astype(BF16), w_up.astype(BF16), w_down.astype(BF16)
    gmix = g_mix.reshape(DEPTH, 1, d)
    gffn = g_ffn.reshape(DEPTH, 1, d)
    aqg = jnp.tile(a_q_gain, (1, 4)).reshape(DEPTH, 1, 256)
    akg = jnp.tile(a_k_gain, (1, 2)).reshape(DEPTH, 1, 128)
    dqg = jnp.tile(d_q_gain, (1, 4)).reshape(DEPTH, 1, 256)
    dkg = jnp.tile(d_k_gain, (1, 4)).reshape(DEPTH, 1, 256)
    bvg = b_v_gain.reshape(DEPTH, 1, 256)
    ws_cat = jnp.transpose(b_w_s, (0, 2, 1, 3)).reshape(DEPTH, B_CHUNK, B_GROUPS * B_CHUNK).astype(BF16)
    bs_t = jnp.repeat(jnp.swapaxes(b_b_s, 1, 2), B_GROUP_W, axis=2)
    wpool = jnp.einsum('lgcd,gh->lgchd', c_w_pool, jnp.eye(len(C_POOLS), dtype=F32)
                       ).reshape(DEPTH, GROUP_W, GROUP_W).astype(BF16)
    cscale = c_scale.reshape(DEPTH, 1, 256)
    sink = a_sink * LOG2E
    common = (gmix, w_in_b, seg, aqg, akg, dqg, dkg)

    xc = ctx
    for l in range(DEPTH):
        last = l == DEPTH - 1
        lat = _in_projection(x, mod, lat_row, l, *common, tables_lat, bvg, ws_cat, bs_t, use_rope=True)
        xc_flat = xc.reshape(1, bsz * lc, d)
        per_batch = lambda o: o.reshape((bsz, o.shape[1] // bsz) + o.shape[2:])
        if last:
            cka, cva, ckd, cvd = map(per_batch, _ctx_kv_projection(
                xc_flat, mod, ctx_row, l, gmix, w_in_b, seg, akg, dkg))
        else:
            cqa, cka, cva, cob, ccin, cqd, ckd, cvd = map(per_batch, _in_projection(
                xc_flat, mod, ctx_row, l, *common, tables_ctx, bvg, ws_cat, bs_t, use_rope=False))
        qa, ka, va, ob, cin, qd, kd, vd = lat

        x = _mixer(x, mod, lat_row, l, qa, ka, va, cka, cva, ob, cin, inv_lat, wpool, cscale, qd, kd, vd, ckd, cvd,
                   bias_all, amask, w_out_b, sink, local=True)
        x = _ffn(x, mod, lat_row, l, gffn, wg_b, wu_b, wd_b)
        if not last:
            xc = _mixer(xc, mod, ctx_row, l, cqa, cka, cva, cka, cva, cob, ccin, inv_ctx, wpool, cscale, cqd, ckd,
                        cvd, ckd, cvd, bias_all, amask, w_out_b, sink, local=False)
            xc = _ffn(xc.reshape(1, bsz * lc, d), mod, ctx_row, l, gffn, wg_b, wu_b, wd_b).reshape(bsz, lc, d)
    return x
```

```python
import functools

import jax
import jax.numpy as jnp
import numpy as np
from jax import lax
from jax.experimental import pallas as pl
from jax.experimental.pallas import tpu as pltpu

F32 = jnp.float32
BF16 = jnp.bfloat16

D_MODEL = 1024
DEPTH = 2
GRID_W = 64
HEAD_DIM = 64
GROUP_W = 256
A_HEADS = 4
A_KV_HEADS = 2
A_WINDOW = 128
A_BLOCK = 128
B_CHUNK = 128
B_GROUPS = 4
B_GROUP_W = 64
C_POOLS = (2, 4, 8, 16)
C_GROUP_W = 64
D_HEADS = 4
D_WIN_ROWS = 8
D_WIN_COLS = 16
FF_DIM = 2816
ROPE_BASE = 10000.0
NORM_EPS = 1e-6
NEG_INF = -1e30
LOG2E = 1.4426950408889634
IN_TOTAL = 2048
OFF_AQ, OFF_AK, OFF_AV, OFF_BU, OFF_BV, OFF_C, OFF_DQ, OFF_DK, OFF_DV = (
    0, 256, 384, 512, 768, 1024, 1280, 1536, 1792)

MOD_ROWS = 16
MOD_TN = D_MODEL
IN_TM = 512
MIX_TQ = 256
FFN_TM = 1024
FFN_TF = 256
BF16_SUBLANES = 16
POOL_HALO = 8
MIX_STEP_TILES = 4
OUT_CHUNK = 256
SCORE_LOOKAHEAD = 2
VMEM_LIMIT = 56 * 1024 * 1024


def _dot(a, b):
    return jnp.dot(a, b, preferred_element_type=F32)


def _dot_nt(a, b):
    return lax.dot_general(a, b, (((1,), (1,)), ((), ())), preferred_element_type=F32)


def _norm_modulate(xt, gain, scale, shift):
    ms = jnp.mean(xt * xt, axis=-1, keepdims=True)
    return (xt * lax.rsqrt(ms + NORM_EPS) * (gain * (1.0 + scale)) + shift).astype(BF16)


def _mod_kernel(c_ref, w_ref, b_ref, o_ref):
    cv = c_ref[...]
    act = cv * jax.nn.sigmoid(cv)
    o_ref[:, 0, :] = _dot(act.astype(BF16), w_ref[0].astype(BF16)) + b_ref[0]


def _modulation(cvec, w_mod, b_mod):
    depth, d, n = w_mod.shape
    return pl.pallas_call(
        _mod_kernel,
        out_shape=jax.ShapeDtypeStruct((depth, MOD_ROWS, n // MOD_TN, 1, MOD_TN), F32),
        grid=(depth, n // MOD_TN),
        in_specs=[
            pl.BlockSpec((MOD_ROWS, d), lambda l, j: (0, 0)),
            pl.BlockSpec((1, d, MOD_TN), lambda l, j: (l, 0, j)),
            pl.BlockSpec((1, 1, MOD_TN), lambda l, j: (l, 0, j)),
        ],
        out_specs=pl.BlockSpec((None, MOD_ROWS, None, 1, MOD_TN), lambda l, j: (l, 0, j, 0, 0)),
        compiler_params=pltpu.CompilerParams(
            dimension_semantics=("arbitrary", "arbitrary"), vmem_limit_bytes=VMEM_LIMIT),
        name="modulation",
    )(cvec, w_mod, b_mod.reshape(depth, 1, n))


def _head_norm(t, seg, gain):
    sq = t * t
    hi = sq.astype(BF16)
    lo = (sq - hi.astype(F32)).astype(BF16)
    ms = _dot(hi, seg) + _dot(lo, seg)
    return t * lax.rsqrt(ms + NORM_EPS) * gain


def _rope(y, cos, sin_signed):
    w = y.shape[-1]
    up = pltpu.roll(y, w - 16, axis=1)
    dn = pltpu.roll(y, 16, axis=1)
    lane = lax.broadcasted_iota(jnp.int32, y.shape, 1)
    swapped = jnp.where((lane & 31) < 16, up, dn)
    return y * cos + swapped * sin_signed


def _swap_middle_heads(q):
    lane = lax.broadcasted_iota(jnp.int32, q.shape, 1)
    from_right = pltpu.roll(q, 3 * HEAD_DIM, axis=1)
    from_left = pltpu.roll(q, HEAD_DIM, axis=1)
    return jnp.where((lane >= HEAD_DIM) & (lane < 2 * HEAD_DIM), from_right,
                     jnp.where((lane >= 2 * HEAD_DIM) & (lane < 3 * HEAD_DIM), from_left, q))


def _in_kernel(x_ref, sh_ref, sc_ref, g_ref, w_ref, seg_ref, aqg_ref, akg_ref, dqg_ref, dkg_ref,
               cosq_ref, sinq_ref, cosk_ref, sink_ref, bvg_ref, ws_ref, bs_ref,
               qa_ref, ka_ref, va_ref, ob_ref, c_ref, qd_ref, kd_ref, vd_ref, v_scr, *, use_rope):
    xt = x_ref[0]
    h = _norm_modulate(xt, g_ref[...], sc_ref[...], sh_ref[...])

    group_w = IN_TOTAL // 4
    assert (OFF_BU, OFF_C, OFF_DK) == (group_w, 2 * group_w, 3 * group_w)

    def group(k):
        return _dot(h, w_ref[:, k * group_w:(k + 1) * group_w])

    seg = seg_ref[...]
    scale = HEAD_DIM ** -0.5 * LOG2E
    tm = xt.shape[0]

    def finish_a(p):
        qa = _head_norm(p[:, 0:256], seg, aqg_ref[...])
        ka = _head_norm(p[:, 256:384], seg[:128, :128], akg_ref[...])
        if use_rope:
            qa = _rope(qa, cosq_ref[...], sinq_ref[...])
            ka = _rope(ka, cosk_ref[...], sink_ref[...])
        qa_ref[0] = (_swap_middle_heads(qa) * scale).astype(BF16)
        ka_ref[0] = ka.astype(BF16)
        v_scr[:, 0:128] = p[:, 384:512]
        vat = v_scr[:, 0:128].T.astype(BF16)
        for j in range(tm // A_BLOCK):
            va_ref[0, j] = vat[:, j * A_BLOCK:(j + 1) * A_BLOCK]

    def finish_b(p):
        u = jax.nn.gelu(p[:, 0:256])
        v = jax.nn.gelu(p[:, 256:512])
        mu = jnp.mean(v, axis=-1, keepdims=True)
        vc = v - mu
        var = jnp.mean(vc * vc, axis=-1, keepdims=True)
        vn = (vc * lax.rsqrt(var + NORM_EPS) * bvg_ref[...]).astype(BF16)
        lane = lax.broadcasted_iota(jnp.int32, (B_CHUNK, GROUP_W), 1)
        zero = jnp.zeros((B_CHUNK, GROUP_W), BF16)
        for ch in range(tm // B_CHUNK):
            rows = slice(ch * B_CHUNK, (ch + 1) * B_CHUNK)
            vch = vn[rows]
            stacked = jnp.concatenate(
                [jnp.where((lane >= g * B_GROUP_W) & (lane < (g + 1) * B_GROUP_W), vch, zero)
                 for g in range(B_GROUPS)], axis=0)
            z = _dot(ws_ref[...], stacked) + bs_ref[...]
            ob_ref[0, rows, :] = (u[rows] * z).astype(BF16)

    def finish_c(p):
        c_ref[0] = p[:, 0:256]
        qd = _head_norm(p[:, 256:512], seg, dqg_ref[...])
        qd_ref[0] = (qd * scale).astype(BF16)

    def finish_d(p):
        kd_ref[0] = _head_norm(p[:, 0:256], seg, dkg_ref[...]).astype(BF16)
        v_scr[:, 128:384] = p[:, 256:512]
        vdt = v_scr[:, 128:384].T.astype(BF16)
        for j in range(tm // MIX_TQ):
            vd_ref[0, j] = vdt[:, j * MIX_TQ:(j + 1) * MIX_TQ]

    order = ((1, finish_b), (0, finish_a), (3, finish_d), (2, finish_c))
    pending = group(order[0][0])
    for pos, (_, finish) in enumerate(order):
        upcoming = group(order[pos + 1][0]) if pos + 1 < len(order) else None
        finish(pending)
        pending = upcoming


def _mod_spec(layer, row, which, batch_axis=0):
    return pl.BlockSpec((None, None, None, 1, D_MODEL),
                        lambda *idx: (layer, row(idx[batch_axis]), which, 0, 0))


def _layer_spec(layer, shape):
    return pl.BlockSpec((None,) + tuple(shape), lambda *_: (layer,) + (0,) * len(shape))


def _in_projection(x, mod, row, layer, g_mix, w_in, seg, aqg, akg, dqg, dkg, tables, bvg, ws_cat, bs_t,
                   *, use_rope):
    b, t, d = x.shape
    cosq, sinq, cosk, sink = tables
    tm = min(IN_TM, t)
    tok = lambda i, bi: (bi, i, 0)
    const2 = lambda i, bi: (0, 0)
    pos2 = (lambda i, bi: (i, 0)) if use_rope else const2
    assert tm % MIX_TQ == 0 and MIX_TQ % A_BLOCK == 0
    out_shapes = (
        jax.ShapeDtypeStruct((b, t, 256), BF16),
        jax.ShapeDtypeStruct((b, t, 128), BF16),
        jax.ShapeDtypeStruct((b, t // A_BLOCK, 128, A_BLOCK), BF16),
        jax.ShapeDtypeStruct((b, t, 256), BF16),
        jax.ShapeDtypeStruct((b, t, 256), F32),
        jax.ShapeDtypeStruct((b, t, 256), BF16),
        jax.ShapeDtypeStruct((b, t, 256), BF16),
        jax.ShapeDtypeStruct((b, t // MIX_TQ, 256, MIX_TQ), BF16),
    )
    blk4 = lambda i, bi: (bi, i, 0, 0)
    out_specs = tuple(
        pl.BlockSpec((1, tm // A_BLOCK, 128, A_BLOCK), blk4) if k == 2 else
        pl.BlockSpec((1, tm // MIX_TQ, 256, MIX_TQ), blk4) if k == 7 else
        pl.BlockSpec((1, tm, s.shape[-1]), tok)
        for k, s in enumerate(out_shapes))
    return pl.pallas_call(
        functools.partial(_in_kernel, use_rope=use_rope),
        out_shape=out_shapes,
        grid=(t // tm, b),
        in_specs=[
            pl.BlockSpec((1, tm, d), tok),
            _mod_spec(layer, row, 0, batch_axis=1),
            _mod_spec(layer, row, 1, batch_axis=1),
            _layer_spec(layer, (1, d)),
            _layer_spec(layer, (d, IN_TOTAL)),
            pl.BlockSpec((256, 256), const2),
            _layer_spec(layer, (1, 256)),
            _layer_spec(layer, (1, 128)),
            _layer_spec(layer, (1, 256)),
            _layer_spec(layer, (1, 256)),
            pl.BlockSpec((tm, 256), pos2),
            pl.BlockSpec((tm, 256), pos2),
            pl.BlockSpec((tm, 128), pos2),
            pl.BlockSpec((tm, 128), pos2),
            _layer_spec(layer, (1, 256)),
            _layer_spec(layer, (B_CHUNK, B_GROUPS * B_CHUNK)),
            _layer_spec(layer, (B_CHUNK, GROUP_W)),
        ],
        out_specs=out_specs,
        scratch_shapes=[pltpu.VMEM((tm, 384), F32)],
        compiler_params=pltpu.CompilerParams(
            dimension_semantics=("arbitrary", "arbitrary"), vmem_limit_bytes=VMEM_LIMIT),
        name="in_projection_rope" if use_rope else "in_projection_ctx",
    )(x, mod, mod, g_mix, w_in, seg, aqg, akg, dqg, dkg, cosq, sinq, cosk, sink, bvg, ws_cat, bs_t)


def _ctx_kv_kernel(x_ref, sh_ref, sc_ref, g_ref, w_ref, seg_ref, akg_ref, dkg_ref,
                   ka_ref, va_ref, kd_ref, vd_ref, v_scr):
    xt = x_ref[0]
    tm = xt.shape[0]
    h = _norm_modulate(xt, g_ref[...], sc_ref[...], sh_ref[...])
    seg = seg_ref[...]
    pa = _dot(h, w_ref[:, OFF_AK:OFF_BU])
    pd = _dot(h, w_ref[:, OFF_DK:IN_TOTAL])
    ka_ref[0] = _head_norm(pa[:, 0:128], seg[:128, :128], akg_ref[...]).astype(BF16)
    kd_ref[0] = _head_norm(pd[:, 0:256], seg, dkg_ref[...]).astype(BF16)
    v_scr[:, 0:128] = pa[:, 128:256]
    v_scr[:, 128:384] = pd[:, 256:512]
    vat = v_scr[:, 0:128].T.astype(BF16)
    for j in range(tm // A_BLOCK):
        va_ref[0, j] = vat[:, j * A_BLOCK:(j + 1) * A_BLOCK]
    vdt = v_scr[:, 128:384].T.astype(BF16)
    for j in range(tm // MIX_TQ):
        vd_ref[0, j] = vdt[:, j * MIX_TQ:(j + 1) * MIX_TQ]


def _ctx_kv_projection(x, mod, row, layer, g_mix, w_in, seg, akg, dkg):
    b, t, d = x.shape
    tm = min(IN_TM, t)
    tok = lambda i, bi: (bi, i, 0)
    blk4 = lambda i, bi: (bi, i, 0, 0)
    out_shapes = (
        jax.ShapeDtypeStruct((b, t, 128), BF16),
        jax.ShapeDtypeStruct((b, t // A_BLOCK, 128, A_BLOCK), BF16),
        jax.ShapeDtypeStruct((b, t, 256), BF16),
        jax.ShapeDtypeStruct((b, t // MIX_TQ, 256, MIX_TQ), BF16),
    )
    return pl.pallas_call(
        _ctx_kv_kernel,
        out_shape=out_shapes,
        grid=(t // tm, b),
        in_specs=[
            pl.BlockSpec((1, tm, d), tok),
            _mod_spec(layer, row, 0, batch_axis=1),
            _mod_spec(layer, row, 1, batch_axis=1),
            _layer_spec(layer, (1, d)),
            _layer_spec(layer, (d, IN_TOTAL)),
            pl.BlockSpec((256, 256), lambda i, bi: (0, 0)),
            _layer_spec(layer, (1, 128)),
            _layer_spec(layer, (1, 256)),
        ],
        out_specs=(pl.BlockSpec((1, tm, 128), tok),
                   pl.BlockSpec((1, tm // A_BLOCK, 128, A_BLOCK), blk4),
                   pl.BlockSpec((1, tm, 256), tok),
                   pl.BlockSpec((1, tm // MIX_TQ, 256, MIX_TQ), blk4)),
        scratch_shapes=[pltpu.VMEM((tm, 384), F32)],
        compiler_params=pltpu.CompilerParams(
            dimension_semantics=("arbitrary", "arbitrary"), vmem_limit_bytes=VMEM_LIMIT),
        name="ctx_kv_projection",
    )(x, mod, mod, g_mix, w_in, seg, akg, dkg)


def _softmax_pv_t(parts, extra_logit):
    m = None
    for s, _ in parts:
        pm = jnp.max(s, axis=0, keepdims=True)
        m = pm if m is None else jnp.maximum(m, pm)
    if extra_logit is not None:
        m = jnp.maximum(m, extra_logit)
    acc = None
    for s, vt in parts:
        dh, n = vt.shape
        vt_ones = jnp.concatenate([vt, jnp.ones((BF16_SUBLANES, n), BF16)], axis=0)
        pv = _dot(vt_ones, jnp.exp2(s - m).astype(BF16))
        acc = pv if acc is None else acc + pv
    denom = acc[dh:dh + 1]
    if extra_logit is not None:
        denom = denom + jnp.exp2(extra_logit - m)
    return acc[:dh] / denom


def _head_lanes(q, head, width=HEAD_DIM):
    lane = lax.broadcasted_iota(jnp.int32, q.shape, 1)
    return jnp.where((lane >= head * width) & (lane < (head + 1) * width), q, jnp.zeros_like(q))


def _pool_tile(c_ref, inv, t0, n_tok, tq):
    y = c_ref[0, pl.ds(t0, tq), :]
    lo_start = pl.multiple_of(jnp.maximum(t0 - POOL_HALO, 0), POOL_HALO)
    hi_start = pl.multiple_of(jnp.minimum(t0 + tq, n_tok - POOL_HALO), POOL_HALO)
    lo = c_ref[0, pl.ds(lo_start, POOL_HALO), :]
    hi = c_ref[0, pl.ds(hi_start, POOL_HALO), :]
    lo = jnp.where(t0 > 0, lo, 0.0)
    hi = jnp.where(t0 + tq < n_tok, hi, 0.0)
    ypad = jnp.concatenate([lo, y, hi], axis=0)

    n = tq + 2 * POOL_HALO
    w2 = ypad + pltpu.roll(ypad, 1, axis=0)
    w4 = pltpu.roll(w2, 1, axis=0) + pltpu.roll(w2, n - 1, axis=0)
    w8 = pltpu.roll(w4, 2, axis=0) + pltpu.roll(w4, n - 2, axis=0)
    w16 = pltpu.roll(w8, 4, axis=0) + pltpu.roll(w8, n - 4, axis=0)
    sums = tuple(w[POOL_HALO:POOL_HALO + tq] for w in (w2, w4, w8, w16))

    lane = lax.broadcasted_iota(jnp.int32, (tq, GROUP_W), 1)
    total = sums[0]
    for gi in range(1, len(C_POOLS)):
        total = jnp.where(lane >= gi * C_GROUP_W, sums[gi], total)
    return total * inv - y


def _mix_kernel(*refs, local, n_tok, layer):
    x_ref = refs[1]
    deferred = []
    for sub in range(x_ref.shape[1] // MIX_TQ):
        _mix_subtile(*refs, sub=sub, deferred=deferred, local=local, n_tok=n_tok, layer=layer)
    while deferred:
        deferred.pop(0)()


def _mix_subtile(sink_ref, x_ref, g1_ref, qa_ref, ka_ref, va_ref, kca_ref, vca_ref, ob_ref, c_ref,
                 inv_ref, wp_ref, cs_ref, qd_ref, kd_ref, vd_ref, kcd_ref, vcd_ref, bias_ref, amask_ref, wo_ref,
                 o_ref, oa_scr, od_scr, mix_scr, *, sub, deferred, local, n_tok, layer):
    tq = MIX_TQ
    ti = pl.program_id(1) * (x_ref.shape[1] // tq) + sub
    t0 = pl.multiple_of(ti * tq, tq)
    tile = slice(sub * tq, (sub + 1) * tq)

    pooled = _pool_tile(c_ref, inv_ref[tile, :], t0, n_tok, tq)
    out_c = (_dot(pooled.astype(BF16), wp_ref[...]) * cs_ref[...]).astype(BF16)


    if local:
        n_tiles = n_tok // tq
        half = D_WIN_ROWS // 2
        pair_entries = []
        for jj in range(3 * tq // GRID_W):
            row_entries = []
            for i in range(0, tq // GRID_W, 2):
                interior = PAIR_INTERIOR + (jj - half - i) + PAIR_COUNT // 2 - 1
                first = PAIR_EDGE + (jj - i) + PAIR_COUNT // 2 - 1 if jj < D_WIN_ROWS else PAIR_MASKED
                last = (PAIR_EDGE + (jj - D_WIN_ROWS - i) + PAIR_COUNT // 2 - 1 if jj >= half
                        else PAIR_MASKED)
                row_entries.append(jnp.where(ti == 0, first, jnp.where(ti == n_tiles - 1, last, interior)))
            pair_entries.append(row_entries)

    n_ctx_blk = vca_ref.shape[1]
    jobs = []

    def a_job(blk):
        cols = slice(blk * A_BLOCK, (blk + 1) * A_BLOCK)

        def scores():
            qblk = qa_ref[0, sub * tq + blk * A_BLOCK:sub * tq + (blk + 1) * A_BLOCK, :]
            qcat = jnp.concatenate([qblk[:, :128], qblk[:, 128:]], axis=0)
            qm = jnp.concatenate([_head_lanes(qcat, kv) for kv in range(A_KV_HEADS)], axis=0)
            if local:
                nblk = ti * (tq // A_BLOCK) + blk
                b0 = jnp.clip(nblk - 1, 0, n_tok // A_BLOCK - 3)
                kstart = pl.multiple_of(b0 * A_BLOCK, A_BLOCK)
                kwin = ka_ref[0, pl.ds(kstart, 3 * A_BLOCK), :]
                last_blk = n_tok // A_BLOCK - 1
                kind = jnp.where(nblk == 0, 1, jnp.where(nblk == last_blk, 2, 0))
                s_all = _dot_nt(jnp.concatenate([kwin, kca_ref[0]], axis=0), qm)
                s_loc, s_ctx = s_all[:3 * A_BLOCK], s_all[3 * A_BLOCK:]
            else:
                s_ctx = _dot_nt(kca_ref[0], qm)
            per_head = []
            for kv in range(A_KV_HEADS):
                rows = slice(kv * HEAD_DIM, (kv + 1) * HEAD_DIM)
                qsl = slice(kv * 2 * A_BLOCK, (kv + 1) * 2 * A_BLOCK)
                parts = []
                if local:
                    s_win = s_loc[:, qsl] + amask_ref[kind]
                    for j in range(3):
                        parts.append((s_win[j * A_BLOCK:(j + 1) * A_BLOCK], va_ref[0, b0 + j, rows, :]))
                for j in range(n_ctx_blk):
                    parts.append((s_ctx[j * A_BLOCK:(j + 1) * A_BLOCK, qsl], vca_ref[0, j, rows, :]))
                per_head.append(parts)
            return per_head

        def finish(per_head):
            for kv, parts in enumerate(per_head):
                col = lax.broadcasted_iota(jnp.int32, (1, 2 * A_BLOCK), 1)
                sink = jnp.where(col < A_BLOCK, sink_ref[layer, 2 * kv], sink_ref[layer, 2 * kv + 1])
                o2 = _softmax_pv_t(parts, sink)
                oa_scr[sub, (2 * kv) * 64:(2 * kv + 1) * 64, cols] = o2[:, :A_BLOCK]
                oa_scr[sub, (2 * kv + 1) * 64:(2 * kv + 2) * 64, cols] = o2[:, A_BLOCK:]

        return scores, finish

    def d_job(pair):
        heads = (2 * pair, 2 * pair + 1)

        def scores():
            qd = qd_ref[0, tile, :]
            qm = jnp.concatenate([_head_lanes(qd, hd) for hd in heads], axis=0)
            if local:
                d0 = jnp.clip(ti - 1, 0, n_tok // tq - 3)
                kwin = kd_ref[0, pl.ds(pl.multiple_of(d0 * tq, tq), 3 * tq), :]
                s_all = _dot_nt(jnp.concatenate([kwin, kcd_ref[0]], axis=0), qm)
                s_loc, s_ctx = s_all[:3 * tq], s_all[3 * tq:]
            else:
                s_ctx = _dot_nt(kcd_ref[0], qm)
            per_head = []
            for k, hd in enumerate(heads):
                rows = slice(hd * HEAD_DIM, (hd + 1) * HEAD_DIM)
                qsl = slice(k * tq, (k + 1) * tq)
                parts = []
                if local:
                    bias = jnp.concatenate(
                        [jnp.concatenate([bias_ref[hd, e] for e in row_entries], axis=1)
                         for row_entries in pair_entries], axis=0)
                    s_win = s_loc[:, qsl] + bias
                    for j in range(3):
                        parts.append((s_win[j * tq:(j + 1) * tq], vd_ref[0, d0 + j, rows, :]))
                parts.append((s_ctx[:, qsl], vcd_ref[0, 0, rows, :]))
                per_head.append(parts)
            return per_head

        def finish(per_head):
            for hd, parts in zip(heads, per_head):
                od_scr[sub, hd * HEAD_DIM:(hd + 1) * HEAD_DIM, :] = _softmax_pv_t(parts, None)

        return scores, finish

    for blk in range(tq // A_BLOCK):
        jobs.append(a_job(blk))
    for pair in range(D_HEADS // 2):
        jobs.append(d_job(pair))
    ahead = min(SCORE_LOOKAHEAD, len(jobs))
    pending = [jobs[j][0]() for j in range(ahead)]
    for j, (_, finish) in enumerate(jobs):
        if j + ahead < len(jobs):
            pending.append(jobs[j + ahead][0]())
        if deferred:
            deferred.pop(0)()
        finish(pending.pop(0))

    mix_scr[sub, :, 0:256] = oa_scr[sub].T.astype(BF16)
    mix_scr[sub, :, 256:512] = ob_ref[0, tile, :]
    mix_scr[sub, :, 512:768] = out_c
    mix_scr[sub, :, 768:1024] = od_scr[sub].T.astype(BF16)

    def out_chunk(c):
        cols = slice(c * OUT_CHUNK, (c + 1) * OUT_CHUNK)

        def run():
            mix = _dot(mix_scr[sub], wo_ref[:, cols])
            o_ref[0, tile, cols] = x_ref[0, tile, cols] + g1_ref[:, cols] * mix

        return run

    deferred.extend(out_chunk(c) for c in range(D_MODEL // OUT_CHUNK))


def _mixer(x, mod, row, layer, qa, ka, va, kca, vca, ob, cin, inv_cnt, wpool, cscale, qd, kd, vd, kcd, vcd, bias,
           amask, w_out, sink, *, local):
    b, t, d = x.shape
    lc = kca.shape[1]
    tq = MIX_TQ
    ts = min(MIX_STEP_TILES * tq, t)
    n_sub = ts // tq
    tok = lambda bi, i: (bi, i, 0)
    per_b = lambda bi, i: (bi, 0, 0)
    per_b4 = lambda bi, i: (bi, 0, 0, 0)

    return pl.pallas_call(
        functools.partial(_mix_kernel, local=local, n_tok=t, layer=layer),
        out_shape=jax.ShapeDtypeStruct((b, t, d), F32),
        grid=(b, t // ts),
        in_specs=[
            pl.BlockSpec(memory_space=pltpu.SMEM),
            pl.BlockSpec((1, ts, d), tok),
            _mod_spec(layer, row, 2),
            pl.BlockSpec((1, ts, 256), tok),
            pl.BlockSpec((1, t, 128), per_b),
            pl.BlockSpec((1, t // A_BLOCK, 128, A_BLOCK), per_b4),
            pl.BlockSpec((1, lc, 128), per_b),
            pl.BlockSpec((1, lc // A_BLOCK, 128, A_BLOCK), per_b4),
            pl.BlockSpec((1, ts, 256), tok),
            pl.BlockSpec((1, t, 256), per_b),
            pl.BlockSpec((ts, 256), lambda bi, i: (i, 0)),
            _layer_spec(layer, (256, 256)),
            _layer_spec(layer, (1, 256)),
            pl.BlockSpec((1, ts, 256), tok),
            pl.BlockSpec((1, t, 256), per_b),
            pl.BlockSpec((1, t // tq, 256, tq), per_b4),
            pl.BlockSpec((1, lc, 256), per_b),
            pl.BlockSpec((1, lc // tq, 256, tq), per_b4),
            _layer_spec(layer, (D_HEADS, PAIR_MASKED + 1, GRID_W, 2 * GRID_W)),
            pl.BlockSpec((3, 3 * A_BLOCK, 2 * A_BLOCK), lambda bi, i: (0, 0, 0)),
            _layer_spec(layer, (d, d)),
        ],
        out_specs=pl.BlockSpec((1, ts, d), tok),
        scratch_shapes=[pltpu.VMEM((n_sub, GROUP_W, tq), F32), pltpu.VMEM((n_sub, GROUP_W, tq), F32),
                        pltpu.VMEM((n_sub, tq, d), BF16)],
        compiler_params=pltpu.CompilerParams(
            dimension_semantics=("arbitrary", "arbitrary"), vmem_limit_bytes=VMEM_LIMIT),
        name="mixer_latent" if local else "mixer_ctx",
    )(sink, x, mod, qa, ka, va, kca, vca, ob, cin, inv_cnt, wpool, cscale, qd, kd, vd, kcd, vcd, bias, amask,
      w_out)


def _ffn_kernel(x_ref, sh_ref, sc_ref, g2_ref, gn_ref, wg_ref, wu_ref, wd_ref, o_ref, act_scr):
    xt = x_ref[0]
    h = _norm_modulate(xt, gn_ref[...], sc_ref[...], sh_ref[...])
    for c in range(FF_DIM // FFN_TF):
        cols = slice(c * FFN_TF, (c + 1) * FFN_TF)
        gate = _dot(h, wg_ref[:, cols])
        up = _dot(h, wu_ref[:, cols])
        act_scr[:, cols] = (gate * jax.nn.sigmoid(gate) * up).astype(BF16)
    o_ref[0] = xt + g2_ref[...] * _dot(act_scr[...], wd_ref[...])


def _ffn(x, mod, row, layer, g_ffn, wg, wu, wd):
    b, t, d = x.shape
    tm = min(FFN_TM, t)
    tok = lambda bi, i: (bi, i, 0)
    per_layer = lambda bi, i: (layer, 0, 0)
    resident = pl.Buffered(1)
    return pl.pallas_call(
        _ffn_kernel,
        out_shape=jax.ShapeDtypeStruct((b, t, d), F32),
        grid=(b, t // tm),
        in_specs=[
            pl.BlockSpec((1, tm, d), tok),
            _mod_spec(layer, row, 3),
            _mod_spec(layer, row, 4),
            _mod_spec(layer, row, 5),
            _layer_spec(layer, (1, d)),
            pl.BlockSpec((None, d, FF_DIM), per_layer, pipeline_mode=resident),
            pl.BlockSpec((None, d, FF_DIM), per_layer, pipeline_mode=resident),
            pl.BlockSpec((None, FF_DIM, d), per_layer, pipeline_mode=resident),
        ],
        out_specs=pl.BlockSpec((1, tm, d), tok),
        scratch_shapes=[pltpu.VMEM((tm, FF_DIM), BF16)],
        compiler_params=pltpu.CompilerParams(
            dimension_semantics=("arbitrary", "arbitrary"), vmem_limit_bytes=VMEM_LIMIT),
        name="ffn",
    )(x, mod, mod, mod, g_ffn, wg, wu, wd)


def _rope_tables(s):
    t = np.arange(s)
    half = 16
    inv = np.power(np.float32(ROPE_BASE), -np.arange(half, dtype=np.float32) / half).astype(np.float32)
    ang_r = (t // GRID_W).astype(np.float32)[:, None] * inv[None, :]
    ang_c = (t % GRID_W).astype(np.float32)[:, None] * inv[None, :]
    cos = np.concatenate([np.cos(ang_r)] * 2 + [np.cos(ang_c)] * 2, axis=-1)
    sin = np.concatenate([-np.sin(ang_r), np.sin(ang_r), -np.sin(ang_c), np.sin(ang_c)], axis=-1)
    return cos.astype(np.float32), sin.astype(np.float32)


def _pool_inverse_counts(n_tok):
    t = np.arange(n_tok)
    cols = []
    for w in C_POOLS:
        lo = np.clip(t - w // 2, 0, n_tok)
        hi = np.clip(t - w // 2 + w, 0, n_tok)
        cols.append(np.repeat((np.float32(1.0) / (hi - lo).astype(np.float32))[:, None], C_GROUP_W, axis=1))
    return jnp.asarray(np.concatenate(cols, axis=1), F32)


def _window_mask():
    kk = np.arange(3 * A_BLOCK)[:, None]
    qq = np.arange(2 * A_BLOCK)[None, :] % A_BLOCK
    kinds = [np.where(np.abs(kk - qq - back * A_BLOCK) <= A_WINDOW, 0.0, NEG_INF) for back in (1, 0, 2)]
    return jnp.asarray(np.stack(kinds), F32)


PAIR_COUNT = 2 * (D_WIN_ROWS - 1)
PAIR_INTERIOR = 0
PAIR_EDGE = PAIR_COUNT
PAIR_MASKED = 2 * PAIR_COUNT


def _neighbour_bias_pairs(rpb):
    cidx = np.arange(GRID_W)
    col_start = np.clip(cidx - D_WIN_COLS // 2, 0, GRID_W - D_WIN_COLS)
    col_ok_t = ((cidx[None, :] >= col_start[:, None]) & (cidx[None, :] < col_start[:, None] + D_WIN_COLS)).T
    coff_t = np.clip(cidx[:, None] - cidx[None, :], -(D_WIN_COLS - 1), D_WIN_COLS - 1) + (D_WIN_COLS - 1)
    select = (np.arange(2 * D_WIN_COLS - 1)[:, None, None] == coff_t[None]).astype(np.float32)
    blocks = lax.dot_general(rpb.astype(F32), jnp.asarray(select), (((3,), (0,)), ((), ())),
                             precision=lax.Precision.HIGHEST)
    blocks = jnp.where(col_ok_t, blocks, NEG_INF)
    d = np.arange(-(D_WIN_ROWS - 1), D_WIN_ROWS)
    seen = (d >= -(D_WIN_ROWS // 2)) & (d < D_WIN_ROWS // 2)
    interior = jnp.where(seen[:, None, None], blocks, NEG_INF)
    pair = lambda t: jnp.concatenate([t[:, :, 1:], t[:, :, :-1]], axis=-1)
    masked = jnp.full(blocks.shape[:2] + (1, GRID_W, 2 * GRID_W), NEG_INF, F32)
    return jnp.concatenate([pair(interior), pair(blocks), masked], axis=2)


def kernel(x, c, ctx, c_ctx, w_mod, b_mod, g_mix, g_ffn, w_in, w_out, a_q_gain, a_k_gain, a_sink,
           b_v_gain, b_w_s, b_b_s, c_w_pool, c_scale, d_q_gain, d_k_gain, d_rpb, w_gate, w_up, w_down):
    bsz, s, d = x.shape
    lc = ctx.shape[1]
    n_rows = s // GRID_W
    assert s % MIX_TQ == 0 and n_rows >= 2 * D_WIN_ROWS and MIX_TQ // GRID_W == D_WIN_ROWS // 2

    cvec = jnp.zeros((MOD_ROWS, d), F32).at[:bsz].set(c).at[bsz].set(c_ctx)
    mod = _modulation(cvec, w_mod, b_mod)
    lat_row = lambda bi: bi
    ctx_row = lambda bi: bsz

    seg = jnp.asarray(np.kron(np.eye(4), np.full((64, 64), 1.0 / 64)), BF16)
    cos1, sin1 = _rope_tables(s)
    tables_lat = tuple(jnp.asarray(np.tile(tab, (1, reps)))
                       for tab, reps in ((cos1, 4), (sin1, 4), (cos1, 2), (sin1, 2)))
    tables_ctx = tuple(jnp.zeros((min(IN_TM, bsz * lc), w), F32) for w in (256, 256, 128, 128))

    amask = _window_mask()
    inv_lat, inv_ctx = _pool_inverse_counts(s), _pool_inverse_counts(lc)
    bias_all = _neighbour_bias_pairs(d_rpb * LOG2E)

    w_in_b = w_in.astype(BF16)
    w_out_b = w_out.astype(BF16)
    wg_b, wu_b, wd_b = w_gate.astype(BF16), w_up.astype(BF16), w_down.astype(BF16)
    gmix = g_mix.reshape(DEPTH, 1, d)
    gffn = g_ffn.reshape(DEPTH, 1, d)
    aqg = jnp.tile(a_q_gain, (1, 4)).reshape(DEPTH, 1, 256)
    akg = jnp.tile(a_k_gain, (1, 2)).reshape(DEPTH, 1, 128)
    dqg = jnp.tile(d_q_gain, (1, 4)).reshape(DEPTH, 1, 256)
    dkg = jnp.tile(d_k_gain, (1, 4)).reshape(DEPTH, 1, 256)
    bvg = b_v_gain.reshape(DEPTH, 1, 256)
    ws_cat = jnp.transpose(b_w_s, (0, 2, 1, 3)).reshape(DEPTH, B_CHUNK, B_GROUPS * B_CHUNK).astype(BF16)
    bs_t = jnp.repeat(jnp.swapaxes(b_b_s, 1, 2), B_GROUP_W, axis=2)
    wpool = jnp.einsum('lgcd,gh->lgchd', c_w_pool, jnp.eye(len(C_POOLS), dtype=F32)
                       ).reshape(DEPTH, GROUP_W, GROUP_W).astype(BF16)
    cscale = c_scale.reshape(DEPTH, 1, 256)
    sink = a_sink * LOG2E
    common = (gmix, w_in_b, seg, aqg, akg, dqg, dkg)

    xc = ctx
    for l in range(DEPTH):
        last = l == DEPTH - 1
        lat = _in_projection(x, mod, lat_row, l, *common, tables_lat, bvg, ws_cat, bs_t, use_rope=True)
        xc_flat = xc.reshape(1, bsz * lc, d)
        per_batch = lambda o: o.reshape((bsz, o.shape[1] // bsz) + o.shape[2:])
        if last:
            cka, cva, ckd, cvd = map(per_batch, _ctx_kv_projection(
                xc_flat, mod, ctx_row, l, gmix, w_in_b, seg, akg, dkg))
        else:
            cqa, cka, cva, cob, ccin, cqd, ckd, cvd = map(per_batch, _in_projection(
                xc_flat, mod, ctx_row, l, *common, tables_ctx, bvg, ws_cat, bs_t, use_rope=False))
        qa, ka, va, ob, cin, qd, kd, vd = lat

        x = _mixer(x, mod, lat_row, l, qa, ka, va, cka, cva, ob, cin, inv_lat, wpool, cscale, qd, kd, vd, ckd, cvd,
                   bias_all, amask, w_out_b, sink, local=True)
        x = _ffn(x, mod, lat_row, l, gffn, wg_b, wu_b, wd_b)
        if not last:
            xc = _mixer(xc, mod, ctx_row, l, cqa, cka, cva, cka, cva, cob, ccin, inv_ctx, wpool, cscale, cqd, ckd,
                        cvd, ckd, cvd, bias_all, amask, w_out_b, sink, local=False)
            xc = _ffn(xc.reshape(1, bsz * lc, d), mod, ctx_row, l, gffn, wg_b, wu_b, wd_b).reshape(bsz, lc, d)
    return x
```

```python
import functools

import jax
import jax.numpy as jnp
import numpy as np
from jax import lax
from jax.experimental import pallas as pl
from jax.experimental.pallas import tpu as pltpu

F32 = jnp.float32
BF16 = jnp.bfloat16

D_MODEL = 1024
DEPTH = 2
GRID_W = 64
HEAD_DIM = 64
GROUP_W = 256
A_HEADS = 4
A_KV_HEADS = 2
A_WINDOW = 128
A_BLOCK = 128
B_CHUNK = 128
B_GROUPS = 4
B_GROUP_W = 64
C_POOLS = (2, 4, 8, 16)
C_GROUP_W = 64
D_HEADS = 4
D_WIN_ROWS = 8
D_WIN_COLS = 16
FF_DIM = 2816
ROPE_BASE = 10000.0
NORM_EPS = 1e-6
NEG_INF = -1e30
LOG2E = 1.4426950408889634
IN_TOTAL = 2048
OFF_AQ, OFF_AK, OFF_AV, OFF_BU, OFF_BV, OFF_C, OFF_DQ, OFF_DK, OFF_DV = (
    0, 256, 384, 512, 768, 1024, 1280, 1536, 1792)

MOD_ROWS = 16
MOD_TN = D_MODEL
IN_TM = 512
MIX_TQ = 256
FFN_TM = 1024
FFN_TF = 256
BF16_SUBLANES = 16
POOL_HALO = 8
MIX_STEP_TILES = 4
OUT_CHUNK = 256
SCORE_LOOKAHEAD = 1
VMEM_LIMIT = 56 * 1024 * 1024


def _dot(a, b):
    return jnp.dot(a, b, preferred_element_type=F32)


def _dot_nt(a, b):
    return lax.dot_general(a, b, (((1,), (1,)), ((), ())), preferred_element_type=F32)


def _norm_modulate(xt, gain, scale, shift):
    ms = jnp.mean(xt * xt, axis=-1, keepdims=True)
    return (xt * lax.rsqrt(ms + NORM_EPS) * (gain * (1.0 + scale)) + shift).astype(BF16)


def _mod_kernel(c_ref, w_ref, b_ref, o_ref):
    cv = c_ref[...]
    act = cv * jax.nn.sigmoid(cv)
    o_ref[:, 0, :] = _dot(act.astype(BF16), w_ref[0].astype(BF16)) + b_ref[0]


def _modulation(cvec, w_mod, b_mod):
    depth, d, n = w_mod.shape
    return pl.pallas_call(
        _mod_kernel,
        out_shape=jax.ShapeDtypeStruct((depth, MOD_ROWS, n // MOD_TN, 1, MOD_TN), F32),
        grid=(depth, n // MOD_TN),
        in_specs=[
            pl.BlockSpec((MOD_ROWS, d), lambda l, j: (0, 0)),
            pl.BlockSpec((1, d, MOD_TN), lambda l, j: (l, 0, j)),
            pl.BlockSpec((1, 1, MOD_TN), lambda l, j: (l, 0, j)),
        ],
        out_specs=pl.BlockSpec((None, MOD_ROWS, None, 1, MOD_TN), lambda l, j: (l, 0, j, 0, 0)),
        compiler_params=pltpu.CompilerParams(
            dimension_semantics=("arbitrary", "arbitrary"), vmem_limit_bytes=VMEM_LIMIT),
        name="modulation",
    )(cvec, w_mod, b_mod.reshape(depth, 1, n))


def _head_norm(t, seg, gain):
    sq = t * t
    hi = sq.astype(BF16)
    lo = (sq - hi.astype(F32)).astype(BF16)
    ms = _dot(hi, seg) + _dot(lo, seg)
    return t * lax.rsqrt(ms + NORM_EPS) * gain


def _rope(y, cos, sin_signed):
    w = y.shape[-1]
    up = pltpu.roll(y, w - 16, axis=1)
    dn = pltpu.roll(y, 16, axis=1)
    lane = lax.broadcasted_iota(jnp.int32, y.shape, 1)
    swapped = jnp.where((lane & 31) < 16, up, dn)
    return y * cos + swapped * sin_signed


def _swap_middle_heads(q):
    lane = lax.broadcasted_iota(jnp.int32, q.shape, 1)
    from_right = pltpu.roll(q, 3 * HEAD_DIM, axis=1)
    from_left = pltpu.roll(q, HEAD_DIM, axis=1)
    return jnp.where((lane >= HEAD_DIM) & (lane < 2 * HEAD_DIM), from_right,
                     jnp.where((lane >= 2 * HEAD_DIM) & (lane < 3 * HEAD_DIM), from_left, q))


def _in_kernel(x_ref, sh_ref, sc_ref, g_ref, w_ref, seg_ref, aqg_ref, akg_ref, dqg_ref, dkg_ref,
               cosq_ref, sinq_ref, cosk_ref, sink_ref, bvg_ref, ws_ref, bs_ref,
               qa_ref, ka_ref, va_ref, ob_ref, c_ref, qd_ref, kd_ref, vd_ref, v_scr, *, use_rope):
    xt = x_ref[0]
    h = _norm_modulate(xt, g_ref[...], sc_ref[...], sh_ref[...])

    group_w = IN_TOTAL // 4
    assert (OFF_BU, OFF_C, OFF_DK) == (group_w, 2 * group_w, 3 * group_w)

    def group(k):
        return _dot(h, w_ref[:, k * group_w:(k + 1) * group_w])

    seg = seg_ref[...]
    scale = HEAD_DIM ** -0.5 * LOG2E
    tm = xt.shape[0]

    def finish_a(p):
        qa = _head_norm(p[:, 0:256], seg, aqg_ref[...])
        ka = _head_norm(p[:, 256:384], seg[:128, :128], akg_ref[...])
        if use_rope:
            qa = _rope(qa, cosq_ref[...], sinq_ref[...])
            ka = _rope(ka, cosk_ref[...], sink_ref[...])
        qa_ref[0] = (_swap_middle_heads(qa) * scale).astype(BF16)
        ka_ref[0] = ka.astype(BF16)
        v_scr[:, 0:128] = p[:, 384:512]
        vat = v_scr[:, 0:128].T.astype(BF16)
        for j in range(tm // A_BLOCK):
            va_ref[0, j] = vat[:, j * A_BLOCK:(j + 1) * A_BLOCK]

    def finish_b(p):
        u = jax.nn.gelu(p[:, 0:256])
        v = jax.nn.gelu(p[:, 256:512])
        mu = jnp.mean(v, axis=-1, keepdims=True)
        vc = v - mu
        var = jnp.mean(vc * vc, axis=-1, keepdims=True)
        vn = (vc * lax.rsqrt(var + NORM_EPS) * bvg_ref[...]).astype(BF16)
        lane = lax.broadcasted_iota(jnp.int32, (B_CHUNK, GROUP_W), 1)
        zero = jnp.zeros((B_CHUNK, GROUP_W), BF16)
        for ch in range(tm // B_CHUNK):
            rows = slice(ch * B_CHUNK, (ch + 1) * B_CHUNK)
            vch = vn[rows]
            stacked = jnp.concatenate(
                [jnp.where((lane >= g * B_GROUP_W) & (lane < (g + 1) * B_GROUP_W), vch, zero)
                 for g in range(B_GROUPS)], axis=0)
            z = _dot(ws_ref[...], stacked) + bs_ref[...]
            ob_ref[0, rows, :] = (u[rows] * z).astype(BF16)

    def finish_c(p):
        c_ref[0] = p[:, 0:256]
        qd = _head_norm(p[:, 256:512], seg, dqg_ref[...])
        qd_ref[0] = (qd * scale).astype(BF16)

    def finish_d(p):
        kd_ref[0] = _head_norm(p[:, 0:256], seg, dkg_ref[...]).astype(BF16)
        v_scr[:, 128:384] = p[:, 256:512]
        vdt = v_scr[:, 128:384].T.astype(BF16)
        for j in range(tm // MIX_TQ):
            vd_ref[0, j] = vdt[:, j * MIX_TQ:(j + 1) * MIX_TQ]

    order = ((1, finish_b), (0, finish_a), (3, finish_d), (2, finish_c))
    pending = group(order[0][0])
    for pos, (_, finish) in enumerate(order):
        upcoming = group(order[pos + 1][0]) if pos + 1 < len(order) else None
        finish(pending)
        pending = upcoming


def _mod_spec(layer, row, which, batch_axis=0):
    return pl.BlockSpec((None, None, None, 1, D_MODEL),
                        lambda *idx: (layer, row(idx[batch_axis]), which, 0, 0))


def _layer_spec(layer, shape):
    return pl.BlockSpec((None,) + tuple(shape), lambda *_: (layer,) + (0,) * len(shape))


def _in_projection(x, mod, row, layer, g_mix, w_in, seg, aqg, akg, dqg, dkg, tables, bvg, ws_cat, bs_t,
                   *, use_rope):
    b, t, d = x.shape
    cosq, sinq, cosk, sink = tables
    tm = min(IN_TM, t)
    tok = lambda i, bi: (bi, i, 0)
    const2 = lambda i, bi: (0, 0)
    pos2 = (lambda i, bi: (i, 0)) if use_rope else const2
    assert tm % MIX_TQ == 0 and MIX_TQ % A_BLOCK == 0
    out_shapes = (
        jax.ShapeDtypeStruct((b, t, 256), BF16),
        jax.ShapeDtypeStruct((b, t, 128), BF16),
        jax.ShapeDtypeStruct((b, t // A_BLOCK, 128, A_BLOCK), BF16),
        jax.ShapeDtypeStruct((b, t, 256), BF16),
        jax.ShapeDtypeStruct((b, t, 256), F32),
        jax.ShapeDtypeStruct((b, t, 256), BF16),
        jax.ShapeDtypeStruct((b, t, 256), BF16),
        jax.ShapeDtypeStruct((b, t // MIX_TQ, 256, MIX_TQ), BF16),
    )
    blk4 = lambda i, bi: (bi, i, 0, 0)
    out_specs = tuple(
        pl.BlockSpec((1, tm // A_BLOCK, 128, A_BLOCK), blk4) if k == 2 else
        pl.BlockSpec((1, tm // MIX_TQ, 256, MIX_TQ), blk4) if k == 7 else
        pl.BlockSpec((1, tm, s.shape[-1]), tok)
        for k, s in enumerate(out_shapes))
    return pl.pallas_call(
        functools.partial(_in_kernel, use_rope=use_rope),
        out_shape=out_shapes,
        grid=(t // tm, b),
        in_specs=[
            pl.BlockSpec((1, tm, d), tok),
            _mod_spec(layer, row, 0, batch_axis=1),
            _mod_spec(layer, row, 1, batch_axis=1),
            _layer_spec(layer, (1, d)),
            _layer_spec(layer, (d, IN_TOTAL)),
            pl.BlockSpec((256, 256), const2),
            _layer_spec(layer, (1, 256)),
            _layer_spec(layer, (1, 128)),
            _layer_spec(layer, (1, 256)),
            _layer_spec(layer, (1, 256)),
            pl.BlockSpec((tm, 256), pos2),
            pl.BlockSpec((tm, 256), pos2),
            pl.BlockSpec((tm, 128), pos2),
            pl.BlockSpec((tm, 128), pos2),
            _layer_spec(layer, (1, 256)),
            _layer_spec(layer, (B_CHUNK, B_GROUPS * B_CHUNK)),
            _layer_spec(layer, (B_CHUNK, GROUP_W)),
        ],
        out_specs=out_specs,
        scratch_shapes=[pltpu.VMEM((tm, 384), F32)],
        compiler_params=pltpu.CompilerParams(
            dimension_semantics=("arbitrary", "arbitrary"), vmem_limit_bytes=VMEM_LIMIT),
        name="in_projection_rope" if use_rope else "in_projection_ctx",
    )(x, mod, mod, g_mix, w_in, seg, aqg, akg, dqg, dkg, cosq, sinq, cosk, sink, bvg, ws_cat, bs_t)


def _ctx_kv_kernel(x_ref, sh_ref, sc_ref, g_ref, w_ref, seg_ref, akg_ref, dkg_ref,
                   ka_ref, va_ref, kd_ref, vd_ref, v_scr):
    xt = x_ref[0]
    tm = xt.shape[0]
    h = _norm_modulate(xt, g_ref[...], sc_ref[...], sh_ref[...])
    seg = seg_ref[...]
    pa = _dot(h, w_ref[:, OFF_AK:OFF_BU])
    pd = _dot(h, w_ref[:, OFF_DK:IN_TOTAL])
    ka_ref[0] = _head_norm(pa[:, 0:128], seg[:128, :128], akg_ref[...]).astype(BF16)
    kd_ref[0] = _head_norm(pd[:, 0:256], seg, dkg_ref[...]).astype(BF16)
    v_scr[:, 0:128] = pa[:, 128:256]
    v_scr[:, 128:384] = pd[:, 256:512]
    vat = v_scr[:, 0:128].T.astype(BF16)
    for j in range(tm // A_BLOCK):
        va_ref[0, j] = vat[:, j * A_BLOCK:(j + 1) * A_BLOCK]
    vdt = v_scr[:, 128:384].T.astype(BF16)
    for j in range(tm // MIX_TQ):
        vd_ref[0, j] = vdt[:, j * MIX_TQ:(j + 1) * MIX_TQ]


def _ctx_kv_projection(x, mod, row, layer, g_mix, w_in, seg, akg, dkg):
    b, t, d = x.shape
    tm = min(IN_TM, t)
    tok = lambda i, bi: (bi, i, 0)
    blk4 = lambda i, bi: (bi, i, 0, 0)
    out_shapes = (
        jax.ShapeDtypeStruct((b, t, 128), BF16),
        jax.ShapeDtypeStruct((b, t // A_BLOCK, 128, A_BLOCK), BF16),
        jax.ShapeDtypeStruct((b, t, 256), BF16),
        jax.ShapeDtypeStruct((b, t // MIX_TQ, 256, MIX_TQ), BF16),
    )
    return pl.pallas_call(
        _ctx_kv_kernel,
        out_shape=out_shapes,
        grid=(t // tm, b),
        in_specs=[
            pl.BlockSpec((1, tm, d), tok),
            _mod_spec(layer, row, 0, batch_axis=1),
            _mod_spec(layer, row, 1, batch_axis=1),
            _layer_spec(layer, (1, d)),
            _layer_spec(layer, (d, IN_TOTAL)),
            pl.BlockSpec((256, 256), lambda i, bi: (0, 0)),
            _layer_spec(layer, (1, 128)),
            _layer_spec(layer, (1, 256)),
        ],
        out_specs=(pl.BlockSpec((1, tm, 128), tok),
                   pl.BlockSpec((1, tm // A_BLOCK, 128, A_BLOCK), blk4),
                   pl.BlockSpec((1, tm, 256), tok),
                   pl.BlockSpec((1, tm // MIX_TQ, 256, MIX_TQ), blk4)),
        scratch_shapes=[pltpu.VMEM((tm, 384), F32)],
        compiler_params=pltpu.CompilerParams(
            dimension_semantics=("arbitrary", "arbitrary"), vmem_limit_bytes=VMEM_LIMIT),
        name="ctx_kv_projection",
    )(x, mod, mod, g_mix, w_in, seg, akg, dkg)


def _softmax_pv_t(parts, extra_logit):
    m = None
    for s, _ in parts:
        pm = jnp.max(s, axis=0, keepdims=True)
        m = pm if m is None else jnp.maximum(m, pm)
    if extra_logit is not None:
        m = jnp.maximum(m, extra_logit)
    acc = None
    for s, vt in parts:
        dh, n = vt.shape
        vt_ones = jnp.concatenate([vt, jnp.ones((BF16_SUBLANES, n), BF16)], axis=0)
        pv = _dot(vt_ones, jnp.exp2(s - m).astype(BF16))
        acc = pv if acc is None else acc + pv
    denom = acc[dh:dh + 1]
    if extra_logit is not None:
        denom = denom + jnp.exp2(extra_logit - m)
    return acc[:dh] / denom


def _head_lanes(q, head, width=HEAD_DIM):
    lane = lax.broadcasted_iota(jnp.int32, q.shape, 1)
    return jnp.where((lane >= head * width) & (lane < (head + 1) * width), q, jnp.zeros_like(q))


def _pool_tile(c_ref, inv, t0, n_tok, tq):
    y = c_ref[0, pl.ds(t0, tq), :]
    lo_start = pl.multiple_of(jnp.maximum(t0 - POOL_HALO, 0), POOL_HALO)
    hi_start = pl.multiple_of(jnp.minimum(t0 + tq, n_tok - POOL_HALO), POOL_HALO)
    lo = c_ref[0, pl.ds(lo_start, POOL_HALO), :]
    hi = c_ref[0, pl.ds(hi_start, POOL_HALO), :]
    lo = jnp.where(t0 > 0, lo, 0.0)
    hi = jnp.where(t0 + tq < n_tok, hi, 0.0)
    ypad = jnp.concatenate([lo, y, hi], axis=0)

    n = tq + 2 * POOL_HALO
    w2 = ypad + pltpu.roll(ypad, 1, axis=0)
    w4 = pltpu.roll(w2, 1, axis=0) + pltpu.roll(w2, n - 1, axis=0)
    w8 = pltpu.roll(w4, 2, axis=0) + pltpu.roll(w4, n - 2, axis=0)
    w16 = pltpu.roll(w8, 4, axis=0) + pltpu.roll(w8, n - 4, axis=0)
    sums = tuple(w[POOL_HALO:POOL_HALO + tq] for w in (w2, w4, w8, w16))

    lane = lax.broadcasted_iota(jnp.int32, (tq, GROUP_W), 1)
    total = sums[0]
    for gi in range(1, len(C_POOLS)):
        total = jnp.where(lane >= gi * C_GROUP_W, sums[gi], total)
    return total * inv - y


def _mix_kernel(*refs, local, n_tok, layer):
    x_ref = refs[1]
    deferred = []
    for sub in range(x_ref.shape[1] // MIX_TQ):
        _mix_subtile(*refs, sub=sub, deferred=deferred, local=local, n_tok=n_tok, layer=layer)
    while deferred:
        deferred.pop(0)()


def _mix_subtile(sink_ref, x_ref, g1_ref, qa_ref, ka_ref, va_ref, kca_ref, vca_ref, ob_ref, c_ref,
                 inv_ref, wp_ref, cs_ref, qd_ref, kd_ref, vd_ref, kcd_ref, vcd_ref, bias_ref, amask_ref, wo_ref,
                 o_ref, oa_scr, od_scr, mix_scr, *, sub, deferred, local, n_tok, layer):
    tq = MIX_TQ
    ti = pl.program_id(1) * (x_ref.shape[1] // tq) + sub
    t0 = pl.multiple_of(ti * tq, tq)
    tile = slice(sub * tq, (sub + 1) * tq)

    pooled = _pool_tile(c_ref, inv_ref[tile, :], t0, n_tok, tq)
    out_c = (_dot(pooled.astype(BF16), wp_ref[...]) * cs_ref[...]).astype(BF16)


    if local:
        n_tiles = n_tok // tq
        half = D_WIN_ROWS // 2
        pair_entries = []
        for jj in range(3 * tq // GRID_W):
            row_entries = []
            for i in range(0, tq // GRID_W, 2):
                interior = PAIR_INTERIOR + (jj - half - i) + PAIR_COUNT // 2 - 1
                first = PAIR_EDGE + (jj - i) + PAIR_COUNT // 2 - 1 if jj < D_WIN_ROWS else PAIR_MASKED
                last = (PAIR_EDGE + (jj - D_WIN_ROWS - i) + PAIR_COUNT // 2 - 1 if jj >= half
                        else PAIR_MASKED)
                row_entries.append(jnp.where(ti == 0, first, jnp.where(ti == n_tiles - 1, last, interior)))
            pair_entries.append(row_entries)

    n_ctx_blk = vca_ref.shape[1]
    jobs = []

    def a_job(blk):
        cols = slice(blk * A_BLOCK, (blk + 1) * A_BLOCK)

        def scores():
            qblk = qa_ref[0, sub * tq + blk * A_BLOCK:sub * tq + (blk + 1) * A_BLOCK, :]
            qcat = jnp.concatenate([qblk[:, :128], qblk[:, 128:]], axis=0)
            qm = jnp.concatenate([_head_lanes(qcat, kv) for kv in range(A_KV_HEADS)], axis=0)
            if local:
                nblk = ti * (tq // A_BLOCK) + blk
                b0 = jnp.clip(nblk - 1, 0, n_tok // A_BLOCK - 3)
                kstart = pl.multiple_of(b0 * A_BLOCK, A_BLOCK)
                kwin = ka_ref[0, pl.ds(kstart, 3 * A_BLOCK), :]
                last_blk = n_tok // A_BLOCK - 1
                kind = jnp.where(nblk == 0, 1, jnp.where(nblk == last_blk, 2, 0))
                s_all = _dot_nt(jnp.concatenate([kwin, kca_ref[0]], axis=0), qm)
                s_loc, s_ctx = s_all[:3 * A_BLOCK], s_all[3 * A_BLOCK:]
            else:
                s_ctx = _dot_nt(kca_ref[0], qm)
            per_head = []
            for kv in range(A_KV_HEADS):
                rows = slice(kv * HEAD_DIM, (kv + 1) * HEAD_DIM)
                qsl = slice(kv * 2 * A_BLOCK, (kv + 1) * 2 * A_BLOCK)
                parts = []
                if local:
                    s_win = s_loc[:, qsl] + amask_ref[kind]
                    for j in range(3):
                        parts.append((s_win[j * A_BLOCK:(j + 1) * A_BLOCK], va_ref[0, b0 + j, rows, :]))
                for j in range(n_ctx_blk):
                    parts.append((s_ctx[j * A_BLOCK:(j + 1) * A_BLOCK, qsl], vca_ref[0, j, rows, :]))
                per_head.append(parts)
            return per_head

        def finish(per_head):
            for kv, parts in enumerate(per_head):
                col = lax.broadcasted_iota(jnp.int32, (1, 2 * A_BLOCK), 1)
                sink = jnp.where(col < A_BLOCK, sink_ref[layer, 2 * kv], sink_ref[layer, 2 * kv + 1])
                o2 = _softmax_pv_t(parts, sink)
                oa_scr[sub, (2 * kv) * 64:(2 * kv + 1) * 64, cols] = o2[:, :A_BLOCK]
                oa_scr[sub, (2 * kv + 1) * 64:(2 * kv + 2) * 64, cols] = o2[:, A_BLOCK:]

        return scores, finish

    def d_job(pair):
        heads = (2 * pair, 2 * pair + 1)

        def scores():
            qd = qd_ref[0, tile, :]
            qm = jnp.concatenate([_head_lanes(qd, hd) for hd in heads], axis=0)
            if local:
                d0 = jnp.clip(ti - 1, 0, n_tok // tq - 3)
                kwin = kd_ref[0, pl.ds(pl.multiple_of(d0 * tq, tq), 3 * tq), :]
                s_all = _dot_nt(jnp.concatenate([kwin, kcd_ref[0]], axis=0), qm)
                s_loc, s_ctx = s_all[:3 * tq], s_all[3 * tq:]
            else:
                s_ctx = _dot_nt(kcd_ref[0], qm)
            per_head = []
            for k, hd in enumerate(heads):
                rows = slice(hd * HEAD_DIM, (hd + 1) * HEAD_DIM)
                qsl = slice(k * tq, (k + 1) * tq)
                parts = []
                if local:
                    bias = jnp.concatenate(
                        [jnp.concatenate([bias_ref[hd, e] for e in row_entries], axis=1)
                         for row_entries in pair_entries], axis=0)
                    s_win = s_loc[:, qsl] + bias
                    for j in range(3):
                        parts.append((s_win[j * tq:(j + 1) * tq], vd_ref[0, d0 + j, rows, :]))
                parts.append((s_ctx[:, qsl], vcd_ref[0, 0, rows, :]))
                per_head.append(parts)
            return per_head

        def finish(per_head):
            for hd, parts in zip(heads, per_head):
                od_scr[sub, hd * HEAD_DIM:(hd + 1) * HEAD_DIM, :] = _softmax_pv_t(parts, None)

        return scores, finish

    for blk in range(tq // A_BLOCK):
        jobs.append(a_job(blk))
    for pair in range(D_HEADS // 2):
        jobs.append(d_job(pair))
    ahead = min(SCORE_LOOKAHEAD, len(jobs))
    pending = [jobs[j][0]() for j in range(ahead)]
    for j, (_, finish) in enumerate(jobs):
        if j + ahead < len(jobs):
            pending.append(jobs[j + ahead][0]())
        if deferred:
            deferred.pop(0)()
        finish(pending.pop(0))

    mix_scr[sub, :, 0:256] = oa_scr[sub].T.astype(BF16)
    mix_scr[sub, :, 256:512] = ob_ref[0, tile, :]
    mix_scr[sub, :, 512:768] = out_c
    mix_scr[sub, :, 768:1024] = od_scr[sub].T.astype(BF16)

    def out_chunk(c):
        cols = slice(c * OUT_CHUNK, (c + 1) * OUT_CHUNK)

        def run():
            mix = _dot(mix_scr[sub], wo_ref[:, cols])
            o_ref[0, tile, cols] = x_ref[0, tile, cols] + g1_ref[:, cols] * mix

        return run

    deferred.extend(out_chunk(c) for c in range(D_MODEL // OUT_CHUNK))


def _mixer(x, mod, row, layer, qa, ka, va, kca, vca, ob, cin, inv_cnt, wpool, cscale, qd, kd, vd, kcd, vcd, bias,
           amask, w_out, sink, *, local):
    b, t, d = x.shape
    lc = kca.shape[1]
    tq = MIX_TQ
    ts = min(MIX_STEP_TILES * tq, t)
    n_sub = ts // tq
    tok = lambda bi, i: (bi, i, 0)
    per_b = lambda bi, i: (bi, 0, 0)
    per_b4 = lambda bi, i: (bi, 0, 0, 0)

    return pl.pallas_call(
        functools.partial(_mix_kernel, local=local, n_tok=t, layer=layer),
        out_shape=jax.ShapeDtypeStruct((b, t, d), F32),
        grid=(b, t // ts),
        in_specs=[
            pl.BlockSpec(memory_space=pltpu.SMEM),
            pl.BlockSpec((1, ts, d), tok),
            _mod_spec(layer, row, 2),
            pl.BlockSpec((1, ts, 256), tok),
            pl.BlockSpec((1, t, 128), per_b),
            pl.BlockSpec((1, t // A_BLOCK, 128, A_BLOCK), per_b4),
            pl.BlockSpec((1, lc, 128), per_b),
            pl.BlockSpec((1, lc // A_BLOCK, 128, A_BLOCK), per_b4),
            pl.BlockSpec((1, ts, 256), tok),
            pl.BlockSpec((1, t, 256), per_b),
            pl.BlockSpec((ts, 256), lambda bi, i: (i, 0)),
            _layer_spec(layer, (256, 256)),
            _layer_spec(layer, (1, 256)),
            pl.BlockSpec((1, ts, 256), tok),
            pl.BlockSpec((1, t, 256), per_b),
            pl.BlockSpec((1, t // tq, 256, tq), per_b4),
            pl.BlockSpec((1, lc, 256), per_b),
            pl.BlockSpec((1, lc // tq, 256, tq), per_b4),
            _layer_spec(layer, (D_HEADS, PAIR_MASKED + 1, GRID_W, 2 * GRID_W)),
            pl.BlockSpec((3, 3 * A_BLOCK, 2 * A_BLOCK), lambda bi, i: (0, 0, 0)),
            _layer_spec(layer, (d, d)),
        ],
        out_specs=pl.BlockSpec((1, ts, d), tok),
        scratch_shapes=[pltpu.VMEM((n_sub, GROUP_W, tq), F32), pltpu.VMEM((n_sub, GROUP_W, tq), F32),
                        pltpu.VMEM((n_sub, tq, d), BF16)],
        compiler_params=pltpu.CompilerParams(
            dimension_semantics=("arbitrary", "arbitrary"), vmem_limit_bytes=VMEM_LIMIT),
        name="mixer_latent" if local else "mixer_ctx",
    )(sink, x, mod, qa, ka, va, kca, vca, ob, cin, inv_cnt, wpool, cscale, qd, kd, vd, kcd, vcd, bias, amask,
      w_out)


def _ffn_kernel(x_ref, sh_ref, sc_ref, g2_ref, gn_ref, wg_ref, wu_ref, wd_ref, o_ref, act_scr):
    xt = x_ref[0]
    h = _norm_modulate(xt, gn_ref[...], sc_ref[...], sh_ref[...])
    for c in range(FF_DIM // FFN_TF):
        cols = slice(c * FFN_TF, (c + 1) * FFN_TF)
        gate = _dot(h, wg_ref[:, cols])
        up = _dot(h, wu_ref[:, cols])
        act_scr[:, cols] = (gate * jax.nn.sigmoid(gate) * up).astype(BF16)
    o_ref[0] = xt + g2_ref[...] * _dot(act_scr[...], wd_ref[...])


def _ffn(x, mod, row, layer, g_ffn, wg, wu, wd):
    b, t, d = x.shape
    tm = min(FFN_TM, t)
    tok = lambda bi, i: (bi, i, 0)
    per_layer = lambda bi, i: (layer, 0, 0)
    resident = pl.Buffered(1)
    return pl.pallas_call(
        _ffn_kernel,
        out_shape=jax.ShapeDtypeStruct((b, t, d), F32),
        grid=(b, t // tm),
        in_specs=[
            pl.BlockSpec((1, tm, d), tok),
            _mod_spec(layer, row, 3),
            _mod_spec(layer, row, 4),
            _mod_spec(layer, row, 5),
            _layer_spec(layer, (1, d)),
            pl.BlockSpec((None, d, FF_DIM), per_layer, pipeline_mode=resident),
            pl.BlockSpec((None, d, FF_DIM), per_layer, pipeline_mode=resident),
            pl.BlockSpec((None, FF_DIM, d), per_layer, pipeline_mode=resident),
        ],
        out_specs=pl.BlockSpec((1, tm, d), tok),
        scratch_shapes=[pltpu.VMEM((tm, FF_DIM), BF16)],
        compiler_params=pltpu.CompilerParams(
            dimension_semantics=("arbitrary", "arbitrary"), vmem_limit_bytes=VMEM_LIMIT),
        name="ffn",
    )(x, mod, mod, mod, g_ffn, wg, wu, wd)


def _rope_tables(s):
    t = np.arange(s)
    half = 16
    inv = np.power(np.float32(ROPE_BASE), -np.arange(half, dtype=np.float32) / half).astype(np.float32)
    ang_r = (t // GRID_W).astype(np.float32)[:, None] * inv[None, :]
    ang_c = (t % GRID_W).astype(np.float32)[:, None] * inv[None, :]
    cos = np.concatenate([np.cos(ang_r)] * 2 + [np.cos(ang_c)] * 2, axis=-1)
    sin = np.concatenate([-np.sin(ang_r), np.sin(ang_r), -np.sin(ang_c), np.sin(ang_c)], axis=-1)
    return cos.astype(np.float32), sin.astype(np.float32)


def _pool_inverse_counts(n_tok):
    t = np.arange(n_tok)
    cols = []
    for w in C_POOLS:
        lo = np.clip(t - w // 2, 0, n_tok)
        hi = np.clip(t - w // 2 + w, 0, n_tok)
        cols.append(np.repeat((np.float32(1.0) / (hi - lo).astype(np.float32))[:, None], C_GROUP_W, axis=1))
    return jnp.asarray(np.concatenate(cols, axis=1), F32)


def _window_mask():
    kk = np.arange(3 * A_BLOCK)[:, None]
    qq = np.arange(2 * A_BLOCK)[None, :] % A_BLOCK
    kinds = [np.where(np.abs(kk - qq - back * A_BLOCK) <= A_WINDOW, 0.0, NEG_INF) for back in (1, 0, 2)]
    return jnp.asarray(np.stack(kinds), F32)


PAIR_COUNT = 2 * (D_WIN_ROWS - 1)
PAIR_INTERIOR = 0
PAIR_EDGE = PAIR_COUNT
PAIR_MASKED = 2 * PAIR_COUNT


def _neighbour_bias_pairs(rpb):
    cidx = np.arange(GRID_W)
    col_start = np.clip(cidx - D_WIN_COLS // 2, 0, GRID_W - D_WIN_COLS)
    col_ok_t = ((cidx[None, :] >= col_start[:, None]) & (cidx[None, :] < col_start[:, None] + D_WIN_COLS)).T
    coff_t = np.clip(cidx[:, None] - cidx[None, :], -(D_WIN_COLS - 1), D_WIN_COLS - 1) + (D_WIN_COLS - 1)
    select = (np.arange(2 * D_WIN_COLS - 1)[:, None, None] == coff_t[None]).astype(np.float32)
    blocks = lax.dot_general(rpb.astype(F32), jnp.asarray(select), (((3,), (0,)), ((), ())),
                             precision=lax.Precision.HIGHEST)
    blocks = jnp.where(col_ok_t, blocks, NEG_INF)
    d = np.arange(-(D_WIN_ROWS - 1), D_WIN_ROWS)
    seen = (d >= -(D_WIN_ROWS // 2)) & (d < D_WIN_ROWS // 2)
    interior = jnp.where(seen[:, None, None], blocks, NEG_INF)
    pair = lambda t: jnp.concatenate([t[:, :, 1:], t[:, :, :-1]], axis=-1)
    masked = jnp.full(blocks.shape[:2] + (1, GRID_W, 2 * GRID_W), NEG_INF, F32)
    return jnp.concatenate([pair(interior), pair(blocks), masked], axis=2)


def kernel(x, c, ctx, c_ctx, w_mod, b_mod, g_mix, g_ffn, w_in, w_out, a_q_gain, a_k_gain, a_sink,
           b_v_gain, b_w_s, b_b_s, c_w_pool, c_scale, d_q_gain, d_k_gain, d_rpb, w_gate, w_up, w_down):
    bsz, s, d = x.shape
    lc = ctx.shape[1]
    n_rows = s // GRID_W
    assert s % MIX_TQ == 0 and n_rows >= 2 * D_WIN_ROWS and MIX_TQ // GRID_W == D_WIN_ROWS // 2

    cvec = jnp.zeros((MOD_ROWS, d), F32).at[:bsz].set(c).at[bsz].set(c_ctx)
    mod = _modulation(cvec, w_mod, b_mod)
    lat_row = lambda bi: bi
    ctx_row = lambda bi: bsz

    seg = jnp.asarray(np.kron(np.eye(4), np.full((64, 64), 1.0 / 64)), BF16)
    cos1, sin1 = _rope_tables(s)
    tables_lat = tuple(jnp.asarray(np.tile(tab, (1, reps)))
                       for tab, reps in ((cos1, 4), (sin1, 4), (cos1, 2), (sin1, 2)))
    tables_ctx = tuple(jnp.zeros((min(IN_TM, bsz * lc), w), F32) for w in (256, 256, 128, 128))

    amask = _window_mask()
    inv_lat, inv_ctx = _pool_inverse_counts(s), _pool_inverse_counts(lc)
    bias_all = _neighbour_bias_pairs(d_rpb * LOG2E)

    w_in_b = w_in.astype(BF16)
    w_out_b = w_out.astype(BF16)
    wg_b, wu_b, wd_b = w_gate.astype(BF16), w_up.astype(BF16), w_down.astype(BF16)
    gmix = g_mix.reshape(DEPTH, 1, d)
    gffn = g_ffn.reshape(DEPTH, 1, d)
    aqg = jnp.tile(a_q_gain, (1, 4)).reshape(DEPTH, 1, 256)
    akg = jnp.tile(a_k_gain, (1, 2)).reshape(DEPTH, 1, 128)
    dqg = jnp.tile(d_q_gain, (1, 4)).reshape(DEPTH, 1, 256)
    dkg = jnp.tile(d_k_gain, (1, 4)).reshape(DEPTH, 1, 256)
    bvg = b_v_gain.reshape(DEPTH, 1, 256)
    ws_cat = jnp.transpose(b_w_s, (0, 2, 1, 3)).reshape(DEPTH, B_CHUNK, B_GROUPS * B_CHUNK).astype(BF16)
    bs_t = jnp.repeat(jnp.swapaxes(b_b_s, 1, 2), B_GROUP_W, axis=2)
    wpool = jnp.einsum('lgcd,gh->lgchd', c_w_pool, jnp.eye(len(C_POOLS), dtype=F32)
                       ).reshape(DEPTH, GROUP_W, GROUP_W).astype(BF16)
    cscale = c_scale.reshape(DEPTH, 1, 256)
    sink = a_sink * LOG2E
    common = (gmix, w_in_b, seg, aqg, akg, dqg, dkg)

    xc = ctx
    for l in range(DEPTH):
        last = l == DEPTH - 1
        lat = _in_projection(x, mod, lat_row, l, *common, tables_lat, bvg, ws_cat, bs_t, use_rope=True)
        xc_flat = xc.reshape(1, bsz * lc, d)
        per_batch = lambda o: o.reshape((bsz, o.shape[1] // bsz) + o.shape[2:])
        if last:
            cka, cva, ckd, cvd = map(per_batch, _ctx_kv_projection(
                xc_flat, mod, ctx_row, l, gmix, w_in_b, seg, akg, dkg))
        else:
            cqa, cka, cva, cob, ccin, cqd, ckd, cvd = map(per_batch, _in_projection(
                xc_flat, mod, ctx_row, l, *common, tables_ctx, bvg, ws_cat, bs_t, use_rope=False))
        qa, ka, va, ob, cin, qd, kd, vd = lat

        x = _mixer(x, mod, lat_row, l, qa, ka, va, cka, cva, ob, cin, inv_lat, wpool, cscale, qd, kd, vd, ckd, cvd,
                   bias_all, amask, w_out_b, sink, local=True)
        x = _ffn(x, mod, lat_row, l, gffn, wg_b, wu_b, wd_b)
        if not last:
            xc = _mixer(xc, mod, ctx_row, l, cqa, cka, cva, cka, cva, cob, ccin, inv_ctx, wpool, cscale, cqd, ckd,
                        cvd, ckd, cvd, bias_all, amask, w_out_b, sink, local=False)
            xc = _ffn(xc.reshape(1, bsz * lc, d), mod, ctx_row, l, gffn, wg_b, wu_b, wd_b).reshape(bsz, lc, d)
    return x
```

```python
import functools

import jax
import jax.numpy as jnp
import numpy as np
from jax import lax
from jax.experimental import pallas as pl
from jax.experimental.pallas import tpu as pltpu

F32 = jnp.float32
BF16 = jnp.bfloat16

D_MODEL = 1024
DEPTH = 2
GRID_W = 64
HEAD_DIM = 64
GROUP_W = 256
A_HEADS = 4
A_KV_HEADS = 2
A_WINDOW = 128
A_BLOCK = 128
B_CHUNK = 128
B_GROUPS = 4
B_GROUP_W = 64
C_POOLS = (2, 4, 8, 16)
C_GROUP_W = 64
D_HEADS = 4
D_WIN_ROWS = 8
D_WIN_COLS = 16
FF_DIM = 2816
ROPE_BASE = 10000.0
NORM_EPS = 1e-6
NEG_INF = -1e30
LOG2E = 1.4426950408889634
IN_TOTAL = 2048
OFF_AQ, OFF_AK, OFF_AV, OFF_BU, OFF_BV, OFF_C, OFF_DQ, OFF_DK, OFF_DV = (
    0, 256, 384, 512, 768, 1024, 1280, 1536, 1792)

MOD_ROWS = 16
MOD_TN = D_MODEL
IN_TM = 1024
MIX_TQ = 256
FFN_TM = 1024
FFN_TF = 256
BF16_SUBLANES = 16
POOL_HALO = 8
MIX_STEP_TILES = 4
OUT_CHUNK = 256
SCORE_LOOKAHEAD = 2
VMEM_LIMIT = 56 * 1024 * 1024


def _dot(a, b):
    return jnp.dot(a, b, preferred_element_type=F32)


def _dot_nt(a, b):
    return lax.dot_general(a, b, (((1,), (1,)), ((), ())), preferred_element_type=F32)


def _norm_modulate(xt, gain, scale, shift):
    ms = jnp.mean(xt * xt, axis=-1, keepdims=True)
    return (xt * lax.rsqrt(ms + NORM_EPS) * (gain * (1.0 + scale)) + shift).astype(BF16)


def _mod_kernel(c_ref, w_ref, b_ref, o_ref):
    cv = c_ref[...]
    act = cv * jax.nn.sigmoid(cv)
    o_ref[:, 0, :] = _dot(act.astype(BF16), w_ref[0].astype(BF16)) + b_ref[0]


def _modulation(cvec, w_mod, b_mod):
    depth, d, n = w_mod.shape
    return pl.pallas_call(
        _mod_kernel,
        out_shape=jax.ShapeDtypeStruct((depth, MOD_ROWS, n // MOD_TN, 1, MOD_TN), F32),
        grid=(depth, n // MOD_TN),
        in_specs=[
            pl.BlockSpec((MOD_ROWS, d), lambda l, j: (0, 0)),
            pl.BlockSpec((1, d, MOD_TN), lambda l, j: (l, 0, j)),
            pl.BlockSpec((1, 1, MOD_TN), lambda l, j: (l, 0, j)),
        ],
        out_specs=pl.BlockSpec((None, MOD_ROWS, None, 1, MOD_TN), lambda l, j: (l, 0, j, 0, 0)),
        compiler_params=pltpu.CompilerParams(
            dimension_semantics=("arbitrary", "arbitrary"), vmem_limit_bytes=VMEM_LIMIT),
        name="modulation",
    )(cvec, w_mod, b_mod.reshape(depth, 1, n))


def _head_norm(t, seg, gain):
    sq = t * t
    hi = sq.astype(BF16)
    lo = (sq - hi.astype(F32)).astype(BF16)
    ms = _dot(hi, seg) + _dot(lo, seg)
    return t * lax.rsqrt(ms + NORM_EPS) * gain


def _rope(y, cos, sin_signed):
    w = y.shape[-1]
    up = pltpu.roll(y, w - 16, axis=1)
    dn = pltpu.roll(y, 16, axis=1)
    lane = lax.broadcasted_iota(jnp.int32, y.shape, 1)
    swapped = jnp.where((lane & 31) < 16, up, dn)
    return y * cos + swapped * sin_signed


def _swap_middle_heads(q):
    lane = lax.broadcasted_iota(jnp.int32, q.shape, 1)
    from_right = pltpu.roll(q, 3 * HEAD_DIM, axis=1)
    from_left = pltpu.roll(q, HEAD_DIM, axis=1)
    return jnp.where((lane >= HEAD_DIM) & (lane < 2 * HEAD_DIM), from_right,
                     jnp.where((lane >= 2 * HEAD_DIM) & (lane < 3 * HEAD_DIM), from_left, q))


def _in_kernel(x_ref, sh_ref, sc_ref, g_ref, w_ref, seg_ref, aqg_ref, akg_ref, dqg_ref, dkg_ref,
               cosq_ref, sinq_ref, cosk_ref, sink_ref, bvg_ref, ws_ref, bs_ref,
               qa_ref, ka_ref, va_ref, ob_ref, c_ref, qd_ref, kd_ref, vd_ref, v_scr, *, use_rope):
    xt = x_ref[0]
    h = _norm_modulate(xt, g_ref[...], sc_ref[...], sh_ref[...])

    group_w = IN_TOTAL // 4
    assert (OFF_BU, OFF_C, OFF_DK) == (group_w, 2 * group_w, 3 * group_w)

    def group(k):
        return _dot(h, w_ref[:, k * group_w:(k + 1) * group_w].astype(BF16))

    seg = seg_ref[...]
    scale = HEAD_DIM ** -0.5 * LOG2E
    tm = xt.shape[0]

    def finish_a(p):
        qa = _head_norm(p[:, 0:256], seg, aqg_ref[...])
        ka = _head_norm(p[:, 256:384], seg[:128, :128], akg_ref[...])
        if use_rope:
            qa = _rope(qa, cosq_ref[...], sinq_ref[...])
            ka = _rope(ka, cosk_ref[...], sink_ref[...])
        qa_ref[0] = (_swap_middle_heads(qa) * scale).astype(BF16)
        ka_ref[0] = ka.astype(BF16)
        v_scr[:, 0:128] = p[:, 384:512]
        vat = v_scr[:, 0:128].T.astype(BF16)
        for j in range(tm // A_BLOCK):
            va_ref[0, j] = vat[:, j * A_BLOCK:(j + 1) * A_BLOCK]

    def finish_b(p):
        u = jax.nn.gelu(p[:, 0:256])
        v = jax.nn.gelu(p[:, 256:512])
        mu = jnp.mean(v, axis=-1, keepdims=True)
        vc = v - mu
        var = jnp.mean(vc * vc, axis=-1, keepdims=True)
        vn = (vc * lax.rsqrt(var + NORM_EPS) * bvg_ref[...]).astype(BF16)
        lane = lax.broadcasted_iota(jnp.int32, (B_CHUNK, GROUP_W), 1)
        zero = jnp.zeros((B_CHUNK, GROUP_W), BF16)
        for ch in range(tm // B_CHUNK):
            rows = slice(ch * B_CHUNK, (ch + 1) * B_CHUNK)
            vch = vn[rows]
            stacked = jnp.concatenate(
                [jnp.where((lane >= g * B_GROUP_W) & (lane < (g + 1) * B_GROUP_W), vch, zero)
                 for g in range(B_GROUPS)], axis=0)
            z = _dot(ws_ref[...], stacked) + bs_ref[...]
            ob_ref[0, rows, :] = (u[rows] * z).astype(BF16)

    def finish_c(p):
        c_ref[0] = p[:, 0:256]
        qd = _head_norm(p[:, 256:512], seg, dqg_ref[...])
        qd_ref[0] = (qd * scale).astype(BF16)

    def finish_d(p):
        kd_ref[0] = _head_norm(p[:, 0:256], seg, dkg_ref[...]).astype(BF16)
        v_scr[:, 128:384] = p[:, 256:512]
        vdt = v_scr[:, 128:384].T.astype(BF16)
        for j in range(tm // MIX_TQ):
            vd_ref[0, j] = vdt[:, j * MIX_TQ:(j + 1) * MIX_TQ]

    order = ((1, finish_b), (0, finish_a), (3, finish_d), (2, finish_c))
    pending = group(order[0][0])
    for pos, (_, finish) in enumerate(order):
        upcoming = group(order[pos + 1][0]) if pos + 1 < len(order) else None
        finish(pending)
        pending = upcoming


def _mod_spec(layer, row, which, batch_axis=0):
    return pl.BlockSpec((None, None, None, 1, D_MODEL),
                        lambda *idx: (layer, row(idx[batch_axis]), which, 0, 0))


def _layer_spec(layer, shape):
    return pl.BlockSpec((None,) + tuple(shape), lambda *_: (layer,) + (0,) * len(shape))


def _in_projection(x, mod, row, layer, g_mix, w_in, seg, aqg, akg, dqg, dkg, tables, bvg, ws_cat, bs_t,
                   *, use_rope):
    b, t, d = x.shape
    cosq, sinq, cosk, sink = tables
    tm = min(IN_TM, t)
    tok = lambda i, bi: (bi, i, 0)
    const2 = lambda i, bi: (0, 0)
    pos2 = (lambda i, bi: (i, 0)) if use_rope else const2
    assert tm % MIX_TQ == 0 and MIX_TQ % A_BLOCK == 0
    out_shapes = (
        jax.ShapeDtypeStruct((b, t, 256), BF16),
        jax.ShapeDtypeStruct((b, t, 128), BF16),
        jax.ShapeDtypeStruct((b, t // A_BLOCK, 128, A_BLOCK), BF16),
        jax.ShapeDtypeStruct((b, t, 256), BF16),
        jax.ShapeDtypeStruct((b, t, 256), F32),
        jax.ShapeDtypeStruct((b, t, 256), BF16),
        jax.ShapeDtypeStruct((b, t, 256), BF16),
        jax.ShapeDtypeStruct((b, t // MIX_TQ, 256, MIX_TQ), BF16),
    )
    blk4 = lambda i, bi: (bi, i, 0, 0)
    out_specs = tuple(
        pl.BlockSpec((1, tm // A_BLOCK, 128, A_BLOCK), blk4) if k == 2 else
        pl.BlockSpec((1, tm // MIX_TQ, 256, MIX_TQ), blk4) if k == 7 else
        pl.BlockSpec((1, tm, s.shape[-1]), tok)
        for k, s in enumerate(out_shapes))
    return pl.pallas_call(
        functools.partial(_in_kernel, use_rope=use_rope),
        out_shape=out_shapes,
        grid=(t // tm, b),
        in_specs=[
            pl.BlockSpec((1, tm, d), tok),
            _mod_spec(layer, row, 0, batch_axis=1),
            _mod_spec(layer, row, 1, batch_axis=1),
            _layer_spec(layer, (1, d)),
            pl.BlockSpec((None, d, IN_TOTAL), lambda *_: (layer, 0, 0), pipeline_mode=pl.Buffered(1)),
            pl.BlockSpec((256, 256), const2),
            _layer_spec(layer, (1, 256)),
            _layer_spec(layer, (1, 128)),
            _layer_spec(layer, (1, 256)),
            _layer_spec(layer, (1, 256)),
            pl.BlockSpec((tm, 256), pos2),
            pl.BlockSpec((tm, 256), pos2),
            pl.BlockSpec((tm, 128), pos2),
            pl.BlockSpec((tm, 128), pos2),
            _layer_spec(layer, (1, 256)),
            _layer_spec(layer, (B_CHUNK, B_GROUPS * B_CHUNK)),
            _layer_spec(layer, (B_CHUNK, GROUP_W)),
        ],
        out_specs=out_specs,
        scratch_shapes=[pltpu.VMEM((tm, 384), F32)],
        compiler_params=pltpu.CompilerParams(
            dimension_semantics=("arbitrary", "arbitrary"), vmem_limit_bytes=VMEM_LIMIT),
        name="in_projection_rope" if use_rope else "in_projection_ctx",
    )(x, mod, mod, g_mix, w_in, seg, aqg, akg, dqg, dkg, cosq, sinq, cosk, sink, bvg, ws_cat, bs_t)


def _ctx_kv_kernel(x_ref, sh_ref, sc_ref, g_ref, w_ref, seg_ref, akg_ref, dkg_ref,
                   ka_ref, va_ref, kd_ref, vd_ref, v_scr):
    xt = x_ref[0]
    tm = xt.shape[0]
    h = _norm_modulate(xt, g_ref[...], sc_ref[...], sh_ref[...])
    seg = seg_ref[...]
    pa = _dot(h, w_ref[:, OFF_AK:OFF_BU].astype(BF16))
    pd = _dot(h, w_ref[:, OFF_DK:IN_TOTAL].astype(BF16))
    ka_ref[0] = _head_norm(pa[:, 0:128], seg[:128, :128], akg_ref[...]).astype(BF16)
    kd_ref[0] = _head_norm(pd[:, 0:256], seg, dkg_ref[...]).astype(BF16)
    v_scr[:, 0:128] = pa[:, 128:256]
    v_scr[:, 128:384] = pd[:, 256:512]
    vat = v_scr[:, 0:128].T.astype(BF16)
    for j in range(tm // A_BLOCK):
        va_ref[0, j] = vat[:, j * A_BLOCK:(j + 1) * A_BLOCK]
    vdt = v_scr[:, 128:384].T.astype(BF16)
    for j in range(tm // MIX_TQ):
        vd_ref[0, j] = vdt[:, j * MIX_TQ:(j + 1) * MIX_TQ]


def _ctx_kv_projection(x, mod, row, layer, g_mix, w_in, seg, akg, dkg):
    b, t, d = x.shape
    tm = min(IN_TM, t)
    tok = lambda i, bi: (bi, i, 0)
    blk4 = lambda i, bi: (bi, i, 0, 0)
    out_shapes = (
        jax.ShapeDtypeStruct((b, t, 128), BF16),
        jax.ShapeDtypeStruct((b, t // A_BLOCK, 128, A_BLOCK), BF16),
        jax.ShapeDtypeStruct((b, t, 256), BF16),
        jax.ShapeDtypeStruct((b, t // MIX_TQ, 256, MIX_TQ), BF16),
    )
    return pl.pallas_call(
        _ctx_kv_kernel,
        out_shape=out_shapes,
        grid=(t // tm, b),
        in_specs=[
            pl.BlockSpec((1, tm, d), tok),
            _mod_spec(layer, row, 0, batch_axis=1),
            _mod_spec(layer, row, 1, batch_axis=1),
            _layer_spec(layer, (1, d)),
            pl.BlockSpec((None, d, IN_TOTAL), lambda *_: (layer, 0, 0), pipeline_mode=pl.Buffered(1)),
            pl.BlockSpec((256, 256), lambda i, bi: (0, 0)),
            _layer_spec(layer, (1, 128)),
            _layer_spec(layer, (1, 256)),
        ],
        out_specs=(pl.BlockSpec((1, tm, 128), tok),
                   pl.BlockSpec((1, tm // A_BLOCK, 128, A_BLOCK), blk4),
                   pl.BlockSpec((1, tm, 256), tok),
                   pl.BlockSpec((1, tm // MIX_TQ, 256, MIX_TQ), blk4)),
        scratch_shapes=[pltpu.VMEM((tm, 384), F32)],
        compiler_params=pltpu.CompilerParams(
            dimension_semantics=("arbitrary", "arbitrary"), vmem_limit_bytes=VMEM_LIMIT),
        name="ctx_kv_projection",
    )(x, mod, mod, g_mix, w_in, seg, akg, dkg)


def _softmax_pv_t(parts, extra_logit):
    m = None
    for s, _ in parts:
        pm = jnp.max(s, axis=0, keepdims=True)
        m = pm if m is None else jnp.maximum(m, pm)
    if extra_logit is not None:
        m = jnp.maximum(m, extra_logit)
    acc = None
    for s, vt in parts:
        dh, n = vt.shape
        vt_ones = jnp.concatenate([vt, jnp.ones((BF16_SUBLANES, n), BF16)], axis=0)
        pv = _dot(vt_ones, jnp.exp2(s - m).astype(BF16))
        acc = pv if acc is None else acc + pv
    denom = acc[dh:dh + 1]
    if extra_logit is not None:
        denom = denom + jnp.exp2(extra_logit - m)
    return acc[:dh] / denom


def _head_lanes(q, head, width=HEAD_DIM):
    lane = lax.broadcasted_iota(jnp.int32, q.shape, 1)
    return jnp.where((lane >= head * width) & (lane < (head + 1) * width), q, jnp.zeros_like(q))


def _pool_tile(c_ref, inv, t0, n_tok, tq):
    y = c_ref[0, pl.ds(t0, tq), :]
    lo_start = pl.multiple_of(jnp.maximum(t0 - POOL_HALO, 0), POOL_HALO)
    hi_start = pl.multiple_of(jnp.minimum(t0 + tq, n_tok - POOL_HALO), POOL_HALO)
    lo = c_ref[0, pl.ds(lo_start, POOL_HALO), :]
    hi = c_ref[0, pl.ds(hi_start, POOL_HALO), :]
    lo = jnp.where(t0 > 0, lo, 0.0)
    hi = jnp.where(t0 + tq < n_tok, hi, 0.0)
    ypad = jnp.concatenate([lo, y, hi], axis=0)

    n = tq + 2 * POOL_HALO
    w2 = ypad + pltpu.roll(ypad, 1, axis=0)
    w4 = pltpu.roll(w2, 1, axis=0) + pltpu.roll(w2, n - 1, axis=0)
    w8 = pltpu.roll(w4, 2, axis=0) + pltpu.roll(w4, n - 2, axis=0)
    w16 = pltpu.roll(w8, 4, axis=0) + pltpu.roll(w8, n - 4, axis=0)
    sums = tuple(w[POOL_HALO:POOL_HALO + tq] for w in (w2, w4, w8, w16))

    lane = lax.broadcasted_iota(jnp.int32, (tq, GROUP_W), 1)
    total = sums[0]
    for gi in range(1, len(C_POOLS)):
        total = jnp.where(lane >= gi * C_GROUP_W, sums[gi], total)
    return total * inv - y


def _mix_kernel(*refs, local, n_tok, layer):
    x_ref = refs[1]
    deferred = []
    for sub in range(x_ref.shape[1] // MIX_TQ):
        _mix_subtile(*refs, sub=sub, deferred=deferred, local=local, n_tok=n_tok, layer=layer)
    while deferred:
        deferred.pop(0)()


def _mix_subtile(sink_ref, x_ref, g1_ref, qa_ref, ka_ref, va_ref, kca_ref, vca_ref, ob_ref, c_ref,
                 inv_ref, wp_ref, cs_ref, qd_ref, kd_ref, vd_ref, kcd_ref, vcd_ref, bias_ref, amask_ref, wo_ref,
                 o_ref, oa_scr, od_scr, mix_scr, *, sub, deferred, local, n_tok, layer):
    tq = MIX_TQ
    ti = pl.program_id(1) * (x_ref.shape[1] // tq) + sub
    t0 = pl.multiple_of(ti * tq, tq)
    tile = slice(sub * tq, (sub + 1) * tq)

    pooled = _pool_tile(c_ref, inv_ref[tile, :], t0, n_tok, tq)
    out_c = (_dot(pooled.astype(BF16), wp_ref[...]) * cs_ref[...]).astype(BF16)


    if local:
        n_tiles = n_tok // tq
        half = D_WIN_ROWS // 2
        pair_entries = []
        for jj in range(3 * tq // GRID_W):
            row_entries = []
            for i in range(0, tq // GRID_W, 2):
                interior = PAIR_INTERIOR + (jj - half - i) + PAIR_COUNT // 2 - 1
                first = PAIR_EDGE + (jj - i) + PAIR_COUNT // 2 - 1 if jj < D_WIN_ROWS else PAIR_MASKED
                last = (PAIR_EDGE + (jj - D_WIN_ROWS - i) + PAIR_COUNT // 2 - 1 if jj >= half
                        else PAIR_MASKED)
                row_entries.append(jnp.where(ti == 0, first, jnp.where(ti == n_tiles - 1, last, interior)))
            pair_entries.append(row_entries)

    n_ctx_blk = vca_ref.shape[1]
    jobs = []

    def a_job(blk):
        cols = slice(blk * A_BLOCK, (blk + 1) * A_BLOCK)

        def scores():
            qblk = qa_ref[0, sub * tq + blk * A_BLOCK:sub * tq + (blk + 1) * A_BLOCK, :]
            qcat = jnp.concatenate([qblk[:, :128], qblk[:, 128:]], axis=0)
            qm = jnp.concatenate([_head_lanes(qcat, kv) for kv in range(A_KV_HEADS)], axis=0)
            if local:
                nblk = ti * (tq // A_BLOCK) + blk
                b0 = jnp.clip(nblk - 1, 0, n_tok // A_BLOCK - 3)
                kstart = pl.multiple_of(b0 * A_BLOCK, A_BLOCK)
                kwin = ka_ref[0, pl.ds(kstart, 3 * A_BLOCK), :]
                last_blk = n_tok // A_BLOCK - 1
                kind = jnp.where(nblk == 0, 1, jnp.where(nblk == last_blk, 2, 0))
                s_all = _dot_nt(jnp.concatenate([kwin, kca_ref[0]], axis=0), qm)
                s_loc, s_ctx = s_all[:3 * A_BLOCK], s_all[3 * A_BLOCK:]
            else:
                s_ctx = _dot_nt(kca_ref[0], qm)
            per_head = []
            for kv in range(A_KV_HEADS):
                rows = slice(kv * HEAD_DIM, (kv + 1) * HEAD_DIM)
                qsl = slice(kv * 2 * A_BLOCK, (kv + 1) * 2 * A_BLOCK)
                parts = []
                if local:
                    s_win = s_loc[:, qsl] + amask_ref[kind]
                    for j in range(3):
                        parts.append((s_win[j * A_BLOCK:(j + 1) * A_BLOCK], va_ref[0, b0 + j, rows, :]))
                for j in range(n_ctx_blk):
                    parts.append((s_ctx[j * A_BLOCK:(j + 1) * A_BLOCK, qsl], vca_ref[0, j, rows, :]))
                per_head.append(parts)
            return per_head

        def finish(per_head):
            for kv, parts in enumerate(per_head):
                col = lax.broadcasted_iota(jnp.int32, (1, 2 * A_BLOCK), 1)
                sink = jnp.where(col < A_BLOCK, sink_ref[layer, 2 * kv], sink_ref[layer, 2 * kv + 1])
                o2 = _softmax_pv_t(parts, sink)
                oa_scr[sub, (2 * kv) * 64:(2 * kv + 1) * 64, cols] = o2[:, :A_BLOCK]
                oa_scr[sub, (2 * kv + 1) * 64:(2 * kv + 2) * 64, cols] = o2[:, A_BLOCK:]

        return scores, finish

    def d_job(pair):
        heads = (2 * pair, 2 * pair + 1)

        def scores():
            qd = qd_ref[0, tile, :]
            qm = jnp.concatenate([_head_lanes(qd, hd) for hd in heads], axis=0)
            if local:
                d0 = jnp.clip(ti - 1, 0, n_tok // tq - 3)
                kwin = kd_ref[0, pl.ds(pl.multiple_of(d0 * tq, tq), 3 * tq), :]
                s_all = _dot_nt(jnp.concatenate([kwin, kcd_ref[0]], axis=0), qm)
                s_loc, s_ctx = s_all[:3 * tq], s_all[3 * tq:]
            else:
                s_ctx = _dot_nt(kcd_ref[0], qm)
            per_head = []
            for k, hd in enumerate(heads):
                rows = slice(hd * HEAD_DIM, (hd + 1) * HEAD_DIM)
                qsl = slice(k * tq, (k + 1) * tq)
                parts = []
                if local:
                    bias = jnp.concatenate(
                        [jnp.concatenate([bias_ref[hd, e] for e in row_entries], axis=1)
                         for row_entries in pair_entries], axis=0)
                    s_win = s_loc[:, qsl] + bias
                    for j in range(3):
                        parts.append((s_win[j * tq:(j + 1) * tq], vd_ref[0, d0 + j, rows, :]))
                parts.append((s_ctx[:, qsl], vcd_ref[0, 0, rows, :]))
                per_head.append(parts)
            return per_head

        def finish(per_head):
            for hd, parts in zip(heads, per_head):
                od_scr[sub, hd * HEAD_DIM:(hd + 1) * HEAD_DIM, :] = _softmax_pv_t(parts, None)

        return scores, finish

    for blk in range(tq // A_BLOCK):
        jobs.append(a_job(blk))
    for pair in range(D_HEADS // 2):
        jobs.append(d_job(pair))
    ahead = min(SCORE_LOOKAHEAD, len(jobs))
    pending = [jobs[j][0]() for j in range(ahead)]
    for j, (_, finish) in enumerate(jobs):
        if j + ahead < len(jobs):
            pending.append(jobs[j + ahead][0]())
        if deferred:
            deferred.pop(0)()
        finish(pending.pop(0))

    mix_scr[sub, :, 0:256] = oa_scr[sub].T.astype(BF16)
    mix_scr[sub, :, 256:512] = ob_ref[0, tile, :]
    mix_scr[sub, :, 512:768] = out_c
    mix_scr[sub, :, 768:1024] = od_scr[sub].T.astype(BF16)

    def out_chunk(c):
        cols = slice(c * OUT_CHUNK, (c + 1) * OUT_CHUNK)

        def run():
            mix = _dot(mix_scr[sub], wo_ref[:, cols])
            o_ref[0, tile, cols] = x_ref[0, tile, cols] + g1_ref[:, cols] * mix

        return run

    deferred.extend(out_chunk(c) for c in range(D_MODEL // OUT_CHUNK))


def _mixer(x, mod, row, layer, qa, ka, va, kca, vca, ob, cin, inv_cnt, wpool, cscale, qd, kd, vd, kcd, vcd, bias,
           amask, w_out, sink, *, local):
    b, t, d = x.shape
    lc = kca.shape[1]
    tq = MIX_TQ
    ts = min(MIX_STEP_TILES * tq, t)
    n_sub = ts // tq
    tok = lambda bi, i: (bi, i, 0)
    per_b = lambda bi, i: (bi, 0, 0)
    per_b4 = lambda bi, i: (bi, 0, 0, 0)

    return pl.pallas_call(
        functools.partial(_mix_kernel, local=local, n_tok=t, layer=layer),
        out_shape=jax.ShapeDtypeStruct((b, t, d), F32),
        grid=(b, t // ts),
        in_specs=[
            pl.BlockSpec(memory_space=pltpu.SMEM),
            pl.BlockSpec((1, ts, d), tok),
            _mod_spec(layer, row, 2),
            pl.BlockSpec((1, ts, 256), tok),
            pl.BlockSpec((1, t, 128), per_b),
            pl.BlockSpec((1, t // A_BLOCK, 128, A_BLOCK), per_b4),
            pl.BlockSpec((1, lc, 128), per_b),
            pl.BlockSpec((1, lc // A_BLOCK, 128, A_BLOCK), per_b4),
            pl.BlockSpec((1, ts, 256), tok),
            pl.BlockSpec((1, t, 256), per_b),
            pl.BlockSpec((ts, 256), lambda bi, i: (i, 0)),
            _layer_spec(layer, (256, 256)),
            _layer_spec(layer, (1, 256)),
            pl.BlockSpec((1, ts, 256), tok),
            pl.BlockSpec((1, t, 256), per_b),
            pl.BlockSpec((1, t // tq, 256, tq), per_b4),
            pl.BlockSpec((1, lc, 256), per_b),
            pl.BlockSpec((1, lc // tq, 256, tq), per_b4),
            _layer_spec(layer, (D_HEADS, PAIR_MASKED + 1, GRID_W, 2 * GRID_W)),
            pl.BlockSpec((3, 3 * A_BLOCK, 2 * A_BLOCK), lambda bi, i: (0, 0, 0)),
            _layer_spec(layer, (d, d)),
        ],
        out_specs=pl.BlockSpec((1, ts, d), tok),
        scratch_shapes=[pltpu.VMEM((n_sub, GROUP_W, tq), F32), pltpu.VMEM((n_sub, GROUP_W, tq), F32),
                        pltpu.VMEM((n_sub, tq, d), BF16)],
        compiler_params=pltpu.CompilerParams(
            dimension_semantics=("arbitrary", "arbitrary"), vmem_limit_bytes=VMEM_LIMIT),
        name="mixer_latent" if local else "mixer_ctx",
    )(sink, x, mod, qa, ka, va, kca, vca, ob, cin, inv_cnt, wpool, cscale, qd, kd, vd, kcd, vcd, bias, amask,
      w_out)


def _ffn_kernel(x_ref, sh_ref, sc_ref, g2_ref, gn_ref, wg_ref, wu_ref, wd_ref, o_ref, act_scr):
    xt = x_ref[0]
    h = _norm_modulate(xt, gn_ref[...], sc_ref[...], sh_ref[...])
    for c in range(FF_DIM // FFN_TF):
        cols = slice(c * FFN_TF, (c + 1) * FFN_TF)
        gate = _dot(h, wg_ref[:, cols])
        up = _dot(h, wu_ref[:, cols])
        act_scr[:, cols] = (gate * jax.nn.sigmoid(gate) * up).astype(BF16)
    o_ref[0] = xt + g2_ref[...] * _dot(act_scr[...], wd_ref[...])


def _ffn(x, mod, row, layer, g_ffn, wg, wu, wd):
    b, t, d = x.shape
    tm = min(FFN_TM, t)
    tok = lambda bi, i: (bi, i, 0)
    per_layer = lambda bi, i: (layer, 0, 0)
    resident = pl.Buffered(1)
    return pl.pallas_call(
        _ffn_kernel,
        out_shape=jax.ShapeDtypeStruct((b, t, d), F32),
        grid=(b, t // tm),
        in_specs=[
            pl.BlockSpec((1, tm, d), tok),
            _mod_spec(layer, row, 3),
            _mod_spec(layer, row, 4),
            _mod_spec(layer, row, 5),
            _layer_spec(layer, (1, d)),
            pl.BlockSpec((None, d, FF_DIM), per_layer, pipeline_mode=resident),
            pl.BlockSpec((None, d, FF_DIM), per_layer, pipeline_mode=resident),
            pl.BlockSpec((None, FF_DIM, d), per_layer, pipeline_mode=resident),
        ],
        out_specs=pl.BlockSpec((1, tm, d), tok),
        scratch_shapes=[pltpu.VMEM((tm, FF_DIM), BF16)],
        compiler_params=pltpu.CompilerParams(
            dimension_semantics=("arbitrary", "arbitrary"), vmem_limit_bytes=VMEM_LIMIT),
        name="ffn",
    )(x, mod, mod, mod, g_ffn, wg, wu, wd)


def _rope_tables(s):
    t = np.arange(s)
    half = 16
    inv = np.power(np.float32(ROPE_BASE), -np.arange(half, dtype=np.float32) / half).astype(np.float32)
    ang_r = (t // GRID_W).astype(np.float32)[:, None] * inv[None, :]
    ang_c = (t % GRID_W).astype(np.float32)[:, None] * inv[None, :]
    cos = np.concatenate([np.cos(ang_r)] * 2 + [np.cos(ang_c)] * 2, axis=-1)
    sin = np.concatenate([-np.sin(ang_r), np.sin(ang_r), -np.sin(ang_c), np.sin(ang_c)], axis=-1)
    return cos.astype(np.float32), sin.astype(np.float32)


def _pool_inverse_counts(n_tok):
    t = np.arange(n_tok)
    cols = []
    for w in C_POOLS:
        lo = np.clip(t - w // 2, 0, n_tok)
        hi = np.clip(t - w // 2 + w, 0, n_tok)
        cols.append(np.repeat((np.float32(1.0) / (hi - lo).astype(np.float32))[:, None], C_GROUP_W, axis=1))
    return jnp.asarray(np.concatenate(cols, axis=1), F32)


def _window_mask():
    kk = np.arange(3 * A_BLOCK)[:, None]
    qq = np.arange(2 * A_BLOCK)[None, :] % A_BLOCK
    kinds = [np.where(np.abs(kk - qq - back * A_BLOCK) <= A_WINDOW, 0.0, NEG_INF) for back in (1, 0, 2)]
    return jnp.asarray(np.stack(kinds), F32)


PAIR_COUNT = 2 * (D_WIN_ROWS - 1)
PAIR_INTERIOR = 0
PAIR_EDGE = PAIR_COUNT
PAIR_MASKED = 2 * PAIR_COUNT


def _neighbour_bias_pairs(rpb):
    cidx = np.arange(GRID_W)
    col_start = np.clip(cidx - D_WIN_COLS // 2, 0, GRID_W - D_WIN_COLS)
    col_ok_t = ((cidx[None, :] >= col_start[:, None]) & (cidx[None, :] < col_start[:, None] + D_WIN_COLS)).T
    coff_t = np.clip(cidx[:, None] - cidx[None, :], -(D_WIN_COLS - 1), D_WIN_COLS - 1) + (D_WIN_COLS - 1)
    select = (np.arange(2 * D_WIN_COLS - 1)[:, None, None] == coff_t[None]).astype(np.float32)
    blocks = lax.dot_general(rpb.astype(F32), jnp.asarray(select), (((3,), (0,)), ((), ())),
                             precision=lax.Precision.HIGHEST)
    blocks = jnp.where(col_ok_t, blocks, NEG_INF)
    d = np.arange(-(D_WIN_ROWS - 1), D_WIN_ROWS)
    seen = (d >= -(D_WIN_ROWS // 2)) & (d < D_WIN_ROWS // 2)
    interior = jnp.where(seen[:, None, None], blocks, NEG_INF)
    pair = lambda t: jnp.concatenate([t[:, :, 1:], t[:, :, :-1]], axis=-1)
    masked = jnp.full(blocks.shape[:2] + (1, GRID_W, 2 * GRID_W), NEG_INF, F32)
    return jnp.concatenate([pair(interior), pair(blocks), masked], axis=2)


def kernel(x, c, ctx, c_ctx, w_mod, b_mod, g_mix, g_ffn, w_in, w_out, a_q_gain, a_k_gain, a_sink,
           b_v_gain, b_w_s, b_b_s, c_w_pool, c_scale, d_q_gain, d_k_gain, d_rpb, w_gate, w_up, w_down):
    bsz, s, d = x.shape
    lc = ctx.shape[1]
    n_rows = s // GRID_W
    assert s % MIX_TQ == 0 and n_rows >= 2 * D_WIN_ROWS and MIX_TQ // GRID_W == D_WIN_ROWS // 2

    cvec = jnp.zeros((MOD_ROWS, d), F32).at[:bsz].set(c).at[bsz].set(c_ctx)
    mod = _modulation(cvec, w_mod, b_mod)
    lat_row = lambda bi: bi
    ctx_row = lambda bi: bsz

    seg = jnp.asarray(np.kron(np.eye(4), np.full((64, 64), 1.0 / 64)), BF16)
    cos1, sin1 = _rope_tables(s)
    tables_lat = tuple(jnp.asarray(np.tile(tab, (1, reps)))
                       for tab, reps in ((cos1, 4), (sin1, 4), (cos1, 2), (sin1, 2)))
    tables_ctx = tuple(jnp.zeros((min(IN_TM, bsz * lc), w), F32) for w in (256, 256, 128, 128))

    amask = _window_mask()
    inv_lat, inv_ctx = _pool_inverse_counts(s), _pool_inverse_counts(lc)
    bias_all = _neighbour_bias_pairs(d_rpb * LOG2E)

    w_in_b = w_in
    w_out_b = w_out.astype(BF16)
    wg_b, wu_b, wd_b = w_gate.astype(BF16), w_up.astype(BF16), w_down.astype(BF16)
    gmix = g_mix.reshape(DEPTH, 1, d)
    gffn = g_ffn.reshape(DEPTH, 1, d)
    aqg = jnp.tile(a_q_gain, (1, 4)).reshape(DEPTH, 1, 256)
    akg = jnp.tile(a_k_gain, (1, 2)).reshape(DEPTH, 1, 128)
    dqg = jnp.tile(d_q_gain, (1, 4)).reshape(DEPTH, 1, 256)
    dkg = jnp.tile(d_k_gain, (1, 4)).reshape(DEPTH, 1, 256)
    bvg = b_v_gain.reshape(DEPTH, 1, 256)
    ws_cat = jnp.transpose(b_w_s, (0, 2, 1, 3)).reshape(DEPTH, B_CHUNK, B_GROUPS * B_CHUNK).astype(BF16)
    bs_t = jnp.repeat(jnp.swapaxes(b_b_s, 1, 2), B_GROUP_W, axis=2)
    wpool = jnp.einsum('lgcd,gh->lgchd', c_w_pool, jnp.eye(len(C_POOLS), dtype=F32)
                       ).reshape(DEPTH, GROUP_W, GROUP_W).astype(BF16)
    cscale = c_scale.reshape(DEPTH, 1, 256)
    sink = a_sink * LOG2E
    common = (gmix, w_in_b, seg, aqg, akg, dqg, dkg)

    xc = ctx
    for l in range(DEPTH):
        last = l == DEPTH - 1
        lat = _in_projection(x, mod, lat_row, l, *common, tables_lat, bvg, ws_cat, bs_t, use_rope=True)
        xc_flat = xc.reshape(1, bsz * lc, d)
        per_batch = lambda o: o.reshape((bsz, o.shape[1] // bsz) + o.shape[2:])
        if last:
            cka, cva, ckd, cvd = map(per_batch, _ctx_kv_projection(
                xc_flat, mod, ctx_row, l, gmix, w_in_b, seg, akg, dkg))
        else:
            cqa, cka, cva, cob, ccin, cqd, ckd, cvd = map(per_batch, _in_projection(
                xc_flat, mod, ctx_row, l, *common, tables_ctx, bvg, ws_cat, bs_t, use_rope=False))
        qa, ka, va, ob, cin, qd, kd, vd = lat

        x = _mixer(x, mod, lat_row, l, qa, ka, va, cka, cva, ob, cin, inv_lat, wpool, cscale, qd, kd, vd, ckd, cvd,
                   bias_all, amask, w_out_b, sink, local=True)
        x = _ffn(x, mod, lat_row, l, gffn, wg_b, wu_b, wd_b)
        if not last:
            xc = _mixer(xc, mod, ctx_row, l, cqa, cka, cva, cka, cva, cob, ccin, inv_ctx, wpool, cscale, cqd, ckd,
                        cvd, ckd, cvd, bias_all, amask, w_out_b, sink, local=False)
            xc = _ffn(xc.reshape(1, bsz * lc, d), mod, ctx_row, l, gffn, wg_b, wu_b, wd_b).reshape(bsz, lc, d)
    return x
```
